```python
import jax, jax.numpy as jnp
from jax import lax
import numpy as np

D_MODEL = 2048
BATCH = 1
SEQ = 8192
DEPTH = 1
DEC_BATCH = 32
DEC_SEQ = 1
PAST_LEN = 16384
PAGE_SIZE = 128

MEM_LEN = 256
ML_HEADS = 8
ML_DH = 128
ML_DIM = ML_HEADS * ML_DH
ML_CONV = 4
ML_QKV_BLOCK = 4
ML_CHUNK = 64
NSA_HEADS = 8
NSA_KV_HEADS = 2
NSA_GROUP = NSA_HEADS // NSA_KV_HEADS
NSA_DH = 128
NSA_DIM = NSA_HEADS * NSA_DH
NSA_KV_W = NSA_KV_HEADS * NSA_DH
CMP_BLOCK = 32
CMP_STRIDE = 16
SEL_BLOCK = 64
SEL_TOPN = 16
WINDOW = 512
Q_BLOCK = 128
N_BAND = WINDOW // Q_BLOCK + 1
MEM_HEADS = 4
MEM_DH = 128
D_FF = 4 * D_MODEL
IN_SIZES = (ML_DIM, ML_DIM, NSA_DIM, 2 * NSA_KV_W, 2 * NSA_KV_W, 2 * NSA_KV_W, 3 * NSA_HEADS)
IN_COLS = sum(IN_SIZES)
EPS = 1e-6
NEG = -1e30
FORCE = 1e30

kernel_name = 'hymba_mlstm_nsa_decode_step'


def rmsnorm(x, g):
    xf = x.astype(jnp.float32)
    y = xf * lax.rsqrt(jnp.mean(xf * xf, axis=-1, keepdims=True) + EPS)
    return (y * g.astype(jnp.float32)).astype(x.dtype)


def masked_softmax(s, mask):
    p = jax.nn.softmax(jnp.where(mask, s, NEG), axis=-1)
    return jnp.where(mask, p, 0.0)


def split_in(u):
    B, T, _ = u.shape
    offs = np.cumsum(IN_SIZES)[:-1].tolist()
    xm, og, q, kc, ks, kw, gt = jnp.split(u, offs, axis=-1)
    kv = lambda a: a.reshape(B, T, 2, NSA_KV_HEADS, NSA_DH)
    return xm, og, q.reshape(B, T, NSA_HEADS, NSA_DH), kv(kc), kv(ks), kv(kw), gt


def causal_conv(x, buf, w, b):
    T = x.shape[1]
    xx = jnp.concatenate([buf.astype(x.dtype), x], axis=1)
    y = b + xx[:, 0:T] * w[0]
    for j in range(1, ML_CONV):
        y = y + xx[:, j:j + T] * w[j]
    return y, xx[:, T:]


def blockdiag(x, w):
    B, T, C = x.shape
    xb = x.reshape(B, T, C // ML_QKV_BLOCK, ML_QKV_BLOCK)
    return jnp.einsum('btnc,ncd->btnd', xb, w).reshape(B, T, C)


def mlstm_recurrence(q, k, v, ig, lf, c0, n0, m0):
    f32 = jnp.float32
    B, T, H, D = q.shape
    L = min(ML_CHUNK, T)
    nc = -(-T // L)
    pad = nc * L - T
    q, k, v = (a.astype(f32) for a in (q, k, v))
    if pad:
        q, k, v = (jnp.pad(a, ((0, 0), (0, pad), (0, 0), (0, 0))) for a in (q, k, v))
        ig = jnp.pad(ig, ((0, 0), (0, pad), (0, 0)), constant_values=NEG)
        lf = jnp.pad(lf, ((0, 0), (0, pad), (0, 0)))
    to_c = lambda a: a.reshape(B, nc, L, H, D).transpose(1, 0, 3, 2, 4)
    to_g = lambda a: a.reshape(B, nc, L, H).transpose(1, 0, 3, 2)
    causal = jnp.tril(jnp.ones((L, L), dtype=bool))

    def step(carry, inp):
        c, n, m = carry
        qc, kc, vc, ic, fc = inp
        b = jnp.cumsum(fc, axis=-1)
        d_in = jnp.where(causal, b[..., :, None] - b[..., None, :] + ic[..., None, :], NEG)
        d_x = b + m[..., None]
        m_t = jnp.maximum(d_x, jnp.max(d_in, axis=-1))
        w_in = jnp.exp(d_in - m_t[..., None])
        w_x = jnp.exp(d_x - m_t)
        s = jnp.einsum('bhtd,bhsd->bhts', qc, kc) * w_in
        num = jnp.einsum('bhts,bhsd->bhtd', s, vc) + w_x[..., None] * jnp.einsum('bhkv,bhtk->bhtv', c, qc)
        den = jnp.sum(s, axis=-1) + w_x * jnp.einsum('bhk,bhtk->bht', n, qc)
        h = num / jnp.maximum(jnp.abs(den), jnp.exp(-m_t))[..., None]
        m_new = m_t[..., -1]
        g_x = jnp.exp(b[..., -1] + m - m_new)
        g_s = jnp.exp(b[..., -1:] - b + ic - m_new[..., None])
        c_new = g_x[..., None, None] * c + jnp.einsum('bhs,bhsk,bhsv->bhkv', g_s, kc, vc)
        n_new = g_x[..., None] * n + jnp.einsum('bhs,bhsk->bhk', g_s, kc)
        return (c_new, n_new, m_new), h

    carry0 = (c0.astype(f32), n0.astype(f32), m0.astype(f32))
    (c, n, m), h = lax.scan(step, carry0, (to_c(q), to_c(k), to_c(v), to_g(ig), to_g(lf)))
    h = h.transpose(1, 0, 3, 2, 4).reshape(B, nc * L, H, D)[:, :T]
    return h, c, n, m


def mlstm_mixer(xm, og, conv_buf, c0, n0, m0, w_conv, b_conv, w_q, w_k, w_v, w_gate, b_gate, g_head):
    B, T, _ = xm.shape
    xc, conv_new = causal_conv(xm, conv_buf, w_conv, b_conv)
    xc = jax.nn.silu(xc)
    q = blockdiag(xc, w_q)
    k = blockdiag(xc, w_k) * (ML_DH ** -0.5)
    v = blockdiag(xm, w_v)
    gates = (jnp.concatenate([q, k, v], axis=-1) @ w_gate + b_gate).astype(jnp.float32)
    ig = gates[..., :ML_HEADS]
    lf = jax.nn.log_sigmoid(gates[..., ML_HEADS:])
    shp = (B, T, ML_HEADS, ML_DH)
    h, c, n, m = mlstm_recurrence(q.reshape(shp), k.reshape(shp), v.reshape(shp), ig, lf, c0, n0, m0)
    h = h * lax.rsqrt(jnp.mean(h * h, axis=-1, keepdims=True) + EPS) * g_head.astype(jnp.float32)
    out = jax.nn.sigmoid(og) * h.reshape(B, T, ML_DIM).astype(og.dtype)
    return out, conv_new, c, n, m


def compress(kv, pe, wpos, wphi):
    B, N = kv.shape[:2]
    r = CMP_BLOCK // CMP_STRIDE
    ns = N // CMP_STRIDE
    nc = ns - r + 1
    sub = kv[:, :ns * CMP_STRIDE].reshape(B, ns, CMP_STRIDE, 2, NSA_KV_HEADS, NSA_DH)
    sub_mean = jnp.mean(sub, axis=2)
    pooled = sub_mean[:, 0:nc]
    feat = None
    for o in range(r):
        sl = slice(o * CMP_STRIDE, (o + 1) * CMP_STRIDE)
        pe_o = jnp.transpose(pe[:, sl], (1, 0, 2))[:, :, None, :]
        wp_o = jnp.transpose(wpos[:, sl], (1, 0, 2))[:, :, None, :]
        f_o = jnp.sum(jax.nn.silu(sub + pe_o) * wp_o, axis=2)[:, o:o + nc]
        feat = f_o if feat is None else feat + f_o
        if o > 0:
            pooled = pooled + sub_mean[:, o:o + nc]
    return pooled / r + jnp.einsum('bjcgd,cde->bjcge', feat, wphi)


def nsa_sparse(q, pos_q, kvc, n_keys, gather_sel):
    f32 = jnp.float32
    B, T = q.shape[:2]
    scale = NSA_DH ** -0.5
    qg = q.reshape(B, T, NSA_KV_HEADS, NSA_GROUP, NSA_DH)
    nc = kvc.shape[1]
    s_c = jnp.einsum('btgrd,bjgd->bgrtj', qg, kvc[:, :, 0]).astype(f32) * scale
    blk_end = jnp.arange(nc) * CMP_STRIDE + (CMP_BLOCK - 1)
    p_c = masked_softmax(s_c, blk_end[None, :] <= pos_q[:, None])
    o_c = jnp.einsum('bgrtj,bjgd->btgrd', p_c.astype(q.dtype), kvc[:, :, 1])
    n_sel = (n_keys - 1) // SEL_BLOCK + 1
    ratio = SEL_BLOCK // CMP_STRIDE
    off = CMP_BLOCK // CMP_STRIDE - 1
    imp = jnp.sum(p_c, axis=2)
    imp = jnp.pad(imp, ((0, 0), (0, 0), (0, 0), (off, ratio * n_sel + ratio - nc)))
    imp_sel = imp[..., 0:ratio * n_sel:ratio]
    for o in range(1, ratio + off):
        imp_sel = imp_sel + imp[..., o:o + ratio * n_sel:ratio]
    blk = jnp.arange(n_sel)[None, :]
    cur = (pos_q // SEL_BLOCK)[:, None]
    forced = (blk == 0) | (blk == cur) | (blk == cur - 1)
    score = jnp.where(forced, FORCE, jnp.where(blk <= cur, imp_sel, NEG))
    n_top = min(SEL_TOPN, n_sel)
    _, idx = lax.top_k(score, n_top)
    pos_k = (idx[..., None] * SEL_BLOCK + jnp.arange(SEL_BLOCK)).reshape(B, NSA_KV_HEADS, T, n_top * SEL_BLOCK)
    kv_s = gather_sel(pos_k)
    s_s = jnp.einsum('btgrd,bgtmd->bgrtm', qg, kv_s[..., 0, :]).astype(f32) * scale
    p_s = masked_softmax(s_s, (pos_k <= pos_q[None, None, :, None])[:, :, None])
    o_s = jnp.einsum('bgrtm,bgtmd->btgrd', p_s.astype(q.dtype), kv_s[..., 1, :])
    return o_c.reshape(B, T, NSA_HEADS, NSA_DH), o_s.reshape(B, T, NSA_HEADS, NSA_DH)


def window_prompt(q, kv):
    f32 = jnp.float32
    B, T = q.shape[:2]
    nb = T // Q_BLOCK
    kb = kv.reshape(B, nb, Q_BLOCK, 2, NSA_KV_HEADS, NSA_DH)
    kp = jnp.pad(kb, ((0, 0), (N_BAND - 1, 0), (0, 0), (0, 0), (0, 0), (0, 0)))
    band = jnp.concatenate([kp[:, o:o + nb] for o in range(N_BAND)], axis=2)
    qb = q.reshape(B, nb, Q_BLOCK, NSA_KV_HEADS, NSA_GROUP, NSA_DH)
    s = jnp.einsum('bnugrd,bnmgd->bngrum', qb, band[:, :, :, 0]).astype(f32) * (NSA_DH ** -0.5)
    t = jnp.arange(nb)[:, None] * Q_BLOCK + jnp.arange(Q_BLOCK)[None, :]
    p = (jnp.arange(nb)[:, None] - (N_BAND - 1)) * Q_BLOCK + jnp.arange(N_BAND * Q_BLOCK)[None, :]
    diff = t[:, :, None] - p[:, None, :]
    mask = (diff >= 0) & (diff <= WINDOW) & (p[:, None, :] >= 0)
    pr = masked_softmax(s, mask[None, :, None, None])
    o = jnp.einsum('bngrum,bnmgd->bnugrd', pr.astype(q.dtype), band[:, :, :, 1])
    return o.reshape(B, T, NSA_HEADS, NSA_DH)


def window_sample(q, pos_q, kv_buf, kv_new, past):
    f32 = jnp.float32
    B, T = q.shape[:2]
    Lb = kv_buf.shape[1]
    kv = jnp.concatenate([kv_buf, kv_new], axis=1)
    p = past - Lb + jnp.arange(Lb + T)
    diff = pos_q[:, None] - p[None, :]
    mask = (diff >= 0) & (diff <= WINDOW)
    qg = q.reshape(B, T, NSA_KV_HEADS, NSA_GROUP, NSA_DH)
    s = jnp.einsum('btgrd,bmgd->bgrtm', qg, kv[:, :, 0]).astype(f32) * (NSA_DH ** -0.5)
    pr = masked_softmax(s, mask)
    o = jnp.einsum('bgrtm,bmgd->btgrd', pr.astype(q.dtype), kv[:, :, 1])
    return o.reshape(B, T, NSA_HEADS, NSA_DH), kv[:, Lb + T - Lb:]


def nsa_prompt(q, kv_c, kv_s, kv_w, pe, wpos, wphi):
    B, T = q.shape[:2]
    kvc = compress(kv_c, pe, wpos, wphi)
    nqb = T // Q_BLOCK
    bidx = jnp.arange(B)[:, None, None, None]
    gidx = jnp.arange(NSA_KV_HEADS)[None, :, None, None]

    def gather(p):
        return kv_s[bidx, p, :, gidx]

    def block(args):
        qb, pb = args
        return nsa_sparse(qb, pb, kvc, T, gather)

    q_blocks = q.reshape(B, nqb, Q_BLOCK, NSA_HEADS, NSA_DH).transpose(1, 0, 2, 3, 4)
    pos_blocks = jnp.arange(T, dtype=jnp.int32).reshape(nqb, Q_BLOCK)
    o_c, o_s = lax.map(block, (q_blocks, pos_blocks))
    unblock = lambda o: o.transpose(1, 0, 2, 3, 4).reshape(B, T, NSA_HEADS, NSA_DH)
    o_w = window_prompt(q, kv_w)
    return unblock(o_c), unblock(o_s), o_w, kv_w[:, T - min(WINDOW, T):]


def nsa_sample(q, kv_c, kv_s, kv_w, pool_c, pool_s, win_buf, page_table, pe, wpos, wphi):
    B, T = q.shape[:2]
    past = page_table.shape[1] * PAGE_SIZE
    past_c = pool_c[page_table].reshape(B, past, 2, NSA_KV_HEADS, NSA_DH)
    kvc = compress(jnp.concatenate([past_c, kv_c], axis=1), pe, wpos, wphi)
    pos = past + jnp.arange(T, dtype=jnp.int32)
    bidx = jnp.arange(B)[:, None, None, None]
    gidx = jnp.arange(NSA_KV_HEADS)[None, :, None, None]

    def gather(p):
        pc = jnp.clip(p, 0, past - 1)
        phys = page_table[bidx, pc // PAGE_SIZE]
        old = pool_s[phys, pc % PAGE_SIZE, :, gidx]
        new = kv_s[bidx, jnp.clip(p - past, 0, T - 1), :, gidx]
        return jnp.where((p < past)[..., None, None], old, new)

    o_c, o_s = nsa_sparse(q, pos, kvc, past + T, gather)
    o_w, win_new = window_sample(q, pos, win_buf, kv_w, past)
    return o_c, o_s, o_w, win_new


def nsa_merge(gt, o_c, o_s, o_w):
    B, T = gt.shape[:2]
    g = jax.nn.sigmoid(gt.astype(jnp.float32)).reshape(B, T, 3, NSA_HEADS, 1).astype(o_c.dtype)
    o = g[:, :, 0] * o_c + g[:, :, 1] * o_s + g[:, :, 2] * o_w
    return o.reshape(B, T, NSA_DIM)


def memory_kv(mem, g, w_kv):
    B, M = mem.shape[:2]
    return (rmsnorm(mem, g) @ w_kv).reshape(B, M, 2, MEM_HEADS, MEM_DH)


def memory_attend(h, kv, w_q, w_o):
    B, T, _ = h.shape
    q = (h @ w_q).reshape(B, T, MEM_HEADS, MEM_DH)
    s = jnp.einsum('bthd,bmhd->bhtm', q, kv[:, :, 0]).astype(jnp.float32) * (MEM_DH ** -0.5)
    p = jax.nn.softmax(s, axis=-1).astype(h.dtype)
    o = jnp.einsum('bhtm,bmhd->bthd', p, kv[:, :, 1]).reshape(B, T, MEM_HEADS * MEM_DH)
    return o @ w_o


def sq_relu_mlp(h, w1, w2):
    return jnp.square(jax.nn.relu(h @ w1)) @ w2


def setup_inputs(seed: int = 0) -> dict:
    key = jax.random.key(seed)
    ks = iter(jax.random.split(key, 48))
    f32 = jnp.float32

    def nrm(shape, scale=1.0):
        return jax.random.normal(next(ks), shape, f32) * scale

    def gain(shape):
        return 1.0 + nrm(shape, 0.02)

    n_pages = PAST_LEN // PAGE_SIZE
    n_phys = (5 * DEC_BATCH * n_pages + 3) // 4
    win_len = min(WINDOW, PAST_LEN)
    perm = jax.random.permutation(next(ks), n_phys)
    page_table = perm[:DEC_BATCH * n_pages].reshape(DEC_BATCH, n_pages).astype(jnp.int32)
    G, dh, H = NSA_KV_HEADS, NSA_DH, ML_HEADS
    b_mgate = jnp.concatenate([nrm((DEPTH, H), 0.1), jnp.linspace(3.0, 6.0, H, dtype=f32)[None, :] + nrm((DEPTH, H), 0.1)], axis=-1)
    return {
        'x_prompt': nrm((BATCH, SEQ, D_MODEL)),
        'x_sample': nrm((DEC_BATCH, DEC_SEQ, D_MODEL)),
        'cache_cmp_kv': nrm((DEPTH, n_phys, PAGE_SIZE, 2, G, dh)),
        'cache_sel_kv': nrm((DEPTH, n_phys, PAGE_SIZE, 2, G, dh)),
        'cache_win_kv': nrm((DEPTH, DEC_BATCH, win_len, 2, G, dh)),
        'cache_mem_kv': nrm((DEPTH, DEC_BATCH, MEM_LEN, 2, MEM_HEADS, MEM_DH)),
        'state_mlstm_c': nrm((DEPTH, DEC_BATCH, H, ML_DH, ML_DH)),
        'state_mlstm_n': nrm((DEPTH, DEC_BATCH, H, ML_DH)),
        'state_mlstm_m': nrm((DEPTH, DEC_BATCH, H)),
        'state_conv': nrm((DEPTH, DEC_BATCH, ML_CONV - 1, ML_DIM)),
        'page_table': page_table,
        'mem_prompt': nrm((BATCH, MEM_LEN, D_MODEL)),
        'g_mix': gain((DEPTH, D_MODEL)),
        'w_in': nrm((DEPTH, D_MODEL, IN_COLS), D_MODEL ** -0.5),
        'w_conv': nrm((DEPTH, ML_CONV, ML_DIM), ML_CONV ** -0.5),
        'b_conv': nrm((DEPTH, ML_DIM), 0.02),
        'w_mq': nrm((DEPTH, ML_DIM // ML_QKV_BLOCK, ML_QKV_BLOCK, ML_QKV_BLOCK), ML_QKV_BLOCK ** -0.5),
        'w_mk': nrm((DEPTH, ML_DIM // ML_QKV_BLOCK, ML_QKV_BLOCK, ML_QKV_BLOCK), ML_QKV_BLOCK ** -0.5),
        'w_mv': nrm((DEPTH, ML_DIM // ML_QKV_BLOCK, ML_QKV_BLOCK, ML_QKV_BLOCK), ML_QKV_BLOCK ** -0.5),
        'w_mgate': nrm((DEPTH, 3 * ML_DIM, 2 * H), (3 * ML_DIM) ** -0.5),
        'b_mgate': b_mgate,
        'g_mhead': gain((DEPTH, H, ML_DH)),
        'cmp_pe': nrm((DEPTH, 2, CMP_BLOCK, dh), 0.1),
        'cmp_wpos': nrm((DEPTH, 2, CMP_BLOCK, dh), CMP_BLOCK ** -0.5),
        'cmp_wphi': nrm((DEPTH, 2, dh, dh), dh ** -0.5),
        'w_out': nrm((DEPTH, ML_DIM + NSA_DIM, D_MODEL), (ML_DIM + NSA_DIM) ** -0.5),
        'g_memx': gain((DEPTH, D_MODEL)),
        'g_mems': gain((DEPTH, D_MODEL)),
        'w_mem_q': nrm((DEPTH, D_MODEL, MEM_HEADS * MEM_DH), D_MODEL ** -0.5),
        'w_mem_kv': nrm((DEPTH, D_MODEL, 2 * MEM_HEADS * MEM_DH), D_MODEL ** -0.5),
        'w_mem_o': nrm((DEPTH, MEM_HEADS * MEM_DH, D_MODEL), (MEM_HEADS * MEM_DH) ** -0.5),
        'g_ffn': gain((DEPTH, D_MODEL)),
        'w_ff1': nrm((DEPTH, D_MODEL, D_FF), D_MODEL ** -0.5),
        'w_ff2': nrm((DEPTH, D_FF, D_MODEL), D_FF ** -0.5),
        'g_final': gain((D_MODEL,)),
    }


def reference(x_prompt, x_sample, cache_cmp_kv, cache_sel_kv, cache_win_kv, cache_mem_kv,
              state_mlstm_c, state_mlstm_n, state_mlstm_m, state_conv, page_table, mem_prompt,
              g_mix, w_in, w_conv, b_conv, w_mq, w_mk, w_mv, w_mgate, b_mgate, g_mhead,
              cmp_pe, cmp_wpos, cmp_wphi, w_out, g_memx, g_mems, w_mem_q, w_mem_kv, w_mem_o,
              g_ffn, w_ff1, w_ff2, g_final):
    f32 = jnp.float32
    B = x_prompt.shape[0]
    hp, hs = x_prompt, x_sample
    p_cmp, p_sel, p_win, p_mem, p_c, p_n, p_m, p_conv = [], [], [], [], [], [], [], []
    s_cmp, s_sel, s_win, s_c, s_n, s_m, s_conv = [], [], [], [], [], [], []
    for l in range(DEPTH):
        ml_w = (w_conv[l], b_conv[l], w_mq[l], w_mk[l], w_mv[l], w_mgate[l], b_mgate[l], g_mhead[l])
        cmp_w = (cmp_pe[l], cmp_wpos[l], cmp_wphi[l])
        xm, og, q, kvc, kvs, kvw, gt = split_in(rmsnorm(hp, g_mix[l]) @ w_in[l])
        ml_o, conv_new, c_new, n_new, m_new = mlstm_mixer(
            xm, og, jnp.zeros((B, ML_CONV - 1, ML_DIM), xm.dtype),
            jnp.zeros((B, ML_HEADS, ML_DH, ML_DH), f32), jnp.zeros((B, ML_HEADS, ML_DH), f32),
            jnp.full((B, ML_HEADS), NEG, f32), *ml_w)
        o_c, o_s, o_w, win_new = nsa_prompt(q, kvc, kvs, kvw, *cmp_w)
        hp = hp + jnp.concatenate([ml_o, nsa_merge(gt, o_c, o_s, o_w)], axis=-1) @ w_out[l]
        mem_kv = memory_kv(mem_prompt, g_mems[l], w_mem_kv[l])
        hp = hp + memory_attend(rmsnorm(hp, g_memx[l]), mem_kv, w_mem_q[l], w_mem_o[l])
        hp = hp + sq_relu_mlp(rmsnorm(hp, g_ffn[l]), w_ff1[l], w_ff2[l])
        p_cmp.append(kvc)
        p_sel.append(kvs)
        p_win.append(win_new)
        p_mem.append(mem_kv)
        p_c.append(c_new)
        p_n.append(n_new)
        p_m.append(m_new)
        p_conv.append(conv_new)
        xm, og, q, kvc, kvs, kvw, gt = split_in(rmsnorm(hs, g_mix[l]) @ w_in[l])
        ml_o, conv_new, c_new, n_new, m_new = mlstm_mixer(
            xm, og, state_conv[l], state_mlstm_c[l], state_mlstm_n[l], state_mlstm_m[l], *ml_w)
        o_c, o_s, o_w, win_new = nsa_sample(q, kvc, kvs, kvw, cache_cmp_kv[l], cache_sel_kv[l],
                                            cache_win_kv[l], page_table, *cmp_w)
        hs = hs + jnp.concatenate([ml_o, nsa_merge(gt, o_c, o_s, o_w)], axis=-1) @ w_out[l]
        hs = hs + memory_attend(rmsnorm(hs, g_memx[l]), cache_mem_kv[l], w_mem_q[l], w_mem_o[l])
        hs = hs + sq_relu_mlp(rmsnorm(hs, g_ffn[l]), w_ff1[l], w_ff2[l])
        s_cmp.append(kvc)
        s_sel.append(kvs)
        s_win.append(win_new)
        s_c.append(c_new)
        s_n.append(n_new)
        s_m.append(m_new)
        s_conv.append(conv_new)
    y_prompt = rmsnorm(hp, g_final)
    y_sample = rmsnorm(hs, g_final)
    new_cmp_kv_p, new_sel_kv_p, new_win_kv_p = jnp.stack(p_cmp), jnp.stack(p_sel), jnp.stack(p_win)
    new_mem_kv_p = jnp.stack(p_mem)
    c_p, n_p, m_p, conv_p = jnp.stack(p_c), jnp.stack(p_n), jnp.stack(p_m), jnp.stack(p_conv)
    new_cmp_kv_s, new_sel_kv_s, new_win_kv_s = jnp.stack(s_cmp), jnp.stack(s_sel), jnp.stack(s_win)
    c_s, n_s, m_s, conv_s = jnp.stack(s_c), jnp.stack(s_n), jnp.stack(s_m), jnp.stack(s_conv)
    return (y_prompt, y_sample, new_cmp_kv_p, new_sel_kv_p, new_win_kv_p, new_mem_kv_p, c_p, n_p, m_p, conv_p,
            new_cmp_kv_s, new_sel_kv_s, new_win_kv_s, c_s, n_s, m_s, conv_s)
```

```python
import functools

import jax
import jax.numpy as jnp
import numpy as np
from jax import lax
from jax.experimental import pallas as pl
from jax.experimental.pallas import tpu as pltpu

f32 = jnp.float32
bf16 = jnp.bfloat16
i32 = jnp.int32

EPS = 1e-6
NEG = -1e30
FORCE = 1e30
ML_HEADS = 8
ML_DH = 128
ML_DIM = ML_HEADS * ML_DH
ML_CONV = 4
ML_QKV_BLOCK = 4
NSA_HEADS = 8
NSA_KV_HEADS = 2
NSA_GROUP = NSA_HEADS // NSA_KV_HEADS
NSA_DH = 128
NSA_DIM = NSA_HEADS * NSA_DH
NSA_KV_W = NSA_KV_HEADS * NSA_DH
CMP_BLOCK = 32
CMP_STRIDE = 16
SEL_BLOCK = 64
SEL_TOPN = 16
WINDOW = 512
Q_BLOCK = 128
PAGE_SIZE = 128
MEM_HEADS = 4
MEM_DH = 128
KV_ROW = 2 * NSA_KV_W

LANES = 128
SUBLANES = 8
VMEM_LIMIT = 56 * 1024 * 1024

MASK_BIAS = -1e9
REMOVED = -3.0e38
SEL_KT = 512


def _cparams(*sem):
    return pltpu.CompilerParams(dimension_semantics=sem, vmem_limit_bytes=VMEM_LIMIT)


def _mm(a, b):
    return jnp.dot(a, b, preferred_element_type=f32)


def _mm_nt(a, b):
    return lax.dot_general(a, b, (((1,), (1,)), ((), ())), preferred_element_type=f32)


def _split2(x):
    h = x.astype(bf16)
    return h, (x - h.astype(f32)).astype(bf16)


def _split3(x):
    h = x.astype(bf16)
    r = x - h.astype(f32)
    m = r.astype(bf16)
    return h, m, (r - m.astype(f32)).astype(bf16)


def _dot3(a, b, mm=_mm):
    ah, al = _split2(a)
    bh, bl = _split2(b)
    return mm(ah, bh) + mm(al, bh) + mm(ah, bl)


def _dot_sel_l(sel, x):
    h, m, l = _split3(x)
    return _mm(sel, h) + _mm(sel, m) + _mm(sel, l)


def _dot_sel_r(x, sel):
    h, m, l = _split3(x)
    return _mm(h, sel) + _mm(m, sel) + _mm(l, sel)


def _rms(x, g):
    return x * lax.rsqrt(jnp.mean(x * x, axis=-1, keepdims=True) + EPS) * g


def _div_pow2(x, d):
    assert d & (d - 1) == 0
    return lax.shift_right_logical(x, jnp.full(x.shape, d.bit_length() - 1, x.dtype))


def _sigmoid(x):
    return 1.0 / (1.0 + jnp.exp(-x))


def _log_sigmoid(x):
    return jnp.minimum(x, 0.0) - jnp.log(1.0 + jnp.exp(-jnp.abs(x)))


def _masked_softmax_rows(s, valid):
    s = jnp.where(valid, s, NEG)
    mx = jnp.max(s, axis=-1, keepdims=True)
    e = jnp.where(valid, jnp.exp(s - mx), 0.0)
    den = jnp.sum(e, axis=-1, keepdims=True)
    return e, jnp.where(den > 0.0, den, 1.0)


def _norm_matmul_kernel(x_ref, g_ref, w_ref, o_ref, xn_ref):
    @pl.when(pl.program_id(1) == 0)
    def _():
        xn_ref[...] = _rms(x_ref[...], g_ref[...]).astype(bf16)

    o_ref[...] = _mm(xn_ref[...], w_ref[...].astype(bf16))


def _norm_matmul(x, g, w, n_cols, tm, tn, name):
    M, K = x.shape
    return pl.pallas_call(
        _norm_matmul_kernel,
        grid=(M // tm, n_cols // tn),
        in_specs=[pl.BlockSpec((tm, K), lambda i, j: (i, 0)),
                  pl.BlockSpec((1, K), lambda i, j: (0, 0)),
                  pl.BlockSpec((K, tn), lambda i, j: (0, j))],
        out_specs=pl.BlockSpec((tm, tn), lambda i, j: (i, j)),
        out_shape=jax.ShapeDtypeStruct((M, n_cols), f32),
        scratch_shapes=[pltpu.VMEM((tm, K), bf16)],
        compiler_params=_cparams("parallel", "arbitrary"),
        name=name,
    )(x, g.reshape(1, K), w)


def _out_proj_kernel(a1_ref, a2_ref, w1_ref, w2_ref, r_ref, o_ref):
    o_ref[...] = (r_ref[...] + _mm(a1_ref[...].astype(bf16), w1_ref[...].astype(bf16))
                  + _mm(a2_ref[...].astype(bf16), w2_ref[...].astype(bf16)))


def _out_proj(a1, a2, w, res, tm, tn, name):
    M, K1 = a1.shape
    K2 = a2.shape[1]
    assert K1 == K2 and w.shape[0] == K1 + K2
    N = w.shape[1]
    return pl.pallas_call(
        _out_proj_kernel,
        grid=(M // tm, N // tn),
        in_specs=[pl.BlockSpec((tm, K1), lambda i, j: (i, 0)),
                  pl.BlockSpec((tm, K2), lambda i, j: (i, 0)),
                  pl.BlockSpec((K1, tn), lambda i, j: (0, j)),
                  pl.BlockSpec((K2, tn), lambda i, j: (1, j)),
                  pl.BlockSpec((tm, tn), lambda i, j: (i, j))],
        out_specs=pl.BlockSpec((tm, tn), lambda i, j: (i, j)),
        out_shape=jax.ShapeDtypeStruct((M, N), f32),
        compiler_params=_cparams("parallel", "arbitrary"),
        name=name,
    )(a1, a2, w, w, res)


def _matmul_res_kernel(a_ref, w_ref, r_ref, o_ref):
    o_ref[...] = r_ref[...] + _mm(a_ref[...].astype(bf16), w_ref[...].astype(bf16))


def _matmul_res(a, w, res, tm, tn, name):
    M, K = a.shape
    N = w.shape[1]
    return pl.pallas_call(
        _matmul_res_kernel,
        grid=(M // tm, N // tn),
        in_specs=[pl.BlockSpec((tm, K), lambda i, j: (i, 0)),
                  pl.BlockSpec((K, tn), lambda i, j: (0, j)),
                  pl.BlockSpec((tm, tn), lambda i, j: (i, j))],
        out_specs=pl.BlockSpec((tm, tn), lambda i, j: (i, j)),
        out_shape=jax.ShapeDtypeStruct((M, N), f32),
        compiler_params=_cparams("parallel", "arbitrary"),
        name=name,
    )(a, w, res)


def _ffn_kernel(h_ref, g_ref, w1_ref, w2_ref, gf_ref, y_ref, xn_ref, acc_ref):
    f = pl.program_id(1)

    @pl.when(f == 0)
    def _():
        xn_ref[...] = _rms(h_ref[...], g_ref[...]).astype(bf16)
        acc_ref[...] = jnp.zeros_like(acc_ref)

    a = _mm(xn_ref[...], w1_ref[...].astype(bf16))
    a = jnp.square(jnp.maximum(a, 0.0))
    acc_ref[...] += _mm(a.astype(bf16), w2_ref[...].astype(bf16))

    @pl.when(f == pl.num_programs(1) - 1)
    def _():
        y_ref[...] = _rms(h_ref[...] + acc_ref[...], gf_ref[...])


def _ffn_final(h, g, w1, w2, g_final, tm, tf, name):
    M, D = h.shape
    F = w1.shape[1]
    return pl.pallas_call(
        _ffn_kernel,
        grid=(M // tm, F // tf),
        in_specs=[pl.BlockSpec((tm, D), lambda i, j: (i, 0)),
                  pl.BlockSpec((1, D), lambda i, j: (0, 0)),
                  pl.BlockSpec((D, tf), lambda i, j: (0, j)),
                  pl.BlockSpec((tf, D), lambda i, j: (j, 0)),
                  pl.BlockSpec((1, D), lambda i, j: (0, 0))],
        out_specs=pl.BlockSpec((tm, D), lambda i, j: (i, 0)),
        out_shape=jax.ShapeDtypeStruct((M, D), f32),
        scratch_shapes=[pltpu.VMEM((tm, D), bf16), pltpu.VMEM((tm, D), f32)],
        compiler_params=_cparams("parallel", "arbitrary"),
        name=name,
    )(h, g.reshape(1, D), w1, w2, g_final.reshape(1, D))


def _mem_prompt_kernel(h_ref, g_ref, wq_ref, k_ref, v_ref, wo_ref, o_ref):
    h = h_ref[...]
    xn = _rms(h, g_ref[...]).astype(bf16)
    q = _mm(xn, wq_ref[...].astype(bf16)) * (MEM_DH ** -0.5)
    outs = []
    for hd in range(MEM_HEADS):
        sl = slice(hd * MEM_DH, (hd + 1) * MEM_DH)
        s = _mm_nt(q[:, sl].astype(bf16), k_ref[:, sl].astype(bf16))
        e = jnp.exp(s - jnp.max(s, axis=-1, keepdims=True))
        p = e / jnp.sum(e, axis=-1, keepdims=True)
        outs.append(_mm(p.astype(bf16), v_ref[:, sl].astype(bf16)))
    o = jnp.concatenate(outs, axis=-1)
    o_ref[...] = h + _mm(o.astype(bf16), wo_ref[...].astype(bf16))


def _mem_prompt(h, g, wq, mem_kv, wo, tm, name):
    M, D = h.shape
    HD = MEM_HEADS * MEM_DH
    ML = mem_kv.shape[0]
    return pl.pallas_call(
        _mem_prompt_kernel,
        grid=(M // tm,),
        in_specs=[pl.BlockSpec((tm, D), lambda i: (i, 0)),
                  pl.BlockSpec((1, D), lambda i: (0, 0)),
                  pl.BlockSpec((D, HD), lambda i: (0, 0)),
                  pl.BlockSpec((ML, HD), lambda i: (0, 0)),
                  pl.BlockSpec((ML, HD), lambda i: (0, 1)),
                  pl.BlockSpec((HD, D), lambda i: (0, 0))],
        out_specs=pl.BlockSpec((tm, D), lambda i: (i, 0)),
        out_shape=jax.ShapeDtypeStruct((M, D), f32),
        compiler_params=_cparams("parallel"),
        name=name,
    )(h, g.reshape(1, D), wq, mem_kv, mem_kv, wo)


def _mem_sample_kernel(q_ref, kv_ref, o_ref):
    HD = MEM_HEADS * MEM_DH
    q = q_ref[...] * (MEM_DH ** -0.5)
    outs = []
    for hd in range(MEM_HEADS):
        sl = slice(hd * MEM_DH, (hd + 1) * MEM_DH)
        k = kv_ref[:, sl]
        v = kv_ref[:, HD + hd * MEM_DH:HD + (hd + 1) * MEM_DH]
        s = jnp.sum(k * q[:, sl], axis=-1, keepdims=True)
        e = jnp.exp(s - jnp.max(s, axis=0, keepdims=True))
        p = e / jnp.sum(e, axis=0, keepdims=True)
        outs.append(jnp.sum(p * v, axis=0, keepdims=True))
    o_ref[...] = jnp.concatenate(outs, axis=-1)


def _mem_sample(q, kv, name):
    B, HD = q.shape
    ML = kv.shape[1]
    out = pl.pallas_call(
        _mem_sample_kernel,
        grid=(B,),
        in_specs=[pl.BlockSpec((None, 1, HD), lambda b: (b, 0, 0)),
                  pl.BlockSpec((None, ML, 2 * HD), lambda b: (b, 0, 0))],
        out_specs=pl.BlockSpec((None, 1, HD), lambda b: (b, 0, 0)),
        out_shape=jax.ShapeDtypeStruct((B, 1, HD), f32),
        compiler_params=_cparams("parallel"),
        name=name,
    )(q.reshape(B, 1, HD), kv)
    return out.reshape(B, HD)


def _blockdiag_weights(w):
    per = LANES // ML_QKV_BLOCK
    nchunk = w.shape[0] // per
    wc = w.reshape(nchunk, per, ML_QKV_BLOCK, ML_QKV_BLOCK)
    eye = jnp.eye(per, dtype=w.dtype)
    full = wc[:, :, :, None, :] * eye[None, :, None, :, None]
    return full.reshape(nchunk, LANES, LANES)


def _ml_qkv_gates(xc, xm, wq_ref, wk_ref, wv_ref, wg_ref, bg_ref):
    qs, ks, vs = [], [], []
    for c in range(ML_DIM // LANES):
        sl = slice(c * LANES, (c + 1) * LANES)
        qs.append(_dot3(xc[:, sl], wq_ref[c]))
        ks.append(_dot3(xc[:, sl], wk_ref[c]) * (ML_DH ** -0.5))
        vs.append(_dot3(xm[:, sl], wv_ref[c]))
    q = jnp.concatenate(qs, axis=-1)
    k = jnp.concatenate(ks, axis=-1)
    v = jnp.concatenate(vs, axis=-1)
    gates = (_dot3(q, wg_ref[0:ML_DIM, :]) + _dot3(k, wg_ref[ML_DIM:2 * ML_DIM, :])
             + _dot3(v, wg_ref[2 * ML_DIM:3 * ML_DIM, :]) + bg_ref[...])
    return q, k, v, gates


def _mlstm_prompt_kernel(xm_ref, og_ref, wconv_ref, bconv_ref, wq_ref, wk_ref, wv_ref, wg_ref, wgt_ref,
                         bg_ref, bgt_ref, gh_ref, o_ref, c_ref, n_ref, m_ref, prev_sc, *, L):
    H, D = ML_HEADS, ML_DH

    @pl.when(pl.program_id(0) == 0)
    def _():
        prev_sc[...] = jnp.zeros_like(prev_sc)
        c_ref[...] = jnp.zeros_like(c_ref)
        n_ref[...] = jnp.zeros_like(n_ref)
        m_ref[...] = jnp.full(m_ref.shape, NEG, f32)

    x = xm_ref[...]
    full = jnp.concatenate([prev_sc[...], x], axis=0)
    y = bconv_ref[...]
    for j in range(ML_CONV):
        off = SUBLANES - (ML_CONV - 1) + j
        y = y + full[off:off + L] * wconv_ref[j:j + 1, :]
    prev_sc[...] = x[L - SUBLANES:L]
    xc = y * _sigmoid(y)

    q, k, v, gates = _ml_qkv_gates(xc, x, wq_ref, wk_ref, wv_ref, wg_ref, bg_ref)
    gates_t = (_dot3(wgt_ref[:, 0:ML_DIM], q, _mm_nt) + _dot3(wgt_ref[:, ML_DIM:2 * ML_DIM], k, _mm_nt)
               + _dot3(wgt_ref[:, 2 * ML_DIM:3 * ML_DIM], v, _mm_nt) + bgt_ref[...])
    ig_c = gates[:, 0:H]
    lf_c = _log_sigmoid(gates[:, H:2 * H])
    ig_r = gates_t[0:H, :]
    lf_r = _log_sigmoid(gates_t[H:2 * H, :])

    t_i = lax.broadcasted_iota(i32, (L, L), 0)
    s_i = lax.broadcasted_iota(i32, (L, L), 1)
    causal = s_i <= t_i
    tri = jnp.where(causal, 1.0, 0.0).astype(bf16)
    b_c = _dot_sel_l(tri, lf_c)
    tri_u = jnp.where(t_i <= s_i, 1.0, 0.0).astype(bf16)
    b_r = _dot_sel_r(lf_r, tri_u)

    for h in range(H):
        sl = slice(h * D, (h + 1) * D)
        qh, kh, vh = q[:, sl], k[:, sl], v[:, sl]
        bc = b_c[:, h:h + 1]
        m_prev = m_ref[h:h + 1, 0:1]
        d_in = jnp.where(causal, bc - b_r[h:h + 1, :] + ig_r[h:h + 1, :], NEG)
        d_x = bc + m_prev
        m_t = jnp.maximum(d_x, jnp.max(d_in, axis=-1, keepdims=True))
        w_in = jnp.exp(d_in - m_t)
        w_x = jnp.exp(d_x - m_t)
        qb = qh.astype(bf16)
        kb = kh.astype(bf16)
        vb = vh.astype(bf16)
        s = _mm_nt(qb, kb) * w_in
        c_old = c_ref[h]
        n_old = n_ref[h:h + 1, :]
        num = _mm(s.astype(bf16), vb) + w_x * _mm(qb, c_old.astype(bf16))
        den = jnp.sum(s, axis=-1, keepdims=True) + w_x * jnp.sum(qh * n_old, axis=-1, keepdims=True)
        hh = num / jnp.maximum(jnp.abs(den), jnp.exp(-m_t))
        m_new = m_t[L - 1:L, :]
        b_last = bc[L - 1:L, :]
        g_x = jnp.exp(b_last + m_prev - m_new)
        g_s = jnp.exp(b_last - bc + ig_c[:, h:h + 1] - m_new)
        ks_ = kh * g_s
        c_ref[h] = g_x * c_old + _mm(ks_.T.astype(bf16), vb)
        n_ref[h:h + 1, :] = g_x * n_old + jnp.sum(ks_, axis=0, keepdims=True)
        m_ref[h:h + 1, :] = jnp.broadcast_to(m_new, (1, LANES))
        hn = hh * lax.rsqrt(jnp.mean(hh * hh, axis=-1, keepdims=True) + EPS) * gh_ref[:, sl]
        o_ref[:, sl] = _sigmoid(og_ref[:, sl]) * hn


def _mlstm_prompt(u, w_conv, b_conv, wbd_q, wbd_k, wbd_v, w_gate, b_gate, g_head, L, name):
    T = u.shape[0]
    H, D = ML_HEADS, ML_DH
    nch = ML_DIM // LANES
    full2 = lambda shape: pl.BlockSpec(shape, lambda i: (0,) * len(shape))
    out, c, n, m = pl.pallas_call(
        functools.partial(_mlstm_prompt_kernel, L=L),
        grid=(T // L,),
        in_specs=[pl.BlockSpec((L, ML_DIM), lambda i: (i, 0)),
                  pl.BlockSpec((L, ML_DIM), lambda i: (i, 1)),
                  full2((ML_CONV, ML_DIM)), full2((1, ML_DIM)),
                  full2((nch, LANES, LANES)), full2((nch, LANES, LANES)), full2((nch, LANES, LANES)),
                  full2((3 * ML_DIM, 2 * H)), full2((2 * H, 3 * ML_DIM)),
                  full2((1, 2 * H)), full2((2 * H, 1)), full2((1, ML_DIM))],
        out_specs=[pl.BlockSpec((L, ML_DIM), lambda i: (i, 0)),
                   full2((H, D, D)), full2((H, D)), full2((H, LANES))],
        out_shape=[jax.ShapeDtypeStruct((T, ML_DIM), f32),
                   jax.ShapeDtypeStruct((H, D, D), f32),
                   jax.ShapeDtypeStruct((H, D), f32),
                   jax.ShapeDtypeStruct((H, LANES), f32)],
        scratch_shapes=[pltpu.VMEM((SUBLANES, ML_DIM), f32)],
        compiler_params=_cparams("arbitrary"),
        name=name,
    )(u, u, w_conv, b_conv.reshape(1, ML_DIM), wbd_q, wbd_k, wbd_v, w_gate, w_gate.T,
      b_gate.reshape(1, 2 * H), b_gate.reshape(2 * H, 1), g_head.reshape(1, ML_DIM))
    return out, c, n, m[:, 0]


def _mlstm_sample_pre_kernel(xm_ref, s0_ref, s1_ref, s2_ref, wconv_ref, bconv_ref, wq_ref, wk_ref, wv_ref,
                             wg_ref, bg_ref, q_ref, k_ref, v_ref, g_ref):
    x = xm_ref[...]
    y = (bconv_ref[...] + s0_ref[...] * wconv_ref[0:1, :] + s1_ref[...] * wconv_ref[1:2, :]
         + s2_ref[...] * wconv_ref[2:3, :] + x * wconv_ref[3:4, :])
    xc = y * _sigmoid(y)
    q, k, v, gates = _ml_qkv_gates(xc, x, wq_ref, wk_ref, wv_ref, wg_ref, bg_ref)
    q_ref[...] = q
    k_ref[...] = k
    v_ref[...] = v
    g_ref[...] = gates


def _mlstm_sample_step_kernel(qc_ref, kc_ref, q_ref, k_ref, v_ref, gt_ref, og_ref, gh_ref, c_ref, n_ref, m_ref,
                              o_ref, cn_ref, nn_ref, mn_ref):
    H, D = ML_HEADS, ML_DH
    gates = gt_ref[...]
    ig = gates[:, 0:H]
    lf = _log_sigmoid(gates[:, H:2 * H])
    m_old = m_ref[...]
    m_new = jnp.maximum(lf + m_old, ig)
    w_in = jnp.exp(ig - m_new)
    w_x = jnp.exp(lf + m_old - m_new)
    mn_ref[...] = m_new
    e_m = jnp.exp(-m_new)
    og = og_ref[...]
    outs = []
    for h in range(H):
        sl = slice(h * D, (h + 1) * D)
        qr, kr, vr = q_ref[h:h + 1, :], k_ref[h:h + 1, :], v_ref[h:h + 1, :]
        qcol, kcol = qc_ref[:, h:h + 1], kc_ref[:, h:h + 1]
        wi, wx = w_in[:, h:h + 1], w_x[:, h:h + 1]
        c_old = c_ref[h]
        n_old = n_ref[h:h + 1, :]
        s = jnp.sum(qr * kr, axis=-1, keepdims=True) * wi
        num = s * vr + wx * jnp.sum(c_old * qcol, axis=0, keepdims=True)
        den = s + wx * jnp.sum(n_old * qr, axis=-1, keepdims=True)
        hh = num / jnp.maximum(jnp.abs(den), e_m[:, h:h + 1])
        cn_ref[h] = wx * c_old + wi * (kcol * vr)
        nn_ref[h:h + 1, :] = wx * n_old + wi * kr
        hn = hh * lax.rsqrt(jnp.mean(hh * hh, axis=-1, keepdims=True) + EPS) * gh_ref[:, sl]
        outs.append(_sigmoid(og[:, sl]) * hn)
    o_ref[...] = jnp.concatenate(outs, axis=-1)


def _mlstm_sample(xm, og, conv_state, c0, n0, m0, w_conv, b_conv, wbd_q, wbd_k, wbd_v, w_gate, b_gate, g_head):
    B = xm.shape[0]
    H, D = ML_HEADS, ML_DH
    sds = lambda *s: jax.ShapeDtypeStruct(s, f32)
    q, k, v, gates = pl.pallas_call(
        _mlstm_sample_pre_kernel,
        out_shape=[sds(B, ML_DIM), sds(B, ML_DIM), sds(B, ML_DIM), sds(B, 2 * H)],
        compiler_params=pltpu.CompilerParams(vmem_limit_bytes=VMEM_LIMIT),
        name="mlstm_sample_pre",
    )(xm, conv_state[:, 0], conv_state[:, 1], conv_state[:, 2], w_conv, b_conv.reshape(1, ML_DIM),
      wbd_q, wbd_k, wbd_v, w_gate, b_gate.reshape(1, 2 * H))
    q3, k3, v3 = (a.reshape(B, H, D) for a in (q, k, v))
    per_b = lambda *s: pl.BlockSpec((None,) + s, lambda b: (b,) + (0,) * len(s))
    out, c, n, m = pl.pallas_call(
        _mlstm_sample_step_kernel,
        grid=(B,),
        in_specs=[per_b(D, H), per_b(D, H), per_b(H, D), per_b(H, D), per_b(H, D), per_b(1, 2 * H),
                  per_b(1, ML_DIM), pl.BlockSpec((1, ML_DIM), lambda b: (0, 0)),
                  per_b(H, D, D), per_b(H, D), per_b(1, H)],
        out_specs=[per_b(1, ML_DIM), per_b(H, D, D), per_b(H, D), per_b(1, H)],
        out_shape=[sds(B, 1, ML_DIM), sds(B, H, D, D), sds(B, H, D), sds(B, 1, H)],
        compiler_params=_cparams("parallel"),
        name="mlstm_sample_step",
    )(q3.transpose(0, 2, 1), k3.transpose(0, 2, 1), q3, k3, v3, gates.reshape(B, 1, 2 * H),
      og.reshape(B, 1, ML_DIM), g_head.reshape(1, ML_DIM), c0, n0, m0.reshape(B, 1, H))
    return out.reshape(B, ML_DIM), c, n, m.reshape(B, H)


def _compress_kernel(*refs, n_pages, prefetch):
    if prefetch:
        refs = refs[1:]
    pages = refs[:n_pages]
    pe_ref, wp_ref, wphi_ref, o_ref, f0_sc, f1_sc, mn_sc = refs[n_pages:]
    step = pl.program_id(1)
    sub = PAGE_SIZE // CMP_STRIDE
    for p in range(n_pages):
        x3 = pages[p][...].reshape(sub, CMP_STRIDE, KV_ROW)
        base = pl.multiple_of((step * n_pages + p) * sub, sub)
        for o, sc in ((0, f0_sc), (1, f1_sc)):
            y = x3 + pe_ref[o][None]
            sc[pl.ds(base, sub), :] = jnp.sum(y * _sigmoid(y) * wp_ref[o][None], axis=1)
        mn_sc[pl.ds(base, sub), :] = jnp.sum(x3, axis=1) * (1.0 / CMP_STRIDE)

    @pl.when(step == pl.num_programs(1) - 1)
    def _():
        ns = f0_sc.shape[0]
        feat = f0_sc[...] + pltpu.roll(f1_sc[...], ns - 1, axis=0)
        mn = mn_sc[...]
        pooled = (mn + pltpu.roll(mn, ns - 1, axis=0)) * (CMP_STRIDE / CMP_BLOCK)
        for c in range(2):
            for g in range(NSA_KV_HEADS):
                sl = slice((c * NSA_KV_HEADS + g) * NSA_DH, (c * NSA_KV_HEADS + g + 1) * NSA_DH)
                o_ref[:, sl] = pooled[:, sl] + _dot3(feat[:, sl], wphi_ref[c])


def _compress_tables(pe, wpos):
    def lay(a):
        r = CMP_BLOCK // CMP_STRIDE
        a = a.reshape(2, r, CMP_STRIDE, NSA_DH).transpose(1, 2, 0, 3)
        a = jnp.broadcast_to(a[:, :, :, None, :], (r, CMP_STRIDE, 2, NSA_KV_HEADS, NSA_DH))
        return a.reshape(r, CMP_STRIDE, KV_ROW)
    return lay(pe), lay(wpos)


def _compress_call(in_specs_pages, args_pages, grid, n_sub, pe_t, wp_t, wphi, prefetch_args, name):
    B = grid[0]
    n_pages = len(args_pages)
    nprefetch = len(prefetch_args)
    const = lambda shape: pl.BlockSpec(shape, lambda *a: (0,) * len(shape))
    in_specs = list(in_specs_pages) + [const(pe_t.shape), const(wp_t.shape), const(wphi.shape)]
    out_spec = pl.BlockSpec((None, n_sub, KV_ROW), lambda b, s, *a: (b, 0, 0))
    scratch = [pltpu.VMEM((n_sub, KV_ROW), f32)] * 3
    kern = functools.partial(_compress_kernel, n_pages=n_pages, prefetch=nprefetch > 0)
    out_shape = jax.ShapeDtypeStruct((B, n_sub, KV_ROW), f32)
    cp = _cparams("parallel", "arbitrary")
    if nprefetch:
        gs = pltpu.PrefetchScalarGridSpec(num_scalar_prefetch=nprefetch, grid=grid, in_specs=in_specs,
                                          out_specs=out_spec, scratch_shapes=scratch)
        return pl.pallas_call(kern, grid_spec=gs, out_shape=out_shape, compiler_params=cp, name=name)(
            *prefetch_args, *args_pages, pe_t, wp_t, wphi)
    return pl.pallas_call(kern, grid=grid, in_specs=in_specs, out_specs=out_spec, out_shape=out_shape,
                          scratch_shapes=scratch, compiler_params=cp, name=name)(*args_pages, pe_t, wp_t, wphi)


def _compress_prompt(u, col_block, pe_t, wp_t, wphi, pages_per_step=8):
    T = u.shape[0]
    P = pages_per_step
    specs = [pl.BlockSpec((PAGE_SIZE, KV_ROW), functools.partial(lambda b, s, p: (s * P + p, col_block), p=p))
             for p in range(P)]
    out = _compress_call(specs, [u] * P, (1, T // (PAGE_SIZE * P)), T // CMP_STRIDE, pe_t, wp_t, wphi, (),
                         "compress_prompt")
    return out[0]


def _compress_paged(pool, page_table, pe_t, wp_t, wphi, pages_per_step=8):
    B, n_pages = page_table.shape
    P = pages_per_step
    specs = [pl.BlockSpec((None, PAGE_SIZE, KV_ROW),
                          functools.partial(lambda b, s, pt, p: (pt[b * n_pages + s * P + p], 0, 0), p=p))
             for p in range(P)]
    return _compress_call(specs, [pool] * P, (B, n_pages // P), n_pages * PAGE_SIZE // CMP_STRIDE, pe_t, wp_t,
                          wphi, (page_table.reshape(-1),), "compress_paged")


def _kv_prep_kernel(ks_ref, kw_ref, ksk_ref, ksv_ref, kwk_ref, kwv_ref):
    rows = ks_ref.shape[0]
    ks = ks_ref[...]
    kw = kw_ref[...]
    r = pl.program_id(0) * rows + lax.broadcasted_iota(i32, (rows, LANES), 0)
    n = lax.broadcasted_iota(i32, (rows, LANES), 1)
    onehot = jnp.where(_div_pow2(r, SEL_BLOCK) == n, 1.0, 0.0).astype(bf16)
    for g in range(NSA_KV_HEADS):
        ksl = slice(g * NSA_DH, (g + 1) * NSA_DH)
        vsl = slice(NSA_KV_W + g * NSA_DH, NSA_KV_W + (g + 1) * NSA_DH)
        ksk_ref[g, :, 0:NSA_DH] = ks[:, ksl].astype(bf16)
        ksk_ref[g, :, NSA_DH:NSA_DH + LANES] = onehot
        ksv_ref[g] = ks[:, vsl].astype(bf16)
        kwk_ref[g] = kw[:, ksl].astype(bf16)
        kwv_ref[g] = kw[:, vsl].astype(bf16)


def _kv_prep(u, ks_col_block, kw_col_block, rows=512):
    T = u.shape[0]
    G = NSA_KV_HEADS
    assert (T - 1) // SEL_BLOCK + 1 <= LANES
    sd = lambda w: jax.ShapeDtypeStruct((G, T, w), bf16)
    ospec = lambda w: pl.BlockSpec((G, rows, w), lambda i: (0, i, 0))
    return pl.pallas_call(
        _kv_prep_kernel,
        grid=(T // rows,),
        in_specs=[pl.BlockSpec((rows, KV_ROW), lambda i: (i, ks_col_block)),
                  pl.BlockSpec((rows, KV_ROW), lambda i: (i, kw_col_block))],
        out_specs=[ospec(NSA_DH + LANES), ospec(NSA_DH), ospec(NSA_DH), ospec(NSA_DH)],
        out_shape=[sd(NSA_DH + LANES), sd(NSA_DH), sd(NSA_DH), sd(NSA_DH)],
        compiler_params=_cparams("parallel"),
        name="kv_prep",
    )(u, u)


def _top_blocks(score, n_top):
    lane = lax.broadcasted_iota(i32, score.shape, 1).astype(f32)
    width = float(score.shape[1])
    work = score
    sel = jnp.zeros(score.shape, dtype=jnp.bool_)
    firsts = []
    for _ in range(n_top):
        mx = jnp.max(work, axis=-1, keepdims=True)
        first = jnp.min(jnp.where(work == mx, lane, width), axis=-1, keepdims=True)
        hit = lane == first
        sel = jnp.logical_or(sel, hit)
        work = jnp.where(hit, REMOVED, work)
        firsts.append(first)
    return sel, firsts


def _nsa_prompt_kernel(q_ref, gt_ref, kck_ref, kcv_ref, ksk_ref, ksv_ref, kwk_ref, kwv_ref, o_ref,
                       m_sc, l_sc, acc_sc, *, n_sel):
    QB, R, D = Q_BLOCK, NSA_GROUP, NSA_DH
    rows = R * QB
    qb = pl.program_id(1)
    q = q_ref[...]
    qs = jnp.concatenate([q[:, r * D:(r + 1) * D] for r in range(R)], axis=0) * (D ** -0.5)
    qs_b = qs.astype(bf16)
    tok = lax.broadcasted_iota(i32, (rows, 1), 0) & (QB - 1)
    pos = qb * QB + tok

    ns = kck_ref.shape[0]
    s = _mm_nt(qs_b, kck_ref[...].astype(bf16))
    j = lax.broadcasted_iota(i32, (1, ns), 1)
    e, den = _masked_softmax_rows(s, j * CMP_STRIDE + (CMP_BLOCK - 1) <= pos)
    p_c = e / den
    o_c = _mm(p_c.astype(bf16), kcv_ref[...].astype(bf16))

    imp = p_c[0:QB]
    for r in range(1, R):
        imp = imp + p_c[r * QB:(r + 1) * QB]
    ratio = SEL_BLOCK // CMP_STRIDE
    off = CMP_BLOCK // CMP_STRIDE - 1
    jj = lax.broadcasted_iota(i32, (ns, LANES), 0)
    nn = lax.broadcasted_iota(i32, (ns, LANES), 1)
    overlap = jnp.where((jj >= ratio * nn - off) & (jj < ratio * nn + ratio), 1.0, 0.0).astype(bf16)
    imp_sel = _dot_sel_r(imp, overlap)
    n_idx = lax.broadcasted_iota(i32, (QB, LANES), 1)
    cur = _div_pow2(qb * QB + lax.broadcasted_iota(i32, (QB, 1), 0), SEL_BLOCK)
    forced = (n_idx == 0) | (n_idx == cur) | (n_idx == cur - 1)
    score = jnp.where(forced, FORCE, jnp.where(n_idx <= cur, imp_sel, NEG))
    score = jnp.where(n_idx < n_sel, score, REMOVED)
    sel, _ = _top_blocks(score, min(SEL_TOPN, n_sel))
    bias = jnp.where(sel & (n_idx <= cur), 0.0, MASK_BIAS).astype(bf16)
    qp = jnp.concatenate([jnp.concatenate([qs_b[r * QB:(r + 1) * QB], bias], axis=1) for r in range(R)], axis=0)

    m_sc[...] = jnp.full(m_sc.shape, NEG, f32)
    l_sc[...] = jnp.zeros_like(l_sc)
    acc_sc[...] = jnp.zeros_like(acc_sc)
    KT = SEL_KT

    def body(kt, carry):
        start = pl.multiple_of(kt * KT, KT)
        sk = _mm_nt(qp, ksk_ref[pl.ds(start, KT), :])
        key = start + lax.broadcasted_iota(i32, (1, KT), 1)
        sk = jnp.where(key <= pos, sk, NEG)
        m_old = m_sc[...]
        m_new = jnp.maximum(m_old, jnp.max(sk, axis=-1, keepdims=True))
        p = jnp.exp(sk - m_new)
        alpha = jnp.exp(m_old - m_new)
        l_sc[...] = alpha * l_sc[...] + jnp.sum(p, axis=-1, keepdims=True)
        acc_sc[...] = alpha * acc_sc[...] + _mm(p.astype(bf16), ksv_ref[pl.ds(start, KT), :])
        m_sc[...] = m_new
        return carry

    lax.fori_loop(0, (qb * QB + QB - 1) // KT + 1, body, 0)
    o_s = acc_sc[...] / l_sc[...]

    nband = WINDOW // QB + 1
    wlen = nband * QB
    wstart = pl.multiple_of(jnp.maximum(qb - (nband - 1), 0) * QB, QB)
    sw = _mm_nt(qs_b, kwk_ref[pl.ds(wstart, wlen), :])
    diff = pos - (wstart + lax.broadcasted_iota(i32, (1, wlen), 1))
    e, den = _masked_softmax_rows(sw, (diff >= 0) & (diff <= WINDOW))
    o_w = _mm((e / den).astype(bf16), kwv_ref[pl.ds(wstart, wlen), :])

    gate = _sigmoid(gt_ref[...])
    for r in range(R):
        rs = slice(r * QB, (r + 1) * QB)
        o_ref[:, r * D:(r + 1) * D] = (gate[:, r:r + 1] * o_c[rs] + gate[:, R + r:R + r + 1] * o_s[rs]
                                       + gate[:, 2 * R + r:2 * R + r + 1] * o_w[rs])


def _nsa_prompt(u, q_col_block, gates_g, kvc, ksk, ksv, kwk, kwv):
    T = u.shape[0]
    G, R, D, QB = NSA_KV_HEADS, NSA_GROUP, NSA_DH, Q_BLOCK
    ns = kvc.shape[0]
    n_sel = (T - 1) // SEL_BLOCK + 1
    assert T % SEL_KT == 0 and T >= (WINDOW // QB + 1) * QB and n_sel <= LANES
    rows = R * QB
    res = lambda w: pl.BlockSpec((None, T, w), lambda g, i: (g, 0, 0))
    return pl.pallas_call(
        functools.partial(_nsa_prompt_kernel, n_sel=n_sel),
        grid=(G, T // QB),
        in_specs=[pl.BlockSpec((QB, R * D), lambda g, i: (i, q_col_block + g)),
                  pl.BlockSpec((None, QB, LANES), lambda g, i: (g, i, 0)),
                  pl.BlockSpec((ns, D), lambda g, i: (0, g)),
                  pl.BlockSpec((ns, D), lambda g, i: (0, G + g)),
                  res(D + LANES), res(D), res(D), res(D)],
        out_specs=pl.BlockSpec((QB, R * D), lambda g, i: (i, g)),
        out_shape=jax.ShapeDtypeStruct((T, NSA_DIM), f32),
        scratch_shapes=[pltpu.VMEM((rows, 1), f32), pltpu.VMEM((rows, 1), f32), pltpu.VMEM((rows, D), f32)],
        compiler_params=_cparams("parallel", "arbitrary"),
        name="nsa_prompt",
    )(u, gates_g, kvc, kvc, ksk, ksv, kwk, kwv)


def _group_gates(gt):
    T = gt.shape[0]
    G, R = NSA_KV_HEADS, NSA_GROUP
    g = gt.reshape(T, 3, G, R).transpose(2, 0, 1, 3).reshape(G, T, 3 * R)
    return jnp.pad(g, ((0, 0), (0, 0), (0, LANES - 3 * R)))


def _nsa_sample_cmp_kernel(q_ref, kvc_ref, oc_ref, idx_ref, *, pos, n_sel, sel_w):
    H, R, D, G = NSA_HEADS, NSA_GROUP, NSA_DH, NSA_KV_HEADS
    ns = kvc_ref.shape[0]
    qs_b = (q_ref[...] * (D ** -0.5)).astype(bf16)
    head = lax.broadcasted_iota(i32, (H, 1), 0)
    grp = _div_pow2(head, R)
    s = jnp.zeros((H, ns), f32)
    for g in range(G):
        s = jnp.where(grp == g, _mm_nt(qs_b, kvc_ref[:, g * D:(g + 1) * D].astype(bf16)), s)
    j = lax.broadcasted_iota(i32, (1, ns), 1)
    e, den = _masked_softmax_rows(s, j * CMP_STRIDE + (CMP_BLOCK - 1) <= pos)
    p_c = e / den
    p_b = p_c.astype(bf16)
    o_c = jnp.zeros((H, D), f32)
    imp = jnp.zeros((H, ns), f32)
    for g in range(G):
        o_c = jnp.where(grp == g, _mm(p_b, kvc_ref[:, (G + g) * D:(G + g + 1) * D].astype(bf16)), o_c)
        imp = jnp.where(grp == g, jnp.sum(jnp.where(grp == g, p_c, 0.0), axis=0, keepdims=True), imp)
    oc_ref[...] = o_c
    ratio = SEL_BLOCK // CMP_STRIDE
    off = CMP_BLOCK // CMP_STRIDE - 1
    jj = lax.broadcasted_iota(i32, (ns, sel_w), 0)
    nn = lax.broadcasted_iota(i32, (ns, sel_w), 1)
    overlap = jnp.where((jj >= ratio * nn - off) & (jj < ratio * nn + ratio), 1.0, 0.0).astype(bf16)
    imp_sel = _dot_sel_r(imp, overlap)
    n_idx = lax.broadcasted_iota(i32, (H, sel_w), 1)
    cur = pos // SEL_BLOCK
    forced = (n_idx == 0) | (n_idx == cur) | (n_idx == cur - 1)
    score = jnp.where(forced, FORCE, jnp.where(n_idx <= cur, imp_sel, NEG))
    score = jnp.where(n_idx < n_sel, score, REMOVED)
    _, firsts = _top_blocks(score, SEL_TOPN)
    lane = lax.broadcasted_iota(i32, (H, LANES), 1)
    idx = jnp.zeros((H, LANES), f32)
    for i, first in enumerate(firsts):
        idx = jnp.where(lane == i, first, idx)
    idx_ref[...] = idx.astype(i32)


def _nsa_sample_cmp(q, kvc, pos):
    B, H, D = q.shape
    ns = kvc.shape[1]
    n_sel = pos // SEL_BLOCK + 1
    assert n_sel >= SEL_TOPN
    sel_w = -(-n_sel // LANES) * LANES
    o_c, idx = pl.pallas_call(
        functools.partial(_nsa_sample_cmp_kernel, pos=pos, n_sel=n_sel, sel_w=sel_w),
        grid=(B,),
        in_specs=[pl.BlockSpec((None, H, D), lambda b: (b, 0, 0)),
                  pl.BlockSpec((None, ns, KV_ROW), lambda b: (b, 0, 0))],
        out_specs=[pl.BlockSpec((None, H, D), lambda b: (b, 0, 0)),
                   pl.BlockSpec((None, H, LANES), lambda b: (b, 0, 0))],
        out_shape=[jax.ShapeDtypeStruct((B, H, D), f32), jax.ShapeDtypeStruct((B, H, LANES), i32)],
        compiler_params=_cparams("parallel"),
        name="nsa_sample_cmp",
    )(q, kvc)
    return o_c, idx[:, ::NSA_GROUP, :SEL_TOPN]


def _nsa_sample_sel_kernel(idx_ref, pt_ref, q_ref, new_ref, *refs, past):
    H, R, D, G = NSA_HEADS, NSA_GROUP, NSA_DH, NSA_KV_HEADS
    blocks, o_ref = refs[:G * SEL_TOPN], refs[G * SEL_TOPN]
    b = pl.program_id(0)
    qs_b = (q_ref[...] * (D ** -0.5)).astype(bf16)
    grp = _div_pow2(lax.broadcasted_iota(i32, (H, 1), 0), R)
    new = new_ref[...]
    o = jnp.zeros((H, D), f32)
    for g in range(G):
        ks, vs = [], []
        lane = lax.broadcasted_iota(i32, (1, SEL_TOPN * SEL_BLOCK), 1)
        slot = _div_pow2(lane, SEL_BLOCK)
        pk_row = lane & (SEL_BLOCK - 1)
        for i in range(SEL_TOPN):
            blk = blocks[g * SEL_TOPN + i][...]
            base = idx_ref[(b * G + g) * SEL_TOPN + i] * SEL_BLOCK
            pk_col = base + lax.broadcasted_iota(i32, (SEL_BLOCK, 1), 0)
            pk_row = pk_row + jnp.where(slot == i, base, 0)
            row = jnp.where(pk_col < past, blk, new)
            ks.append(row[:, g * D:(g + 1) * D].astype(bf16))
            vs.append(row[:, (G + g) * D:(G + g + 1) * D].astype(bf16))
        s = _mm_nt(qs_b, jnp.concatenate(ks, axis=0))
        e, den = _masked_softmax_rows(s, pk_row <= past)
        o = jnp.where(grp == g, _mm((e / den).astype(bf16), jnp.concatenate(vs, axis=0)), o)
    o_ref[...] = o


def _nsa_sample_sel(q, ks_new, pool, page_table, idx, past):
    B, H, D = q.shape
    G = NSA_KV_HEADS
    n_pages = page_table.shape[1]
    halves = PAGE_SIZE // SEL_BLOCK
    pool_h = pool.reshape(pool.shape[0] * halves, SEL_BLOCK, KV_ROW)
    last_old = past // SEL_BLOCK - 1

    def blk_map(b, idx_r, pt_r, g, i):
        blk = jnp.minimum(idx_r[(b * G + g) * SEL_TOPN + i], last_old)
        return (pt_r[b * n_pages + blk // halves] * halves + blk % halves, 0, 0)

    specs = [pl.BlockSpec((None, SEL_BLOCK, KV_ROW), functools.partial(blk_map, g=g, i=i))
             for g in range(G) for i in range(SEL_TOPN)]
    gs = pltpu.PrefetchScalarGridSpec(
        num_scalar_prefetch=2, grid=(B,),
        in_specs=[pl.BlockSpec((None, H, D), lambda b, *_: (b, 0, 0)),
                  pl.BlockSpec((None, 1, KV_ROW), lambda b, *_: (b, 0, 0))] + specs,
        out_specs=pl.BlockSpec((None, H, D), lambda b, *_: (b, 0, 0)))
    return pl.pallas_call(
        functools.partial(_nsa_sample_sel_kernel, past=past),
        grid_spec=gs,
        out_shape=jax.ShapeDtypeStruct((B, H, D), f32),
        compiler_params=_cparams("arbitrary"),
        name="nsa_sample_sel",
    )(idx.reshape(-1), page_table.reshape(-1), q, ks_new.reshape(B, 1, KV_ROW), *([pool_h] * (G * SEL_TOPN)))


def _nsa_sample_win_kernel(q_ref, buf_ref, new_ref, oc_ref, os_ref, gt_ref, o_ref, win_ref, *, past):
    H, R, D, G = NSA_HEADS, NSA_GROUP, NSA_DH, NSA_KV_HEADS
    Lb = buf_ref.shape[0]
    qs = q_ref[...] * (D ** -0.5)
    qs_b = qs.astype(bf16)
    grp = _div_pow2(lax.broadcasted_iota(i32, (H, 1), 0), R)
    new = new_ref[...]
    diff = past - (past - Lb + lax.broadcasted_iota(i32, (1, Lb), 1))
    valid = (diff >= 0) & (diff <= WINDOW)
    o_w = jnp.zeros((H, D), f32)
    for g in range(G):
        ksl = slice(g * D, (g + 1) * D)
        vsl = slice((G + g) * D, (G + g + 1) * D)
        s_b = jnp.where(valid, _mm_nt(qs_b, buf_ref[:, ksl].astype(bf16)), NEG)
        s_n = jnp.sum(qs * new[:, ksl], axis=-1, keepdims=True)
        mx = jnp.maximum(jnp.max(s_b, axis=-1, keepdims=True), s_n)
        e_b = jnp.where(valid, jnp.exp(s_b - mx), 0.0)
        e_n = jnp.exp(s_n - mx)
        den = jnp.sum(e_b, axis=-1, keepdims=True) + e_n
        og = (_mm((e_b / den).astype(bf16), buf_ref[:, vsl].astype(bf16)) + (e_n / den) * new[:, vsl])
        o_w = jnp.where(grp == g, og, o_w)
    gate = _sigmoid(gt_ref[...])
    o_ref[...] = gate[:, 0:1] * oc_ref[...] + gate[:, 1:2] * os_ref[...] + gate[:, 2:3] * o_w
    win_ref[0:Lb - 1, :] = buf_ref[1:Lb, :]
    win_ref[Lb - 1:Lb, :] = new


def _nsa_sample_win(q, win_buf, kw_new, o_c, o_s, gt, past):
    B, H, D = q.shape
    Lb = win_buf.shape[1]
    assert Lb == WINDOW
    per_b = lambda *s: pl.BlockSpec((None,) + s, lambda b: (b,) + (0,) * len(s))
    return pl.pallas_call(
        functools.partial(_nsa_sample_win_kernel, past=past),
        grid=(B,),
        in_specs=[per_b(H, D), per_b(Lb, KV_ROW), per_b(1, KV_ROW), per_b(H, D), per_b(H, D), per_b(H, 3)],
        out_specs=[per_b(H, D), per_b(Lb, KV_ROW)],
        out_shape=[jax.ShapeDtypeStruct((B, H, D), f32), jax.ShapeDtypeStruct((B, Lb, KV_ROW), f32)],
        compiler_params=_cparams("parallel"),
        name="nsa_sample_win",
    )(q, win_buf, kw_new.reshape(B, 1, KV_ROW), o_c, o_s, gt.reshape(B, 3, H).transpose(0, 2, 1))


def kernel(x_prompt, x_sample, cache_cmp_kv, cache_sel_kv, cache_win_kv, cache_mem_kv, state_mlstm_c, state_mlstm_n, state_mlstm_m, state_conv, page_table, mem_prompt, g_mix, w_in, w_conv, b_conv, w_mq, w_mk, w_mv, w_mgate, b_mgate, g_mhead, cmp_pe, cmp_wpos, cmp_wphi, w_out, g_memx, g_mems, w_mem_q, w_mem_kv, w_mem_o, g_ffn, w_ff1, w_ff2, g_final):
    depth = w_in.shape[0]
    assert depth == 1 and x_prompt.shape[0] == 1 and x_sample.shape[1] == 1
    T, Dm = x_prompt.shape[1:]
    B = x_sample.shape[0]
    G, Dh, H = NSA_KV_HEADS, NSA_DH, ML_HEADS
    past = page_table.shape[1] * PAGE_SIZE
    assert (past + 1) // CMP_STRIDE == past // CMP_STRIDE
    n_main = 2 * ML_DIM + NSA_DIM + 3 * KV_ROW
    n_gate = 3 * NSA_HEADS
    l = 0
    hp = x_prompt.reshape(T, Dm)
    hs = x_sample.reshape(B, Dm)

    wbd_q, wbd_k, wbd_v = (_blockdiag_weights(w[l]) for w in (w_mq, w_mk, w_mv))
    pe_t, wp_t = _compress_tables(cmp_pe[l], cmp_wpos[l])
    w_gt = jnp.pad(w_in[l][:, n_main:], ((0, 0), (0, LANES - n_gate)))
    q_cb = 2 * ML_DIM // KV_ROW
    c_cb = (2 * ML_DIM + NSA_DIM) // KV_ROW

    u = _norm_matmul(hp, g_mix[l], w_in[l], n_main, 512, 512, "in_proj_p")
    gt = _norm_matmul(hp, g_mix[l], w_gt, LANES, 512, LANES, "in_gate_p")[:, :n_gate]
    ml_o, c_p, n_p, m_p = _mlstm_prompt(u, w_conv[l], b_conv[l], wbd_q, wbd_k, wbd_v, w_mgate[l], b_mgate[l],
                                        g_mhead[l], 128, "mlstm_prompt")
    kvc_p = _compress_prompt(u, c_cb, pe_t, wp_t, cmp_wphi[l])
    ksk, ksv, kwk, kwv = _kv_prep(u, c_cb + 1, c_cb + 2)
    nsa_o = _nsa_prompt(u, q_cb, _group_gates(gt), kvc_p, ksk, ksv, kwk, kwv)
    h1 = _out_proj(ml_o, nsa_o, w_out[l], hp, 512, 512, "out_proj_p")
    mem_kv = _norm_matmul(mem_prompt.reshape(-1, Dm), g_mems[l], w_mem_kv[l], w_mem_kv.shape[-1], 256, 512, "mem_kv")
    h2 = _mem_prompt(h1, g_memx[l], w_mem_q[l], mem_kv, w_mem_o[l], 512, "mem_attn_p")
    y_p = _ffn_final(h2, g_ffn[l], w_ff1[l], w_ff2[l], g_final, 512, 512, "ffn_p")

    kv_shape = lambda a: a.reshape(1, 1, a.shape[0], 2, G, Dh)
    kvw_p = u[:, (c_cb + 2) * KV_ROW:(c_cb + 3) * KV_ROW]
    out_p = (y_p.reshape(1, T, Dm),
             kv_shape(u[:, c_cb * KV_ROW:(c_cb + 1) * KV_ROW]),
             kv_shape(u[:, (c_cb + 1) * KV_ROW:(c_cb + 2) * KV_ROW]),
             kv_shape(kvw_p[T - min(WINDOW, T):]),
             mem_kv.reshape(1, 1, -1, 2, MEM_HEADS, MEM_DH),
             c_p[None, None], n_p[None, None], m_p[None, None],
             u[T - (ML_CONV - 1):, :ML_DIM][None, None])

    us = _norm_matmul(hs, g_mix[l], w_in[l], n_main, B, 512, "in_proj_s")
    gts = _norm_matmul(hs, g_mix[l], w_gt, LANES, B, LANES, "in_gate_s")[:, :n_gate]
    xm_s, og_s = us[:, :ML_DIM], us[:, ML_DIM:2 * ML_DIM]
    q_s = us[:, 2 * ML_DIM:2 * ML_DIM + NSA_DIM].reshape(B, NSA_HEADS, Dh)
    kc_s, ks_s, kw_s = (us[:, (c_cb + i) * KV_ROW:(c_cb + i + 1) * KV_ROW] for i in range(3))
    ml_os, c_s, n_s, m_s = _mlstm_sample(xm_s, og_s, state_conv[l], state_mlstm_c[l], state_mlstm_n[l],
                                         state_mlstm_m[l], w_conv[l], b_conv[l], wbd_q, wbd_k, wbd_v,
                                         w_mgate[l], b_mgate[l], g_mhead[l])
    pool_c = cache_cmp_kv[l].reshape(-1, PAGE_SIZE, KV_ROW)
    pool_s = cache_sel_kv[l].reshape(-1, PAGE_SIZE, KV_ROW)
    win_buf = cache_win_kv[l].reshape(B, -1, KV_ROW)
    kvc_s = _compress_paged(pool_c, page_table, pe_t, wp_t, cmp_wphi[l])
    o_cs, idx = _nsa_sample_cmp(q_s, kvc_s, past)
    o_ss = _nsa_sample_sel(q_s, ks_s, pool_s, page_table, idx, past)
    nsa_os, win_new = _nsa_sample_win(q_s, win_buf, kw_s, o_cs, o_ss, gts, past)
    h1s = _out_proj(ml_os, nsa_os.reshape(B, NSA_DIM), w_out[l], hs, B, 512, "out_proj_s")
    qm_s = _norm_matmul(h1s, g_memx[l], w_mem_q[l], w_mem_q.shape[-1], B, 512, "mem_q_s")
    om_s = _mem_sample(qm_s, cache_mem_kv[l].reshape(B, cache_mem_kv.shape[2], -1), "mem_attn_s")
    h2s = _matmul_res(om_s, w_mem_o[l], h1s, B, 512, "mem_o_s")
    y_s = _ffn_final(h2s, g_ffn[l], w_ff1[l], w_ff2[l], g_final, B, 512, "ffn_s")

    kv_s_shape = lambda a: a.reshape(1, B, 1, 2, G, Dh)
    conv_s = jnp.concatenate([state_conv[l][:, 1:], xm_s[:, None, :]], axis=1)
    out_s = (y_s.reshape(B, 1, Dm), kv_s_shape(kc_s), kv_s_shape(ks_s),
             win_new.reshape(1, B, -1, 2, G, Dh), c_s[None], n_s[None], m_s[None], conv_s[None])

    return (out_p[0], out_s[0]) + out_p[1:] + out_s[1:]
```

```python
import functools

import jax
import jax.numpy as jnp
import numpy as np
from jax import lax
from jax.experimental import pallas as pl
from jax.experimental.pallas import tpu as pltpu

f32 = jnp.float32
bf16 = jnp.bfloat16
i32 = jnp.int32

EPS = 1e-6
NEG = -1e30
FORCE = 1e30
ML_HEADS = 8
ML_DH = 128
ML_DIM = ML_HEADS * ML_DH
ML_CONV = 4
ML_QKV_BLOCK = 4
NSA_HEADS = 8
NSA_KV_HEADS = 2
NSA_GROUP = NSA_HEADS // NSA_KV_HEADS
NSA_DH = 128
NSA_DIM = NSA_HEADS * NSA_DH
NSA_KV_W = NSA_KV_HEADS * NSA_DH
CMP_BLOCK = 32
CMP_STRIDE = 16
SEL_BLOCK = 64
SEL_TOPN = 16
WINDOW = 512
Q_BLOCK = 128
PAGE_SIZE = 128
MEM_HEADS = 4
MEM_DH = 128
KV_ROW = 2 * NSA_KV_W
KV_CHUNKS = KV_ROW // NSA_DH

LANES = 128
SUBLANES = 8
VMEM_LIMIT = 56 * 1024 * 1024

MASK_BIAS = -1e9
REMOVED = -3.0e38
SEL_KT = 512


def _cparams(*sem):
    return pltpu.CompilerParams(dimension_semantics=sem, vmem_limit_bytes=VMEM_LIMIT)


def _mm(a, b):
    return jnp.dot(a, b, preferred_element_type=f32)


def _mm_nt(a, b):
    return lax.dot_general(a, b, (((1,), (1,)), ((), ())), preferred_element_type=f32)


def _split2(x):
    h = x.astype(bf16)
    return h, (x - h.astype(f32)).astype(bf16)


def _split3(x):
    h = x.astype(bf16)
    r = x - h.astype(f32)
    m = r.astype(bf16)
    return h, m, (r - m.astype(f32)).astype(bf16)


def _dot3(a, b, mm=_mm):
    ah, al = _split2(a)
    bh, bl = _split2(b)
    return mm(ah, bh) + mm(al, bh) + mm(ah, bl)


def _dot_sel_l(sel, x):
    h, m, l = _split3(x)
    return _mm(sel, h) + _mm(sel, m) + _mm(sel, l)


def _dot_sel_r(x, sel):
    h, m, l = _split3(x)
    return _mm(h, sel) + _mm(m, sel) + _mm(l, sel)


def _rms(x, g):
    return x * lax.rsqrt(jnp.mean(x * x, axis=-1, keepdims=True) + EPS) * g


def _div_pow2(x, d):
    assert d & (d - 1) == 0
    return lax.shift_right_logical(x, jnp.full(x.shape, d.bit_length() - 1, x.dtype))


def _sigmoid(x):
    return 1.0 / (1.0 + jnp.exp(-x))


def _log_sigmoid(x):
    return jnp.minimum(x, 0.0) - jnp.log(1.0 + jnp.exp(-jnp.abs(x)))


def _masked_softmax_rows(s, valid):
    s = jnp.where(valid, s, NEG)
    mx = jnp.max(s, axis=-1, keepdims=True)
    e = jnp.where(valid, jnp.exp(s - mx), 0.0)
    den = jnp.sum(e, axis=-1, keepdims=True)
    return e, jnp.where(den > 0.0, den, 1.0)


def _norm_matmul_kernel(x_ref, g_ref, w_ref, o_ref, xn_ref):
    @pl.when(pl.program_id(1) == 0)
    def _():
        xn_ref[...] = _rms(x_ref[...], g_ref[...]).astype(bf16)

    o_ref[...] = _mm(xn_ref[...], w_ref[...])


def _norm_matmul(x, g, w, n_cols, tm, tn, name):
    M, K = x.shape
    return pl.pallas_call(
        _norm_matmul_kernel,
        grid=(M // tm, n_cols // tn),
        in_specs=[pl.BlockSpec((tm, K), lambda i, j: (i, 0)),
                  pl.BlockSpec((1, K), lambda i, j: (0, 0)),
                  pl.BlockSpec((K, tn), lambda i, j: (0, j))],
        out_specs=pl.BlockSpec((tm, tn), lambda i, j: (i, j)),
        out_shape=jax.ShapeDtypeStruct((M, n_cols), f32),
        scratch_shapes=[pltpu.VMEM((tm, K), bf16)],
        compiler_params=_cparams("parallel", "arbitrary"),
        name=name,
    )(x, g.reshape(1, K), w)


def _out_proj_kernel(a1_ref, a2_ref, w1_ref, w2_ref, r_ref, o_ref):
    o_ref[...] = (r_ref[...] + _mm(a1_ref[...].astype(bf16), w1_ref[...])
                  + _mm(a2_ref[...].astype(bf16), w2_ref[...]))


def _out_proj(a1, a2, w, res, tm, tn, name):
    M, K1 = a1.shape
    K2 = a2.shape[1]
    assert K1 == K2 and w.shape[0] == K1 + K2
    N = w.shape[1]
    return pl.pallas_call(
        _out_proj_kernel,
        grid=(M // tm, N // tn),
        in_specs=[pl.BlockSpec((tm, K1), lambda i, j: (i, 0)),
                  pl.BlockSpec((tm, K2), lambda i, j: (i, 0)),
                  pl.BlockSpec((K1, tn), lambda i, j: (0, j)),
                  pl.BlockSpec((K2, tn), lambda i, j: (1, j)),
                  pl.BlockSpec((tm, tn), lambda i, j: (i, j))],
        out_specs=pl.BlockSpec((tm, tn), lambda i, j: (i, j)),
        out_shape=jax.ShapeDtypeStruct((M, N), f32),
        compiler_params=_cparams("parallel", "arbitrary"),
        name=name,
    )(a1, a2, w, w, res)


def _matmul_res_kernel(a_ref, w_ref, r_ref, o_ref):
    o_ref[...] = r_ref[...] + _mm(a_ref[...].astype(bf16), w_ref[...])


def _matmul_res(a, w, res, tm, tn, name):
    M, K = a.shape
    N = w.shape[1]
    return pl.pallas_call(
        _matmul_res_kernel,
        grid=(M // tm, N // tn),
        in_specs=[pl.BlockSpec((tm, K), lambda i, j: (i, 0)),
                  pl.BlockSpec((K, tn), lambda i, j: (0, j)),
                  pl.BlockSpec((tm, tn), lambda i, j: (i, j))],
        out_specs=pl.BlockSpec((tm, tn), lambda i, j: (i, j)),
        out_shape=jax.ShapeDtypeStruct((M, N), f32),
        compiler_params=_cparams("parallel", "arbitrary"),
        name=name,
    )(a, w, res)


def _ffn_kernel(h_ref, g_ref, w1_ref, w2_ref, gf_ref, y_ref, xn_ref, acc_ref):
    f = pl.program_id(1)

    @pl.when(f == 0)
    def _():
        xn_ref[...] = _rms(h_ref[...], g_ref[...]).astype(bf16)
        acc_ref[...] = jnp.zeros_like(acc_ref)

    a = _mm(xn_ref[...], w1_ref[...])
    a = jnp.square(jnp.maximum(a, 0.0))
    acc_ref[...] += _mm(a.astype(bf16), w2_ref[...])

    @pl.when(f == pl.num_programs(1) - 1)
    def _():
        y_ref[...] = _rms(h_ref[...] + acc_ref[...], gf_ref[...])


def _ffn_final(h, g, w1, w2, g_final, tm, tf, name):
    M, D = h.shape
    F = w1.shape[1]
    return pl.pallas_call(
        _ffn_kernel,
        grid=(M // tm, F // tf),
        in_specs=[pl.BlockSpec((tm, D), lambda i, j: (i, 0)),
                  pl.BlockSpec((1, D), lambda i, j: (0, 0)),
                  pl.BlockSpec((D, tf), lambda i, j: (0, j)),
                  pl.BlockSpec((tf, D), lambda i, j: (j, 0)),
                  pl.BlockSpec((1, D), lambda i, j: (0, 0))],
        out_specs=pl.BlockSpec((tm, D), lambda i, j: (i, 0)),
        out_shape=jax.ShapeDtypeStruct((M, D), f32),
        scratch_shapes=[pltpu.VMEM((tm, D), bf16), pltpu.VMEM((tm, D), f32)],
        compiler_params=_cparams("parallel", "arbitrary"),
        name=name,
    )(h, g.reshape(1, D), w1, w2, g_final.reshape(1, D))


def _mem_prompt_kernel(h_ref, g_ref, wq_ref, k_ref, v_ref, wo_ref, o_ref):
    h = h_ref[...]
    xn = _rms(h, g_ref[...]).astype(bf16)
    q = _mm(xn, wq_ref[...]) * (MEM_DH ** -0.5)
    outs = []
    for hd in range(MEM_HEADS):
        sl = slice(hd * MEM_DH, (hd + 1) * MEM_DH)
        s = _mm_nt(q[:, sl].astype(bf16), k_ref[:, sl].astype(bf16))
        e = jnp.exp(s - jnp.max(s, axis=-1, keepdims=True))
        p = e / jnp.sum(e, axis=-1, keepdims=True)
        outs.append(_mm(p.astype(bf16), v_ref[:, sl].astype(bf16)))
    o = jnp.concatenate(outs, axis=-1)
    o_ref[...] = h + _mm(o.astype(bf16), wo_ref[...])


def _mem_prompt(h, g, wq, mem_kv, wo, tm, name):
    M, D = h.shape
    HD = MEM_HEADS * MEM_DH
    ML = mem_kv.shape[0]
    return pl.pallas_call(
        _mem_prompt_kernel,
        grid=(M // tm,),
        in_specs=[pl.BlockSpec((tm, D), lambda i: (i, 0)),
                  pl.BlockSpec((1, D), lambda i: (0, 0)),
                  pl.BlockSpec((D, HD), lambda i: (0, 0)),
                  pl.BlockSpec((ML, HD), lambda i: (0, 0)),
                  pl.BlockSpec((ML, HD), lambda i: (0, 1)),
                  pl.BlockSpec((HD, D), lambda i: (0, 0))],
        out_specs=pl.BlockSpec((tm, D), lambda i: (i, 0)),
        out_shape=jax.ShapeDtypeStruct((M, D), f32),
        compiler_params=_cparams("parallel"),
        name=name,
    )(h, g.reshape(1, D), wq, mem_kv, mem_kv, wo)


def _mem_sample_kernel(q_ref, kv_ref, o_ref):
    per_tok = 2 * MEM_HEADS
    ML = kv_ref.shape[0] // per_tok
    q = q_ref[...] * (MEM_DH ** -0.5)
    outs = []
    for hd in range(MEM_HEADS):
        sl = slice(hd * MEM_DH, (hd + 1) * MEM_DH)
        k = kv_ref[pl.ds(hd, ML, stride=per_tok), :]
        v = kv_ref[pl.ds(MEM_HEADS + hd, ML, stride=per_tok), :]
        s = jnp.sum(k * q[:, sl], axis=-1, keepdims=True)
        e = jnp.exp(s - jnp.max(s, axis=0, keepdims=True))
        p = e / jnp.sum(e, axis=0, keepdims=True)
        outs.append(jnp.sum(p * v, axis=0, keepdims=True))
    o_ref[...] = jnp.concatenate(outs, axis=-1)


def _mem_sample(q, kv, name):
    B, HD = q.shape
    rows = kv.shape[1]
    out = pl.pallas_call(
        _mem_sample_kernel,
        grid=(B,),
        in_specs=[pl.BlockSpec((None, 1, HD), lambda b: (b, 0, 0)),
                  pl.BlockSpec((None, rows, MEM_DH), lambda b: (b, 0, 0))],
        out_specs=pl.BlockSpec((None, 1, HD), lambda b: (b, 0, 0)),
        out_shape=jax.ShapeDtypeStruct((B, 1, HD), f32),
        compiler_params=_cparams("parallel"),
        name=name,
    )(q.reshape(B, 1, HD), kv)
    return out.reshape(B, HD)


def _blockdiag_weights(w):
    per = LANES // ML_QKV_BLOCK
    nchunk = w.shape[0] // per
    wc = w.reshape(nchunk, per, ML_QKV_BLOCK, ML_QKV_BLOCK)
    eye = jnp.eye(per, dtype=w.dtype)
    full = wc[:, :, :, None, :] * eye[None, :, None, :, None]
    return full.reshape(nchunk, LANES, LANES)


def _ml_qkv_gates(xc, xm, wq_ref, wk_ref, wv_ref, wg_ref, bg_ref):
    qs, ks, vs = [], [], []
    for c in range(ML_DIM // LANES):
        sl = slice(c * LANES, (c + 1) * LANES)
        qs.append(_dot3(xc[:, sl], wq_ref[c]))
        ks.append(_dot3(xc[:, sl], wk_ref[c]) * (ML_DH ** -0.5))
        vs.append(_dot3(xm[:, sl], wv_ref[c]))
    q = jnp.concatenate(qs, axis=-1)
    k = jnp.concatenate(ks, axis=-1)
    v = jnp.concatenate(vs, axis=-1)
    gates = (_dot3(q, wg_ref[0:ML_DIM, :]) + _dot3(k, wg_ref[ML_DIM:2 * ML_DIM, :])
             + _dot3(v, wg_ref[2 * ML_DIM:3 * ML_DIM, :]) + bg_ref[...])
    return q, k, v, gates


def _mlstm_prompt_kernel(xm_ref, og_ref, wconv_ref, bconv_ref, wq_ref, wk_ref, wv_ref, wg_ref, wgt_ref,
                         bg_ref, bgt_ref, gh_ref, o_ref, c_ref, n_ref, m_ref, prev_sc, *, L):
    H, D = ML_HEADS, ML_DH

    @pl.when(pl.program_id(0) == 0)
    def _():
        prev_sc[...] = jnp.zeros_like(prev_sc)
        c_ref[...] = jnp.zeros_like(c_ref)
        n_ref[...] = jnp.zeros_like(n_ref)
        m_ref[...] = jnp.full(m_ref.shape, NEG, f32)

    x = xm_ref[...]
    full = jnp.concatenate([prev_sc[...], x], axis=0)
    y = bconv_ref[...]
    for j in range(ML_CONV):
        off = SUBLANES - (ML_CONV - 1) + j
        y = y + full[off:off + L] * wconv_ref[j:j + 1, :]
    prev_sc[...] = x[L - SUBLANES:L]
    xc = y * _sigmoid(y)

    q, k, v, gates = _ml_qkv_gates(xc, x, wq_ref, wk_ref, wv_ref, wg_ref, bg_ref)
    gates_t = (_dot3(wgt_ref[:, 0:ML_DIM], q, _mm_nt) + _dot3(wgt_ref[:, ML_DIM:2 * ML_DIM], k, _mm_nt)
               + _dot3(wgt_ref[:, 2 * ML_DIM:3 * ML_DIM], v, _mm_nt) + bgt_ref[...])
    ig_c = gates[:, 0:H]
    lf_c = _log_sigmoid(gates[:, H:2 * H])
    ig_r = gates_t[0:H, :]
    lf_r = _log_sigmoid(gates_t[H:2 * H, :])

    t_i = lax.broadcasted_iota(i32, (L, L), 0)
    s_i = lax.broadcasted_iota(i32, (L, L), 1)
    causal = s_i <= t_i
    tri = jnp.where(causal, 1.0, 0.0).astype(bf16)
    b_c = _dot_sel_l(tri, lf_c)
    tri_u = jnp.where(t_i <= s_i, 1.0, 0.0).astype(bf16)
    b_r = _dot_sel_r(lf_r, tri_u)

    for h in range(H):
        sl = slice(h * D, (h + 1) * D)
        qh, kh, vh = q[:, sl], k[:, sl], v[:, sl]
        bc = b_c[:, h:h + 1]
        m_prev = m_ref[h:h + 1, 0:1]
        d_in = jnp.where(causal, bc - b_r[h:h + 1, :] + ig_r[h:h + 1, :], NEG)
        d_x = bc + m_prev
        m_t = jnp.maximum(d_x, jnp.max(d_in, axis=-1, keepdims=True))
        w_in = jnp.exp(d_in - m_t)
        w_x = jnp.exp(d_x - m_t)
        qb = qh.astype(bf16)
        kb = kh.astype(bf16)
        vb = vh.astype(bf16)
        s = _mm_nt(qb, kb) * w_in
        c_old = c_ref[h]
        n_old = n_ref[h:h + 1, :]
        num = _mm(s.astype(bf16), vb) + w_x * _mm(qb, c_old.astype(bf16))
        den = jnp.sum(s, axis=-1, keepdims=True) + w_x * jnp.sum(qh * n_old, axis=-1, keepdims=True)
        hh = num / jnp.maximum(jnp.abs(den), jnp.exp(-m_t))
        m_new = m_t[L - 1:L, :]
        b_last = bc[L - 1:L, :]
        g_x = jnp.exp(b_last + m_prev - m_new)
        g_s = jnp.exp(b_last - bc + ig_c[:, h:h + 1] - m_new)
        ks_ = kh * g_s
        c_ref[h] = g_x * c_old + _mm(ks_.T.astype(bf16), vb)
        n_ref[h:h + 1, :] = g_x * n_old + jnp.sum(ks_, axis=0, keepdims=True)
        m_ref[h:h + 1, :] = jnp.broadcast_to(m_new, (1, LANES))
        hn = hh * lax.rsqrt(jnp.mean(hh * hh, axis=-1, keepdims=True) + EPS) * gh_ref[:, sl]
        o_ref[:, sl] = _sigmoid(og_ref[:, sl]) * hn


def _mlstm_prompt(u, w_conv, b_conv, wbd_q, wbd_k, wbd_v, w_gate, b_gate, g_head, L, name):
    T = u.shape[0]
    H, D = ML_HEADS, ML_DH
    nch = ML_DIM // LANES
    full2 = lambda shape: pl.BlockSpec(shape, lambda i: (0,) * len(shape))
    out, c, n, m = pl.pallas_call(
        functools.partial(_mlstm_prompt_kernel, L=L),
        grid=(T // L,),
        in_specs=[pl.BlockSpec((L, ML_DIM), lambda i: (i, 0)),
                  pl.BlockSpec((L, ML_DIM), lambda i: (i, 1)),
                  full2((ML_CONV, ML_DIM)), full2((1, ML_DIM)),
                  full2((nch, LANES, LANES)), full2((nch, LANES, LANES)), full2((nch, LANES, LANES)),
                  full2((3 * ML_DIM, 2 * H)), full2((2 * H, 3 * ML_DIM)),
                  full2((1, 2 * H)), full2((2 * H, 1)), full2((1, ML_DIM))],
        out_specs=[pl.BlockSpec((L, ML_DIM), lambda i: (i, 0)),
                   full2((H, D, D)), full2((H, D)), full2((H, LANES))],
        out_shape=[jax.ShapeDtypeStruct((T, ML_DIM), f32),
                   jax.ShapeDtypeStruct((H, D, D), f32),
                   jax.ShapeDtypeStruct((H, D), f32),
                   jax.ShapeDtypeStruct((H, LANES), f32)],
        scratch_shapes=[pltpu.VMEM((SUBLANES, ML_DIM), f32)],
        compiler_params=_cparams("arbitrary"),
        name=name,
    )(u, u, w_conv, b_conv.reshape(1, ML_DIM), wbd_q, wbd_k, wbd_v, w_gate, w_gate.T,
      b_gate.reshape(1, 2 * H), b_gate.reshape(2 * H, 1), g_head.reshape(1, ML_DIM))
    return out, c, n, m[:, 0]


def _mlstm_sample_pre_kernel(xm_ref, s0_ref, s1_ref, s2_ref, wconv_ref, bconv_ref, wq_ref, wk_ref, wv_ref,
                             wg_ref, bg_ref, q_ref, k_ref, v_ref, g_ref):
    x = xm_ref[...]
    y = (bconv_ref[...] + s0_ref[...] * wconv_ref[0:1, :] + s1_ref[...] * wconv_ref[1:2, :]
         + s2_ref[...] * wconv_ref[2:3, :] + x * wconv_ref[3:4, :])
    xc = y * _sigmoid(y)
    q, k, v, gates = _ml_qkv_gates(xc, x, wq_ref, wk_ref, wv_ref, wg_ref, bg_ref)
    q_ref[...] = q
    k_ref[...] = k
    v_ref[...] = v
    g_ref[...] = gates


def _mlstm_sample_step_kernel(qc_ref, kc_ref, q_ref, k_ref, v_ref, gt_ref, og_ref, gh_ref, c_ref, n_ref, m_ref,
                              o_ref, cn_ref, nn_ref, mn_ref):
    H, D = ML_HEADS, ML_DH
    gates = gt_ref[...]
    ig = gates[:, 0:H]
    lf = _log_sigmoid(gates[:, H:2 * H])
    m_old = m_ref[...]
    m_new = jnp.maximum(lf + m_old, ig)
    w_in = jnp.exp(ig - m_new)
    w_x = jnp.exp(lf + m_old - m_new)
    mn_ref[...] = m_new
    e_m = jnp.exp(-m_new)
    og = og_ref[...]
    outs = []
    for h in range(H):
        sl = slice(h * D, (h + 1) * D)
        qr, kr, vr = q_ref[h:h + 1, :], k_ref[h:h + 1, :], v_ref[h:h + 1, :]
        qcol, kcol = qc_ref[:, h:h + 1], kc_ref[:, h:h + 1]
        wi, wx = w_in[:, h:h + 1], w_x[:, h:h + 1]
        c_old = c_ref[h]
        n_old = n_ref[h:h + 1, :]
        s = jnp.sum(qr * kr, axis=-1, keepdims=True) * wi
        num = s * vr + wx * jnp.sum(c_old * qcol, axis=0, keepdims=True)
        den = s + wx * jnp.sum(n_old * qr, axis=-1, keepdims=True)
        hh = num / jnp.maximum(jnp.abs(den), e_m[:, h:h + 1])
        cn_ref[h] = wx * c_old + wi * (kcol * vr)
        nn_ref[h:h + 1, :] = wx * n_old + wi * kr
        hn = hh * lax.rsqrt(jnp.mean(hh * hh, axis=-1, keepdims=True) + EPS) * gh_ref[:, sl]
        outs.append(_sigmoid(og[:, sl]) * hn)
    o_ref[...] = jnp.concatenate(outs, axis=-1)


def _mlstm_sample(xm, og, conv_state, c0, n0, m0, w_conv, b_conv, wbd_q, wbd_k, wbd_v, w_gate, b_gate, g_head):
    B = xm.shape[0]
    H, D = ML_HEADS, ML_DH
    sds = lambda *s: jax.ShapeDtypeStruct(s, f32)
    q, k, v, gates = pl.pallas_call(
        _mlstm_sample_pre_kernel,
        out_shape=[sds(B, ML_DIM), sds(B, ML_DIM), sds(B, ML_DIM), sds(B, 2 * H)],
        compiler_params=pltpu.CompilerParams(vmem_limit_bytes=VMEM_LIMIT),
        name="mlstm_sample_pre",
    )(xm, conv_state[:, 0], conv_state[:, 1], conv_state[:, 2], w_conv, b_conv.reshape(1, ML_DIM),
      wbd_q, wbd_k, wbd_v, w_gate, b_gate.reshape(1, 2 * H))
    q3, k3, v3 = (a.reshape(B, H, D) for a in (q, k, v))
    per_b = lambda *s: pl.BlockSpec((None,) + s, lambda b: (b,) + (0,) * len(s))
    out, c, n, m = pl.pallas_call(
        _mlstm_sample_step_kernel,
        grid=(B,),
        in_specs=[per_b(D, H), per_b(D, H), per_b(H, D), per_b(H, D), per_b(H, D), per_b(1, 2 * H),
                  per_b(1, ML_DIM), pl.BlockSpec((1, ML_DIM), lambda b: (0, 0)),
                  per_b(H, D, D), per_b(H, D), per_b(1, H)],
        out_specs=[per_b(1, ML_DIM), per_b(H, D, D), per_b(H, D), per_b(1, H)],
        out_shape=[sds(B, 1, ML_DIM), sds(B, H, D, D), sds(B, H, D), sds(B, 1, H)],
        compiler_params=_cparams("parallel"),
        name="mlstm_sample_step",
    )(q3.transpose(0, 2, 1), k3.transpose(0, 2, 1), q3, k3, v3, gates.reshape(B, 1, 2 * H),
      og.reshape(B, 1, ML_DIM), g_head.reshape(1, ML_DIM), c0, n0, m0.reshape(B, 1, H))
    return out.reshape(B, ML_DIM), c, n, m.reshape(B, H)


def _compress_kernel(x_ref, pe_ref, wp_ref, wphi_ref, o_ref, f0_sc, f1_sc, mn_sc):
    step = pl.program_id(0)
    sub = x_ref.shape[0] // CMP_STRIDE
    x3 = x_ref[...].reshape(sub, CMP_STRIDE, KV_ROW)
    base = pl.multiple_of(step * sub, sub)
    for o, sc in ((0, f0_sc), (1, f1_sc)):
        y = x3 + pe_ref[o][None]
        sc[pl.ds(base, sub), :] = jnp.sum(y * _sigmoid(y) * wp_ref[o][None], axis=1)
    mn_sc[pl.ds(base, sub), :] = jnp.sum(x3, axis=1) * (1.0 / CMP_STRIDE)

    @pl.when(step == pl.num_programs(0) - 1)
    def _():
        ns = f0_sc.shape[0]
        feat = f0_sc[...] + pltpu.roll(f1_sc[...], ns - 1, axis=0)
        mn = mn_sc[...]
        pooled = (mn + pltpu.roll(mn, ns - 1, axis=0)) * (CMP_STRIDE / CMP_BLOCK)
        for c in range(2):
            for g in range(NSA_KV_HEADS):
                sl = slice((c * NSA_KV_HEADS + g) * NSA_DH, (c * NSA_KV_HEADS + g + 1) * NSA_DH)
                o_ref[:, sl] = pooled[:, sl] + _dot3(feat[:, sl], wphi_ref[c])


def _compress_tables(pe, wpos):
    def lay(a):
        r = CMP_BLOCK // CMP_STRIDE
        a = a.reshape(2, r, CMP_STRIDE, NSA_DH).transpose(1, 2, 0, 3)
        a = jnp.broadcast_to(a[:, :, :, None, :], (r, CMP_STRIDE, 2, NSA_KV_HEADS, NSA_DH))
        return a.reshape(r, CMP_STRIDE, KV_ROW)
    return lay(pe), lay(wpos)


def _compress_prompt(u, col_block, pe_t, wp_t, wphi, rows=512):
    T = u.shape[0]
    n_sub = T // CMP_STRIDE
    const = lambda shape: pl.BlockSpec(shape, lambda s: (0,) * len(shape))
    return pl.pallas_call(
        _compress_kernel,
        grid=(T // rows,),
        in_specs=[pl.BlockSpec((rows, KV_ROW), lambda s: (s, col_block)),
                  const(pe_t.shape), const(wp_t.shape), const(wphi.shape)],
        out_specs=const((n_sub, KV_ROW)),
        out_shape=jax.ShapeDtypeStruct((n_sub, KV_ROW), f32),
        scratch_shapes=[pltpu.VMEM((n_sub, KV_ROW), f32)] * 3,
        compiler_params=_cparams("arbitrary"),
        name="compress_prompt",
    )(u, pe_t, wp_t, wphi)


def _compress_paged_kernel(pt_ref, *refs, n_pages):
    pages = refs[:n_pages]
    pe_ref, wp_ref, wphi_ref, o_ref, f0_sc, f1_sc, mn_sc = refs[n_pages:]
    step = pl.program_id(1)
    sub = PAGE_SIZE // CMP_STRIDE
    tiles = CMP_STRIDE * KV_CHUNKS // SUBLANES
    out_rows = sub * SUBLANES
    for p in range(n_pages):
        xh = pages[p][...].reshape(sub, tiles, SUBLANES, LANES) * 0.5
        base = pl.multiple_of((step * n_pages + p) * out_rows, out_rows)
        for o, sc in ((0, f0_sc), (1, f1_sc)):
            y = xh + pe_ref[o][None]
            b = y * wp_ref[o][None]
            sc[pl.ds(base, out_rows), :] = jnp.sum(b + b * jnp.tanh(y), axis=1).reshape(out_rows, LANES)
        mn_sc[pl.ds(base, out_rows), :] = (jnp.sum(xh, axis=1) * (2.0 / CMP_STRIDE)).reshape(out_rows, LANES)

    @pl.when(step == pl.num_programs(1) - 1)
    def _():
        ns = f0_sc.shape[0] // SUBLANES
        for c in range(KV_CHUNKS):
            col = lambda sc: (sc[pl.ds(c, ns, stride=SUBLANES), :]
                              + sc[pl.ds(KV_CHUNKS + c, ns, stride=SUBLANES), :])
            feat = col(f0_sc) + pltpu.roll(col(f1_sc), ns - 1, axis=0)
            mn = col(mn_sc)
            pooled = (mn + pltpu.roll(mn, ns - 1, axis=0)) * (CMP_STRIDE / CMP_BLOCK)
            o_ref[:, c * LANES:(c + 1) * LANES] = pooled + _dot3(feat, wphi_ref[c // NSA_KV_HEADS])


def _compress_paged(pool, page_table, pe_t, wp_t, wphi, pages_per_step=8):
    B, n_pages = page_table.shape
    P = pages_per_step
    n_sub = n_pages * PAGE_SIZE // CMP_STRIDE
    r = CMP_BLOCK // CMP_STRIDE
    tiles = CMP_STRIDE * KV_CHUNKS // SUBLANES
    pe4 = (0.5 * pe_t).reshape(r, tiles, SUBLANES, LANES)
    wp4 = wp_t.reshape(r, tiles, SUBLANES, LANES)
    page_rows = PAGE_SIZE * KV_CHUNKS
    specs = [pl.BlockSpec((None, page_rows, LANES),
                          functools.partial(lambda b, s, pt, p: (pt[b * n_pages + s * P + p], 0, 0), p=p))
             for p in range(P)]
    const = lambda shape: pl.BlockSpec(shape, lambda *a: (0,) * len(shape))
    gs = pltpu.PrefetchScalarGridSpec(
        num_scalar_prefetch=1, grid=(B, n_pages // P),
        in_specs=specs + [const(pe4.shape), const(wp4.shape), const(wphi.shape)],
        out_specs=pl.BlockSpec((None, n_sub, KV_ROW), lambda b, s, pt: (b, 0, 0)),
        scratch_shapes=[pltpu.VMEM((n_sub * SUBLANES, LANES), f32)] * 3)
    return pl.pallas_call(
        functools.partial(_compress_paged_kernel, n_pages=P),
        grid_spec=gs,
        out_shape=jax.ShapeDtypeStruct((B, n_sub, KV_ROW), f32),
        compiler_params=_cparams("parallel", "arbitrary"),
        name="compress_paged",
    )(page_table.reshape(-1), *([pool] * P), pe4, wp4, wphi)


def _kv_prep_kernel(ks_ref, kw_ref, ksk_ref, ksv_ref, kwk_ref, kwv_ref):
    rows = ks_ref.shape[0]
    ks = ks_ref[...]
    kw = kw_ref[...]
    r = pl.program_id(0) * rows + lax.broadcasted_iota(i32, (rows, LANES), 0)
    n = lax.broadcasted_iota(i32, (rows, LANES), 1)
    onehot = jnp.where(_div_pow2(r, SEL_BLOCK) == n, 1.0, 0.0).astype(bf16)
    for g in range(NSA_KV_HEADS):
        ksl = slice(g * NSA_DH, (g + 1) * NSA_DH)
        vsl = slice(NSA_KV_W + g * NSA_DH, NSA_KV_W + (g + 1) * NSA_DH)
        ksk_ref[g, :, 0:NSA_DH] = ks[:, ksl].astype(bf16)
        ksk_ref[g, :, NSA_DH:NSA_DH + LANES] = onehot
        ksv_ref[g] = ks[:, vsl].astype(bf16)
        kwk_ref[g] = kw[:, ksl].astype(bf16)
        kwv_ref[g] = kw[:, vsl].astype(bf16)


def _kv_prep(u, ks_col_block, kw_col_block, rows=512):
    T = u.shape[0]
    G = NSA_KV_HEADS
    assert (T - 1) // SEL_BLOCK + 1 <= LANES
    sd = lambda w: jax.ShapeDtypeStruct((G, T, w), bf16)
    ospec = lambda w: pl.BlockSpec((G, rows, w), lambda i: (0, i, 0))
    return pl.pallas_call(
        _kv_prep_kernel,
        grid=(T // rows,),
        in_specs=[pl.BlockSpec((rows, KV_ROW), lambda i: (i, ks_col_block)),
                  pl.BlockSpec((rows, KV_ROW), lambda i: (i, kw_col_block))],
        out_specs=[ospec(NSA_DH + LANES), ospec(NSA_DH), ospec(NSA_DH), ospec(NSA_DH)],
        out_shape=[sd(NSA_DH + LANES), sd(NSA_DH), sd(NSA_DH), sd(NSA_DH)],
        compiler_params=_cparams("parallel"),
        name="kv_prep",
    )(u, u)


def _top_blocks(score, n_top, axis=1):
    lane = lax.broadcasted_iota(i32, score.shape, axis).astype(f32)
    width = float(score.shape[axis])
    work = score
    sel = jnp.zeros(score.shape, dtype=jnp.bool_)
    firsts = []
    for _ in range(n_top):
        mx = jnp.max(work, axis=axis, keepdims=True)
        first = jnp.min(jnp.where(work == mx, lane, width), axis=axis, keepdims=True)
        hit = lane == first
        sel = jnp.logical_or(sel, hit)
        work = jnp.where(hit, REMOVED, work)
        firsts.append(first)
    return sel, firsts


def _nsa_prompt_kernel(q_ref, gt_ref, kck_ref, kcv_ref, ksk_ref, ksv_ref, kwk_ref, kwv_ref, o_ref,
                       s_sc, m_sc, l_sc, acc_sc, *, n_sel):
    QB, R, D = Q_BLOCK, NSA_GROUP, NSA_DH
    rows = R * QB
    qb = pl.program_id(1)
    q = q_ref[...]
    qs = jnp.concatenate([q[:, r * D:(r + 1) * D] for r in range(R)], axis=0) * (D ** -0.5)
    qs_b = qs.astype(bf16)
    tok = lax.broadcasted_iota(i32, (rows, 1), 0) & (QB - 1)
    pos = qb * QB + tok

    ns = kck_ref.shape[0]
    s = _mm_nt(qs_b, kck_ref[...].astype(bf16))
    j = lax.broadcasted_iota(i32, (1, ns), 1)
    e, den = _masked_softmax_rows(s, j * CMP_STRIDE + (CMP_BLOCK - 1) <= pos)
    p_c = e / den
    o_c = _mm(p_c.astype(bf16), kcv_ref[...].astype(bf16))

    imp = p_c[0:QB]
    for r in range(1, R):
        imp = imp + p_c[r * QB:(r + 1) * QB]
    ratio = SEL_BLOCK // CMP_STRIDE
    off = CMP_BLOCK // CMP_STRIDE - 1
    nn = lax.broadcasted_iota(i32, (LANES, ns), 0)
    jj = lax.broadcasted_iota(i32, (LANES, ns), 1)
    overlap_t = jnp.where((jj >= ratio * nn - off) & (jj < ratio * nn + ratio), 1.0, 0.0).astype(bf16)
    imp_sel = sum(_mm_nt(overlap_t, part) for part in _split3(imp))
    n_idx = lax.broadcasted_iota(i32, (LANES, QB), 0)
    cur = _div_pow2(qb * QB + lax.broadcasted_iota(i32, (LANES, QB), 1), SEL_BLOCK)
    forced = (n_idx == 0) | (n_idx == cur) | (n_idx == cur - 1)
    score = jnp.where(forced, FORCE, jnp.where(n_idx <= cur, imp_sel, NEG))
    score = jnp.where(n_idx < n_sel, score, REMOVED)
    sel, _ = _top_blocks(score, min(SEL_TOPN, n_sel), axis=0)
    bias = jnp.where(sel & (n_idx <= cur), 0.0, MASK_BIAS).T.astype(bf16)
    qp = jnp.concatenate([jnp.concatenate([qs_b[r * QB:(r + 1) * QB], bias], axis=1) for r in range(R)], axis=0)

    KT = SEL_KT
    last = (qb * QB + QB - 1) // KT

    def scores(kt):
        return _mm_nt(qp, ksk_ref[pl.ds(pl.multiple_of(kt * KT, KT), KT), :])

    def keep(kt, sk):
        s_sc[kt] = sk
        mx = m_sc[...]
        for c in range(KT // LANES):
            mx = jnp.maximum(mx, sk[:, c * LANES:(c + 1) * LANES])
        m_sc[...] = mx

    def pass1(kt, carry):
        keep(kt, scores(kt))
        return carry

    m_sc[...] = jnp.full(m_sc.shape, NEG, f32)
    lax.fori_loop(0, last, pass1, 0)
    key = last * KT + lax.broadcasted_iota(i32, (1, KT), 1)
    keep(last, jnp.where(key <= pos, scores(last), NEG))
    m_sc[...] = jnp.broadcast_to(jnp.max(m_sc[...], axis=-1, keepdims=True), m_sc.shape)
    l_sc[...] = jnp.zeros_like(l_sc)
    acc_sc[...] = jnp.zeros_like(acc_sc)

    def pass2(kt, carry):
        sk = s_sc[kt]
        mb = m_sc[...]
        ps = [jnp.exp(sk[:, c * LANES:(c + 1) * LANES] - mb) for c in range(KT // LANES)]
        l_sc[...] += sum(ps[1:], ps[0])
        p = jnp.concatenate(ps, axis=1).astype(bf16)
        acc_sc[...] += _mm(p, ksv_ref[pl.ds(pl.multiple_of(kt * KT, KT), KT), :])
        return carry

    lax.fori_loop(0, last + 1, pass2, 0)
    o_s = acc_sc[...] / jnp.sum(l_sc[...], axis=-1, keepdims=True)

    nband = WINDOW // QB + 1
    wlen = nband * QB
    wstart = pl.multiple_of(jnp.maximum(qb - (nband - 1), 0) * QB, QB)
    sw = _mm_nt(qs_b, kwk_ref[pl.ds(wstart, wlen), :])
    diff = pos - (wstart + lax.broadcasted_iota(i32, (1, wlen), 1))
    e, den = _masked_softmax_rows(sw, (diff >= 0) & (diff <= WINDOW))
    o_w = _mm((e / den).astype(bf16), kwv_ref[pl.ds(wstart, wlen), :])

    gate = _sigmoid(gt_ref[...])
    for r in range(R):
        rs = slice(r * QB, (r + 1) * QB)
        o_ref[:, r * D:(r + 1) * D] = (gate[:, r:r + 1] * o_c[rs] + gate[:, R + r:R + r + 1] * o_s[rs]
                                       + gate[:, 2 * R + r:2 * R + r + 1] * o_w[rs])


def _nsa_prompt(u, q_col_block, gates_g, kvc, ksk, ksv, kwk, kwv):
    T = u.shape[0]
    G, R, D, QB = NSA_KV_HEADS, NSA_GROUP, NSA_DH, Q_BLOCK
    ns = kvc.shape[0]
    n_sel = (T - 1) // SEL_BLOCK + 1
    assert T % SEL_KT == 0 and T >= (WINDOW // QB + 1) * QB and n_sel <= LANES
    rows = R * QB
    res = lambda w: pl.BlockSpec((None, T, w), lambda g, i: (g, 0, 0))
    return pl.pallas_call(
        functools.partial(_nsa_prompt_kernel, n_sel=n_sel),
        grid=(G, T // QB),
        in_specs=[pl.BlockSpec((QB, R * D), lambda g, i: (i, q_col_block + g)),
                  pl.BlockSpec((None, QB, LANES), lambda g, i: (g, i, 0)),
                  pl.BlockSpec((ns, D), lambda g, i: (0, g)),
                  pl.BlockSpec((ns, D), lambda g, i: (0, G + g)),
                  res(D + LANES), res(D), res(D), res(D)],
        out_specs=pl.BlockSpec((QB, R * D), lambda g, i: (i, g)),
        out_shape=jax.ShapeDtypeStruct((T, NSA_DIM), f32),
        scratch_shapes=[pltpu.VMEM((T // SEL_KT, rows, SEL_KT), f32), pltpu.VMEM((rows, LANES), f32),
                        pltpu.VMEM((rows, LANES), f32), pltpu.VMEM((rows, D), f32)],
        compiler_params=_cparams("parallel", "arbitrary"),
        name="nsa_prompt",
    )(u, gates_g, kvc, kvc, ksk, ksv, kwk, kwv)


def _group_gates(gt):
    T = gt.shape[0]
    G, R = NSA_KV_HEADS, NSA_GROUP
    g = gt.reshape(T, 3, G, R).transpose(2, 0, 1, 3).reshape(G, T, 3 * R)
    return jnp.pad(g, ((0, 0), (0, 0), (0, LANES - 3 * R)))


def _nsa_sample_cmp_kernel(q_ref, kvc_ref, oc_ref, idx_ref, *, pos, n_sel, sel_w):
    H, R, D, G = NSA_HEADS, NSA_GROUP, NSA_DH, NSA_KV_HEADS
    ns = kvc_ref.shape[0]
    qs_b = (q_ref[...] * (D ** -0.5)).astype(bf16)
    head = lax.broadcasted_iota(i32, (H, 1), 0)
    grp = _div_pow2(head, R)
    s = jnp.zeros((H, ns), f32)
    for g in range(G):
        s = jnp.where(grp == g, _mm_nt(qs_b, kvc_ref[:, g * D:(g + 1) * D].astype(bf16)), s)
    j = lax.broadcasted_iota(i32, (1, ns), 1)
    e, den = _masked_softmax_rows(s, j * CMP_STRIDE + (CMP_BLOCK - 1) <= pos)
    p_c = e / den
    p_b = p_c.astype(bf16)
    o_c = jnp.zeros((H, D), f32)
    imp = jnp.zeros((H, ns), f32)
    for g in range(G):
        o_c = jnp.where(grp == g, _mm(p_b, kvc_ref[:, (G + g) * D:(G + g + 1) * D].astype(bf16)), o_c)
        imp = jnp.where(grp == g, jnp.sum(jnp.where(grp == g, p_c, 0.0), axis=0, keepdims=True), imp)
    oc_ref[...] = o_c
    ratio = SEL_BLOCK // CMP_STRIDE
    off = CMP_BLOCK // CMP_STRIDE - 1
    jj = lax.broadcasted_iota(i32, (ns, sel_w), 0)
    nn = lax.broadcasted_iota(i32, (ns, sel_w), 1)
    overlap = jnp.where((jj >= ratio * nn - off) & (jj < ratio * nn + ratio), 1.0, 0.0).astype(bf16)
    imp_sel = _dot_sel_r(imp, overlap)
    n_idx = lax.broadcasted_iota(i32, (H, sel_w), 1)
    cur = pos // SEL_BLOCK
    forced = (n_idx == 0) | (n_idx == cur) | (n_idx == cur - 1)
    score = jnp.where(forced, FORCE, jnp.where(n_idx <= cur, imp_sel, NEG))
    score = jnp.where(n_idx < n_sel, score, REMOVED)
    _, firsts = _top_blocks(score, SEL_TOPN)
    lane = lax.broadcasted_iota(i32, (H, LANES), 1)
    idx = jnp.zeros((H, LANES), f32)
    for i, first in enumerate(firsts):
        idx = jnp.where(lane == i, first, idx)
    idx_ref[...] = idx.astype(i32)


def _nsa_sample_cmp(q, kvc, pos):
    B, H, D = q.shape
    ns = kvc.shape[1]
    n_sel = pos // SEL_BLOCK + 1
    assert n_sel >= SEL_TOPN
    sel_w = -(-n_sel // LANES) * LANES
    o_c, idx = pl.pallas_call(
        functools.partial(_nsa_sample_cmp_kernel, pos=pos, n_sel=n_sel, sel_w=sel_w),
        grid=(B,),
        in_specs=[pl.BlockSpec((None, H, D), lambda b: (b, 0, 0)),
                  pl.BlockSpec((None, ns, KV_ROW), lambda b: (b, 0, 0))],
        out_specs=[pl.BlockSpec((None, H, D), lambda b: (b, 0, 0)),
                   pl.BlockSpec((None, H, LANES), lambda b: (b, 0, 0))],
        out_shape=[jax.ShapeDtypeStruct((B, H, D), f32), jax.ShapeDtypeStruct((B, H, LANES), i32)],
        compiler_params=_cparams("parallel"),
        name="nsa_sample_cmp",
    )(q, kvc)
    return o_c, idx[:, ::NSA_GROUP, :SEL_TOPN]


def _nsa_sample_sel_kernel(idx_ref, pt_ref, q_ref, new_ref, *refs, past):
    H, R, D, G = NSA_HEADS, NSA_GROUP, NSA_DH, NSA_KV_HEADS
    blocks, o_ref = refs[:G * SEL_TOPN], refs[G * SEL_TOPN]
    b = pl.program_id(0)
    qs_b = (q_ref[...] * (D ** -0.5)).astype(bf16)
    grp = _div_pow2(lax.broadcasted_iota(i32, (H, 1), 0), R)
    per_tile = SUBLANES // KV_CHUNKS
    half = SEL_BLOCK // per_tile
    o = jnp.zeros((H, D), f32)
    for g in range(G):
        ks, vs = [], []
        lane = lax.broadcasted_iota(i32, (1, SEL_TOPN * SEL_BLOCK), 1)
        slot = _div_pow2(lane, SEL_BLOCK)
        in_slot = lane & (SEL_BLOCK - 1)
        pk_row = (in_slot & (half - 1)) * per_tile + _div_pow2(in_slot, half)
        for i in range(SEL_TOPN):
            blk = blocks[g * SEL_TOPN + i]
            base = idx_ref[(b * G + g) * SEL_TOPN + i] * SEL_BLOCK
            pk_row = pk_row + jnp.where(slot == i, base, 0)
            for par in range(per_tile):
                pk_col = base + per_tile * lax.broadcasted_iota(i32, (half, 1), 0) + par
                old = pk_col < past
                k = blk[pl.ds(par * KV_CHUNKS + g, half, stride=SUBLANES), :]
                v = blk[pl.ds(par * KV_CHUNKS + G + g, half, stride=SUBLANES), :]
                ks.append(jnp.where(old, k, new_ref[g:g + 1, :]).astype(bf16))
                vs.append(jnp.where(old, v, new_ref[G + g:G + g + 1, :]).astype(bf16))
        s = _mm_nt(qs_b, jnp.concatenate(ks, axis=0))
        e, den = _masked_softmax_rows(s, pk_row <= past)
        o = jnp.where(grp == g, _mm((e / den).astype(bf16), jnp.concatenate(vs, axis=0)), o)
    o_ref[...] = o


def _nsa_sample_sel(q, ks_new, pool, page_table, idx, past):
    B, H, D = q.shape
    G = NSA_KV_HEADS
    n_pages = page_table.shape[1]
    halves = PAGE_SIZE // SEL_BLOCK
    blk_rows = SEL_BLOCK * KV_CHUNKS
    pool_h = pool.reshape(pool.shape[0] * halves, blk_rows, D)
    last_old = past // SEL_BLOCK - 1

    def blk_map(b, idx_r, pt_r, g, i):
        blk = jnp.minimum(idx_r[(b * G + g) * SEL_TOPN + i], last_old)
        return (pt_r[b * n_pages + blk // halves] * halves + blk % halves, 0, 0)

    specs = [pl.BlockSpec((None, blk_rows, D), functools.partial(blk_map, g=g, i=i))
             for g in range(G) for i in range(SEL_TOPN)]
    gs = pltpu.PrefetchScalarGridSpec(
        num_scalar_prefetch=2, grid=(B,),
        in_specs=[pl.BlockSpec((None, H, D), lambda b, *_: (b, 0, 0)),
                  pl.BlockSpec((None, KV_CHUNKS, D), lambda b, *_: (b, 0, 0))] + specs,
        out_specs=pl.BlockSpec((None, H, D), lambda b, *_: (b, 0, 0)))
    return pl.pallas_call(
        functools.partial(_nsa_sample_sel_kernel, past=past),
        grid_spec=gs,
        out_shape=jax.ShapeDtypeStruct((B, H, D), f32),
        compiler_params=_cparams("arbitrary"),
        name="nsa_sample_sel",
    )(idx.reshape(-1), page_table.reshape(-1), q, ks_new, *([pool_h] * (G * SEL_TOPN)))


def _nsa_sample_win_kernel(q_ref, buf_ref, new_ref, oc_ref, os_ref, gt_ref, o_ref, win_ref, *, past):
    H, R, D, G = NSA_HEADS, NSA_GROUP, NSA_DH, NSA_KV_HEADS
    rows = buf_ref.shape[0]
    Lb = rows // KV_CHUNKS
    per_tile = SUBLANES // KV_CHUNKS
    half = Lb // per_tile
    qs = q_ref[...] * (D ** -0.5)
    qs_b = qs.astype(bf16)
    grp = _div_pow2(lax.broadcasted_iota(i32, (H, 1), 0), R)
    lane = lax.broadcasted_iota(i32, (1, Lb), 1)
    key_pos = past - Lb + (lane & (half - 1)) * per_tile + _div_pow2(lane, half)
    diff = past - key_pos
    valid = (diff >= 0) & (diff <= WINDOW)
    o_w = jnp.zeros((H, D), f32)
    for g in range(G):
        chunk = lambda c: jnp.concatenate(
            [buf_ref[pl.ds(par * KV_CHUNKS + c, half, stride=SUBLANES), :] for par in range(per_tile)], axis=0)
        new_k, new_v = new_ref[g:g + 1, :], new_ref[G + g:G + g + 1, :]
        s_b = jnp.where(valid, _mm_nt(qs_b, chunk(g).astype(bf16)), NEG)
        s_n = jnp.sum(qs * new_k, axis=-1, keepdims=True)
        mx = jnp.maximum(jnp.max(s_b, axis=-1, keepdims=True), s_n)
        e_b = jnp.where(valid, jnp.exp(s_b - mx), 0.0)
        e_n = jnp.exp(s_n - mx)
        den = jnp.sum(e_b, axis=-1, keepdims=True) + e_n
        og = _mm((e_b / den).astype(bf16), chunk(G + g).astype(bf16)) + (e_n / den) * new_v
        o_w = jnp.where(grp == g, og, o_w)
    gate = _sigmoid(gt_ref[...])
    o_ref[...] = gate[:, 0:1] * oc_ref[...] + gate[:, 1:2] * os_ref[...] + gate[:, 2:3] * o_w
    win_ref[0:rows - KV_CHUNKS, :] = buf_ref[KV_CHUNKS:rows, :]
    win_ref[rows - KV_CHUNKS:rows, :] = new_ref[...]


def _nsa_sample_win(q, win_buf, kw_new, o_c, o_s, gt, past):
    B, H, D = q.shape
    rows = win_buf.shape[1]
    assert rows == WINDOW * KV_CHUNKS
    per_b = lambda *s: pl.BlockSpec((None,) + s, lambda b: (b,) + (0,) * len(s))
    return pl.pallas_call(
        functools.partial(_nsa_sample_win_kernel, past=past),
        grid=(B,),
        in_specs=[per_b(H, D), per_b(rows, D), per_b(KV_CHUNKS, D), per_b(H, D), per_b(H, D), per_b(H, 3)],
        out_specs=[per_b(H, D), per_b(rows, D)],
        out_shape=[jax.ShapeDtypeStruct((B, H, D), f32), jax.ShapeDtypeStruct((B, rows, D), f32)],
        compiler_params=_cparams("parallel"),
        name="nsa_sample_win",
    )(q, win_buf, kw_new, o_c, o_s, gt.reshape(B, 3, H).transpose(0, 2, 1))


def kernel(x_prompt, x_sample, cache_cmp_kv, cache_sel_kv, cache_win_kv, cache_mem_kv, state_mlstm_c, state_mlstm_n, state_mlstm_m, state_conv, page_table, mem_prompt, g_mix, w_in, w_conv, b_conv, w_mq, w_mk, w_mv, w_mgate, b_mgate, g_mhead, cmp_pe, cmp_wpos, cmp_wphi, w_out, g_memx, g_mems, w_mem_q, w_mem_kv, w_mem_o, g_ffn, w_ff1, w_ff2, g_final):
    depth = w_in.shape[0]
    assert depth == 1 and x_prompt.shape[0] == 1 and x_sample.shape[1] == 1
    T, Dm = x_prompt.shape[1:]
    B = x_sample.shape[0]
    G, Dh, H = NSA_KV_HEADS, NSA_DH, ML_HEADS
    past = page_table.shape[1] * PAGE_SIZE
    assert (past + 1) // CMP_STRIDE == past // CMP_STRIDE
    n_main = 2 * ML_DIM + NSA_DIM + 3 * KV_ROW
    n_gate = 3 * NSA_HEADS
    l = 0
    hp = x_prompt.reshape(T, Dm)
    hs = x_sample.reshape(B, Dm)

    wbd_q, wbd_k, wbd_v = (_blockdiag_weights(w[l]) for w in (w_mq, w_mk, w_mv))
    pe_t, wp_t = _compress_tables(cmp_pe[l], cmp_wpos[l])
    wb_in, wb_out, wb_mq, wb_mkv, wb_mo, wb_f1, wb_f2 = (
        w[l].astype(bf16) for w in (w_in, w_out, w_mem_q, w_mem_kv, w_mem_o, w_ff1, w_ff2))
    w_gt = jnp.pad(wb_in[:, n_main:], ((0, 0), (0, LANES - n_gate)))
    q_cb = 2 * ML_DIM // KV_ROW
    c_cb = (2 * ML_DIM + NSA_DIM) // KV_ROW
    Dff = w_ff1.shape[-1]
    n_mq, n_mkv = w_mem_q.shape[-1], w_mem_kv.shape[-1]
    tm_in, tm, tn, tf = 1024, 512, 512, 1024

    u = _norm_matmul(hp, g_mix[l], wb_in, n_main, tm_in, tn, "in_proj_p")
    gt = _norm_matmul(hp, g_mix[l], w_gt, LANES, tm_in, LANES, "in_gate_p")[:, :n_gate]
    ml_o, c_p, n_p, m_p = _mlstm_prompt(u, w_conv[l], b_conv[l], wbd_q, wbd_k, wbd_v, w_mgate[l], b_mgate[l],
                                        g_mhead[l], 128, "mlstm_prompt")
    kvc_p = _compress_prompt(u, c_cb, pe_t, wp_t, cmp_wphi[l])
    ksk, ksv, kwk, kwv = _kv_prep(u, c_cb + 1, c_cb + 2)
    nsa_o = _nsa_prompt(u, q_cb, _group_gates(gt), kvc_p, ksk, ksv, kwk, kwv)
    h1 = _out_proj(ml_o, nsa_o, wb_out, hp, tm, Dm, "out_proj_p")
    mem_kv = _norm_matmul(mem_prompt.reshape(-1, Dm), g_mems[l], wb_mkv, n_mkv, mem_prompt.shape[1], tn, "mem_kv")
    h2 = _mem_prompt(h1, g_memx[l], wb_mq, mem_kv, wb_mo, tm, "mem_attn_p")
    y_p = _ffn_final(h2, g_ffn[l], wb_f1, wb_f2, g_final, tm, tf, "ffn_p")

    kv_shape = lambda a: a.reshape(1, 1, a.shape[0], 2, G, Dh)
    kvw_p = u[:, (c_cb + 2) * KV_ROW:(c_cb + 3) * KV_ROW]
    out_p = (y_p.reshape(1, T, Dm),
             kv_shape(u[:, c_cb * KV_ROW:(c_cb + 1) * KV_ROW]),
             kv_shape(u[:, (c_cb + 1) * KV_ROW:(c_cb + 2) * KV_ROW]),
             kv_shape(kvw_p[T - min(WINDOW, T):]),
             mem_kv.reshape(1, 1, -1, 2, MEM_HEADS, MEM_DH),
             c_p[None, None], n_p[None, None], m_p[None, None],
             u[T - (ML_CONV - 1):, :ML_DIM][None, None])

    us = _norm_matmul(hs, g_mix[l], wb_in, n_main, B, tn, "in_proj_s")
    gts = _norm_matmul(hs, g_mix[l], w_gt, LANES, B, LANES, "in_gate_s")[:, :n_gate]
    xm_s, og_s = us[:, :ML_DIM], us[:, ML_DIM:2 * ML_DIM]
    q_s = us[:, 2 * ML_DIM:2 * ML_DIM + NSA_DIM].reshape(B, NSA_HEADS, Dh)
    kc_s, ks_s, kw_s = (us[:, (c_cb + i) * KV_ROW:(c_cb + i + 1) * KV_ROW] for i in range(3))
    ml_os, c_s, n_s, m_s = _mlstm_sample(xm_s, og_s, state_conv[l], state_mlstm_c[l], state_mlstm_n[l],
                                         state_mlstm_m[l], w_conv[l], b_conv[l], wbd_q, wbd_k, wbd_v,
                                         w_mgate[l], b_mgate[l], g_mhead[l])
    pool_c = cache_cmp_kv.reshape(-1, PAGE_SIZE * KV_CHUNKS, Dh)
    pool_s = cache_sel_kv.reshape(-1, PAGE_SIZE * KV_CHUNKS, Dh)
    win_buf = cache_win_kv.reshape(B, -1, Dh)
    mem_buf = cache_mem_kv.reshape(B, -1, MEM_DH)
    kvc_s = _compress_paged(pool_c, page_table, pe_t, wp_t, cmp_wphi[l])
    o_cs, idx = _nsa_sample_cmp(q_s, kvc_s, past)
    o_ss = _nsa_sample_sel(q_s, ks_s.reshape(B, KV_CHUNKS, Dh), pool_s, page_table, idx, past)
    nsa_os, win_new = _nsa_sample_win(q_s, win_buf, kw_s.reshape(B, KV_CHUNKS, Dh), o_cs, o_ss, gts, past)
    h1s = _out_proj(ml_os, nsa_os.reshape(B, NSA_DIM), wb_out, hs, B, Dm, "out_proj_s")
    qm_s = _norm_matmul(h1s, g_memx[l], wb_mq, n_mq, B, n_mq, "mem_q_s")
    om_s = _mem_sample(qm_s, mem_buf, "mem_attn_s")
    h2s = _matmul_res(om_s, wb_mo, h1s, B, Dm, "mem_o_s")
    y_s = _ffn_final(h2s, g_ffn[l], wb_f1, wb_f2, g_final, B, tf, "ffn_s")

    kv_s_shape = lambda a: a.reshape(1, B, 1, 2, G, Dh)
    conv_s = jnp.concatenate([state_conv[l][:, 1:], xm_s[:, None, :]], axis=1)
    out_s = (y_s.reshape(B, 1, Dm), kv_s_shape(kc_s), kv_s_shape(ks_s),
             win_new.reshape(1, B, -1, 2, G, Dh), c_s[None], n_s[None], m_s[None], conv_s[None])

    return (out_p[0], out_s[0]) + out_p[1:] + out_s[1:]
```

```python
import functools

import jax
import jax.numpy as jnp
import numpy as np
from jax import lax
from jax.experimental import pallas as pl
from jax.experimental.pallas import tpu as pltpu

f32 = jnp.float32
bf16 = jnp.bfloat16
i32 = jnp.int32

EPS = 1e-6
NEG = -1e30
FORCE = 1e30
ML_HEADS = 8
ML_DH = 128
ML_DIM = ML_HEADS * ML_DH
ML_CONV = 4
ML_QKV_BLOCK = 4
NSA_HEADS = 8
NSA_KV_HEADS = 2
NSA_GROUP = NSA_HEADS // NSA_KV_HEADS
NSA_DH = 128
NSA_DIM = NSA_HEADS * NSA_DH
NSA_KV_W = NSA_KV_HEADS * NSA_DH
CMP_BLOCK = 32
CMP_STRIDE = 16
SEL_BLOCK = 64
SEL_TOPN = 16
WINDOW = 512
Q_BLOCK = 128
PAGE_SIZE = 128
MEM_HEADS = 4
MEM_DH = 128
KV_ROW = 2 * NSA_KV_W
KV_CHUNKS = KV_ROW // NSA_DH

LANES = 128
SUBLANES = 8
VMEM_LIMIT = 56 * 1024 * 1024

MASK_BIAS = -1e9
REMOVED = -3.0e38
SEL_KT = 1024


def _cparams(*sem):
    return pltpu.CompilerParams(dimension_semantics=sem, vmem_limit_bytes=VMEM_LIMIT)


def _mm(a, b):
    return jnp.dot(a, b, preferred_element_type=f32)


def _mm_nt(a, b):
    return lax.dot_general(a, b, (((1,), (1,)), ((), ())), preferred_element_type=f32)


def _split2(x):
    h = x.astype(bf16)
    return h, (x - h.astype(f32)).astype(bf16)


def _split3(x):
    h = x.astype(bf16)
    r = x - h.astype(f32)
    m = r.astype(bf16)
    return h, m, (r - m.astype(f32)).astype(bf16)


def _dot3(a, b, mm=_mm):
    ah, al = _split2(a)
    bh, bl = _split2(b)
    return mm(ah, bh) + mm(al, bh) + mm(ah, bl)


def _dot_sel_l(sel, x):
    h, m, l = _split3(x)
    return _mm(sel, h) + _mm(sel, m) + _mm(sel, l)


def _dot_sel_r(x, sel):
    h, m, l = _split3(x)
    return _mm(h, sel) + _mm(m, sel) + _mm(l, sel)


def _rms(x, g):
    return x * lax.rsqrt(jnp.mean(x * x, axis=-1, keepdims=True) + EPS) * g


def _div_pow2(x, d):
    assert d & (d - 1) == 0
    return lax.shift_right_logical(x, jnp.full(x.shape, d.bit_length() - 1, x.dtype))


def _sigmoid(x):
    return 1.0 / (1.0 + jnp.exp(-x))


def _log_sigmoid(x):
    return jnp.minimum(x, 0.0) - jnp.log(1.0 + jnp.exp(-jnp.abs(x)))


def _masked_softmax_rows(s, valid):
    s = jnp.where(valid, s, NEG)
    mx = jnp.max(s, axis=-1, keepdims=True)
    e = jnp.where(valid, jnp.exp(s - mx), 0.0)
    den = jnp.sum(e, axis=-1, keepdims=True)
    return e, jnp.where(den > 0.0, den, 1.0)


def _biased_softmax_rows(s, bias):
    s = s + bias
    mx = jnp.max(s, axis=-1, keepdims=True)
    e = jnp.exp(s - mx)
    inv = jnp.where(mx > 0.5 * NEG, 1.0 / jnp.sum(e, axis=-1, keepdims=True), 0.0)
    return e * inv


def _norm_matmul_kernel(x_ref, g_ref, w_ref, o_ref, xn_ref):
    @pl.when(pl.program_id(1) == 0)
    def _():
        xn_ref[...] = _rms(x_ref[...], g_ref[...]).astype(bf16)

    o_ref[...] = _mm(xn_ref[...], w_ref[...])


def _norm_matmul(x, g, w, n_cols, tm, tn, name):
    M, K = x.shape
    return pl.pallas_call(
        _norm_matmul_kernel,
        grid=(M // tm, n_cols // tn),
        in_specs=[pl.BlockSpec((tm, K), lambda i, j: (i, 0)),
                  pl.BlockSpec((1, K), lambda i, j: (0, 0)),
                  pl.BlockSpec((K, tn), lambda i, j: (0, j))],
        out_specs=pl.BlockSpec((tm, tn), lambda i, j: (i, j)),
        out_shape=jax.ShapeDtypeStruct((M, n_cols), f32),
        scratch_shapes=[pltpu.VMEM((tm, K), bf16)],
        compiler_params=_cparams("parallel", "arbitrary"),
        name=name,
    )(x, g.reshape(1, K), w)


def _out_proj_kernel(a1_ref, a2_ref, w1_ref, w2_ref, r_ref, o_ref):
    o_ref[...] = (r_ref[...] + _mm(a1_ref[...].astype(bf16), w1_ref[...])
                  + _mm(a2_ref[...].astype(bf16), w2_ref[...]))


def _out_proj(a1, a2, w, res, tm, tn, name):
    M, K1 = a1.shape
    K2 = a2.shape[1]
    assert K1 == K2 and w.shape[0] == K1 + K2
    N = w.shape[1]
    return pl.pallas_call(
        _out_proj_kernel,
        grid=(M // tm, N // tn),
        in_specs=[pl.BlockSpec((tm, K1), lambda i, j: (i, 0)),
                  pl.BlockSpec((tm, K2), lambda i, j: (i, 0)),
                  pl.BlockSpec((K1, tn), lambda i, j: (0, j)),
                  pl.BlockSpec((K2, tn), lambda i, j: (1, j)),
                  pl.BlockSpec((tm, tn), lambda i, j: (i, j))],
        out_specs=pl.BlockSpec((tm, tn), lambda i, j: (i, j)),
        out_shape=jax.ShapeDtypeStruct((M, N), f32),
        compiler_params=_cparams("parallel", "arbitrary"),
        name=name,
    )(a1, a2, w, w, res)


def _matmul_res_kernel(a_ref, w_ref, r_ref, o_ref):
    o_ref[...] = r_ref[...] + _mm(a_ref[...].astype(bf16), w_ref[...])


def _matmul_res(a, w, res, tm, tn, name):
    M, K = a.shape
    N = w.shape[1]
    return pl.pallas_call(
        _matmul_res_kernel,
        grid=(M // tm, N // tn),
        in_specs=[pl.BlockSpec((tm, K), lambda i, j: (i, 0)),
                  pl.BlockSpec((K, tn), lambda i, j: (0, j)),
                  pl.BlockSpec((tm, tn), lambda i, j: (i, j))],
        out_specs=pl.BlockSpec((tm, tn), lambda i, j: (i, j)),
        out_shape=jax.ShapeDtypeStruct((M, N), f32),
        compiler_params=_cparams("parallel", "arbitrary"),
        name=name,
    )(a, w, res)


def _ffn_kernel(h_ref, g_ref, w1_ref, w2_ref, gf_ref, y_ref, xn_ref, acc_ref):
    f = pl.program_id(1)

    @pl.when(f == 0)
    def _():
        xn_ref[...] = _rms(h_ref[...], g_ref[...]).astype(bf16)
        acc_ref[...] = jnp.zeros_like(acc_ref)

    a = _mm(xn_ref[...], w1_ref[...])
    a = jnp.square(jnp.maximum(a, 0.0))
    acc_ref[...] += _mm(a.astype(bf16), w2_ref[...])

    @pl.when(f == pl.num_programs(1) - 1)
    def _():
        y_ref[...] = _rms(h_ref[...] + acc_ref[...], gf_ref[...])


def _ffn_final(h, g, w1, w2, g_final, tm, tf, name):
    M, D = h.shape
    F = w1.shape[1]
    return pl.pallas_call(
        _ffn_kernel,
        grid=(M // tm, F // tf),
        in_specs=[pl.BlockSpec((tm, D), lambda i, j: (i, 0)),
                  pl.BlockSpec((1, D), lambda i, j: (0, 0)),
                  pl.BlockSpec((D, tf), lambda i, j: (0, j)),
                  pl.BlockSpec((tf, D), lambda i, j: (j, 0)),
                  pl.BlockSpec((1, D), lambda i, j: (0, 0))],
        out_specs=pl.BlockSpec((tm, D), lambda i, j: (i, 0)),
        out_shape=jax.ShapeDtypeStruct((M, D), f32),
        scratch_shapes=[pltpu.VMEM((tm, D), bf16), pltpu.VMEM((tm, D), f32)],
        compiler_params=_cparams("parallel", "arbitrary"),
        name=name,
    )(h, g.reshape(1, D), w1, w2, g_final.reshape(1, D))


def _mem_prompt_kernel(h_ref, g_ref, wq_ref, k_ref, v_ref, wo_ref, o_ref):
    h = h_ref[...]
    xn = _rms(h, g_ref[...]).astype(bf16)
    q = _mm(xn, wq_ref[...]) * (MEM_DH ** -0.5)
    outs = []
    for hd in range(MEM_HEADS):
        sl = slice(hd * MEM_DH, (hd + 1) * MEM_DH)
        s = _mm_nt(q[:, sl].astype(bf16), k_ref[:, sl].astype(bf16))
        e = jnp.exp(s - jnp.max(s, axis=-1, keepdims=True))
        p = e / jnp.sum(e, axis=-1, keepdims=True)
        outs.append(_mm(p.astype(bf16), v_ref[:, sl].astype(bf16)))
    o = jnp.concatenate(outs, axis=-1)
    o_ref[...] = h + _mm(o.astype(bf16), wo_ref[...])


def _mem_prompt(h, g, wq, mem_kv, wo, tm, name):
    M, D = h.shape
    HD = MEM_HEADS * MEM_DH
    ML = mem_kv.shape[0]
    return pl.pallas_call(
        _mem_prompt_kernel,
        grid=(M // tm,),
        in_specs=[pl.BlockSpec((tm, D), lambda i: (i, 0)),
                  pl.BlockSpec((1, D), lambda i: (0, 0)),
                  pl.BlockSpec((D, HD), lambda i: (0, 0)),
                  pl.BlockSpec((ML, HD), lambda i: (0, 0)),
                  pl.BlockSpec((ML, HD), lambda i: (0, 1)),
                  pl.BlockSpec((HD, D), lambda i: (0, 0))],
        out_specs=pl.BlockSpec((tm, D), lambda i: (i, 0)),
        out_shape=jax.ShapeDtypeStruct((M, D), f32),
        compiler_params=_cparams("parallel"),
        name=name,
    )(h, g.reshape(1, D), wq, mem_kv, mem_kv, wo)


def _mem_sample_kernel(q_ref, kv_ref, o_ref):
    per_tok = 2 * MEM_HEADS
    ML = kv_ref.shape[0] // per_tok
    q = q_ref[...] * (MEM_DH ** -0.5)
    outs = []
    for hd in range(MEM_HEADS):
        sl = slice(hd * MEM_DH, (hd + 1) * MEM_DH)
        k = kv_ref[pl.ds(hd, ML, stride=per_tok), :]
        v = kv_ref[pl.ds(MEM_HEADS + hd, ML, stride=per_tok), :]
        s = jnp.sum(k * q[:, sl], axis=-1, keepdims=True)
        e = jnp.exp(s - jnp.max(s, axis=0, keepdims=True))
        p = e / jnp.sum(e, axis=0, keepdims=True)
        outs.append(jnp.sum(p * v, axis=0, keepdims=True))
    o_ref[...] = jnp.concatenate(outs, axis=-1)


def _mem_sample(q, kv, name):
    B, HD = q.shape
    rows = kv.shape[1]
    out = pl.pallas_call(
        _mem_sample_kernel,
        grid=(B,),
        in_specs=[pl.BlockSpec((None, 1, HD), lambda b: (b, 0, 0)),
                  pl.BlockSpec((None, rows, MEM_DH), lambda b: (b, 0, 0))],
        out_specs=pl.BlockSpec((None, 1, HD), lambda b: (b, 0, 0)),
        out_shape=jax.ShapeDtypeStruct((B, 1, HD), f32),
        compiler_params=_cparams("parallel"),
        name=name,
    )(q.reshape(B, 1, HD), kv)
    return out.reshape(B, HD)


def _blockdiag_weights(w):
    per = LANES // ML_QKV_BLOCK
    nchunk = w.shape[0] // per
    wc = w.reshape(nchunk, per, ML_QKV_BLOCK, ML_QKV_BLOCK)
    eye = jnp.eye(per, dtype=w.dtype)
    full = wc[:, :, :, None, :] * eye[None, :, None, :, None]
    return full.reshape(nchunk, LANES, LANES)


def _split_weight(w):
    return jnp.stack(_split2(w))


def _dot3_pre(a_parts, w_hi, w_lo, mm=_mm):
    ah, al = a_parts
    return mm(ah, w_hi) + mm(al, w_hi) + mm(ah, w_lo)


def _ml_qkv_gates(xc, xm, wq_ref, wk_ref, wv_ref, wg_ref, bg_ref):
    qs, ks, vs = [], [], []
    for c in range(ML_DIM // LANES):
        sl = slice(c * LANES, (c + 1) * LANES)
        xc_parts = _split2(xc[:, sl])
        qs.append(_dot3_pre(xc_parts, wq_ref[0, c], wq_ref[1, c]))
        ks.append(_dot3_pre(xc_parts, wk_ref[0, c], wk_ref[1, c]) * (ML_DH ** -0.5))
        vs.append(_dot3_pre(_split2(xm[:, sl]), wv_ref[0, c], wv_ref[1, c]))
    q = jnp.concatenate(qs, axis=-1)
    k = jnp.concatenate(ks, axis=-1)
    v = jnp.concatenate(vs, axis=-1)
    qkv_parts = _split2(jnp.concatenate([q, k, v], axis=-1))
    gates = _dot3_pre(qkv_parts, wg_ref[0], wg_ref[1]) + bg_ref[...]
    return q, k, v, gates, qkv_parts


def _mlstm_prompt_kernel(xm_ref, og_ref, wconv_ref, bconv_ref, wq_ref, wk_ref, wv_ref, wg_ref, wgt_ref,
                         bg_ref, bgt_ref, gh_ref, o_ref, c_ref, n_ref, m_ref, prev_sc, *, L):
    H, D = ML_HEADS, ML_DH

    @pl.when(pl.program_id(0) == 0)
    def _():
        prev_sc[...] = jnp.zeros_like(prev_sc)
        c_ref[...] = jnp.zeros_like(c_ref)
        n_ref[...] = jnp.zeros_like(n_ref)
        m_ref[...] = jnp.full(m_ref.shape, NEG, f32)

    x = xm_ref[...]
    full = jnp.concatenate([prev_sc[...], x], axis=0)
    y = bconv_ref[...]
    for j in range(ML_CONV):
        off = SUBLANES - (ML_CONV - 1) + j
        y = y + full[off:off + L] * wconv_ref[j:j + 1, :]
    prev_sc[...] = x[L - SUBLANES:L]
    xc = y * _sigmoid(y)

    q, k, v, gates, (qkv_h, qkv_l) = _ml_qkv_gates(xc, x, wq_ref, wk_ref, wv_ref, wg_ref, bg_ref)
    gates_t = (_mm_nt(wgt_ref[0], qkv_h) + _mm_nt(wgt_ref[0], qkv_l) + _mm_nt(wgt_ref[1], qkv_h)
               + bgt_ref[...])
    ig_c = gates[:, 0:H]
    lf_c = _log_sigmoid(gates[:, H:2 * H])
    ig_r = gates_t[0:H, :]
    lf_r = _log_sigmoid(gates_t[H:2 * H, :])

    t_i = lax.broadcasted_iota(i32, (L, L), 0)
    s_i = lax.broadcasted_iota(i32, (L, L), 1)
    causal = s_i <= t_i
    tri = jnp.where(causal, 1.0, 0.0).astype(bf16)
    b_c = _dot_sel_l(tri, lf_c)
    tri_u = jnp.where(t_i <= s_i, 1.0, 0.0).astype(bf16)
    b_r = _dot_sel_r(lf_r, tri_u)

    for h in range(H):
        sl = slice(h * D, (h + 1) * D)
        qh, kh, vh = q[:, sl], k[:, sl], v[:, sl]
        bc = b_c[:, h:h + 1]
        m_prev = m_ref[h:h + 1, 0:1]
        d_in = jnp.where(causal, bc - b_r[h:h + 1, :] + ig_r[h:h + 1, :], NEG)
        d_x = bc + m_prev
        m_t = jnp.maximum(d_x, jnp.max(d_in, axis=-1, keepdims=True))
        w_in = jnp.exp(d_in - m_t)
        w_x = jnp.exp(d_x - m_t)
        qb = qh.astype(bf16)
        kb = kh.astype(bf16)
        vb = vh.astype(bf16)
        s = _mm_nt(qb, kb) * w_in
        c_old = c_ref[h]
        n_old = n_ref[h:h + 1, :]
        num = _mm(s.astype(bf16), vb) + w_x * _mm(qb, c_old.astype(bf16))
        den = jnp.sum(s, axis=-1, keepdims=True) + w_x * jnp.sum(qh * n_old, axis=-1, keepdims=True)
        hh = num / jnp.maximum(jnp.abs(den), jnp.exp(-m_t))
        m_new = m_t[L - 1:L, :]
        b_last = bc[L - 1:L, :]
        g_x = jnp.exp(b_last + m_prev - m_new)
        g_s = jnp.exp(b_last - bc + ig_c[:, h:h + 1] - m_new)
        ks_ = kh * g_s
        c_ref[h] = g_x * c_old + _mm(ks_.T.astype(bf16), vb)
        n_ref[h:h + 1, :] = g_x * n_old + jnp.sum(ks_, axis=0, keepdims=True)
        m_ref[h:h + 1, :] = jnp.broadcast_to(m_new, (1, LANES))
        hn = hh * lax.rsqrt(jnp.mean(hh * hh, axis=-1, keepdims=True) + EPS) * gh_ref[:, sl]
        o_ref[:, sl] = _sigmoid(og_ref[:, sl]) * hn


def _mlstm_prompt(u, w_conv, b_conv, wbd_q, wbd_k, wbd_v, w_gate, w_gate_t, b_gate, g_head, L, name):
    T = u.shape[0]
    H, D = ML_HEADS, ML_DH
    nch = ML_DIM // LANES
    full2 = lambda shape: pl.BlockSpec(shape, lambda i: (0,) * len(shape))
    out, c, n, m = pl.pallas_call(
        functools.partial(_mlstm_prompt_kernel, L=L),
        grid=(T // L,),
        in_specs=[pl.BlockSpec((L, ML_DIM), lambda i: (i, 0)),
                  pl.BlockSpec((L, ML_DIM), lambda i: (i, 1)),
                  full2((ML_CONV, ML_DIM)), full2((1, ML_DIM)),
                  full2((2, nch, LANES, LANES)), full2((2, nch, LANES, LANES)), full2((2, nch, LANES, LANES)),
                  full2((2, 3 * ML_DIM, 2 * H)), full2((2, 2 * H, 3 * ML_DIM)),
                  full2((1, 2 * H)), full2((2 * H, 1)), full2((1, ML_DIM))],
        out_specs=[pl.BlockSpec((L, ML_DIM), lambda i: (i, 0)),
                   full2((H, D, D)), full2((H, D)), full2((H, LANES))],
        out_shape=[jax.ShapeDtypeStruct((T, ML_DIM), f32),
                   jax.ShapeDtypeStruct((H, D, D), f32),
                   jax.ShapeDtypeStruct((H, D), f32),
                   jax.ShapeDtypeStruct((H, LANES), f32)],
        scratch_shapes=[pltpu.VMEM((SUBLANES, ML_DIM), f32)],
        compiler_params=_cparams("arbitrary"),
        name=name,
    )(u, u, w_conv, b_conv.reshape(1, ML_DIM), wbd_q, wbd_k, wbd_v, w_gate, w_gate_t,
      b_gate.reshape(1, 2 * H), b_gate.reshape(2 * H, 1), g_head.reshape(1, ML_DIM))
    return out, c, n, m[:, 0]


def _mlstm_sample_pre_kernel(xm_ref, s0_ref, s1_ref, s2_ref, wconv_ref, bconv_ref, wq_ref, wk_ref, wv_ref,
                             wg_ref, bg_ref, q_ref, k_ref, v_ref, g_ref):
    x = xm_ref[...]
    y = (bconv_ref[...] + s0_ref[...] * wconv_ref[0:1, :] + s1_ref[...] * wconv_ref[1:2, :]
         + s2_ref[...] * wconv_ref[2:3, :] + x * wconv_ref[3:4, :])
    xc = y * _sigmoid(y)
    q, k, v, gates, _ = _ml_qkv_gates(xc, x, wq_ref, wk_ref, wv_ref, wg_ref, bg_ref)
    q_ref[...] = q
    k_ref[...] = k
    v_ref[...] = v
    g_ref[...] = gates


def _mlstm_sample_step_kernel(qc_ref, kc_ref, q_ref, k_ref, v_ref, gt_ref, og_ref, gh_ref, c_ref, n_ref, m_ref,
                              o_ref, cn_ref, nn_ref, mn_ref):
    H, D = ML_HEADS, ML_DH
    gates = gt_ref[...]
    ig = gates[:, 0:H]
    lf = _log_sigmoid(gates[:, H:2 * H])
    m_old = m_ref[...]
    m_new = jnp.maximum(lf + m_old, ig)
    w_in = jnp.exp(ig - m_new)
    w_x = jnp.exp(lf + m_old - m_new)
    mn_ref[...] = m_new
    e_m = jnp.exp(-m_new)
    og = og_ref[...]
    outs = []
    for h in range(H):
        sl = slice(h * D, (h + 1) * D)
        qr, kr, vr = q_ref[h:h + 1, :], k_ref[h:h + 1, :], v_ref[h:h + 1, :]
        qcol, kcol = qc_ref[:, h:h + 1], kc_ref[:, h:h + 1]
        wi, wx = w_in[:, h:h + 1], w_x[:, h:h + 1]
        c_old = c_ref[h]
        n_old = n_ref[h:h + 1, :]
        s = jnp.sum(qr * kr, axis=-1, keepdims=True) * wi
        num = s * vr + wx * jnp.sum(c_old * qcol, axis=0, keepdims=True)
        den = s + wx * jnp.sum(n_old * qr, axis=-1, keepdims=True)
        hh = num / jnp.maximum(jnp.abs(den), e_m[:, h:h + 1])
        cn_ref[h] = wx * c_old + wi * (kcol * vr)
        nn_ref[h:h + 1, :] = wx * n_old + wi * kr
        hn = hh * lax.rsqrt(jnp.mean(hh * hh, axis=-1, keepdims=True) + EPS) * gh_ref[:, sl]
        outs.append(_sigmoid(og[:, sl]) * hn)
    o_ref[...] = jnp.concatenate(outs, axis=-1)


def _mlstm_sample(xm, og, conv_state, c0, n0, m0, w_conv, b_conv, wbd_q, wbd_k, wbd_v, w_gate, b_gate, g_head):
    B = xm.shape[0]
    H, D = ML_HEADS, ML_DH
    sds = lambda *s: jax.ShapeDtypeStruct(s, f32)
    q, k, v, gates = pl.pallas_call(
        _mlstm_sample_pre_kernel,
        out_shape=[sds(B, ML_DIM), sds(B, ML_DIM), sds(B, ML_DIM), sds(B, 2 * H)],
        compiler_params=pltpu.CompilerParams(vmem_limit_bytes=VMEM_LIMIT),
        name="mlstm_sample_pre",
    )(xm, conv_state[:, 0], conv_state[:, 1], conv_state[:, 2], w_conv, b_conv.reshape(1, ML_DIM),
      wbd_q, wbd_k, wbd_v, w_gate, b_gate.reshape(1, 2 * H))
    q3, k3, v3 = (a.reshape(B, H, D) for a in (q, k, v))
    per_b = lambda *s: pl.BlockSpec((None,) + s, lambda b: (b,) + (0,) * len(s))
    out, c, n, m = pl.pallas_call(
        _mlstm_sample_step_kernel,
        grid=(B,),
        in_specs=[per_b(D, H), per_b(D, H), per_b(H, D), per_b(H, D), per_b(H, D), per_b(1, 2 * H),
                  per_b(1, ML_DIM), pl.BlockSpec((1, ML_DIM), lambda b: (0, 0)),
                  per_b(H, D, D), per_b(H, D), per_b(1, H)],
        out_specs=[per_b(1, ML_DIM), per_b(H, D, D), per_b(H, D), per_b(1, H)],
        out_shape=[sds(B, 1, ML_DIM), sds(B, H, D, D), sds(B, H, D), sds(B, 1, H)],
        compiler_params=_cparams("parallel"),
        name="mlstm_sample_step",
    )(q3.transpose(0, 2, 1), k3.transpose(0, 2, 1), q3, k3, v3, gates.reshape(B, 1, 2 * H),
      og.reshape(B, 1, ML_DIM), g_head.reshape(1, ML_DIM), c0, n0, m0.reshape(B, 1, H))
    return out.reshape(B, ML_DIM), c, n, m.reshape(B, H)


def _compress_kernel(x_ref, pe_ref, wp_ref, wphi_ref, o_ref, f0_sc, f1_sc, mn_sc):
    step = pl.program_id(0)
    sub = x_ref.shape[0] // CMP_STRIDE
    x3 = x_ref[...].reshape(sub, CMP_STRIDE, KV_ROW)
    base = pl.multiple_of(step * sub, sub)
    for o, sc in ((0, f0_sc), (1, f1_sc)):
        y = x3 + pe_ref[o][None]
        sc[pl.ds(base, sub), :] = jnp.sum(y * _sigmoid(y) * wp_ref[o][None], axis=1)
    mn_sc[pl.ds(base, sub), :] = jnp.sum(x3, axis=1) * (1.0 / CMP_STRIDE)

    @pl.when(step == pl.num_programs(0) - 1)
    def _():
        ns = f0_sc.shape[0]
        feat = f0_sc[...] + pltpu.roll(f1_sc[...], ns - 1, axis=0)
        mn = mn_sc[...]
        pooled = (mn + pltpu.roll(mn, ns - 1, axis=0)) * (CMP_STRIDE / CMP_BLOCK)
        for c in range(2):
            for g in range(NSA_KV_HEADS):
                sl = slice((c * NSA_KV_HEADS + g) * NSA_DH, (c * NSA_KV_HEADS + g + 1) * NSA_DH)
                o_ref[:, sl] = pooled[:, sl] + _dot3(feat[:, sl], wphi_ref[c])


def _compress_tables(pe, wpos):
    def lay(a):
        r = CMP_BLOCK // CMP_STRIDE
        a = a.reshape(2, r, CMP_STRIDE, NSA_DH).transpose(1, 2, 0, 3)
        a = jnp.broadcast_to(a[:, :, :, None, :], (r, CMP_STRIDE, 2, NSA_KV_HEADS, NSA_DH))
        return a.reshape(r, CMP_STRIDE, KV_ROW)
    return lay(pe), lay(wpos)


def _compress_prompt(u, col_block, pe_t, wp_t, wphi, rows=512):
    T = u.shape[0]
    n_sub = T // CMP_STRIDE
    const = lambda shape: pl.BlockSpec(shape, lambda s: (0,) * len(shape))
    return pl.pallas_call(
        _compress_kernel,
        grid=(T // rows,),
        in_specs=[pl.BlockSpec((rows, KV_ROW), lambda s: (s, col_block)),
                  const(pe_t.shape), const(wp_t.shape), const(wphi.shape)],
        out_specs=const((n_sub, KV_ROW)),
        out_shape=jax.ShapeDtypeStruct((n_sub, KV_ROW), f32),
        scratch_shapes=[pltpu.VMEM((n_sub, KV_ROW), f32)] * 3,
        compiler_params=_cparams("arbitrary"),
        name="compress_prompt",
    )(u, pe_t, wp_t, wphi)


def _compress_paged_kernel(pt_ref, *refs, n_pages):
    pages = refs[:n_pages]
    pe_ref, wp_ref, wphi_ref, o_ref, f0_sc, f1_sc, mn_sc = refs[n_pages:]
    step = pl.program_id(1)
    sub = PAGE_SIZE // CMP_STRIDE
    tiles = CMP_STRIDE * KV_CHUNKS // SUBLANES
    out_rows = sub * SUBLANES
    for p in range(n_pages):
        xh = pages[p][...].reshape(sub, tiles, SUBLANES, LANES) * 0.5
        base = pl.multiple_of((step * n_pages + p) * out_rows, out_rows)
        for o, sc in ((0, f0_sc), (1, f1_sc)):
            y = xh + pe_ref[o][None]
            b = y * wp_ref[o][None]
            sc[pl.ds(base, out_rows), :] = jnp.sum(b + b * jnp.tanh(y), axis=1).reshape(out_rows, LANES)
        mn_sc[pl.ds(base, out_rows), :] = (jnp.sum(xh, axis=1) * (2.0 / CMP_STRIDE)).reshape(out_rows, LANES)

    @pl.when(step == pl.num_programs(1) - 1)
    def _():
        ns = f0_sc.shape[0] // SUBLANES
        for c in range(KV_CHUNKS):
            col = lambda sc: (sc[pl.ds(c, ns, stride=SUBLANES), :]
                              + sc[pl.ds(KV_CHUNKS + c, ns, stride=SUBLANES), :])
            feat = col(f0_sc) + pltpu.roll(col(f1_sc), ns - 1, axis=0)
            mn = col(mn_sc)
            pooled = (mn + pltpu.roll(mn, ns - 1, axis=0)) * (CMP_STRIDE / CMP_BLOCK)
            o_ref[:, c * LANES:(c + 1) * LANES] = pooled + _dot3(feat, wphi_ref[c // NSA_KV_HEADS])


def _compress_paged(pool, page_table, pe_t, wp_t, wphi, pages_per_step=16):
    B, n_pages = page_table.shape
    P = pages_per_step
    n_sub = n_pages * PAGE_SIZE // CMP_STRIDE
    r = CMP_BLOCK // CMP_STRIDE
    tiles = CMP_STRIDE * KV_CHUNKS // SUBLANES
    pe4 = (0.5 * pe_t).reshape(r, tiles, SUBLANES, LANES)
    wp4 = wp_t.reshape(r, tiles, SUBLANES, LANES)
    page_rows = PAGE_SIZE * KV_CHUNKS
    specs = [pl.BlockSpec((None, page_rows, LANES),
                          functools.partial(lambda b, s, pt, p: (pt[b * n_pages + s * P + p], 0, 0), p=p))
             for p in range(P)]
    const = lambda shape: pl.BlockSpec(shape, lambda *a: (0,) * len(shape))
    gs = pltpu.PrefetchScalarGridSpec(
        num_scalar_prefetch=1, grid=(B, n_pages // P),
        in_specs=specs + [const(pe4.shape), const(wp4.shape), const(wphi.shape)],
        out_specs=pl.BlockSpec((None, n_sub, KV_ROW), lambda b, s, pt: (b, 0, 0)),
        scratch_shapes=[pltpu.VMEM((n_sub * SUBLANES, LANES), f32)] * 3)
    return pl.pallas_call(
        functools.partial(_compress_paged_kernel, n_pages=P),
        grid_spec=gs,
        out_shape=jax.ShapeDtypeStruct((B, n_sub, KV_ROW), f32),
        compiler_params=_cparams("parallel", "arbitrary"),
        name="compress_paged",
    )(page_table.reshape(-1), *([pool] * P), pe4, wp4, wphi)


def _kv_prep_kernel(ks_ref, kw_ref, ksk_ref, ksv_ref, kwk_ref, kwv_ref):
    rows = ks_ref.shape[0]
    ks = ks_ref[...]
    kw = kw_ref[...]
    r = pl.program_id(0) * rows + lax.broadcasted_iota(i32, (rows, LANES), 0)
    n = lax.broadcasted_iota(i32, (rows, LANES), 1)
    onehot = jnp.where(_div_pow2(r, SEL_BLOCK) == n, 1.0, 0.0).astype(bf16)
    for g in range(NSA_KV_HEADS):
        ksl = slice(g * NSA_DH, (g + 1) * NSA_DH)
        vsl = slice(NSA_KV_W + g * NSA_DH, NSA_KV_W + (g + 1) * NSA_DH)
        ksk_ref[g, :, 0:NSA_DH] = ks[:, ksl].astype(bf16)
        ksk_ref[g, :, NSA_DH:NSA_DH + LANES] = onehot
        ksv_ref[g] = ks[:, vsl].astype(bf16)
        kwk_ref[g] = kw[:, ksl].astype(bf16)
        kwv_ref[g] = kw[:, vsl].astype(bf16)


def _kv_prep(u, ks_col_block, kw_col_block, rows=512):
    T = u.shape[0]
    G = NSA_KV_HEADS
    assert (T - 1) // SEL_BLOCK + 1 <= LANES
    sd = lambda w: jax.ShapeDtypeStruct((G, T, w), bf16)
    ospec = lambda w: pl.BlockSpec((G, rows, w), lambda i: (0, i, 0))
    return pl.pallas_call(
        _kv_prep_kernel,
        grid=(T // rows,),
        in_specs=[pl.BlockSpec((rows, KV_ROW), lambda i: (i, ks_col_block)),
                  pl.BlockSpec((rows, KV_ROW), lambda i: (i, kw_col_block))],
        out_specs=[ospec(NSA_DH + LANES), ospec(NSA_DH), ospec(NSA_DH), ospec(NSA_DH)],
        out_shape=[sd(NSA_DH + LANES), sd(NSA_DH), sd(NSA_DH), sd(NSA_DH)],
        compiler_params=_cparams("parallel"),
        name="kv_prep",
    )(u, u)


def _top_blocks(score, n_top, axis=1):
    lane = lax.broadcasted_iota(i32, score.shape, axis).astype(f32)
    width = float(score.shape[axis])
    work = score
    sel = jnp.zeros(score.shape, dtype=jnp.bool_)
    firsts = []
    for _ in range(n_top):
        mx = jnp.max(work, axis=axis, keepdims=True)
        first = jnp.min(jnp.where(work == mx, lane, width), axis=axis, keepdims=True)
        hit = lane == first
        sel = jnp.logical_or(sel, hit)
        work = jnp.where(hit, REMOVED, work)
        firsts.append(first)
    return sel, firsts


def _nsa_prompt_kernel(q_ref, gt_ref, kck_ref, kcv_ref, ksk_ref, ksv_ref, kwk_ref, kwv_ref, o_ref,
                       s_sc, m_sc, l_sc, acc_sc, *, n_sel):
    QB, R, D = Q_BLOCK, NSA_GROUP, NSA_DH
    rows = R * QB
    qb = pl.program_id(1)
    q = q_ref[...]
    qs = jnp.concatenate([q[:, r * D:(r + 1) * D] for r in range(R)], axis=0) * (D ** -0.5)
    qs_b = qs.astype(bf16)
    pos = qb * QB + lax.broadcasted_iota(i32, (QB, 1), 0)

    def head_bias(valid):
        return jnp.concatenate([jnp.where(valid, 0.0, NEG)] * R, axis=0)

    ns = kck_ref.shape[0]
    s = _mm_nt(qs_b, kck_ref[...].astype(bf16))
    j = lax.broadcasted_iota(i32, (1, ns), 1)
    p_c = _biased_softmax_rows(s, head_bias(j * CMP_STRIDE + (CMP_BLOCK - 1) <= pos))
    o_c = _mm(p_c.astype(bf16), kcv_ref[...].astype(bf16))

    imp = p_c[0:QB]
    for r in range(1, R):
        imp = imp + p_c[r * QB:(r + 1) * QB]
    ratio = SEL_BLOCK // CMP_STRIDE
    off = CMP_BLOCK // CMP_STRIDE - 1
    nn = lax.broadcasted_iota(i32, (LANES, ns), 0)
    jj = lax.broadcasted_iota(i32, (LANES, ns), 1)
    overlap_t = jnp.where((jj >= ratio * nn - off) & (jj < ratio * nn + ratio), 1.0, 0.0).astype(bf16)
    imp_sel = sum(_mm_nt(overlap_t, part) for part in _split3(imp))
    n_idx = lax.broadcasted_iota(i32, (LANES, QB), 0)
    cur = _div_pow2(qb * QB + lax.broadcasted_iota(i32, (LANES, QB), 1), SEL_BLOCK)
    forced = (n_idx == 0) | (n_idx == cur) | (n_idx == cur - 1)
    score = jnp.where(forced, FORCE, jnp.where(n_idx <= cur, imp_sel, NEG))
    score = jnp.where(n_idx < n_sel, score, REMOVED)
    sel, _ = _top_blocks(score, min(SEL_TOPN, n_sel), axis=0)
    bias = jnp.where(sel & (n_idx <= cur), 0.0, MASK_BIAS).T.astype(bf16)
    qp = jnp.concatenate([jnp.concatenate([qs_b[r * QB:(r + 1) * QB], bias], axis=1) for r in range(R)], axis=0)

    KT = SEL_KT
    last = (qb * QB + QB - 1) // KT

    def scores(kt):
        return _mm_nt(qp, ksk_ref[pl.ds(pl.multiple_of(kt * KT, KT), KT), :])

    def keep(kt, sk):
        s_sc[kt] = sk
        mx = m_sc[...]
        for c in range(KT // LANES):
            mx = jnp.maximum(mx, sk[:, c * LANES:(c + 1) * LANES])
        m_sc[...] = mx

    def pass1(kt, carry):
        keep(kt, scores(kt))
        return carry

    m_sc[...] = jnp.full(m_sc.shape, NEG, f32)
    lax.fori_loop(0, last, pass1, 0)
    key = last * KT + lax.broadcasted_iota(i32, (1, KT), 1)
    keep(last, scores(last) + head_bias(key <= pos))
    m_sc[...] = jnp.broadcast_to(jnp.max(m_sc[...], axis=-1, keepdims=True), m_sc.shape)
    l_sc[...] = jnp.zeros_like(l_sc)
    acc_sc[...] = jnp.zeros_like(acc_sc)

    def pass2(kt, carry):
        sk = s_sc[kt]
        mb = m_sc[...]
        ps = [jnp.exp(sk[:, c * LANES:(c + 1) * LANES] - mb) for c in range(KT // LANES)]
        l_sc[...] += sum(ps[1:], ps[0])
        p = jnp.concatenate(ps, axis=1).astype(bf16)
        acc_sc[...] += _mm(p, ksv_ref[pl.ds(pl.multiple_of(kt * KT, KT), KT), :])
        return carry

    lax.fori_loop(0, last + 1, pass2, 0)
    o_s = acc_sc[...] * (1.0 / jnp.sum(l_sc[...], axis=-1, keepdims=True))

    nband = WINDOW // QB + 1
    wlen = nband * QB
    wstart = pl.multiple_of(jnp.maximum(qb - (nband - 1), 0) * QB, QB)
    sw = _mm_nt(qs_b, kwk_ref[pl.ds(wstart, wlen), :])
    diff = pos - (wstart + lax.broadcasted_iota(i32, (1, wlen), 1))
    p_w = _biased_softmax_rows(sw, head_bias((diff >= 0) & (diff <= WINDOW)))
    o_w = _mm(p_w.astype(bf16), kwv_ref[pl.ds(wstart, wlen), :])

    gate = _sigmoid(gt_ref[...])
    for r in range(R):
        rs = slice(r * QB, (r + 1) * QB)
        o_ref[:, r * D:(r + 1) * D] = (gate[:, r:r + 1] * o_c[rs] + gate[:, R + r:R + r + 1] * o_s[rs]
                                       + gate[:, 2 * R + r:2 * R + r + 1] * o_w[rs])


def _nsa_prompt(u, q_col_block, gates_g, kvc, ksk, ksv, kwk, kwv):
    T = u.shape[0]
    G, R, D, QB = NSA_KV_HEADS, NSA_GROUP, NSA_DH, Q_BLOCK
    ns = kvc.shape[0]
    n_sel = (T - 1) // SEL_BLOCK + 1
    assert T % SEL_KT == 0 and T >= (WINDOW // QB + 1) * QB and n_sel <= LANES
    rows = R * QB
    res = lambda w: pl.BlockSpec((None, T, w), lambda g, i: (g, 0, 0))
    return pl.pallas_call(
        functools.partial(_nsa_prompt_kernel, n_sel=n_sel),
        grid=(G, T // QB),
        in_specs=[pl.BlockSpec((QB, R * D), lambda g, i: (i, q_col_block + g)),
                  pl.BlockSpec((None, QB, LANES), lambda g, i: (g, i, 0)),
                  pl.BlockSpec((ns, D), lambda g, i: (0, g)),
                  pl.BlockSpec((ns, D), lambda g, i: (0, G + g)),
                  res(D + LANES), res(D), res(D), res(D)],
        out_specs=pl.BlockSpec((QB, R * D), lambda g, i: (i, g)),
        out_shape=jax.ShapeDtypeStruct((T, NSA_DIM), f32),
        scratch_shapes=[pltpu.VMEM((T // SEL_KT, rows, SEL_KT), f32), pltpu.VMEM((rows, LANES), f32),
                        pltpu.VMEM((rows, LANES), f32), pltpu.VMEM((rows, D), f32)],
        compiler_params=_cparams("parallel", "arbitrary"),
        name="nsa_prompt",
    )(u, gates_g, kvc, kvc, ksk, ksv, kwk, kwv)


def _group_gates(gt):
    T = gt.shape[0]
    G, R = NSA_KV_HEADS, NSA_GROUP
    g = gt.reshape(T, 3, G, R).transpose(2, 0, 1, 3).reshape(G, T, 3 * R)
    return jnp.pad(g, ((0, 0), (0, 0), (0, LANES - 3 * R)))


def _nsa_sample_cmp_kernel(q_ref, kvc_ref, oc_ref, idx_ref, *, pos, n_sel, sel_w):
    H, R, D, G = NSA_HEADS, NSA_GROUP, NSA_DH, NSA_KV_HEADS
    ns = kvc_ref.shape[0]
    qs_b = (q_ref[...] * (D ** -0.5)).astype(bf16)
    head = lax.broadcasted_iota(i32, (H, 1), 0)
    grp = _div_pow2(head, R)
    s = jnp.zeros((H, ns), f32)
    for g in range(G):
        s = jnp.where(grp == g, _mm_nt(qs_b, kvc_ref[:, g * D:(g + 1) * D].astype(bf16)), s)
    j = lax.broadcasted_iota(i32, (1, ns), 1)
    e, den = _masked_softmax_rows(s, j * CMP_STRIDE + (CMP_BLOCK - 1) <= pos)
    p_c = e / den
    p_b = p_c.astype(bf16)
    o_c = jnp.zeros((H, D), f32)
    imp = jnp.zeros((H, ns), f32)
    for g in range(G):
        o_c = jnp.where(grp == g, _mm(p_b, kvc_ref[:, (G + g) * D:(G + g + 1) * D].astype(bf16)), o_c)
        imp = jnp.where(grp == g, jnp.sum(jnp.where(grp == g, p_c, 0.0), axis=0, keepdims=True), imp)
    oc_ref[...] = o_c
    ratio = SEL_BLOCK // CMP_STRIDE
    off = CMP_BLOCK // CMP_STRIDE - 1
    nn = lax.broadcasted_iota(i32, (sel_w, ns), 0)
    jj = lax.broadcasted_iota(i32, (sel_w, ns), 1)
    overlap_t = jnp.where((jj >= ratio * nn - off) & (jj < ratio * nn + ratio), 1.0, 0.0).astype(bf16)
    imp_sel = sum(_mm_nt(overlap_t, part) for part in _split3(imp))
    n_idx = lax.broadcasted_iota(i32, (sel_w, H), 0)
    cur = pos // SEL_BLOCK
    forced = (n_idx == 0) | (n_idx == cur) | (n_idx == cur - 1)
    score = jnp.where(forced, FORCE, jnp.where(n_idx <= cur, imp_sel, NEG))
    score = jnp.where(n_idx < n_sel, score, REMOVED)
    _, firsts = _top_blocks(score, SEL_TOPN, axis=0)
    idx_ref[...] = jnp.concatenate(firsts, axis=0).astype(i32)


def _nsa_sample_cmp(q, kvc, pos):
    B, H, D = q.shape
    ns = kvc.shape[1]
    n_sel = pos // SEL_BLOCK + 1
    assert n_sel >= SEL_TOPN
    sel_w = -(-n_sel // LANES) * LANES
    o_c, idx = pl.pallas_call(
        functools.partial(_nsa_sample_cmp_kernel, pos=pos, n_sel=n_sel, sel_w=sel_w),
        grid=(B,),
        in_specs=[pl.BlockSpec((None, H, D), lambda b: (b, 0, 0)),
                  pl.BlockSpec((None, ns, KV_ROW), lambda b: (b, 0, 0))],
        out_specs=[pl.BlockSpec((None, H, D), lambda b: (b, 0, 0)),
                   pl.BlockSpec((None, SEL_TOPN, H), lambda b: (b, 0, 0))],
        out_shape=[jax.ShapeDtypeStruct((B, H, D), f32), jax.ShapeDtypeStruct((B, SEL_TOPN, H), i32)],
        compiler_params=_cparams("parallel"),
        name="nsa_sample_cmp",
    )(q, kvc)
    return o_c, idx[:, :, ::NSA_GROUP].transpose(0, 2, 1)


def _nsa_sample_sel_kernel(idx_ref, pt_ref, q_ref, new_ref, *refs, past):
    H, R, D, G = NSA_HEADS, NSA_GROUP, NSA_DH, NSA_KV_HEADS
    blocks, o_ref = refs[:G * SEL_TOPN], refs[G * SEL_TOPN]
    b = pl.program_id(0)
    qs_b = (q_ref[...] * (D ** -0.5)).astype(bf16)
    grp = _div_pow2(lax.broadcasted_iota(i32, (H, 1), 0), R)
    per_tile = SUBLANES // KV_CHUNKS
    half = SEL_BLOCK // per_tile
    o = jnp.zeros((H, D), f32)
    for g in range(G):
        ks, vs = [], []
        lane = lax.broadcasted_iota(i32, (1, SEL_TOPN * SEL_BLOCK), 1)
        slot = _div_pow2(lane, SEL_BLOCK)
        in_slot = lane & (SEL_BLOCK - 1)
        pk_row = (in_slot & (half - 1)) * per_tile + _div_pow2(in_slot, half)
        for i in range(SEL_TOPN):
            blk = blocks[g * SEL_TOPN + i]
            base = idx_ref[(b * G + g) * SEL_TOPN + i] * SEL_BLOCK
            pk_row = pk_row + jnp.where(slot == i, base, 0)
            for par in range(per_tile):
                pk_col = base + per_tile * lax.broadcasted_iota(i32, (half, 1), 0) + par
                old = pk_col < past
                k = blk[pl.ds(par * KV_CHUNKS + g, half, stride=SUBLANES), :]
                v = blk[pl.ds(par * KV_CHUNKS + G + g, half, stride=SUBLANES), :]
                ks.append(jnp.where(old, k, new_ref[g:g + 1, :]).astype(bf16))
                vs.append(jnp.where(old, v, new_ref[G + g:G + g + 1, :]).astype(bf16))
        s = _mm_nt(qs_b, jnp.concatenate(ks, axis=0))
        e, den = _masked_softmax_rows(s, pk_row <= past)
        o = jnp.where(grp == g, _mm((e / den).astype(bf16), jnp.concatenate(vs, axis=0)), o)
    o_ref[...] = o


def _nsa_sample_sel(q, ks_new, pool, page_table, idx, past):
    B, H, D = q.shape
    G = NSA_KV_HEADS
    n_pages = page_table.shape[1]
    halves = PAGE_SIZE // SEL_BLOCK
    blk_rows = SEL_BLOCK * KV_CHUNKS
    pool_h = pool.reshape(pool.shape[0] * halves, blk_rows, D)
    last_old = past // SEL_BLOCK - 1

    def blk_map(b, idx_r, pt_r, g, i):
        blk = jnp.minimum(idx_r[(b * G + g) * SEL_TOPN + i], last_old)
        return (pt_r[b * n_pages + blk // halves] * halves + blk % halves, 0, 0)

    specs = [pl.BlockSpec((None, blk_rows, D), functools.partial(blk_map, g=g, i=i))
             for g in range(G) for i in range(SEL_TOPN)]
    gs = pltpu.PrefetchScalarGridSpec(
        num_scalar_prefetch=2, grid=(B,),
        in_specs=[pl.BlockSpec((None, H, D), lambda b, *_: (b, 0, 0)),
                  pl.BlockSpec((None, KV_CHUNKS, D), lambda b, *_: (b, 0, 0))] + specs,
        out_specs=pl.BlockSpec((None, H, D), lambda b, *_: (b, 0, 0)))
    return pl.pallas_call(
        functools.partial(_nsa_sample_sel_kernel, past=past),
        grid_spec=gs,
        out_shape=jax.ShapeDtypeStruct((B, H, D), f32),
        compiler_params=_cparams("arbitrary"),
        name="nsa_sample_sel",
    )(idx.reshape(-1), page_table.reshape(-1), q, ks_new, *([pool_h] * (G * SEL_TOPN)))


def _nsa_sample_win_kernel(q_ref, buf_ref, new_ref, oc_ref, os_ref, gt_ref, o_ref, win_ref, *, past):
    H, R, D, G = NSA_HEADS, NSA_GROUP, NSA_DH, NSA_KV_HEADS
    rows = buf_ref.shape[0]
    Lb = rows // KV_CHUNKS
    per_tile = SUBLANES // KV_CHUNKS
    half = Lb // per_tile
    qs = q_ref[...] * (D ** -0.5)
    qs_b = qs.astype(bf16)
    grp = _div_pow2(lax.broadcasted_iota(i32, (H, 1), 0), R)
    lane = lax.broadcasted_iota(i32, (1, Lb), 1)
    key_pos = past - Lb + (lane & (half - 1)) * per_tile + _div_pow2(lane, half)
    diff = past - key_pos
    valid = (diff >= 0) & (diff <= WINDOW)
    o_w = jnp.zeros((H, D), f32)
    for g in range(G):
        chunk = lambda c: jnp.concatenate(
            [buf_ref[pl.ds(par * KV_CHUNKS + c, half, stride=SUBLANES), :] for par in range(per_tile)], axis=0)
        new_k, new_v = new_ref[g:g + 1, :], new_ref[G + g:G + g + 1, :]
        s_b = jnp.where(valid, _mm_nt(qs_b, chunk(g).astype(bf16)), NEG)
        s_n = jnp.sum(qs * new_k, axis=-1, keepdims=True)
        mx = jnp.maximum(jnp.max(s_b, axis=-1, keepdims=True), s_n)
        e_b = jnp.where(valid, jnp.exp(s_b - mx), 0.0)
        e_n = jnp.exp(s_n - mx)
        den = jnp.sum(e_b, axis=-1, keepdims=True) + e_n
        og = _mm((e_b / den).astype(bf16), chunk(G + g).astype(bf16)) + (e_n / den) * new_v
        o_w = jnp.where(grp == g, og, o_w)
    gate = _sigmoid(gt_ref[...])
    o_ref[...] = gate[:, 0:1] * oc_ref[...] + gate[:, 1:2] * os_ref[...] + gate[:, 2:3] * o_w
    win_ref[0:rows - KV_CHUNKS, :] = buf_ref[KV_CHUNKS:rows, :]
    win_ref[rows - KV_CHUNKS:rows, :] = new_ref[...]


def _nsa_sample_win(q, win_buf, kw_new, o_c, o_s, gt, past):
    B, H, D = q.shape
    rows = win_buf.shape[1]
    assert rows == WINDOW * KV_CHUNKS
    per_b = lambda *s: pl.BlockSpec((None,) + s, lambda b: (b,) + (0,) * len(s))
    return pl.pallas_call(
        functools.partial(_nsa_sample_win_kernel, past=past),
        grid=(B,),
        in_specs=[per_b(H, D), per_b(rows, D), per_b(KV_CHUNKS, D), per_b(H, D), per_b(H, D), per_b(H, 3)],
        out_specs=[per_b(H, D), per_b(rows, D)],
        out_shape=[jax.ShapeDtypeStruct((B, H, D), f32), jax.ShapeDtypeStruct((B, rows, D), f32)],
        compiler_params=_cparams("parallel"),
        name="nsa_sample_win",
    )(q, win_buf, kw_new, o_c, o_s, gt.reshape(B, 3, H).transpose(0, 2, 1))


def kernel(x_prompt, x_sample, cache_cmp_kv, cache_sel_kv, cache_win_kv, cache_mem_kv, state_mlstm_c, state_mlstm_n, state_mlstm_m, state_conv, page_table, mem_prompt, g_mix, w_in, w_conv, b_conv, w_mq, w_mk, w_mv, w_mgate, b_mgate, g_mhead, cmp_pe, cmp_wpos, cmp_wphi, w_out, g_memx, g_mems, w_mem_q, w_mem_kv, w_mem_o, g_ffn, w_ff1, w_ff2, g_final):
    depth = w_in.shape[0]
    assert depth == 1 and x_prompt.shape[0] == 1 and x_sample.shape[1] == 1
    T, Dm = x_prompt.shape[1:]
    B = x_sample.shape[0]
    G, Dh, H = NSA_KV_HEADS, NSA_DH, ML_HEADS
    past = page_table.shape[1] * PAGE_SIZE
    assert (past + 1) // CMP_STRIDE == past // CMP_STRIDE
    n_main = 2 * ML_DIM + NSA_DIM + 3 * KV_ROW
    n_gate = 3 * NSA_HEADS
    l = 0
    hp = x_prompt.reshape(T, Dm)
    hs = x_sample.reshape(B, Dm)

    wbd_q, wbd_k, wbd_v = (_split_weight(_blockdiag_weights(w[l])) for w in (w_mq, w_mk, w_mv))
    w_mg, w_mg_t = _split_weight(w_mgate[l]), _split_weight(w_mgate[l].T)
    pe_t, wp_t = _compress_tables(cmp_pe[l], cmp_wpos[l])
    wb_in, wb_out, wb_mq, wb_mkv, wb_mo, wb_f1, wb_f2 = (
        w[l].astype(bf16) for w in (w_in, w_out, w_mem_q, w_mem_kv, w_mem_o, w_ff1, w_ff2))
    w_gt = jnp.pad(wb_in[:, n_main:], ((0, 0), (0, LANES - n_gate)))
    q_cb = 2 * ML_DIM // KV_ROW
    c_cb = (2 * ML_DIM + NSA_DIM) // KV_ROW
    Dff = w_ff1.shape[-1]
    n_mq, n_mkv = w_mem_q.shape[-1], w_mem_kv.shape[-1]
    tm_in, tm, tn, tf = 1024, 512, 512, 1024

    u = _norm_matmul(hp, g_mix[l], wb_in, n_main, tm_in, tn, "in_proj_p")
    gt = _norm_matmul(hp, g_mix[l], w_gt, LANES, tm_in, LANES, "in_gate_p")[:, :n_gate]
    ml_o, c_p, n_p, m_p = _mlstm_prompt(u, w_conv[l], b_conv[l], wbd_q, wbd_k, wbd_v, w_mg, w_mg_t, b_mgate[l],
                                        g_mhead[l], 128, "mlstm_prompt")
    kvc_p = _compress_prompt(u, c_cb, pe_t, wp_t, cmp_wphi[l])
    ksk, ksv, kwk, kwv = _kv_prep(u, c_cb + 1, c_cb + 2)
    nsa_o = _nsa_prompt(u, q_cb, _group_gates(gt), kvc_p, ksk, ksv, kwk, kwv)
    h1 = _out_proj(ml_o, nsa_o, wb_out, hp, tm, Dm, "out_proj_p")
    mem_kv = _norm_matmul(mem_prompt.reshape(-1, Dm), g_mems[l], wb_mkv, n_mkv, mem_prompt.shape[1], tn, "mem_kv")
    h2 = _mem_prompt(h1, g_memx[l], wb_mq, mem_kv, wb_mo, tm, "mem_attn_p")
    y_p = _ffn_final(h2, g_ffn[l], wb_f1, wb_f2, g_final, tm, tf, "ffn_p")

    kv_shape = lambda a: a.reshape(1, 1, a.shape[0], 2, G, Dh)
    kvw_p = u[:, (c_cb + 2) * KV_ROW:(c_cb + 3) * KV_ROW]
    out_p = (y_p.reshape(1, T, Dm),
             kv_shape(u[:, c_cb * KV_ROW:(c_cb + 1) * KV_ROW]),
             kv_shape(u[:, (c_cb + 1) * KV_ROW:(c_cb + 2) * KV_ROW]),
             kv_shape(kvw_p[T - min(WINDOW, T):]),
             mem_kv.reshape(1, 1, -1, 2, MEM_HEADS, MEM_DH),
             c_p[None, None], n_p[None, None], m_p[None, None],
             u[T - (ML_CONV - 1):, :ML_DIM][None, None])

    us = _norm_matmul(hs, g_mix[l], wb_in, n_main, B, tn, "in_proj_s")
    gts = _norm_matmul(hs, g_mix[l], w_gt, LANES, B, LANES, "in_gate_s")[:, :n_gate]
    xm_s, og_s = us[:, :ML_DIM], us[:, ML_DIM:2 * ML_DIM]
    q_s = us[:, 2 * ML_DIM:2 * ML_DIM + NSA_DIM].reshape(B, NSA_HEADS, Dh)
    kc_s, ks_s, kw_s = (us[:, (c_cb + i) * KV_ROW:(c_cb + i + 1) * KV_ROW] for i in range(3))
    ml_os, c_s, n_s, m_s = _mlstm_sample(xm_s, og_s, state_conv[l], state_mlstm_c[l], state_mlstm_n[l],
                                         state_mlstm_m[l], w_conv[l], b_conv[l], wbd_q, wbd_k, wbd_v,
                                         w_mg, b_mgate[l], g_mhead[l])
    pool_c = cache_cmp_kv.reshape(-1, PAGE_SIZE * KV_CHUNKS, Dh)
    pool_s = cache_sel_kv.reshape(-1, PAGE_SIZE * KV_CHUNKS, Dh)
    win_buf = cache_win_kv.reshape(B, -1, Dh)
    mem_buf = cache_mem_kv.reshape(B, -1, MEM_DH)
    kvc_s = _compress_paged(pool_c, page_table, pe_t, wp_t, cmp_wphi[l])
    o_cs, idx = _nsa_sample_cmp(q_s, kvc_s, past)
    o_ss = _nsa_sample_sel(q_s, ks_s.reshape(B, KV_CHUNKS, Dh), pool_s, page_table, idx, past)
    nsa_os, win_new = _nsa_sample_win(q_s, win_buf, kw_s.reshape(B, KV_CHUNKS, Dh), o_cs, o_ss, gts, past)
    h1s = _out_proj(ml_os, nsa_os.reshape(B, NSA_DIM), wb_out, hs, B, Dm, "out_proj_s")
    qm_s = _norm_matmul(h1s, g_memx[l], wb_mq, n_mq, B, n_mq, "mem_q_s")
    om_s = _mem_sample(qm_s, mem_buf, "mem_attn_s")
    h2s = _matmul_res(om_s, wb_mo, h1s, B, Dm, "mem_o_s")
    y_s = _ffn_final(h2s, g_ffn[l], wb_f1, wb_f2, g_final, B, tf, "ffn_s")

    kv_s_shape = lambda a: a.reshape(1, B, 1, 2, G, Dh)
    conv_s = jnp.concatenate([state_conv[l][:, 1:], xm_s[:, None, :]], axis=1)
    out_s = (y_s.reshape(B, 1, Dm), kv_s_shape(kc_s), kv_s_shape(ks_s),
             win_new.reshape(1, B, -1, 2, G, Dh), c_s[None], n_s[None], m_s[None], conv_s[None])

    return (out_p[0], out_s[0]) + out_p[1:] + out_s[1:]
```

```python
import functools

import jax
import jax.numpy as jnp
import numpy as np
from jax import lax
from jax.experimental import pallas as pl
from jax.experimental.pallas import tpu as pltpu

f32 = jnp.float32
bf16 = jnp.bfloat16
i32 = jnp.int32

EPS = 1e-6
NEG = -1e30
FORCE = 1e30
ML_HEADS = 8
ML_DH = 128
ML_DIM = ML_HEADS * ML_DH
ML_CONV = 4
ML_QKV_BLOCK = 4
NSA_HEADS = 8
NSA_KV_HEADS = 2
NSA_GROUP = NSA_HEADS // NSA_KV_HEADS
NSA_DH = 128
NSA_DIM = NSA_HEADS * NSA_DH
NSA_KV_W = NSA_KV_HEADS * NSA_DH
CMP_BLOCK = 32
CMP_STRIDE = 16
SEL_BLOCK = 64
SEL_TOPN = 16
WINDOW = 512
Q_BLOCK = 128
PAGE_SIZE = 128
MEM_HEADS = 4
MEM_DH = 128
KV_ROW = 2 * NSA_KV_W
KV_CHUNKS = KV_ROW // NSA_DH

LANES = 128
SUBLANES = 8
VMEM_LIMIT = 56 * 1024 * 1024

MASK_BIAS = -1e9
REMOVED = -3.0e38
SEL_KT = 1024
LOG2E = 1.4426950408889634
SAFE_SPAN = 100.0


def _cparams(*sem):
    return pltpu.CompilerParams(dimension_semantics=sem, vmem_limit_bytes=VMEM_LIMIT)


def _mm(a, b):
    return jnp.dot(a, b, preferred_element_type=f32)


def _mm_nt(a, b):
    return lax.dot_general(a, b, (((1,), (1,)), ((), ())), preferred_element_type=f32)


def _split2(x):
    h = x.astype(bf16)
    return h, (x - h.astype(f32)).astype(bf16)


def _split3(x):
    h = x.astype(bf16)
    r = x - h.astype(f32)
    m = r.astype(bf16)
    return h, m, (r - m.astype(f32)).astype(bf16)


def _dot3(a, b, mm=_mm):
    ah, al = _split2(a)
    bh, bl = _split2(b)
    return mm(ah, bh) + mm(al, bh) + mm(ah, bl)


def _dot_sel_l(sel, x):
    h, m, l = _split3(x)
    return _mm(sel, h) + _mm(sel, m) + _mm(sel, l)


def _dot_sel_r(x, sel):
    h, m, l = _split3(x)
    return _mm(h, sel) + _mm(m, sel) + _mm(l, sel)


def _rms(x, g):
    return x * lax.rsqrt(jnp.mean(x * x, axis=-1, keepdims=True) + EPS) * g


def _div_pow2(x, d):
    assert d & (d - 1) == 0
    return lax.shift_right_logical(x, jnp.full(x.shape, d.bit_length() - 1, x.dtype))


def _sigmoid(x):
    return 1.0 / (1.0 + jnp.exp(-x))


def _log_sigmoid(x):
    return jnp.minimum(x, 0.0) - jnp.log(1.0 + jnp.exp(-jnp.abs(x)))


def _masked_softmax_rows(s, valid):
    s = jnp.where(valid, s, NEG)
    mx = jnp.max(s, axis=-1, keepdims=True)
    e = jnp.where(valid, jnp.exp(s - mx), 0.0)
    den = jnp.sum(e, axis=-1, keepdims=True)
    return e, jnp.where(den > 0.0, den, 1.0)


def _biased_softmax_rows(s, bias):
    s = s + bias
    mx = jnp.max(s, axis=-1, keepdims=True)
    e = jnp.exp(s - mx)
    inv = jnp.where(mx > 0.5 * NEG, 1.0 / jnp.sum(e, axis=-1, keepdims=True), 0.0)
    return e * inv


def _norm_matmul_kernel(x_ref, g_ref, w_ref, o_ref, xn_ref):
    @pl.when(pl.program_id(1) == 0)
    def _():
        xn_ref[...] = _rms(x_ref[...], g_ref[...]).astype(bf16)

    o_ref[...] = _mm(xn_ref[...], w_ref[...])


def _norm_matmul(x, g, w, n_cols, tm, tn, name):
    M, K = x.shape
    return pl.pallas_call(
        _norm_matmul_kernel,
        grid=(M // tm, n_cols // tn),
        in_specs=[pl.BlockSpec((tm, K), lambda i, j: (i, 0)),
                  pl.BlockSpec((1, K), lambda i, j: (0, 0)),
                  pl.BlockSpec((K, tn), lambda i, j: (0, j))],
        out_specs=pl.BlockSpec((tm, tn), lambda i, j: (i, j)),
        out_shape=jax.ShapeDtypeStruct((M, n_cols), f32),
        scratch_shapes=[pltpu.VMEM((tm, K), bf16)],
        compiler_params=_cparams("parallel", "arbitrary"),
        name=name,
    )(x, g.reshape(1, K), w)


def _out_proj_kernel(a1_ref, a2_ref, w1_ref, w2_ref, r_ref, o_ref):
    o_ref[...] = (r_ref[...] + _mm(a1_ref[...].astype(bf16), w1_ref[...])
                  + _mm(a2_ref[...].astype(bf16), w2_ref[...]))


def _out_proj(a1, a2, w, res, tm, tn, name):
    M, K1 = a1.shape
    K2 = a2.shape[1]
    assert K1 == K2 and w.shape[0] == K1 + K2
    N = w.shape[1]
    return pl.pallas_call(
        _out_proj_kernel,
        grid=(M // tm, N // tn),
        in_specs=[pl.BlockSpec((tm, K1), lambda i, j: (i, 0)),
                  pl.BlockSpec((tm, K2), lambda i, j: (i, 0)),
                  pl.BlockSpec((K1, tn), lambda i, j: (0, j)),
                  pl.BlockSpec((K2, tn), lambda i, j: (1, j)),
                  pl.BlockSpec((tm, tn), lambda i, j: (i, j))],
        out_specs=pl.BlockSpec((tm, tn), lambda i, j: (i, j)),
        out_shape=jax.ShapeDtypeStruct((M, N), f32),
        compiler_params=_cparams("parallel", "arbitrary"),
        name=name,
    )(a1, a2, w, w, res)


def _matmul_res_kernel(a_ref, w_ref, r_ref, o_ref):
    o_ref[...] = r_ref[...] + _mm(a_ref[...].astype(bf16), w_ref[...])


def _matmul_res(a, w, res, tm, tn, name):
    M, K = a.shape
    N = w.shape[1]
    return pl.pallas_call(
        _matmul_res_kernel,
        grid=(M // tm, N // tn),
        in_specs=[pl.BlockSpec((tm, K), lambda i, j: (i, 0)),
                  pl.BlockSpec((K, tn), lambda i, j: (0, j)),
                  pl.BlockSpec((tm, tn), lambda i, j: (i, j))],
        out_specs=pl.BlockSpec((tm, tn), lambda i, j: (i, j)),
        out_shape=jax.ShapeDtypeStruct((M, N), f32),
        compiler_params=_cparams("parallel", "arbitrary"),
        name=name,
    )(a, w, res)


def _ffn_kernel(h_ref, g_ref, w1_ref, w2_ref, gf_ref, y_ref, xn_ref, acc_ref):
    f = pl.program_id(1)

    @pl.when(f == 0)
    def _():
        xn_ref[...] = _rms(h_ref[...], g_ref[...]).astype(bf16)
        acc_ref[...] = jnp.zeros_like(acc_ref)

    a = _mm(xn_ref[...], w1_ref[...])
    a = jnp.square(jnp.maximum(a, 0.0))
    acc_ref[...] += _mm(a.astype(bf16), w2_ref[...])

    @pl.when(f == pl.num_programs(1) - 1)
    def _():
        y_ref[...] = _rms(h_ref[...] + acc_ref[...], gf_ref[...])


def _ffn_final(h, g, w1, w2, g_final, tm, tf, name):
    M, D = h.shape
    F = w1.shape[1]
    return pl.pallas_call(
        _ffn_kernel,
        grid=(M // tm, F // tf),
        in_specs=[pl.BlockSpec((tm, D), lambda i, j: (i, 0)),
                  pl.BlockSpec((1, D), lambda i, j: (0, 0)),
                  pl.BlockSpec((D, tf), lambda i, j: (0, j)),
                  pl.BlockSpec((tf, D), lambda i, j: (j, 0)),
                  pl.BlockSpec((1, D), lambda i, j: (0, 0))],
        out_specs=pl.BlockSpec((tm, D), lambda i, j: (i, 0)),
        out_shape=jax.ShapeDtypeStruct((M, D), f32),
        scratch_shapes=[pltpu.VMEM((tm, D), bf16), pltpu.VMEM((tm, D), f32)],
        compiler_params=_cparams("parallel", "arbitrary"),
        name=name,
    )(h, g.reshape(1, D), w1, w2, g_final.reshape(1, D))


def _mem_prompt_kernel(h_ref, g_ref, wq_ref, k_ref, v_ref, wo_ref, o_ref):
    h = h_ref[...]
    xn = _rms(h, g_ref[...]).astype(bf16)
    q = _mm(xn, wq_ref[...]) * (MEM_DH ** -0.5)
    outs = []
    for hd in range(MEM_HEADS):
        sl = slice(hd * MEM_DH, (hd + 1) * MEM_DH)
        s = _mm_nt(q[:, sl].astype(bf16), k_ref[:, sl].astype(bf16))
        e = jnp.exp(s - jnp.max(s, axis=-1, keepdims=True))
        p = e / jnp.sum(e, axis=-1, keepdims=True)
        outs.append(_mm(p.astype(bf16), v_ref[:, sl].astype(bf16)))
    o = jnp.concatenate(outs, axis=-1)
    o_ref[...] = h + _mm(o.astype(bf16), wo_ref[...])


def _mem_prompt(h, g, wq, mem_kv, wo, tm, name):
    M, D = h.shape
    HD = MEM_HEADS * MEM_DH
    ML = mem_kv.shape[0]
    return pl.pallas_call(
        _mem_prompt_kernel,
        grid=(M // tm,),
        in_specs=[pl.BlockSpec((tm, D), lambda i: (i, 0)),
                  pl.BlockSpec((1, D), lambda i: (0, 0)),
                  pl.BlockSpec((D, HD), lambda i: (0, 0)),
                  pl.BlockSpec((ML, HD), lambda i: (0, 0)),
                  pl.BlockSpec((ML, HD), lambda i: (0, 1)),
                  pl.BlockSpec((HD, D), lambda i: (0, 0))],
        out_specs=pl.BlockSpec((tm, D), lambda i: (i, 0)),
        out_shape=jax.ShapeDtypeStruct((M, D), f32),
        compiler_params=_cparams("parallel"),
        name=name,
    )(h, g.reshape(1, D), wq, mem_kv, mem_kv, wo)


def _mem_sample_kernel(q_ref, kv_ref, o_ref):
    per_tok = 2 * MEM_HEADS
    ML = kv_ref.shape[0] // per_tok
    q = q_ref[...] * (MEM_DH ** -0.5)
    outs = []
    for hd in range(MEM_HEADS):
        sl = slice(hd * MEM_DH, (hd + 1) * MEM_DH)
        k = kv_ref[pl.ds(hd, ML, stride=per_tok), :]
        v = kv_ref[pl.ds(MEM_HEADS + hd, ML, stride=per_tok), :]
        s = jnp.sum(k * q[:, sl], axis=-1, keepdims=True)
        e = jnp.exp(s - jnp.max(s, axis=0, keepdims=True))
        p = e / jnp.sum(e, axis=0, keepdims=True)
        outs.append(jnp.sum(p * v, axis=0, keepdims=True))
    o_ref[...] = jnp.concatenate(outs, axis=-1)


def _mem_sample(q, kv, name):
    B, HD = q.shape
    rows = kv.shape[1]
    out = pl.pallas_call(
        _mem_sample_kernel,
        grid=(B,),
        in_specs=[pl.BlockSpec((None, 1, HD), lambda b: (b, 0, 0)),
                  pl.BlockSpec((None, rows, MEM_DH), lambda b: (b, 0, 0))],
        out_specs=pl.BlockSpec((None, 1, HD), lambda b: (b, 0, 0)),
        out_shape=jax.ShapeDtypeStruct((B, 1, HD), f32),
        compiler_params=_cparams("parallel"),
        name=name,
    )(q.reshape(B, 1, HD), kv)
    return out.reshape(B, HD)


def _blockdiag_weights(w):
    per = LANES // ML_QKV_BLOCK
    nchunk = w.shape[0] // per
    wc = w.reshape(nchunk, per, ML_QKV_BLOCK, ML_QKV_BLOCK)
    eye = jnp.eye(per, dtype=w.dtype)
    full = wc[:, :, :, None, :] * eye[None, :, None, :, None]
    return full.reshape(nchunk, LANES, LANES)


def _split_weight(w):
    return jnp.stack(_split2(w))


def _dot3_pre(a_parts, w_hi, w_lo, mm=_mm):
    ah, al = a_parts
    return mm(ah, w_hi) + mm(al, w_hi) + mm(ah, w_lo)


def _ml_qkv_gates(xc, xm, wq_ref, wk_ref, wv_ref, wg_ref, bg_ref):
    qs, ks, vs = [], [], []
    for c in range(ML_DIM // LANES):
        sl = slice(c * LANES, (c + 1) * LANES)
        xc_parts = _split2(xc[:, sl])
        qs.append(_dot3_pre(xc_parts, wq_ref[0, c], wq_ref[1, c]))
        ks.append(_dot3_pre(xc_parts, wk_ref[0, c], wk_ref[1, c]) * (ML_DH ** -0.5))
        vs.append(_dot3_pre(_split2(xm[:, sl]), wv_ref[0, c], wv_ref[1, c]))
    q = jnp.concatenate(qs, axis=-1)
    k = jnp.concatenate(ks, axis=-1)
    v = jnp.concatenate(vs, axis=-1)
    qkv_parts = _split2(jnp.concatenate([q, k, v], axis=-1))
    gates = _dot3_pre(qkv_parts, wg_ref[0], wg_ref[1]) + bg_ref[...]
    return q, k, v, gates, qkv_parts


def _mlstm_prompt_kernel(xm_ref, og_ref, wconv_ref, bconv_ref, wq_ref, wk_ref, wv_ref, wg_ref, wgt_ref,
                         bg_ref, bgt_ref, gh_ref, o_ref, c_ref, n_ref, m_ref, prev_sc, *, L):
    H, D = ML_HEADS, ML_DH

    @pl.when(pl.program_id(0) == 0)
    def _():
        prev_sc[...] = jnp.zeros_like(prev_sc)
        c_ref[...] = jnp.zeros_like(c_ref)
        n_ref[...] = jnp.zeros_like(n_ref)
        m_ref[...] = jnp.full(m_ref.shape, NEG, f32)

    x = xm_ref[...]
    full = jnp.concatenate([prev_sc[...], x], axis=0)
    y = bconv_ref[...]
    for j in range(ML_CONV):
        off = SUBLANES - (ML_CONV - 1) + j
        y = y + full[off:off + L] * wconv_ref[j:j + 1, :]
    prev_sc[...] = x[L - SUBLANES:L]
    xc = y * _sigmoid(y)

    q, k, v, gates, (qkv_h, qkv_l) = _ml_qkv_gates(xc, x, wq_ref, wk_ref, wv_ref, wg_ref, bg_ref)
    gates_t = (_mm_nt(wgt_ref[0], qkv_h) + _mm_nt(wgt_ref[0], qkv_l) + _mm_nt(wgt_ref[1], qkv_h)
               + bgt_ref[...])
    ig_c = gates[:, 0:H]
    lf_c = _log_sigmoid(gates[:, H:2 * H])
    ig_r = gates_t[0:H, :]
    lf_r = _log_sigmoid(gates_t[H:2 * H, :])

    t_i = lax.broadcasted_iota(i32, (L, L), 0)
    s_i = lax.broadcasted_iota(i32, (L, L), 1)
    causal = s_i <= t_i
    tri = jnp.where(causal, 1.0, 0.0).astype(bf16)
    b_c = _dot_sel_l(tri, lf_c)
    tri_u = jnp.where(t_i <= s_i, 1.0, 0.0).astype(bf16)
    b_r = _dot_sel_r(lf_r, tri_u)

    for h in range(H):
        sl = slice(h * D, (h + 1) * D)
        qh, kh, vh = q[:, sl], k[:, sl], v[:, sl]
        bc = b_c[:, h:h + 1]
        m_prev = m_ref[h:h + 1, 0:1]
        d_in = jnp.where(causal, bc - b_r[h:h + 1, :] + ig_r[h:h + 1, :], NEG)
        d_x = bc + m_prev
        m_t = jnp.maximum(d_x, jnp.max(d_in, axis=-1, keepdims=True))
        w_in = jnp.exp(d_in - m_t)
        w_x = jnp.exp(d_x - m_t)
        qb = qh.astype(bf16)
        kb = kh.astype(bf16)
        vb = vh.astype(bf16)
        s = _mm_nt(qb, kb) * w_in
        c_old = c_ref[h]
        n_old = n_ref[h:h + 1, :]
        num = _mm(s.astype(bf16), vb) + w_x * _mm(qb, c_old.astype(bf16))
        den = jnp.sum(s, axis=-1, keepdims=True) + w_x * jnp.sum(qh * n_old, axis=-1, keepdims=True)
        hh = num / jnp.maximum(jnp.abs(den), jnp.exp(-m_t))
        m_new = m_t[L - 1:L, :]
        b_last = bc[L - 1:L, :]
        g_x = jnp.exp(b_last + m_prev - m_new)
        g_s = jnp.exp(b_last - bc + ig_c[:, h:h + 1] - m_new)
        ks_ = kh * g_s
        c_ref[h] = g_x * c_old + _mm(ks_.T.astype(bf16), vb)
        n_ref[h:h + 1, :] = g_x * n_old + jnp.sum(ks_, axis=0, keepdims=True)
        m_ref[h:h + 1, :] = jnp.broadcast_to(m_new, (1, LANES))
        hn = hh * lax.rsqrt(jnp.mean(hh * hh, axis=-1, keepdims=True) + EPS) * gh_ref[:, sl]
        o_ref[:, sl] = _sigmoid(og_ref[:, sl]) * hn


def _mlstm_prompt(u, w_conv, b_conv, wbd_q, wbd_k, wbd_v, w_gate, w_gate_t, b_gate, g_head, L, name):
    T = u.shape[0]
    H, D = ML_HEADS, ML_DH
    nch = ML_DIM // LANES
    full2 = lambda shape: pl.BlockSpec(shape, lambda i: (0,) * len(shape))
    out, c, n, m = pl.pallas_call(
        functools.partial(_mlstm_prompt_kernel, L=L),
        grid=(T // L,),
        in_specs=[pl.BlockSpec((L, ML_DIM), lambda i: (i, 0)),
                  pl.BlockSpec((L, ML_DIM), lambda i: (i, 1)),
                  full2((ML_CONV, ML_DIM)), full2((1, ML_DIM)),
                  full2((2, nch, LANES, LANES)), full2((2, nch, LANES, LANES)), full2((2, nch, LANES, LANES)),
                  full2((2, 3 * ML_DIM, 2 * H)), full2((2, 2 * H, 3 * ML_DIM)),
                  full2((1, 2 * H)), full2((2 * H, 1)), full2((1, ML_DIM))],
        out_specs=[pl.BlockSpec((L, ML_DIM), lambda i: (i, 0)),
                   full2((H, D, D)), full2((H, D)), full2((H, LANES))],
        out_shape=[jax.ShapeDtypeStruct((T, ML_DIM), f32),
                   jax.ShapeDtypeStruct((H, D, D), f32),
                   jax.ShapeDtypeStruct((H, D), f32),
                   jax.ShapeDtypeStruct((H, LANES), f32)],
        scratch_shapes=[pltpu.VMEM((SUBLANES, ML_DIM), f32)],
        compiler_params=_cparams("arbitrary"),
        name=name,
    )(u, u, w_conv, b_conv.reshape(1, ML_DIM), wbd_q, wbd_k, wbd_v, w_gate, w_gate_t,
      b_gate.reshape(1, 2 * H), b_gate.reshape(2 * H, 1), g_head.reshape(1, ML_DIM))
    return out, c, n, m[:, 0]


def _mlstm_sample_pre_kernel(xm_ref, s0_ref, s1_ref, s2_ref, wconv_ref, bconv_ref, wq_ref, wk_ref, wv_ref,
                             wg_ref, bg_ref, q_ref, k_ref, v_ref, g_ref):
    x = xm_ref[...]
    y = (bconv_ref[...] + s0_ref[...] * wconv_ref[0:1, :] + s1_ref[...] * wconv_ref[1:2, :]
         + s2_ref[...] * wconv_ref[2:3, :] + x * wconv_ref[3:4, :])
    xc = y * _sigmoid(y)
    q, k, v, gates, _ = _ml_qkv_gates(xc, x, wq_ref, wk_ref, wv_ref, wg_ref, bg_ref)
    q_ref[...] = q
    k_ref[...] = k
    v_ref[...] = v
    g_ref[...] = gates


def _mlstm_sample_step_kernel(qc_ref, kc_ref, q_ref, k_ref, v_ref, gt_ref, og_ref, gh_ref, c_ref, n_ref, m_ref,
                              o_ref, cn_ref, nn_ref, mn_ref):
    H, D = ML_HEADS, ML_DH
    for b in range(qc_ref.shape[0]):
        gates = gt_ref[b]
        ig = gates[:, 0:H]
        lf = _log_sigmoid(gates[:, H:2 * H])
        m_old = m_ref[b]
        m_new = jnp.maximum(lf + m_old, ig)
        w_in = jnp.exp(ig - m_new)
        w_x = jnp.exp(lf + m_old - m_new)
        mn_ref[b] = m_new
        e_m = jnp.exp(-m_new)
        og = og_ref[b]
        outs = []
        for h in range(H):
            sl = slice(h * D, (h + 1) * D)
            qr, kr, vr = q_ref[b, h:h + 1, :], k_ref[b, h:h + 1, :], v_ref[b, h:h + 1, :]
            qcol, kcol = qc_ref[b, :, h:h + 1], kc_ref[b, :, h:h + 1]
            wi, wx = w_in[:, h:h + 1], w_x[:, h:h + 1]
            c_old = c_ref[b, h]
            n_old = n_ref[b, h:h + 1, :]
            s = jnp.sum(qr * kr, axis=-1, keepdims=True) * wi
            num = s * vr + wx * jnp.sum(c_old * qcol, axis=0, keepdims=True)
            den = s + wx * jnp.sum(n_old * qr, axis=-1, keepdims=True)
            hh = num / jnp.maximum(jnp.abs(den), e_m[:, h:h + 1])
            cn_ref[b, h] = wx * c_old + wi * (kcol * vr)
            nn_ref[b, h:h + 1, :] = wx * n_old + wi * kr
            hn = hh * lax.rsqrt(jnp.mean(hh * hh, axis=-1, keepdims=True) + EPS) * gh_ref[:, sl]
            outs.append(_sigmoid(og[:, sl]) * hn)
        o_ref[b] = jnp.concatenate(outs, axis=-1)


def _mlstm_sample(xm, og, conv_state, c0, n0, m0, w_conv, b_conv, wbd_q, wbd_k, wbd_v, w_gate, b_gate, g_head):
    B = xm.shape[0]
    H, D = ML_HEADS, ML_DH
    sds = lambda *s: jax.ShapeDtypeStruct(s, f32)
    q, k, v, gates = pl.pallas_call(
        _mlstm_sample_pre_kernel,
        out_shape=[sds(B, ML_DIM), sds(B, ML_DIM), sds(B, ML_DIM), sds(B, 2 * H)],
        compiler_params=pltpu.CompilerParams(vmem_limit_bytes=VMEM_LIMIT),
        name="mlstm_sample_pre",
    )(xm, conv_state[:, 0], conv_state[:, 1], conv_state[:, 2], w_conv, b_conv.reshape(1, ML_DIM),
      wbd_q, wbd_k, wbd_v, w_gate, b_gate.reshape(1, 2 * H))
    q3, k3, v3 = (a.reshape(B, H, D) for a in (q, k, v))
    S = 4 if B % 4 == 0 else 1
    per_b = lambda *s: pl.BlockSpec((S,) + s, lambda b: (b,) + (0,) * len(s))
    out, c, n, m = pl.pallas_call(
        _mlstm_sample_step_kernel,
        grid=(B // S,),
        in_specs=[per_b(D, H), per_b(D, H), per_b(H, D), per_b(H, D), per_b(H, D), per_b(1, 2 * H),
                  per_b(1, ML_DIM), pl.BlockSpec((1, ML_DIM), lambda b: (0, 0)),
                  per_b(H, D, D), per_b(H, D), per_b(1, H)],
        out_specs=[per_b(1, ML_DIM), per_b(H, D, D), per_b(H, D), per_b(1, H)],
        out_shape=[sds(B, 1, ML_DIM), sds(B, H, D, D), sds(B, H, D), sds(B, 1, H)],
        compiler_params=_cparams("parallel"),
        name="mlstm_sample_step",
    )(q3.transpose(0, 2, 1), k3.transpose(0, 2, 1), q3, k3, v3, gates.reshape(B, 1, 2 * H),
      og.reshape(B, 1, ML_DIM), g_head.reshape(1, ML_DIM), c0, n0, m0.reshape(B, 1, H))
    return out.reshape(B, ML_DIM), c, n, m.reshape(B, H)


def _compress_kernel(x_ref, pe_ref, wp_ref, wphi_ref, o_ref, f0_sc, f1_sc, mn_sc):
    step = pl.program_id(0)
    sub = x_ref.shape[0] // CMP_STRIDE
    x3 = x_ref[...].reshape(sub, CMP_STRIDE, KV_ROW)
    base = pl.multiple_of(step * sub, sub)
    for o, sc in ((0, f0_sc), (1, f1_sc)):
        y = x3 + pe_ref[o][None]
        sc[pl.ds(base, sub), :] = jnp.sum(y * _sigmoid(y) * wp_ref[o][None], axis=1)
    mn_sc[pl.ds(base, sub), :] = jnp.sum(x3, axis=1) * (1.0 / CMP_STRIDE)

    @pl.when(step == pl.num_programs(0) - 1)
    def _():
        ns = f0_sc.shape[0]
        feat = f0_sc[...] + pltpu.roll(f1_sc[...], ns - 1, axis=0)
        mn = mn_sc[...]
        pooled = (mn + pltpu.roll(mn, ns - 1, axis=0)) * (CMP_STRIDE / CMP_BLOCK)
        for c in range(2):
            for g in range(NSA_KV_HEADS):
                sl = slice((c * NSA_KV_HEADS + g) * NSA_DH, (c * NSA_KV_HEADS + g + 1) * NSA_DH)
                o_ref[:, sl] = pooled[:, sl] + _dot3(feat[:, sl], wphi_ref[c])


def _compress_tables(pe, wpos):
    def lay(a):
        r = CMP_BLOCK // CMP_STRIDE
        a = a.reshape(2, r, CMP_STRIDE, NSA_DH).transpose(1, 2, 0, 3)
        a = jnp.broadcast_to(a[:, :, :, None, :], (r, CMP_STRIDE, 2, NSA_KV_HEADS, NSA_DH))
        return a.reshape(r, CMP_STRIDE, KV_ROW)
    return lay(pe), lay(wpos)


def _compress_prompt(u, col_block, pe_t, wp_t, wphi, rows=512):
    T = u.shape[0]
    n_sub = T // CMP_STRIDE
    const = lambda shape: pl.BlockSpec(shape, lambda s: (0,) * len(shape))
    return pl.pallas_call(
        _compress_kernel,
        grid=(T // rows,),
        in_specs=[pl.BlockSpec((rows, KV_ROW), lambda s: (s, col_block)),
                  const(pe_t.shape), const(wp_t.shape), const(wphi.shape)],
        out_specs=const((n_sub, KV_ROW)),
        out_shape=jax.ShapeDtypeStruct((n_sub, KV_ROW), f32),
        scratch_shapes=[pltpu.VMEM((n_sub, KV_ROW), f32)] * 3,
        compiler_params=_cparams("arbitrary"),
        name="compress_prompt",
    )(u, pe_t, wp_t, wphi)


def _compress_paged_kernel(pt_ref, *refs, n_pages):
    pages = refs[:n_pages]
    pe_ref, wp_ref, wphi_ref, o_ref, f0_sc, f1_sc, mn_sc = refs[n_pages:]
    step = pl.program_id(1)
    sub = PAGE_SIZE // CMP_STRIDE
    tiles = CMP_STRIDE * KV_CHUNKS // SUBLANES
    packed = 2 * SUBLANES
    ptiles = CMP_STRIDE * KV_CHUNKS // packed
    out_rows = sub * SUBLANES
    for p in range(n_pages):
        x = pages[p][...]
        base = pl.multiple_of((step * n_pages + p) * out_rows, out_rows)
        xh = (x.astype(bf16) * 0.5).reshape(sub, ptiles, packed, LANES)
        for o, sc in ((0, f0_sc), (1, f1_sc)):
            y = xh + pe_ref[o][None]
            b = y * wp_ref[o][None]
            z = b + b * jnp.tanh(y)
            z = (z[:, 0] + z[:, 1]) + (z[:, 2] + z[:, 3])
            zf = z.astype(f32).reshape(sub, packed // SUBLANES, SUBLANES, LANES)
            sc[pl.ds(base, out_rows), :] = jnp.sum(zf, axis=1).reshape(out_rows, LANES)
        mean = jnp.sum(x.reshape(sub, tiles, SUBLANES, LANES), axis=1) * (1.0 / CMP_STRIDE)
        mn_sc[pl.ds(base, out_rows), :] = mean.reshape(out_rows, LANES)

    @pl.when(step == pl.num_programs(1) - 1)
    def _():
        ns = f0_sc.shape[0] // SUBLANES
        for c in range(KV_CHUNKS):
            col = lambda sc: (sc[pl.ds(c, ns, stride=SUBLANES), :]
                              + sc[pl.ds(KV_CHUNKS + c, ns, stride=SUBLANES), :])
            feat = col(f0_sc) + pltpu.roll(col(f1_sc), ns - 1, axis=0)
            mn = col(mn_sc)
            pooled = (mn + pltpu.roll(mn, ns - 1, axis=0)) * (CMP_STRIDE / CMP_BLOCK)
            o_ref[:, c * LANES:(c + 1) * LANES] = pooled + _dot3(feat, wphi_ref[c // NSA_KV_HEADS])


def _compress_paged(pool, page_table, pe_t, wp_t, wphi, pages_per_step=16):
    B, n_pages = page_table.shape
    P = pages_per_step
    n_sub = n_pages * PAGE_SIZE // CMP_STRIDE
    r = CMP_BLOCK // CMP_STRIDE
    packed = 2 * SUBLANES
    ptiles = CMP_STRIDE * KV_CHUNKS // packed
    pe4 = (0.5 * pe_t).astype(bf16).reshape(r, ptiles, packed, LANES)
    wp4 = wp_t.astype(bf16).reshape(r, ptiles, packed, LANES)
    page_rows = PAGE_SIZE * KV_CHUNKS
    specs = [pl.BlockSpec((None, page_rows, LANES),
                          functools.partial(lambda b, s, pt, p: (pt[b * n_pages + s * P + p], 0, 0), p=p))
             for p in range(P)]
    const = lambda shape: pl.BlockSpec(shape, lambda *a: (0,) * len(shape))
    gs = pltpu.PrefetchScalarGridSpec(
        num_scalar_prefetch=1, grid=(B, n_pages // P),
        in_specs=specs + [const(pe4.shape), const(wp4.shape), const(wphi.shape)],
        out_specs=pl.BlockSpec((None, n_sub, KV_ROW), lambda b, s, pt: (b, 0, 0)),
        scratch_shapes=[pltpu.VMEM((n_sub * SUBLANES, LANES), f32)] * 3)
    return pl.pallas_call(
        functools.partial(_compress_paged_kernel, n_pages=P),
        grid_spec=gs,
        out_shape=jax.ShapeDtypeStruct((B, n_sub, KV_ROW), f32),
        compiler_params=_cparams("parallel", "arbitrary"),
        name="compress_paged",
    )(page_table.reshape(-1), *([pool] * P), pe4, wp4, wphi)


def _kv_prep_kernel(ks_ref, kw_ref, ksk_ref, ksv_ref, kwk_ref, kwv_ref, kn2_ref):
    rows = ks_ref.shape[0]
    ks = ks_ref[...]
    kw = kw_ref[...]
    r = pl.program_id(0) * rows + lax.broadcasted_iota(i32, (rows, LANES), 0)
    n = lax.broadcasted_iota(i32, (rows, LANES), 1)
    onehot = jnp.where(_div_pow2(r, SEL_BLOCK) == n, 1.0, 0.0).astype(bf16)
    ones_col = jnp.where(n == 0, 1.0, 0.0).astype(bf16)

    @pl.when(pl.program_id(0) == 0)
    def _():
        kn2_ref[...] = jnp.zeros_like(kn2_ref)

    for g in range(NSA_KV_HEADS):
        ksl = slice(g * NSA_DH, (g + 1) * NSA_DH)
        vsl = slice(NSA_KV_W + g * NSA_DH, NSA_KV_W + (g + 1) * NSA_DH)
        kb = ks[:, ksl].astype(bf16)
        ksk_ref[g, :, 0:NSA_DH] = kb
        ksk_ref[g, :, NSA_DH:NSA_DH + LANES] = onehot
        ksv_ref[g, :, 0:NSA_DH] = ks[:, vsl].astype(bf16)
        ksv_ref[g, :, NSA_DH:NSA_DH + LANES] = ones_col
        kwk_ref[g] = kw[:, ksl].astype(bf16)
        kwv_ref[g] = kw[:, vsl].astype(bf16)
        kf = kb.astype(f32)
        n2 = jnp.max(jnp.sum(kf * kf, axis=-1, keepdims=True), axis=0, keepdims=True)
        kn2_ref[g] = jnp.maximum(kn2_ref[g], jnp.broadcast_to(n2, kn2_ref.shape[1:]))


def _kv_prep(u, ks_col_block, kw_col_block, rows=512):
    T = u.shape[0]
    G = NSA_KV_HEADS
    assert (T - 1) // SEL_BLOCK + 1 <= LANES
    sd = lambda w: jax.ShapeDtypeStruct((G, T, w), bf16)
    ospec = lambda w: pl.BlockSpec((G, rows, w), lambda i: (0, i, 0))
    return pl.pallas_call(
        _kv_prep_kernel,
        grid=(T // rows,),
        in_specs=[pl.BlockSpec((rows, KV_ROW), lambda i: (i, ks_col_block)),
                  pl.BlockSpec((rows, KV_ROW), lambda i: (i, kw_col_block))],
        out_specs=[ospec(NSA_DH + LANES), ospec(NSA_DH + LANES), ospec(NSA_DH), ospec(NSA_DH),
                   pl.BlockSpec((G, SUBLANES, LANES), lambda i: (0, 0, 0))],
        out_shape=[sd(NSA_DH + LANES), sd(NSA_DH + LANES), sd(NSA_DH), sd(NSA_DH),
                   jax.ShapeDtypeStruct((G, SUBLANES, LANES), f32)],
        compiler_params=_cparams("arbitrary"),
        name="kv_prep",
    )(u, u)


def _top_blocks(score, n_top, axis=1):
    lane = lax.broadcasted_iota(i32, score.shape, axis).astype(f32)
    width = float(score.shape[axis])
    work = score
    sel = jnp.zeros(score.shape, dtype=jnp.bool_)
    firsts = []
    for _ in range(n_top):
        mx = jnp.max(work, axis=axis, keepdims=True)
        first = jnp.min(jnp.where(work == mx, lane, width), axis=axis, keepdims=True)
        hit = lane == first
        sel = jnp.logical_or(sel, hit)
        work = jnp.where(hit, REMOVED, work)
        firsts.append(first)
    return sel, firsts


def _nsa_prompt_kernel(q_ref, gt_ref, kck_ref, kcv_ref, ksk_ref, ksv_ref, kwk_ref, kwv_ref, kn2_ref, o_ref,
                       shift_sc, m_sc, acc_sc, *, n_sel):
    QB, R, D = Q_BLOCK, NSA_GROUP, NSA_DH
    rows = R * QB
    qb = pl.program_id(1)
    q = q_ref[...]
    qs = jnp.concatenate([q[:, r * D:(r + 1) * D] for r in range(R)], axis=0) * (D ** -0.5)
    qs_b = qs.astype(bf16)
    pos = qb * QB + lax.broadcasted_iota(i32, (QB, 1), 0)

    def head_bias(valid):
        return jnp.concatenate([jnp.where(valid, 0.0, NEG)] * R, axis=0)

    ns = kck_ref.shape[0]
    s = _mm_nt(qs_b, kck_ref[...].astype(bf16))
    j = lax.broadcasted_iota(i32, (1, ns), 1)
    p_c = _biased_softmax_rows(s, head_bias(j * CMP_STRIDE + (CMP_BLOCK - 1) <= pos))
    o_c = _mm(p_c.astype(bf16), kcv_ref[...].astype(bf16))

    imp = p_c[0:QB]
    for r in range(1, R):
        imp = imp + p_c[r * QB:(r + 1) * QB]
    ratio = SEL_BLOCK // CMP_STRIDE
    off = CMP_BLOCK // CMP_STRIDE - 1
    nn = lax.broadcasted_iota(i32, (LANES, ns), 0)
    jj = lax.broadcasted_iota(i32, (LANES, ns), 1)
    overlap_t = jnp.where((jj >= ratio * nn - off) & (jj < ratio * nn + ratio), 1.0, 0.0).astype(bf16)
    imp_sel = sum(_mm_nt(overlap_t, part) for part in _split3(imp))
    n_idx = lax.broadcasted_iota(i32, (LANES, QB), 0)
    cur = _div_pow2(qb * QB + lax.broadcasted_iota(i32, (LANES, QB), 1), SEL_BLOCK)
    forced = (n_idx == 0) | (n_idx == cur) | (n_idx == cur - 1)
    score = jnp.where(forced, FORCE, jnp.where(n_idx <= cur, imp_sel, NEG))
    score = jnp.where(n_idx < n_sel, score, REMOVED)
    sel, _ = _top_blocks(score, min(SEL_TOPN, n_sel), axis=0)
    bias = jnp.where(sel & (n_idx <= cur), 0.0, MASK_BIAS).T

    KT = SEL_KT
    last = (qb * QB + QB - 1) // KT
    q2_b = (qs * LOG2E).astype(bf16)
    q2 = q2_b.astype(f32)
    bound = jnp.sqrt(jnp.sum(q2 * q2, axis=-1, keepdims=True) * kn2_ref[0:1, 0:1]) * 1.01 + 1e-3
    k_self = ksk_ref[pl.ds(pl.multiple_of(qb * QB, QB), QB), 0:D].astype(f32)
    s_self = jnp.concatenate([jnp.sum(q2[r * QB:(r + 1) * QB] * k_self, axis=-1, keepdims=True) for r in range(R)],
                             axis=0)
    shift_sc[...] = bound
    diag_bias = head_bias(last * KT + lax.broadcasted_iota(i32, (1, KT), 1) <= pos)

    def shifted_queries(shift):
        return jnp.concatenate(
            [jnp.concatenate([q2_b[r * QB:(r + 1) * QB], (bias - shift[r * QB:(r + 1) * QB]).astype(bf16)], axis=1)
             for r in range(R)], axis=0)

    def key_tile(ref, kt):
        return ref[pl.ds(pl.multiple_of(kt * KT, KT), KT), :]

    @pl.when(jnp.max(bound - s_self) > SAFE_SPAN)
    def _():
        qp0 = shifted_queries(jnp.zeros_like(bound))

        def lane_max(kt, sk):
            mx = m_sc[...]
            for c in range(KT // LANES):
                mx = jnp.maximum(mx, sk[:, c * LANES:(c + 1) * LANES])
            m_sc[...] = mx

        def max_pass(kt, carry):
            lane_max(kt, _mm_nt(qp0, key_tile(ksk_ref, kt)))
            return carry

        m_sc[...] = jnp.full(m_sc.shape, NEG, f32)
        lax.fori_loop(0, last, max_pass, 0)
        lane_max(last, _mm_nt(qp0, key_tile(ksk_ref, last)) + diag_bias)
        shift_sc[...] = jnp.max(m_sc[...], axis=-1, keepdims=True)

    qp = shifted_queries(shift_sc[...])
    acc_sc[...] = jnp.zeros_like(acc_sc)

    def accumulate(kt, s2):
        acc_sc[...] += _mm(jnp.exp2(s2).astype(bf16), key_tile(ksv_ref, kt))

    def body(kt, carry):
        accumulate(kt, _mm_nt(qp, key_tile(ksk_ref, kt)))
        return carry

    lax.fori_loop(0, last, body, 0)
    accumulate(last, _mm_nt(qp, key_tile(ksk_ref, last)) + diag_bias)
    acc = acc_sc[...]
    o_s = acc[:, 0:D] * (1.0 / acc[:, D:D + 1])

    nband = WINDOW // QB + 1
    wlen = nband * QB
    wstart = pl.multiple_of(jnp.maximum(qb - (nband - 1), 0) * QB, QB)
    sw = _mm_nt(qs_b, kwk_ref[pl.ds(wstart, wlen), :])
    diff = pos - (wstart + lax.broadcasted_iota(i32, (1, wlen), 1))
    p_w = _biased_softmax_rows(sw, head_bias((diff >= 0) & (diff <= WINDOW)))
    o_w = _mm(p_w.astype(bf16), kwv_ref[pl.ds(wstart, wlen), :])

    gate = _sigmoid(gt_ref[...])
    for r in range(R):
        rs = slice(r * QB, (r + 1) * QB)
        o_ref[:, r * D:(r + 1) * D] = (gate[:, r:r + 1] * o_c[rs] + gate[:, R + r:R + r + 1] * o_s[rs]
                                       + gate[:, 2 * R + r:2 * R + r + 1] * o_w[rs])


def _nsa_prompt(u, q_col_block, gates_g, kvc, ksk, ksv, kwk, kwv, kn2):
    T = u.shape[0]
    G, R, D, QB = NSA_KV_HEADS, NSA_GROUP, NSA_DH, Q_BLOCK
    ns = kvc.shape[0]
    n_sel = (T - 1) // SEL_BLOCK + 1
    assert T % SEL_KT == 0 and T >= (WINDOW // QB + 1) * QB and n_sel <= LANES
    rows = R * QB
    res = lambda w: pl.BlockSpec((None, T, w), lambda g, i: (g, 0, 0))
    return pl.pallas_call(
        functools.partial(_nsa_prompt_kernel, n_sel=n_sel),
        grid=(G, T // QB),
        in_specs=[pl.BlockSpec((QB, R * D), lambda g, i: (i, q_col_block + g)),
                  pl.BlockSpec((None, QB, LANES), lambda g, i: (g, i, 0)),
                  pl.BlockSpec((ns, D), lambda g, i: (0, g)),
                  pl.BlockSpec((ns, D), lambda g, i: (0, G + g)),
                  res(D + LANES), res(D + LANES), res(D), res(D),
                  pl.BlockSpec((None, SUBLANES, LANES), lambda g, i: (g, 0, 0))],
        out_specs=pl.BlockSpec((QB, R * D), lambda g, i: (i, g)),
        out_shape=jax.ShapeDtypeStruct((T, NSA_DIM), f32),
        scratch_shapes=[pltpu.VMEM((rows, 1), f32), pltpu.VMEM((rows, LANES), f32),
                        pltpu.VMEM((rows, D + LANES), f32)],
        compiler_params=_cparams("parallel", "arbitrary"),
        name="nsa_prompt",
    )(u, gates_g, kvc, kvc, ksk, ksv, kwk, kwv, kn2)


def _group_gates(gt):
    T = gt.shape[0]
    G, R = NSA_KV_HEADS, NSA_GROUP
    g = gt.reshape(T, 3, G, R).transpose(2, 0, 1, 3).reshape(G, T, 3 * R)
    return jnp.pad(g, ((0, 0), (0, 0), (0, LANES - 3 * R)))


def _nsa_sample_cmp_kernel(q_ref, kvc_ref, oc_ref, idx_ref, *, pos, n_sel, sel_w):
    H, R, D, G = NSA_HEADS, NSA_GROUP, NSA_DH, NSA_KV_HEADS
    ns = kvc_ref.shape[0]
    qs_b = (q_ref[...] * (D ** -0.5)).astype(bf16)
    head = lax.broadcasted_iota(i32, (H, 1), 0)
    grp = _div_pow2(head, R)
    s = jnp.zeros((H, ns), f32)
    for g in range(G):
        s = jnp.where(grp == g, _mm_nt(qs_b, kvc_ref[:, g * D:(g + 1) * D].astype(bf16)), s)
    j = lax.broadcasted_iota(i32, (1, ns), 1)
    e, den = _masked_softmax_rows(s, j * CMP_STRIDE + (CMP_BLOCK - 1) <= pos)
    p_c = e / den
    p_b = p_c.astype(bf16)
    o_c = jnp.zeros((H, D), f32)
    imp = jnp.zeros((H, ns), f32)
    for g in range(G):
        o_c = jnp.where(grp == g, _mm(p_b, kvc_ref[:, (G + g) * D:(G + g + 1) * D].astype(bf16)), o_c)
        imp = jnp.where(grp == g, jnp.sum(jnp.where(grp == g, p_c, 0.0), axis=0, keepdims=True), imp)
    oc_ref[...] = o_c
    ratio = SEL_BLOCK // CMP_STRIDE
    off = CMP_BLOCK // CMP_STRIDE - 1
    nn = lax.broadcasted_iota(i32, (sel_w, ns), 0)
    jj = lax.broadcasted_iota(i32, (sel_w, ns), 1)
    overlap_t = jnp.where((jj >= ratio * nn - off) & (jj < ratio * nn + ratio), 1.0, 0.0).astype(bf16)
    imp_sel = sum(_mm_nt(overlap_t, part) for part in _split3(imp))
    n_idx = lax.broadcasted_iota(i32, (sel_w, H), 0)
    cur = pos // SEL_BLOCK
    forced = (n_idx == 0) | (n_idx == cur) | (n_idx == cur - 1)
    score = jnp.where(forced, FORCE, jnp.where(n_idx <= cur, imp_sel, NEG))
    score = jnp.where(n_idx < n_sel, score, REMOVED)
    _, firsts = _top_blocks(score, SEL_TOPN, axis=0)
    idx_ref[...] = jnp.concatenate(firsts, axis=0).astype(i32)


def _nsa_sample_cmp(q, kvc, pos):
    B, H, D = q.shape
    ns = kvc.shape[1]
    n_sel = pos // SEL_BLOCK + 1
    assert n_sel >= SEL_TOPN
    sel_w = -(-n_sel // SUBLANES) * SUBLANES
    o_c, idx = pl.pallas_call(
        functools.partial(_nsa_sample_cmp_kernel, pos=pos, n_sel=n_sel, sel_w=sel_w),
        grid=(B,),
        in_specs=[pl.BlockSpec((None, H, D), lambda b: (b, 0, 0)),
                  pl.BlockSpec((None, ns, KV_ROW), lambda b: (b, 0, 0))],
        out_specs=[pl.BlockSpec((None, H, D), lambda b: (b, 0, 0)),
                   pl.BlockSpec((None, SEL_TOPN, H), lambda b: (b, 0, 0))],
        out_shape=[jax.ShapeDtypeStruct((B, H, D), f32), jax.ShapeDtypeStruct((B, SEL_TOPN, H), i32)],
        compiler_params=_cparams("parallel"),
        name="nsa_sample_cmp",
    )(q, kvc)
    return o_c, idx[:, :, ::NSA_GROUP].transpose(0, 2, 1)


def _nsa_sample_sel_kernel(idx_ref, pt_ref, q_ref, new_ref, *refs, past):
    H, R, D, G = NSA_HEADS, NSA_GROUP, NSA_DH, NSA_KV_HEADS
    blocks, o_ref = refs[:G * SEL_TOPN], refs[G * SEL_TOPN]
    b = pl.program_id(0)
    qs_b = (q_ref[...] * (D ** -0.5)).astype(bf16)
    grp = _div_pow2(lax.broadcasted_iota(i32, (H, 1), 0), R)
    per_tile = SUBLANES // KV_CHUNKS
    half = SEL_BLOCK // per_tile
    o = jnp.zeros((H, D), f32)
    for g in range(G):
        ks, vs = [], []
        lane = lax.broadcasted_iota(i32, (1, SEL_TOPN * SEL_BLOCK), 1)
        slot = _div_pow2(lane, SEL_BLOCK)
        in_slot = lane & (SEL_BLOCK - 1)
        pk_row = (in_slot & (half - 1)) * per_tile + _div_pow2(in_slot, half)
        for i in range(SEL_TOPN):
            blk = blocks[g * SEL_TOPN + i]
            base = idx_ref[(b * G + g) * SEL_TOPN + i] * SEL_BLOCK
            pk_row = pk_row + jnp.where(slot == i, base, 0)
            for par in range(per_tile):
                pk_col = base + per_tile * lax.broadcasted_iota(i32, (half, 1), 0) + par
                old = pk_col < past
                k = blk[pl.ds(par * KV_CHUNKS + g, half, stride=SUBLANES), :]
                v = blk[pl.ds(par * KV_CHUNKS + G + g, half, stride=SUBLANES), :]
                ks.append(jnp.where(old, k, new_ref[g:g + 1, :]).astype(bf16))
                vs.append(jnp.where(old, v, new_ref[G + g:G + g + 1, :]).astype(bf16))
        s = _mm_nt(qs_b, jnp.concatenate(ks, axis=0))
        e, den = _masked_softmax_rows(s, pk_row <= past)
        o = jnp.where(grp == g, _mm((e / den).astype(bf16), jnp.concatenate(vs, axis=0)), o)
    o_ref[...] = o


def _nsa_sample_sel(q, ks_new, pool, page_table, idx, past):
    B, H, D = q.shape
    G = NSA_KV_HEADS
    n_pages = page_table.shape[1]
    halves = PAGE_SIZE // SEL_BLOCK
    blk_rows = SEL_BLOCK * KV_CHUNKS
    pool_h = pool.reshape(pool.shape[0] * halves, blk_rows, D)
    last_old = past // SEL_BLOCK - 1

    def blk_map(b, idx_r, pt_r, g, i):
        blk = jnp.minimum(idx_r[(b * G + g) * SEL_TOPN + i], last_old)
        return (pt_r[b * n_pages + blk // halves] * halves + blk % halves, 0, 0)

    specs = [pl.BlockSpec((None, blk_rows, D), functools.partial(blk_map, g=g, i=i))
             for g in range(G) for i in range(SEL_TOPN)]
    gs = pltpu.PrefetchScalarGridSpec(
        num_scalar_prefetch=2, grid=(B,),
        in_specs=[pl.BlockSpec((None, H, D), lambda b, *_: (b, 0, 0)),
                  pl.BlockSpec((None, KV_CHUNKS, D), lambda b, *_: (b, 0, 0))] + specs,
        out_specs=pl.BlockSpec((None, H, D), lambda b, *_: (b, 0, 0)))
    return pl.pallas_call(
        functools.partial(_nsa_sample_sel_kernel, past=past),
        grid_spec=gs,
        out_shape=jax.ShapeDtypeStruct((B, H, D), f32),
        compiler_params=_cparams("arbitrary"),
        name="nsa_sample_sel",
    )(idx.reshape(-1), page_table.reshape(-1), q, ks_new, *([pool_h] * (G * SEL_TOPN)))


def _nsa_sample_win_kernel(q_ref, buf_ref, new_ref, oc_ref, os_ref, gt_ref, o_ref, win_ref, *, past):
    H, R, D, G = NSA_HEADS, NSA_GROUP, NSA_DH, NSA_KV_HEADS
    rows = buf_ref.shape[0]
    Lb = rows // KV_CHUNKS
    per_tile = SUBLANES // KV_CHUNKS
    half = Lb // per_tile
    qs = q_ref[...] * (D ** -0.5)
    qs_b = qs.astype(bf16)
    grp = _div_pow2(lax.broadcasted_iota(i32, (H, 1), 0), R)
    lane = lax.broadcasted_iota(i32, (1, Lb), 1)
    key_pos = past - Lb + (lane & (half - 1)) * per_tile + _div_pow2(lane, half)
    diff = past - key_pos
    valid = (diff >= 0) & (diff <= WINDOW)
    o_w = jnp.zeros((H, D), f32)
    for g in range(G):
        chunk = lambda c: jnp.concatenate(
            [buf_ref[pl.ds(par * KV_CHUNKS + c, half, stride=SUBLANES), :] for par in range(per_tile)], axis=0)
        new_k, new_v = new_ref[g:g + 1, :], new_ref[G + g:G + g + 1, :]
        s_b = jnp.where(valid, _mm_nt(qs_b, chunk(g).astype(bf16)), NEG)
        s_n = jnp.sum(qs * new_k, axis=-1, keepdims=True)
        mx = jnp.maximum(jnp.max(s_b, axis=-1, keepdims=True), s_n)
        e_b = jnp.where(valid, jnp.exp(s_b - mx), 0.0)
        e_n = jnp.exp(s_n - mx)
        den = jnp.sum(e_b, axis=-1, keepdims=True) + e_n
        og = _mm((e_b / den).astype(bf16), chunk(G + g).astype(bf16)) + (e_n / den) * new_v
        o_w = jnp.where(grp == g, og, o_w)
    gate = _sigmoid(gt_ref[...])
    o_ref[...] = gate[:, 0:1] * oc_ref[...] + gate[:, 1:2] * os_ref[...] + gate[:, 2:3] * o_w
    win_ref[0:rows - KV_CHUNKS, :] = buf_ref[KV_CHUNKS:rows, :]
    win_ref[rows - KV_CHUNKS:rows, :] = new_ref[...]


def _nsa_sample_win(q, win_buf, kw_new, o_c, o_s, gt, past):
    B, H, D = q.shape
    rows = win_buf.shape[1]
    assert rows == WINDOW * KV_CHUNKS
    per_b = lambda *s: pl.BlockSpec((None,) + s, lambda b: (b,) + (0,) * len(s))
    return pl.pallas_call(
        functools.partial(_nsa_sample_win_kernel, past=past),
        grid=(B,),
        in_specs=[per_b(H, D), per_b(rows, D), per_b(KV_CHUNKS, D), per_b(H, D), per_b(H, D), per_b(H, 3)],
        out_specs=[per_b(H, D), per_b(rows, D)],
        out_shape=[jax.ShapeDtypeStruct((B, H, D), f32), jax.ShapeDtypeStruct((B, rows, D), f32)],
        compiler_params=_cparams("parallel"),
        name="nsa_sample_win",
    )(q, win_buf, kw_new, o_c, o_s, gt.reshape(B, 3, H).transpose(0, 2, 1))


def kernel(x_prompt, x_sample, cache_cmp_kv, cache_sel_kv, cache_win_kv, cache_mem_kv, state_mlstm_c, state_mlstm_n, state_mlstm_m, state_conv, page_table, mem_prompt, g_mix, w_in, w_conv, b_conv, w_mq, w_mk, w_mv, w_mgate, b_mgate, g_mhead, cmp_pe, cmp_wpos, cmp_wphi, w_out, g_memx, g_mems, w_mem_q, w_mem_kv, w_mem_o, g_ffn, w_ff1, w_ff2, g_final):
    depth = w_in.shape[0]
    assert depth == 1 and x_prompt.shape[0] == 1 and x_sample.shape[1] == 1
    T, Dm = x_prompt.shape[1:]
    B = x_sample.shape[0]
    G, Dh, H = NSA_KV_HEADS, NSA_DH, ML_HEADS
    past = page_table.shape[1] * PAGE_SIZE
    assert (past + 1) // CMP_STRIDE == past // CMP_STRIDE
    n_main = 2 * ML_DIM + NSA_DIM + 3 * KV_ROW
    n_gate = 3 * NSA_HEADS
    l = 0
    hp = x_prompt.reshape(T, Dm)
    hs = x_sample.reshape(B, Dm)

    wbd_q, wbd_k, wbd_v = (_split_weight(_blockdiag_weights(w[l])) for w in (w_mq, w_mk, w_mv))
    w_mg, w_mg_t = _split_weight(w_mgate[l]), _split_weight(w_mgate[l].T)
    pe_t, wp_t = _compress_tables(cmp_pe[l], cmp_wpos[l])
    wb_in, wb_out, wb_mq, wb_mkv, wb_mo, wb_f1, wb_f2 = (
        w[l].astype(bf16) for w in (w_in, w_out, w_mem_q, w_mem_kv, w_mem_o, w_ff1, w_ff2))
    w_gt = jnp.pad(wb_in[:, n_main:], ((0, 0), (0, LANES - n_gate)))
    q_cb = 2 * ML_DIM // KV_ROW
    c_cb = (2 * ML_DIM + NSA_DIM) // KV_ROW
    Dff = w_ff1.shape[-1]
    n_mq, n_mkv = w_mem_q.shape[-1], w_mem_kv.shape[-1]
    tm_in, tm, tn, tf = 1024, 512, 512, 1024

    u = _norm_matmul(hp, g_mix[l], wb_in, n_main, tm_in, tn, "in_proj_p")
    gt = _norm_matmul(hp, g_mix[l], w_gt, LANES, tm_in, LANES, "in_gate_p")[:, :n_gate]
    ml_o, c_p, n_p, m_p = _mlstm_prompt(u, w_conv[l], b_conv[l], wbd_q, wbd_k, wbd_v, w_mg, w_mg_t, b_mgate[l],
                                        g_mhead[l], 128, "mlstm_prompt")
    kvc_p = _compress_prompt(u, c_cb, pe_t, wp_t, cmp_wphi[l])
    nsa_o = _nsa_prompt(u, q_cb, _group_gates(gt), kvc_p, *_kv_prep(u, c_cb + 1, c_cb + 2))
    h1 = _out_proj(ml_o, nsa_o, wb_out, hp, tm, Dm, "out_proj_p")
    mem_kv = _norm_matmul(mem_prompt.reshape(-1, Dm), g_mems[l], wb_mkv, n_mkv, mem_prompt.shape[1], tn, "mem_kv")
    h2 = _mem_prompt(h1, g_memx[l], wb_mq, mem_kv, wb_mo, tm, "mem_attn_p")
    y_p = _ffn_final(h2, g_ffn[l], wb_f1, wb_f2, g_final, tm, tf, "ffn_p")

    kv_shape = lambda a: a.reshape(1, 1, a.shape[0], 2, G, Dh)
    kvw_p = u[:, (c_cb + 2) * KV_ROW:(c_cb + 3) * KV_ROW]
    out_p = (y_p.reshape(1, T, Dm),
             kv_shape(u[:, c_cb * KV_ROW:(c_cb + 1) * KV_ROW]),
             kv_shape(u[:, (c_cb + 1) * KV_ROW:(c_cb + 2) * KV_ROW]),
             kv_shape(kvw_p[T - min(WINDOW, T):]),
             mem_kv.reshape(1, 1, -1, 2, MEM_HEADS, MEM_DH),
             c_p[None, None], n_p[None, None], m_p[None, None],
             u[T - (ML_CONV - 1):, :ML_DIM][None, None])

    us = _norm_matmul(hs, g_mix[l], wb_in, n_main, B, tn, "in_proj_s")
    gts = _norm_matmul(hs, g_mix[l], w_gt, LANES, B, LANES, "in_gate_s")[:, :n_gate]
    xm_s, og_s = us[:, :ML_DIM], us[:, ML_DIM:2 * ML_DIM]
    q_s = us[:, 2 * ML_DIM:2 * ML_DIM + NSA_DIM].reshape(B, NSA_HEADS, Dh)
    kc_s, ks_s, kw_s = (us[:, (c_cb + i) * KV_ROW:(c_cb + i + 1) * KV_ROW] for i in range(3))
    ml_os, c_s, n_s, m_s = _mlstm_sample(xm_s, og_s, state_conv[l], state_mlstm_c[l], state_mlstm_n[l],
                                         state_mlstm_m[l], w_conv[l], b_conv[l], wbd_q, wbd_k, wbd_v,
                                         w_mg, b_mgate[l], g_mhead[l])
    pool_c = cache_cmp_kv.reshape(-1, PAGE_SIZE * KV_CHUNKS, Dh)
    pool_s = cache_sel_kv.reshape(-1, PAGE_SIZE * KV_CHUNKS, Dh)
    win_buf = cache_win_kv.reshape(B, -1, Dh)
    mem_buf = cache_mem_kv.reshape(B, -1, MEM_DH)
    kvc_s = _compress_paged(pool_c, page_table, pe_t, wp_t, cmp_wphi[l])
    o_cs, idx = _nsa_sample_cmp(q_s, kvc_s, past)
    o_ss = _nsa_sample_sel(q_s, ks_s.reshape(B, KV_CHUNKS, Dh), pool_s, page_table, idx, past)
    nsa_os, win_new = _nsa_sample_win(q_s, win_buf, kw_s.reshape(B, KV_CHUNKS, Dh), o_cs, o_ss, gts, past)
    h1s = _out_proj(ml_os, nsa_os.reshape(B, NSA_DIM), wb_out, hs, B, Dm, "out_proj_s")
    qm_s = _norm_matmul(h1s, g_memx[l], wb_mq, n_mq, B, n_mq, "mem_q_s")
    om_s = _mem_sample(qm_s, mem_buf, "mem_attn_s")
    h2s = _matmul_res(om_s, wb_mo, h1s, B, Dm, "mem_o_s")
    y_s = _ffn_final(h2s, g_ffn[l], wb_f1, wb_f2, g_final, B, tf, "ffn_s")

    kv_s_shape = lambda a: a.reshape(1, B, 1, 2, G, Dh)
    conv_s = jnp.concatenate([state_conv[l][:, 1:], xm_s[:, None, :]], axis=1)
    out_s = (y_s.reshape(B, 1, Dm), kv_s_shape(kc_s), kv_s_shape(ks_s),
             win_new.reshape(1, B, -1, 2, G, Dh), c_s[None], n_s[None], m_s[None], conv_s[None])

    return (out_p[0], out_s[0]) + out_p[1:] + out_s[1:]
```

```python
import functools

import jax
import jax.numpy as jnp
import numpy as np
from jax import lax
from jax.experimental import pallas as pl
from jax.experimental.pallas import tpu as pltpu

f32 = jnp.float32
bf16 = jnp.bfloat16
i32 = jnp.int32

EPS = 1e-6
NEG = -1e30
FORCE = 1e30
ML_HEADS = 8
ML_DH = 128
ML_DIM = ML_HEADS * ML_DH
ML_CONV = 4
ML_QKV_BLOCK = 4
NSA_HEADS = 8
NSA_KV_HEADS = 2
NSA_GROUP = NSA_HEADS // NSA_KV_HEADS
NSA_DH = 128
NSA_DIM = NSA_HEADS * NSA_DH
NSA_KV_W = NSA_KV_HEADS * NSA_DH
CMP_BLOCK = 32
CMP_STRIDE = 16
SEL_BLOCK = 64
SEL_TOPN = 16
WINDOW = 512
Q_BLOCK = 128
PAGE_SIZE = 128
MEM_HEADS = 4
MEM_DH = 128
KV_ROW = 2 * NSA_KV_W
KV_CHUNKS = KV_ROW // NSA_DH

LANES = 128
SUBLANES = 8
VMEM_LIMIT = 56 * 1024 * 1024

MASK_BIAS = -1e9
REMOVED = -3.0e38
SEL_KT = 1024
LOG2E = 1.4426950408889634
SAFE_SPAN = 100.0


def _cparams(*sem):
    return pltpu.CompilerParams(dimension_semantics=sem, vmem_limit_bytes=VMEM_LIMIT)


def _mm(a, b):
    return jnp.dot(a, b, preferred_element_type=f32)


def _mm_nt(a, b):
    return lax.dot_general(a, b, (((1,), (1,)), ((), ())), preferred_element_type=f32)


def _split2(x):
    h = x.astype(bf16)
    return h, (x - h.astype(f32)).astype(bf16)


def _split3(x):
    h = x.astype(bf16)
    r = x - h.astype(f32)
    m = r.astype(bf16)
    return h, m, (r - m.astype(f32)).astype(bf16)


def _dot3(a, b, mm=_mm):
    ah, al = _split2(a)
    bh, bl = _split2(b)
    return mm(ah, bh) + mm(al, bh) + mm(ah, bl)


def _dot_sel_l(sel, x):
    h, m, l = _split3(x)
    return _mm(sel, h) + _mm(sel, m) + _mm(sel, l)


def _dot_sel_r(x, sel):
    h, m, l = _split3(x)
    return _mm(h, sel) + _mm(m, sel) + _mm(l, sel)


def _rms(x, g):
    return x * lax.rsqrt(jnp.mean(x * x, axis=-1, keepdims=True) + EPS) * g


def _div_pow2(x, d):
    assert d & (d - 1) == 0
    return lax.shift_right_logical(x, jnp.full(x.shape, d.bit_length() - 1, x.dtype))


def _sigmoid(x):
    return 1.0 / (1.0 + jnp.exp(-x))


def _log_sigmoid(x):
    return jnp.minimum(x, 0.0) - jnp.log(1.0 + jnp.exp(-jnp.abs(x)))


def _masked_softmax_rows(s, valid):
    s = jnp.where(valid, s, NEG)
    mx = jnp.max(s, axis=-1, keepdims=True)
    e = jnp.where(valid, jnp.exp(s - mx), 0.0)
    den = jnp.sum(e, axis=-1, keepdims=True)
    return e, jnp.where(den > 0.0, den, 1.0)


def _biased_softmax_rows(s, bias):
    s = s + bias
    mx = jnp.max(s, axis=-1, keepdims=True)
    e = jnp.exp(s - mx)
    inv = jnp.where(mx > 0.5 * NEG, 1.0 / jnp.sum(e, axis=-1, keepdims=True), 0.0)
    return e * inv


def _norm_matmul_kernel(x_ref, g_ref, w_ref, o_ref, xn_ref):
    @pl.when(pl.program_id(1) == 0)
    def _():
        xn_ref[...] = _rms(x_ref[...], g_ref[...]).astype(bf16)

    o_ref[...] = _mm(xn_ref[...], w_ref[...])


def _norm_matmul(x, g, w, n_cols, tm, tn, name):
    M, K = x.shape
    return pl.pallas_call(
        _norm_matmul_kernel,
        grid=(M // tm, n_cols // tn),
        in_specs=[pl.BlockSpec((tm, K), lambda i, j: (i, 0)),
                  pl.BlockSpec((1, K), lambda i, j: (0, 0)),
                  pl.BlockSpec((K, tn), lambda i, j: (0, j))],
        out_specs=pl.BlockSpec((tm, tn), lambda i, j: (i, j)),
        out_shape=jax.ShapeDtypeStruct((M, n_cols), f32),
        scratch_shapes=[pltpu.VMEM((tm, K), bf16)],
        compiler_params=_cparams("parallel", "arbitrary"),
        name=name,
    )(x, g.reshape(1, K), w)


def _out_proj_kernel(a1_ref, a2_ref, w1_ref, w2_ref, r_ref, o_ref):
    o_ref[...] = (r_ref[...] + _mm(a1_ref[...].astype(bf16), w1_ref[...])
                  + _mm(a2_ref[...].astype(bf16), w2_ref[...]))


def _out_proj(a1, a2, w, res, tm, tn, name):
    M, K1 = a1.shape
    K2 = a2.shape[1]
    assert K1 == K2 and w.shape[0] == K1 + K2
    N = w.shape[1]
    return pl.pallas_call(
        _out_proj_kernel,
        grid=(M // tm, N // tn),
        in_specs=[pl.BlockSpec((tm, K1), lambda i, j: (i, 0)),
                  pl.BlockSpec((tm, K2), lambda i, j: (i, 0)),
                  pl.BlockSpec((K1, tn), lambda i, j: (0, j)),
                  pl.BlockSpec((K2, tn), lambda i, j: (1, j)),
                  pl.BlockSpec((tm, tn), lambda i, j: (i, j))],
        out_specs=pl.BlockSpec((tm, tn), lambda i, j: (i, j)),
        out_shape=jax.ShapeDtypeStruct((M, N), f32),
        compiler_params=_cparams("parallel", "arbitrary"),
        name=name,
    )(a1, a2, w, w, res)


def _matmul_res_kernel(a_ref, w_ref, r_ref, o_ref):
    o_ref[...] = r_ref[...] + _mm(a_ref[...].astype(bf16), w_ref[...])


def _matmul_res(a, w, res, tm, tn, name):
    M, K = a.shape
    N = w.shape[1]
    return pl.pallas_call(
        _matmul_res_kernel,
        grid=(M // tm, N // tn),
        in_specs=[pl.BlockSpec((tm, K), lambda i, j: (i, 0)),
                  pl.BlockSpec((K, tn), lambda i, j: (0, j)),
                  pl.BlockSpec((tm, tn), lambda i, j: (i, j))],
        out_specs=pl.BlockSpec((tm, tn), lambda i, j: (i, j)),
        out_shape=jax.ShapeDtypeStruct((M, N), f32),
        compiler_params=_cparams("parallel", "arbitrary"),
        name=name,
    )(a, w, res)


def _ffn_kernel(h_ref, g_ref, w1_ref, w2_ref, gf_ref, y_ref, xn_ref, acc_ref):
    f = pl.program_id(1)

    @pl.when(f == 0)
    def _():
        xn_ref[...] = _rms(h_ref[...], g_ref[...]).astype(bf16)
        acc_ref[...] = jnp.zeros_like(acc_ref)

    a = _mm(xn_ref[...], w1_ref[...])
    a = jnp.square(jnp.maximum(a, 0.0))
    acc_ref[...] += _mm(a.astype(bf16), w2_ref[...])

    @pl.when(f == pl.num_programs(1) - 1)
    def _():
        y_ref[...] = _rms(h_ref[...] + acc_ref[...], gf_ref[...])


def _ffn_final(h, g, w1, w2, g_final, tm, tf, name):
    M, D = h.shape
    F = w1.shape[1]
    return pl.pallas_call(
        _ffn_kernel,
        grid=(M // tm, F // tf),
        in_specs=[pl.BlockSpec((tm, D), lambda i, j: (i, 0)),
                  pl.BlockSpec((1, D), lambda i, j: (0, 0)),
                  pl.BlockSpec((D, tf), lambda i, j: (0, j)),
                  pl.BlockSpec((tf, D), lambda i, j: (j, 0)),
                  pl.BlockSpec((1, D), lambda i, j: (0, 0))],
        out_specs=pl.BlockSpec((tm, D), lambda i, j: (i, 0)),
        out_shape=jax.ShapeDtypeStruct((M, D), f32),
        scratch_shapes=[pltpu.VMEM((tm, D), bf16), pltpu.VMEM((tm, D), f32)],
        compiler_params=_cparams("parallel", "arbitrary"),
        name=name,
    )(h, g.reshape(1, D), w1, w2, g_final.reshape(1, D))


def _mem_prompt_kernel(h_ref, g_ref, wq_ref, k_ref, v_ref, wo_ref, o_ref):
    h = h_ref[...]
    xn = _rms(h, g_ref[...]).astype(bf16)
    q = _mm(xn, wq_ref[...]) * (MEM_DH ** -0.5)
    outs = []
    for hd in range(MEM_HEADS):
        sl = slice(hd * MEM_DH, (hd + 1) * MEM_DH)
        s = _mm_nt(q[:, sl].astype(bf16), k_ref[:, sl].astype(bf16))
        e = jnp.exp(s - jnp.max(s, axis=-1, keepdims=True))
        p = e / jnp.sum(e, axis=-1, keepdims=True)
        outs.append(_mm(p.astype(bf16), v_ref[:, sl].astype(bf16)))
    o = jnp.concatenate(outs, axis=-1)
    o_ref[...] = h + _mm(o.astype(bf16), wo_ref[...])


def _mem_prompt(h, g, wq, mem_kv, wo, tm, name):
    M, D = h.shape
    HD = MEM_HEADS * MEM_DH
    ML = mem_kv.shape[0]
    return pl.pallas_call(
        _mem_prompt_kernel,
        grid=(M // tm,),
        in_specs=[pl.BlockSpec((tm, D), lambda i: (i, 0)),
                  pl.BlockSpec((1, D), lambda i: (0, 0)),
                  pl.BlockSpec((D, HD), lambda i: (0, 0)),
                  pl.BlockSpec((ML, HD), lambda i: (0, 0)),
                  pl.BlockSpec((ML, HD), lambda i: (0, 1)),
                  pl.BlockSpec((HD, D), lambda i: (0, 0))],
        out_specs=pl.BlockSpec((tm, D), lambda i: (i, 0)),
        out_shape=jax.ShapeDtypeStruct((M, D), f32),
        compiler_params=_cparams("parallel"),
        name=name,
    )(h, g.reshape(1, D), wq, mem_kv, mem_kv, wo)


def _mem_sample_kernel(q_ref, kv_ref, o_ref):
    per_tok = 2 * MEM_HEADS
    ML = kv_ref.shape[0] // per_tok
    q = q_ref[...] * (MEM_DH ** -0.5)
    outs = []
    for hd in range(MEM_HEADS):
        sl = slice(hd * MEM_DH, (hd + 1) * MEM_DH)
        k = kv_ref[pl.ds(hd, ML, stride=per_tok), :]
        v = kv_ref[pl.ds(MEM_HEADS + hd, ML, stride=per_tok), :]
        s = jnp.sum(k * q[:, sl], axis=-1, keepdims=True)
        e = jnp.exp(s - jnp.max(s, axis=0, keepdims=True))
        p = e / jnp.sum(e, axis=0, keepdims=True)
        outs.append(jnp.sum(p * v, axis=0, keepdims=True))
    o_ref[...] = jnp.concatenate(outs, axis=-1)


def _mem_sample(q, kv, name):
    B, HD = q.shape
    rows = kv.shape[1]
    out = pl.pallas_call(
        _mem_sample_kernel,
        grid=(B,),
        in_specs=[pl.BlockSpec((None, 1, HD), lambda b: (b, 0, 0)),
                  pl.BlockSpec((None, rows, MEM_DH), lambda b: (b, 0, 0))],
        out_specs=pl.BlockSpec((None, 1, HD), lambda b: (b, 0, 0)),
        out_shape=jax.ShapeDtypeStruct((B, 1, HD), f32),
        compiler_params=_cparams("parallel"),
        name=name,
    )(q.reshape(B, 1, HD), kv)
    return out.reshape(B, HD)


def _blockdiag_weights(w):
    n = w.shape[0]
    per = LANES // ML_QKV_BLOCK
    nchunk = w.shape[1] // per
    wc = w.reshape(n, nchunk, per, ML_QKV_BLOCK, ML_QKV_BLOCK)
    eye = jnp.eye(per, dtype=w.dtype)
    full = wc[:, :, :, :, None, :] * eye[None, None, :, None, :, None]
    return full.reshape(n, nchunk, LANES, LANES)


def _ml_qkv_gates(xc, xm, wbd_ref, wg_ref, bg_ref):
    xc_b, xm_b = xc.astype(bf16), xm.astype(bf16)
    qs, ks, vs = [], [], []
    for c in range(ML_DIM // LANES):
        sl = slice(c * LANES, (c + 1) * LANES)
        qs.append(_mm(xc_b[:, sl], wbd_ref[0, c]))
        ks.append(_mm(xc_b[:, sl], wbd_ref[1, c]) * (ML_DH ** -0.5))
        vs.append(_mm(xm_b[:, sl], wbd_ref[2, c]))
    q = jnp.concatenate(qs, axis=-1)
    k = jnp.concatenate(ks, axis=-1)
    v = jnp.concatenate(vs, axis=-1)
    qkv_b = jnp.concatenate([q, k, v], axis=-1).astype(bf16)
    gates = _mm(qkv_b, wg_ref[...]) + bg_ref[...]
    return q, k, v, gates, qkv_b


def _mlstm_prompt_kernel(xm_ref, og_ref, wconv_ref, bconv_ref, wbd_ref, wg_ref, wgt_ref,
                         bg_ref, bgt_ref, gh_ref, o_ref, c_ref, n_ref, m_ref, prev_sc, *, L):
    H, D = ML_HEADS, ML_DH

    @pl.when(pl.program_id(0) == 0)
    def _():
        prev_sc[...] = jnp.zeros_like(prev_sc)
        c_ref[...] = jnp.zeros_like(c_ref)
        n_ref[...] = jnp.zeros_like(n_ref)
        m_ref[...] = jnp.full(m_ref.shape, NEG, f32)

    x = xm_ref[...]
    full = jnp.concatenate([prev_sc[...], x], axis=0)
    y = bconv_ref[...]
    for j in range(ML_CONV):
        off = SUBLANES - (ML_CONV - 1) + j
        y = y + full[off:off + L] * wconv_ref[j:j + 1, :]
    prev_sc[...] = x[L - SUBLANES:L]
    xc = y * _sigmoid(y)

    q, k, v, gates, qkv_b = _ml_qkv_gates(xc, x, wbd_ref, wg_ref, bg_ref)
    gates_t = _mm_nt(wgt_ref[...], qkv_b) + bgt_ref[...]
    ig_c = gates[:, 0:H]
    lf_c = _log_sigmoid(gates[:, H:2 * H])
    ig_r = gates_t[0:H, :]
    lf_r = _log_sigmoid(gates_t[H:2 * H, :])

    t_i = lax.broadcasted_iota(i32, (L, L), 0)
    s_i = lax.broadcasted_iota(i32, (L, L), 1)
    causal = s_i <= t_i
    tri = jnp.where(causal, 1.0, 0.0).astype(bf16)
    b_c = _dot_sel_l(tri, lf_c)
    tri_u = jnp.where(t_i <= s_i, 1.0, 0.0).astype(bf16)
    b_r = _dot_sel_r(lf_r, tri_u)

    for h in range(H):
        sl = slice(h * D, (h + 1) * D)
        qh, kh, vh = q[:, sl], k[:, sl], v[:, sl]
        bc = b_c[:, h:h + 1]
        m_prev = m_ref[h:h + 1, 0:1]
        d_in = jnp.where(causal, bc - b_r[h:h + 1, :] + ig_r[h:h + 1, :], NEG)
        d_x = bc + m_prev
        m_t = jnp.maximum(d_x, jnp.max(d_in, axis=-1, keepdims=True))
        w_in = jnp.exp(d_in - m_t)
        w_x = jnp.exp(d_x - m_t)
        qb = qh.astype(bf16)
        kb = kh.astype(bf16)
        vb = vh.astype(bf16)
        s = _mm_nt(qb, kb) * w_in
        c_old = c_ref[h]
        n_old = n_ref[h:h + 1, :]
        num = _mm(s.astype(bf16), vb) + w_x * _mm(qb, c_old.astype(bf16))
        den = jnp.sum(s, axis=-1, keepdims=True) + w_x * jnp.sum(qh * n_old, axis=-1, keepdims=True)
        hh = num / jnp.maximum(jnp.abs(den), jnp.exp(-m_t))
        m_new = m_t[L - 1:L, :]
        b_last = bc[L - 1:L, :]
        g_x = jnp.exp(b_last + m_prev - m_new)
        g_s = jnp.exp(b_last - bc + ig_c[:, h:h + 1] - m_new)
        ks_ = kh * g_s
        c_ref[h] = g_x * c_old + _mm(ks_.T.astype(bf16), vb)
        n_ref[h:h + 1, :] = g_x * n_old + jnp.sum(ks_, axis=0, keepdims=True)
        m_ref[h:h + 1, :] = jnp.broadcast_to(m_new, (1, LANES))
        hn = hh * lax.rsqrt(jnp.mean(hh * hh, axis=-1, keepdims=True) + EPS) * gh_ref[:, sl]
        o_ref[:, sl] = _sigmoid(og_ref[:, sl]) * hn


def _mlstm_prompt(u, w_conv, b_conv, wbd, w_gate, w_gate_t, b_gate, g_head, L, name):
    T = u.shape[0]
    H, D = ML_HEADS, ML_DH
    nch = ML_DIM // LANES
    full2 = lambda shape: pl.BlockSpec(shape, lambda i: (0,) * len(shape))
    out, c, n, m = pl.pallas_call(
        functools.partial(_mlstm_prompt_kernel, L=L),
        grid=(T // L,),
        in_specs=[pl.BlockSpec((L, ML_DIM), lambda i: (i, 0)),
                  pl.BlockSpec((L, ML_DIM), lambda i: (i, 1)),
                  full2((ML_CONV, ML_DIM)), full2((1, ML_DIM)),
                  full2((3, nch, LANES, LANES)),
                  full2((3 * ML_DIM, 2 * H)), full2((2 * H, 3 * ML_DIM)),
                  full2((1, 2 * H)), full2((2 * H, 1)), full2((1, ML_DIM))],
        out_specs=[pl.BlockSpec((L, ML_DIM), lambda i: (i, 0)),
                   full2((H, D, D)), full2((H, D)), full2((H, LANES))],
        out_shape=[jax.ShapeDtypeStruct((T, ML_DIM), f32),
                   jax.ShapeDtypeStruct((H, D, D), f32),
                   jax.ShapeDtypeStruct((H, D), f32),
                   jax.ShapeDtypeStruct((H, LANES), f32)],
        scratch_shapes=[pltpu.VMEM((SUBLANES, ML_DIM), f32)],
        compiler_params=_cparams("arbitrary"),
        name=name,
    )(u, u, w_conv, b_conv.reshape(1, ML_DIM), wbd, w_gate, w_gate_t,
      b_gate.reshape(1, 2 * H), b_gate.reshape(2 * H, 1), g_head.reshape(1, ML_DIM))
    return out, c, n, m[:, 0]


def _mlstm_sample_pre_kernel(xm_ref, s0_ref, s1_ref, s2_ref, wconv_ref, bconv_ref, wbd_ref,
                             wg_ref, bg_ref, q_ref, k_ref, v_ref, g_ref):
    x = xm_ref[...]
    y = (bconv_ref[...] + s0_ref[...] * wconv_ref[0:1, :] + s1_ref[...] * wconv_ref[1:2, :]
         + s2_ref[...] * wconv_ref[2:3, :] + x * wconv_ref[3:4, :])
    xc = y * _sigmoid(y)
    q, k, v, gates, _ = _ml_qkv_gates(xc, x, wbd_ref, wg_ref, bg_ref)
    q_ref[...] = q
    k_ref[...] = k
    v_ref[...] = v
    g_ref[...] = gates


def _mlstm_sample_step_kernel(qc_ref, kc_ref, q_ref, k_ref, v_ref, gt_ref, og_ref, gh_ref, c_ref, n_ref, m_ref,
                              o_ref, cn_ref, nn_ref, mn_ref):
    H, D = ML_HEADS, ML_DH
    for b in range(qc_ref.shape[0]):
        gates = gt_ref[b]
        ig = gates[:, 0:H]
        lf = _log_sigmoid(gates[:, H:2 * H])
        m_old = m_ref[b]
        m_new = jnp.maximum(lf + m_old, ig)
        w_in = jnp.exp(ig - m_new)
        w_x = jnp.exp(lf + m_old - m_new)
        mn_ref[b] = m_new
        e_m = jnp.exp(-m_new)
        og = og_ref[b]
        outs = []
        for h in range(H):
            sl = slice(h * D, (h + 1) * D)
            qr, kr, vr = q_ref[b, h:h + 1, :], k_ref[b, h:h + 1, :], v_ref[b, h:h + 1, :]
            qcol, kcol = qc_ref[b, :, h:h + 1], kc_ref[b, :, h:h + 1]
            wi, wx = w_in[:, h:h + 1], w_x[:, h:h + 1]
            c_old = c_ref[b, h]
            n_old = n_ref[b, h:h + 1, :]
            s = jnp.sum(qr * kr, axis=-1, keepdims=True) * wi
            num = s * vr + wx * jnp.sum(c_old * qcol, axis=0, keepdims=True)
            den = s + wx * jnp.sum(n_old * qr, axis=-1, keepdims=True)
            hh = num / jnp.maximum(jnp.abs(den), e_m[:, h:h + 1])
            cn_ref[b, h] = wx * c_old + wi * (kcol * vr)
            nn_ref[b, h:h + 1, :] = wx * n_old + wi * kr
            hn = hh * lax.rsqrt(jnp.mean(hh * hh, axis=-1, keepdims=True) + EPS) * gh_ref[:, sl]
            outs.append(_sigmoid(og[:, sl]) * hn)
        o_ref[b] = jnp.concatenate(outs, axis=-1)


def _mlstm_sample(xm, og, conv_state, c0, n0, m0, w_conv, b_conv, wbd, w_gate, b_gate, g_head):
    B = xm.shape[0]
    H, D = ML_HEADS, ML_DH
    sds = lambda *s: jax.ShapeDtypeStruct(s, f32)
    q, k, v, gates = pl.pallas_call(
        _mlstm_sample_pre_kernel,
        out_shape=[sds(B, ML_DIM), sds(B, ML_DIM), sds(B, ML_DIM), sds(B, 2 * H)],
        compiler_params=pltpu.CompilerParams(vmem_limit_bytes=VMEM_LIMIT),
        name="mlstm_sample_pre",
    )(xm, conv_state[:, 0], conv_state[:, 1], conv_state[:, 2], w_conv, b_conv.reshape(1, ML_DIM),
      wbd, w_gate, b_gate.reshape(1, 2 * H))
    q3, k3, v3 = (a.reshape(B, H, D) for a in (q, k, v))
    S = 1
    per_b = lambda *s: pl.BlockSpec((S,) + s, lambda b: (b,) + (0,) * len(s))
    out, c, n, m = pl.pallas_call(
        _mlstm_sample_step_kernel,
        grid=(B // S,),
        in_specs=[per_b(D, H), per_b(D, H), per_b(H, D), per_b(H, D), per_b(H, D), per_b(1, 2 * H),
                  per_b(1, ML_DIM), pl.BlockSpec((1, ML_DIM), lambda b: (0, 0)),
                  per_b(H, D, D), per_b(H, D), per_b(1, H)],
        out_specs=[per_b(1, ML_DIM), per_b(H, D, D), per_b(H, D), per_b(1, H)],
        out_shape=[sds(B, 1, ML_DIM), sds(B, H, D, D), sds(B, H, D), sds(B, 1, H)],
        compiler_params=_cparams("parallel"),
        name="mlstm_sample_step",
    )(q3.transpose(0, 2, 1), k3.transpose(0, 2, 1), q3, k3, v3, gates.reshape(B, 1, 2 * H),
      og.reshape(B, 1, ML_DIM), g_head.reshape(1, ML_DIM), c0, n0, m0.reshape(B, 1, H))
    return out.reshape(B, ML_DIM), c, n, m.reshape(B, H)


def _compress_kernel(x_ref, pe_ref, wp_ref, wphi_ref, o_ref, f0_sc, f1_sc, mn_sc):
    step = pl.program_id(0)
    sub = x_ref.shape[0] // CMP_STRIDE
    x3 = x_ref[...].reshape(sub, CMP_STRIDE, KV_ROW)
    base = pl.multiple_of(step * sub, sub)
    for o, sc in ((0, f0_sc), (1, f1_sc)):
        y = x3 + pe_ref[o][None]
        sc[pl.ds(base, sub), :] = jnp.sum(y * _sigmoid(y) * wp_ref[o][None], axis=1)
    mn_sc[pl.ds(base, sub), :] = jnp.sum(x3, axis=1) * (1.0 / CMP_STRIDE)

    @pl.when(step == pl.num_programs(0) - 1)
    def _():
        ns = f0_sc.shape[0]
        feat = f0_sc[...] + pltpu.roll(f1_sc[...], ns - 1, axis=0)
        mn = mn_sc[...]
        pooled = (mn + pltpu.roll(mn, ns - 1, axis=0)) * (CMP_STRIDE / CMP_BLOCK)
        for c in range(2):
            for g in range(NSA_KV_HEADS):
                sl = slice((c * NSA_KV_HEADS + g) * NSA_DH, (c * NSA_KV_HEADS + g + 1) * NSA_DH)
                o_ref[:, sl] = pooled[:, sl] + _dot3(feat[:, sl], wphi_ref[c])


def _compress_tables(pe, wpos):
    def lay(a):
        r = CMP_BLOCK // CMP_STRIDE
        a = a.reshape(2, r, CMP_STRIDE, NSA_DH).transpose(1, 2, 0, 3)
        a = jnp.broadcast_to(a[:, :, :, None, :], (r, CMP_STRIDE, 2, NSA_KV_HEADS, NSA_DH))
        return a.reshape(r, CMP_STRIDE, KV_ROW)
    return lay(pe), lay(wpos)


def _compress_prompt(u, col_block, pe_t, wp_t, wphi, rows=512):
    T = u.shape[0]
    n_sub = T // CMP_STRIDE
    const = lambda shape: pl.BlockSpec(shape, lambda s: (0,) * len(shape))
    return pl.pallas_call(
        _compress_kernel,
        grid=(T // rows,),
        in_specs=[pl.BlockSpec((rows, KV_ROW), lambda s: (s, col_block)),
                  const(pe_t.shape), const(wp_t.shape), const(wphi.shape)],
        out_specs=const((n_sub, KV_ROW)),
        out_shape=jax.ShapeDtypeStruct((n_sub, KV_ROW), f32),
        scratch_shapes=[pltpu.VMEM((n_sub, KV_ROW), f32)] * 3,
        compiler_params=_cparams("arbitrary"),
        name="compress_prompt",
    )(u, pe_t, wp_t, wphi)


def _compress_paged_kernel(pt_ref, *refs, n_pages):
    pages = refs[:n_pages]
    pe_ref, wp_ref, wphi_ref, o_ref, f0_sc, f1_sc, mn_sc = refs[n_pages:]
    step = pl.program_id(1)
    sub = PAGE_SIZE // CMP_STRIDE
    tiles = CMP_STRIDE * KV_CHUNKS // SUBLANES
    packed = 2 * SUBLANES
    ptiles = CMP_STRIDE * KV_CHUNKS // packed
    out_rows = sub * SUBLANES
    for p in range(n_pages):
        x = pages[p][...]
        base = pl.multiple_of((step * n_pages + p) * out_rows, out_rows)
        xh = (x.astype(bf16) * 0.5).reshape(sub, ptiles, packed, LANES)
        for o, sc in ((0, f0_sc), (1, f1_sc)):
            y = xh + pe_ref[o][None]
            b = y * wp_ref[o][None]
            z = b + b * jnp.tanh(y)
            z = (z[:, 0] + z[:, 1]) + (z[:, 2] + z[:, 3])
            zf = z.astype(f32).reshape(sub, packed // SUBLANES, SUBLANES, LANES)
            sc[pl.ds(base, out_rows), :] = jnp.sum(zf, axis=1).reshape(out_rows, LANES)
        mean = jnp.sum(x.reshape(sub, tiles, SUBLANES, LANES), axis=1) * (1.0 / CMP_STRIDE)
        mn_sc[pl.ds(base, out_rows), :] = mean.reshape(out_rows, LANES)

    @pl.when(step == pl.num_programs(1) - 1)
    def _():
        ns = f0_sc.shape[0] // SUBLANES
        for c in range(KV_CHUNKS):
            col = lambda sc: (sc[pl.ds(c, ns, stride=SUBLANES), :]
                              + sc[pl.ds(KV_CHUNKS + c, ns, stride=SUBLANES), :])
            feat = col(f0_sc) + pltpu.roll(col(f1_sc), ns - 1, axis=0)
            mn = col(mn_sc)
            pooled = (mn + pltpu.roll(mn, ns - 1, axis=0)) * (CMP_STRIDE / CMP_BLOCK)
            o_ref[:, c * LANES:(c + 1) * LANES] = pooled + _dot3(feat, wphi_ref[c // NSA_KV_HEADS])


def _compress_paged(pool, page_table, pe_t, wp_t, wphi, pages_per_step=16):
    B, n_pages = page_table.shape
    P = pages_per_step
    n_sub = n_pages * PAGE_SIZE // CMP_STRIDE
    r = CMP_BLOCK // CMP_STRIDE
    packed = 2 * SUBLANES
    ptiles = CMP_STRIDE * KV_CHUNKS // packed
    pe4 = (0.5 * pe_t).astype(bf16).reshape(r, ptiles, packed, LANES)
    wp4 = wp_t.astype(bf16).reshape(r, ptiles, packed, LANES)
    page_rows = PAGE_SIZE * KV_CHUNKS
    specs = [pl.BlockSpec((None, page_rows, LANES),
                          functools.partial(lambda b, s, pt, p: (pt[b * n_pages + s * P + p], 0, 0), p=p))
             for p in range(P)]
    const = lambda shape: pl.BlockSpec(shape, lambda *a: (0,) * len(shape))
    gs = pltpu.PrefetchScalarGridSpec(
        num_scalar_prefetch=1, grid=(B, n_pages // P),
        in_specs=specs + [const(pe4.shape), const(wp4.shape), const(wphi.shape)],
        out_specs=pl.BlockSpec((None, n_sub, KV_ROW), lambda b, s, pt: (b, 0, 0)),
        scratch_shapes=[pltpu.VMEM((n_sub * SUBLANES, LANES), f32)] * 3)
    return pl.pallas_call(
        functools.partial(_compress_paged_kernel, n_pages=P),
        grid_spec=gs,
        out_shape=jax.ShapeDtypeStruct((B, n_sub, KV_ROW), f32),
        compiler_params=_cparams("parallel", "arbitrary"),
        name="compress_paged",
    )(page_table.reshape(-1), *([pool] * P), pe4, wp4, wphi)


def _kv_prep_kernel(ks_ref, kw_ref, ksk_ref, ksv_ref, kwk_ref, kwv_ref, kn2_ref):
    rows = ks_ref.shape[0]
    ks = ks_ref[...]
    kw = kw_ref[...]
    r = pl.program_id(0) * rows + lax.broadcasted_iota(i32, (rows, LANES), 0)
    n = lax.broadcasted_iota(i32, (rows, LANES), 1)
    onehot = jnp.where(_div_pow2(r, SEL_BLOCK) == n, 1.0, 0.0).astype(bf16)
    ones_col = jnp.where(n == 0, 1.0, 0.0).astype(bf16)

    @pl.when(pl.program_id(0) == 0)
    def _():
        kn2_ref[...] = jnp.zeros_like(kn2_ref)

    for g in range(NSA_KV_HEADS):
        ksl = slice(g * NSA_DH, (g + 1) * NSA_DH)
        vsl = slice(NSA_KV_W + g * NSA_DH, NSA_KV_W + (g + 1) * NSA_DH)
        kb = ks[:, ksl].astype(bf16)
        ksk_ref[g, :, 0:NSA_DH] = kb
        ksk_ref[g, :, NSA_DH:NSA_DH + LANES] = onehot
        ksv_ref[g, :, 0:NSA_DH] = ks[:, vsl].astype(bf16)
        ksv_ref[g, :, NSA_DH:NSA_DH + LANES] = ones_col
        kwk_ref[g] = kw[:, ksl].astype(bf16)
        kwv_ref[g] = kw[:, vsl].astype(bf16)
        kf = kb.astype(f32)
        n2 = jnp.max(jnp.sum(kf * kf, axis=-1, keepdims=True), axis=0, keepdims=True)
        kn2_ref[g] = jnp.maximum(kn2_ref[g], jnp.broadcast_to(n2, kn2_ref.shape[1:]))


def _kv_prep(u, ks_col_block, kw_col_block, rows=512):
    T = u.shape[0]
    G = NSA_KV_HEADS
    assert (T - 1) // SEL_BLOCK + 1 <= LANES
    sd = lambda w: jax.ShapeDtypeStruct((G, T, w), bf16)
    ospec = lambda w: pl.BlockSpec((G, rows, w), lambda i: (0, i, 0))
    return pl.pallas_call(
        _kv_prep_kernel,
        grid=(T // rows,),
        in_specs=[pl.BlockSpec((rows, KV_ROW), lambda i: (i, ks_col_block)),
                  pl.BlockSpec((rows, KV_ROW), lambda i: (i, kw_col_block))],
        out_specs=[ospec(NSA_DH + LANES), ospec(NSA_DH + LANES), ospec(NSA_DH), ospec(NSA_DH),
                   pl.BlockSpec((G, SUBLANES, LANES), lambda i: (0, 0, 0))],
        out_shape=[sd(NSA_DH + LANES), sd(NSA_DH + LANES), sd(NSA_DH), sd(NSA_DH),
                   jax.ShapeDtypeStruct((G, SUBLANES, LANES), f32)],
        compiler_params=_cparams("arbitrary"),
        name="kv_prep",
    )(u, u)


def _top_blocks(score, n_top, axis):
    lane = lax.broadcasted_iota(i32, score.shape, axis).astype(f32)
    width = float(score.shape[axis])
    work = score
    firsts = []
    for _ in range(n_top):
        mx = jnp.max(work, axis=axis, keepdims=True)
        first = jnp.min(jnp.where(work == mx, lane, width), axis=axis, keepdims=True)
        work = jnp.where(lane == first, REMOVED, work)
        firsts.append(first)
    return work, firsts


def _nsa_prompt_kernel(x_ref, gmix_ref, wgt_ref, q_ref, kck_ref, kcv_ref, ksk_ref, ksv_ref, kwk_ref, kwv_ref, kn2_ref,
                       o_ref, shift_sc, m_sc, acc_sc, *, n_sel):
    R, D = NSA_GROUP, NSA_DH
    QB = q_ref.shape[0]
    rows = R * QB
    qb = pl.program_id(1)
    q = q_ref[...]
    qs = jnp.concatenate([q[:, r * D:(r + 1) * D] for r in range(R)], axis=0) * (D ** -0.5)
    qs_b = qs.astype(bf16)
    pos = qb * QB + lax.broadcasted_iota(i32, (QB, 1), 0)

    def head_bias(valid):
        return jnp.concatenate([jnp.where(valid, 0.0, NEG)] * R, axis=0)

    ns = kck_ref.shape[0]
    s = _mm_nt(qs_b, kck_ref[...].astype(bf16))
    j = lax.broadcasted_iota(i32, (1, ns), 1)
    p_c = _biased_softmax_rows(s, head_bias(j * CMP_STRIDE + (CMP_BLOCK - 1) <= pos))
    o_c = _mm(p_c.astype(bf16), kcv_ref[...].astype(bf16))

    imp = p_c[0:QB]
    for r in range(1, R):
        imp = imp + p_c[r * QB:(r + 1) * QB]
    ratio = SEL_BLOCK // CMP_STRIDE
    off = CMP_BLOCK // CMP_STRIDE - 1
    nn = lax.broadcasted_iota(i32, (LANES, ns), 0)
    jj = lax.broadcasted_iota(i32, (LANES, ns), 1)
    overlap_t = jnp.where((jj >= ratio * nn - off) & (jj < ratio * nn + ratio), 1.0, 0.0).astype(bf16)
    imp_sel = sum(_mm_nt(overlap_t, part) for part in _split3(imp))
    n_idx = lax.broadcasted_iota(i32, (LANES, QB), 0)
    cur = _div_pow2(qb * QB + lax.broadcasted_iota(i32, (LANES, QB), 1), SEL_BLOCK)
    forced = (n_idx == 0) | (n_idx == cur) | (n_idx == cur - 1)
    score = jnp.where(forced, FORCE, jnp.where(n_idx <= cur, imp_sel, NEG))
    score = jnp.where(n_idx < n_sel, score, REMOVED)
    taken, _ = _top_blocks(score, min(SEL_TOPN, n_sel), axis=0)
    bias = jnp.where((taken < 0.5 * REMOVED) & (n_idx <= cur), 0.0, MASK_BIAS).T

    KT = SEL_KT
    last = (qb * QB + QB - 1) // KT
    q2_b = (qs * LOG2E).astype(bf16)
    q2 = q2_b.astype(f32)
    bound = jnp.sqrt(jnp.sum(q2 * q2, axis=-1, keepdims=True) * kn2_ref[0:1, 0:1]) * 1.01 + 1e-3
    k_self = ksk_ref[pl.ds(pl.multiple_of(qb * QB, QB), QB), 0:D].astype(f32)
    s_self = jnp.concatenate([jnp.sum(q2[r * QB:(r + 1) * QB] * k_self, axis=-1, keepdims=True) for r in range(R)],
                             axis=0)
    shift_sc[...] = bound
    diag_bias = head_bias(last * KT + lax.broadcasted_iota(i32, (1, KT), 1) <= pos)

    def shifted_queries(shift):
        return jnp.concatenate(
            [jnp.concatenate([q2_b[r * QB:(r + 1) * QB], (bias - shift[r * QB:(r + 1) * QB]).astype(bf16)], axis=1)
             for r in range(R)], axis=0)

    def key_tile(ref, kt):
        return ref[pl.ds(pl.multiple_of(kt * KT, KT), KT), :]

    @pl.when(jnp.max(bound - s_self) > SAFE_SPAN)
    def _():
        qp0 = shifted_queries(jnp.zeros_like(bound))

        def lane_max(kt, sk):
            mx = m_sc[...]
            for c in range(KT // LANES):
                mx = jnp.maximum(mx, sk[:, c * LANES:(c + 1) * LANES])
            m_sc[...] = mx

        def max_pass(kt, carry):
            lane_max(kt, _mm_nt(qp0, key_tile(ksk_ref, kt)))
            return carry

        m_sc[...] = jnp.full(m_sc.shape, NEG, f32)
        lax.fori_loop(0, last, max_pass, 0)
        lane_max(last, _mm_nt(qp0, key_tile(ksk_ref, last)) + diag_bias)
        shift_sc[...] = jnp.max(m_sc[...], axis=-1, keepdims=True)

    qp = shifted_queries(shift_sc[...])
    acc_sc[...] = jnp.zeros_like(acc_sc)

    def accumulate(kt, s2):
        acc_sc[...] += _mm(jnp.exp2(s2).astype(bf16), key_tile(ksv_ref, kt))

    def body(kt, carry):
        accumulate(kt, _mm_nt(qp, key_tile(ksk_ref, kt)))
        return carry

    lax.fori_loop(0, last, body, 0)
    accumulate(last, _mm_nt(qp, key_tile(ksk_ref, last)) + diag_bias)
    acc = acc_sc[...]
    o_s = acc[:, 0:D] * (1.0 / acc[:, D:D + 1])

    wlen = WINDOW + QB
    wstart = pl.multiple_of(jnp.maximum(qb * QB - WINDOW, 0), int(np.gcd(QB, WINDOW)))
    sw = _mm_nt(qs_b, kwk_ref[pl.ds(wstart, wlen), :])
    diff = pos - (wstart + lax.broadcasted_iota(i32, (1, wlen), 1))
    p_w = _biased_softmax_rows(sw, head_bias((diff >= 0) & (diff <= WINDOW)))
    o_w = _mm(p_w.astype(bf16), kwv_ref[pl.ds(wstart, wlen), :])

    gate = _sigmoid(_mm(_rms(x_ref[...], gmix_ref[...]).astype(bf16), wgt_ref[...]))
    for r in range(R):
        rs = slice(r * QB, (r + 1) * QB)
        o_ref[:, r * D:(r + 1) * D] = (gate[:, r:r + 1] * o_c[rs] + gate[:, R + r:R + r + 1] * o_s[rs]
                                       + gate[:, 2 * R + r:2 * R + r + 1] * o_w[rs])


def _nsa_prompt(x, g_mix, w_gate_g, u, q_col_block, kvc, ksk, ksv, kwk, kwv, kn2, tq):
    T, Dm = x.shape
    G, R, D, QB = NSA_KV_HEADS, NSA_GROUP, NSA_DH, tq
    ns = kvc.shape[0]
    n_sel = (T - 1) // SEL_BLOCK + 1
    assert T % SEL_KT == 0 and T % QB == 0 and T >= WINDOW + QB and n_sel <= LANES
    rows = R * QB
    res = lambda w: pl.BlockSpec((None, T, w), lambda g, i: (g, 0, 0))
    return pl.pallas_call(
        functools.partial(_nsa_prompt_kernel, n_sel=n_sel),
        grid=(G, T // QB),
        in_specs=[pl.BlockSpec((QB, Dm), lambda g, i: (i, 0)),
                  pl.BlockSpec((1, Dm), lambda g, i: (0, 0)),
                  pl.BlockSpec((None, Dm, LANES), lambda g, i: (g, 0, 0)),
                  pl.BlockSpec((QB, R * D), lambda g, i: (i, q_col_block + g)),
                  pl.BlockSpec((ns, D), lambda g, i: (0, g)),
                  pl.BlockSpec((ns, D), lambda g, i: (0, G + g)),
                  res(D + LANES), res(D + LANES), res(D), res(D),
                  pl.BlockSpec((None, SUBLANES, LANES), lambda g, i: (g, 0, 0))],
        out_specs=pl.BlockSpec((QB, R * D), lambda g, i: (i, g)),
        out_shape=jax.ShapeDtypeStruct((T, NSA_DIM), f32),
        scratch_shapes=[pltpu.VMEM((rows, 1), f32), pltpu.VMEM((rows, LANES), f32),
                        pltpu.VMEM((rows, D + LANES), f32)],
        compiler_params=_cparams("parallel", "arbitrary"),
        name="nsa_prompt",
    )(x, g_mix.reshape(1, Dm), w_gate_g, u, kvc, kvc, ksk, ksv, kwk, kwv, kn2)


def _group_gate_weights(w_gate):
    Dm = w_gate.shape[0]
    G, R = NSA_KV_HEADS, NSA_GROUP
    w = w_gate.reshape(Dm, 3, G, R).transpose(2, 0, 1, 3).reshape(G, Dm, 3 * R)
    return jnp.pad(w, ((0, 0), (0, 0), (0, LANES - 3 * R)))


def _nsa_sample_cmp_kernel(q_ref, kvc_ref, oc_ref, idx_ref, *, pos, n_sel, sel_w):
    H, R, D, G = NSA_HEADS, NSA_GROUP, NSA_DH, NSA_KV_HEADS
    ns = kvc_ref.shape[0]
    qs_b = (q_ref[...] * (D ** -0.5)).astype(bf16)
    head = lax.broadcasted_iota(i32, (H, 1), 0)
    grp = _div_pow2(head, R)
    s = jnp.zeros((H, ns), f32)
    for g in range(G):
        s = jnp.where(grp == g, _mm_nt(qs_b, kvc_ref[:, g * D:(g + 1) * D].astype(bf16)), s)
    j = lax.broadcasted_iota(i32, (1, ns), 1)
    e, den = _masked_softmax_rows(s, j * CMP_STRIDE + (CMP_BLOCK - 1) <= pos)
    p_c = e / den
    p_b = p_c.astype(bf16)
    o_c = jnp.zeros((H, D), f32)
    imp = jnp.zeros((H, ns), f32)
    for g in range(G):
        o_c = jnp.where(grp == g, _mm(p_b, kvc_ref[:, (G + g) * D:(G + g + 1) * D].astype(bf16)), o_c)
        imp = jnp.where(grp == g, jnp.sum(jnp.where(grp == g, p_c, 0.0), axis=0, keepdims=True), imp)
    oc_ref[...] = o_c
    ratio = SEL_BLOCK // CMP_STRIDE
    off = CMP_BLOCK // CMP_STRIDE - 1
    nn = lax.broadcasted_iota(i32, (sel_w, ns), 0)
    jj = lax.broadcasted_iota(i32, (sel_w, ns), 1)
    overlap_t = jnp.where((jj >= ratio * nn - off) & (jj < ratio * nn + ratio), 1.0, 0.0).astype(bf16)
    imp_sel = sum(_mm_nt(overlap_t, part) for part in _split3(imp))
    n_idx = lax.broadcasted_iota(i32, (sel_w, H), 0)
    cur = pos // SEL_BLOCK
    forced = (n_idx == 0) | (n_idx == cur) | (n_idx == cur - 1)
    score = jnp.where(forced, FORCE, jnp.where(n_idx <= cur, imp_sel, NEG))
    score = jnp.where(n_idx < n_sel, score, REMOVED)
    _, firsts = _top_blocks(score, SEL_TOPN, axis=0)
    idx_ref[...] = jnp.concatenate(firsts, axis=0).astype(i32)


def _nsa_sample_cmp(q, kvc, pos):
    B, H, D = q.shape
    ns = kvc.shape[1]
    n_sel = pos // SEL_BLOCK + 1
    assert n_sel >= SEL_TOPN
    sel_w = -(-n_sel // SUBLANES) * SUBLANES
    o_c, idx = pl.pallas_call(
        functools.partial(_nsa_sample_cmp_kernel, pos=pos, n_sel=n_sel, sel_w=sel_w),
        grid=(B,),
        in_specs=[pl.BlockSpec((None, H, D), lambda b: (b, 0, 0)),
                  pl.BlockSpec((None, ns, KV_ROW), lambda b: (b, 0, 0))],
        out_specs=[pl.BlockSpec((None, H, D), lambda b: (b, 0, 0)),
                   pl.BlockSpec((None, SEL_TOPN, H), lambda b: (b, 0, 0))],
        out_shape=[jax.ShapeDtypeStruct((B, H, D), f32), jax.ShapeDtypeStruct((B, SEL_TOPN, H), i32)],
        compiler_params=_cparams("parallel"),
        name="nsa_sample_cmp",
    )(q, kvc)
    return o_c, idx[:, :, ::NSA_GROUP].transpose(0, 2, 1)


def _nsa_sample_sel_kernel(idx_ref, pt_ref, q_ref, new_ref, *refs, past):
    H, R, D, G = NSA_HEADS, NSA_GROUP, NSA_DH, NSA_KV_HEADS
    blocks, o_ref = refs[:G * SEL_TOPN], refs[G * SEL_TOPN]
    b = pl.program_id(0)
    qs_b = (q_ref[...] * (D ** -0.5)).astype(bf16)
    grp = _div_pow2(lax.broadcasted_iota(i32, (H, 1), 0), R)
    per_tile = SUBLANES // KV_CHUNKS
    half = SEL_BLOCK // per_tile
    o = jnp.zeros((H, D), f32)
    for g in range(G):
        ks, vs = [], []
        lane = lax.broadcasted_iota(i32, (1, SEL_TOPN * SEL_BLOCK), 1)
        slot = _div_pow2(lane, SEL_BLOCK)
        in_slot = lane & (SEL_BLOCK - 1)
        pk_row = (in_slot & (half - 1)) * per_tile + _div_pow2(in_slot, half)
        for i in range(SEL_TOPN):
            blk = blocks[g * SEL_TOPN + i]
            base = idx_ref[(b * G + g) * SEL_TOPN + i] * SEL_BLOCK
            pk_row = pk_row + jnp.where(slot == i, base, 0)
            for par in range(per_tile):
                pk_col = base + per_tile * lax.broadcasted_iota(i32, (half, 1), 0) + par
                old = pk_col < past
                k = blk[pl.ds(par * KV_CHUNKS + g, half, stride=SUBLANES), :]
                v = blk[pl.ds(par * KV_CHUNKS + G + g, half, stride=SUBLANES), :]
                ks.append(jnp.where(old, k, new_ref[g:g + 1, :]).astype(bf16))
                vs.append(jnp.where(old, v, new_ref[G + g:G + g + 1, :]).astype(bf16))
        s = _mm_nt(qs_b, jnp.concatenate(ks, axis=0))
        e, den = _masked_softmax_rows(s, pk_row <= past)
        o = jnp.where(grp == g, _mm((e / den).astype(bf16), jnp.concatenate(vs, axis=0)), o)
    o_ref[...] = o


def _nsa_sample_sel(q, ks_new, pool, page_table, idx, past):
    B, H, D = q.shape
    G = NSA_KV_HEADS
    n_pages = page_table.shape[1]
    halves = PAGE_SIZE // SEL_BLOCK
    blk_rows = SEL_BLOCK * KV_CHUNKS
    pool_h = pool.reshape(pool.shape[0] * halves, blk_rows, D)
    last_old = past // SEL_BLOCK - 1

    def blk_map(b, idx_r, pt_r, g, i):
        blk = jnp.minimum(idx_r[(b * G + g) * SEL_TOPN + i], last_old)
        return (pt_r[b * n_pages + blk // halves] * halves + blk % halves, 0, 0)

    specs = [pl.BlockSpec((None, blk_rows, D), functools.partial(blk_map, g=g, i=i))
             for g in range(G) for i in range(SEL_TOPN)]
    gs = pltpu.PrefetchScalarGridSpec(
        num_scalar_prefetch=2, grid=(B,),
        in_specs=[pl.BlockSpec((None, H, D), lambda b, *_: (b, 0, 0)),
                  pl.BlockSpec((None, KV_CHUNKS, D), lambda b, *_: (b, 0, 0))] + specs,
        out_specs=pl.BlockSpec((None, H, D), lambda b, *_: (b, 0, 0)))
    return pl.pallas_call(
        functools.partial(_nsa_sample_sel_kernel, past=past),
        grid_spec=gs,
        out_shape=jax.ShapeDtypeStruct((B, H, D), f32),
        compiler_params=_cparams("arbitrary"),
        name="nsa_sample_sel",
    )(idx.reshape(-1), page_table.reshape(-1), q, ks_new, *([pool_h] * (G * SEL_TOPN)))


def _nsa_sample_win_kernel(q_ref, buf_ref, new_ref, oc_ref, os_ref, gt_ref, o_ref, win_ref, *, past):
    H, R, D, G = NSA_HEADS, NSA_GROUP, NSA_DH, NSA_KV_HEADS
    rows = buf_ref.shape[0]
    Lb = rows // KV_CHUNKS
    per_tile = SUBLANES // KV_CHUNKS
    half = Lb // per_tile
    qs = q_ref[...] * (D ** -0.5)
    qs_b = qs.astype(bf16)
    grp = _div_pow2(lax.broadcasted_iota(i32, (H, 1), 0), R)
    lane = lax.broadcasted_iota(i32, (1, Lb), 1)
    key_pos = past - Lb + (lane & (half - 1)) * per_tile + _div_pow2(lane, half)
    diff = past - key_pos
    valid = (diff >= 0) & (diff <= WINDOW)
    o_w = jnp.zeros((H, D), f32)
    for g in range(G):
        chunk = lambda c: jnp.concatenate(
            [buf_ref[pl.ds(par * KV_CHUNKS + c, half, stride=SUBLANES), :] for par in range(per_tile)], axis=0)
        new_k, new_v = new_ref[g:g + 1, :], new_ref[G + g:G + g + 1, :]
        s_b = jnp.where(valid, _mm_nt(qs_b, chunk(g).astype(bf16)), NEG)
        s_n = jnp.sum(qs * new_k, axis=-1, keepdims=True)
        mx = jnp.maximum(jnp.max(s_b, axis=-1, keepdims=True), s_n)
        e_b = jnp.where(valid, jnp.exp(s_b - mx), 0.0)
        e_n = jnp.exp(s_n - mx)
        den = jnp.sum(e_b, axis=-1, keepdims=True) + e_n
        og = _mm((e_b / den).astype(bf16), chunk(G + g).astype(bf16)) + (e_n / den) * new_v
        o_w = jnp.where(grp == g, og, o_w)
    gate = _sigmoid(gt_ref[...])
    o_ref[...] = gate[:, 0:1] * oc_ref[...] + gate[:, 1:2] * os_ref[...] + gate[:, 2:3] * o_w
    win_ref[0:rows - KV_CHUNKS, :] = buf_ref[KV_CHUNKS:rows, :]
    win_ref[rows - KV_CHUNKS:rows, :] = new_ref[...]


def _nsa_sample_win(q, win_buf, kw_new, o_c, o_s, gt, past):
    B, H, D = q.shape
    rows = win_buf.shape[1]
    assert rows == WINDOW * KV_CHUNKS
    per_b = lambda *s: pl.BlockSpec((None,) + s, lambda b: (b,) + (0,) * len(s))
    return pl.pallas_call(
        functools.partial(_nsa_sample_win_kernel, past=past),
        grid=(B,),
        in_specs=[per_b(H, D), per_b(rows, D), per_b(KV_CHUNKS, D), per_b(H, D), per_b(H, D), per_b(H, 3)],
        out_specs=[per_b(H, D), per_b(rows, D)],
        out_shape=[jax.ShapeDtypeStruct((B, H, D), f32), jax.ShapeDtypeStruct((B, rows, D), f32)],
        compiler_params=_cparams("parallel"),
        name="nsa_sample_win",
    )(q, win_buf, kw_new, o_c, o_s, gt.reshape(B, 3, H).transpose(0, 2, 1))


def kernel(x_prompt, x_sample, cache_cmp_kv, cache_sel_kv, cache_win_kv, cache_mem_kv, state_mlstm_c, state_mlstm_n, state_mlstm_m, state_conv, page_table, mem_prompt, g_mix, w_in, w_conv, b_conv, w_mq, w_mk, w_mv, w_mgate, b_mgate, g_mhead, cmp_pe, cmp_wpos, cmp_wphi, w_out, g_memx, g_mems, w_mem_q, w_mem_kv, w_mem_o, g_ffn, w_ff1, w_ff2, g_final):
    depth = w_in.shape[0]
    assert depth == 1 and x_prompt.shape[0] == 1 and x_sample.shape[1] == 1
    T, Dm = x_prompt.shape[1:]
    B = x_sample.shape[0]
    G, Dh, H = NSA_KV_HEADS, NSA_DH, ML_HEADS
    past = page_table.shape[1] * PAGE_SIZE
    assert (past + 1) // CMP_STRIDE == past // CMP_STRIDE
    n_main = 2 * ML_DIM + NSA_DIM + 3 * KV_ROW
    n_gate = 3 * NSA_HEADS
    l = 0
    hp = x_prompt.reshape(T, Dm)
    hs = x_sample.reshape(B, Dm)

    wbd = _blockdiag_weights(jnp.stack([w_mq[l], w_mk[l], w_mv[l]])).astype(bf16)
    w_mg = w_mgate[l].astype(bf16)
    w_mg_t = w_mg.T
    pe_t, wp_t = _compress_tables(cmp_pe[l], cmp_wpos[l])
    wb_in, wb_out, wb_mq, wb_mkv, wb_mo, wb_f1, wb_f2 = (
        w[l].astype(bf16) for w in (w_in, w_out, w_mem_q, w_mem_kv, w_mem_o, w_ff1, w_ff2))
    w_gt = jnp.pad(wb_in[:, n_main:], ((0, 0), (0, LANES - n_gate)))
    q_cb = 2 * ML_DIM // KV_ROW
    c_cb = (2 * ML_DIM + NSA_DIM) // KV_ROW
    n_mq, n_mkv = w_mem_q.shape[-1], w_mem_kv.shape[-1]
    tm_in, tm, tn, tf, tq = 1024, 512, 512, 1024, 256

    u = _norm_matmul(hp, g_mix[l], wb_in, n_main, tm_in, tn, "in_proj_p")
    ml_o, c_p, n_p, m_p = _mlstm_prompt(u, w_conv[l], b_conv[l], wbd, w_mg, w_mg_t, b_mgate[l],
                                        g_mhead[l], 128, "mlstm_prompt")
    kvc_p = _compress_prompt(u, c_cb, pe_t, wp_t, cmp_wphi[l])
    nsa_o = _nsa_prompt(hp, g_mix[l], _group_gate_weights(wb_in[:, n_main:]), u, q_cb, kvc_p,
                        *_kv_prep(u, c_cb + 1, c_cb + 2), tq)
    h1 = _out_proj(ml_o, nsa_o, wb_out, hp, tm, Dm, "out_proj_p")
    mem_kv = _norm_matmul(mem_prompt.reshape(-1, Dm), g_mems[l], wb_mkv, n_mkv, mem_prompt.shape[1], tn, "mem_kv")
    h2 = _mem_prompt(h1, g_memx[l], wb_mq, mem_kv, wb_mo, tm, "mem_attn_p")
    y_p = _ffn_final(h2, g_ffn[l], wb_f1, wb_f2, g_final, tm, tf, "ffn_p")

    kv_shape = lambda a: a.reshape(1, 1, a.shape[0], 2, G, Dh)
    kvw_p = u[:, (c_cb + 2) * KV_ROW:(c_cb + 3) * KV_ROW]
    out_p = (y_p.reshape(1, T, Dm),
             kv_shape(u[:, c_cb * KV_ROW:(c_cb + 1) * KV_ROW]),
             kv_shape(u[:, (c_cb + 1) * KV_ROW:(c_cb + 2) * KV_ROW]),
             kv_shape(kvw_p[T - min(WINDOW, T):]),
             mem_kv.reshape(1, 1, -1, 2, MEM_HEADS, MEM_DH),
             c_p[None, None], n_p[None, None], m_p[None, None],
             u[T - (ML_CONV - 1):, :ML_DIM][None, None])

    us = _norm_matmul(hs, g_mix[l], wb_in, n_main, B, tn, "in_proj_s")
    gts = _norm_matmul(hs, g_mix[l], w_gt, LANES, B, LANES, "in_gate_s")[:, :n_gate]
    xm_s, og_s = us[:, :ML_DIM], us[:, ML_DIM:2 * ML_DIM]
    q_s = us[:, 2 * ML_DIM:2 * ML_DIM + NSA_DIM].reshape(B, NSA_HEADS, Dh)
    kc_s, ks_s, kw_s = (us[:, (c_cb + i) * KV_ROW:(c_cb + i + 1) * KV_ROW] for i in range(3))
    ml_os, c_s, n_s, m_s = _mlstm_sample(xm_s, og_s, state_conv[l], state_mlstm_c[l], state_mlstm_n[l],
                                         state_mlstm_m[l], w_conv[l], b_conv[l], wbd,
                                         w_mg, b_mgate[l], g_mhead[l])
    pool_c = cache_cmp_kv.reshape(-1, PAGE_SIZE * KV_CHUNKS, Dh)
    pool_s = cache_sel_kv.reshape(-1, PAGE_SIZE * KV_CHUNKS, Dh)
    win_buf = cache_win_kv.reshape(B, -1, Dh)
    mem_buf = cache_mem_kv.reshape(B, -1, MEM_DH)
    kvc_s = _compress_paged(pool_c, page_table, pe_t, wp_t, cmp_wphi[l])
    o_cs, idx = _nsa_sample_cmp(q_s, kvc_s, past)
    o_ss = _nsa_sample_sel(q_s, ks_s.reshape(B, KV_CHUNKS, Dh), pool_s, page_table, idx, past)
    nsa_os, win_new = _nsa_sample_win(q_s, win_buf, kw_s.reshape(B, KV_CHUNKS, Dh), o_cs, o_ss, gts, past)
    h1s = _out_proj(ml_os, nsa_os.reshape(B, NSA_DIM), wb_out, hs, B, Dm, "out_proj_s")
    qm_s = _norm_matmul(h1s, g_memx[l], wb_mq, n_mq, B, n_mq, "mem_q_s")
    om_s = _mem_sample(qm_s, mem_buf, "mem_attn_s")
    h2s = _matmul_res(om_s, wb_mo, h1s, B, Dm, "mem_o_s")
    y_s = _ffn_final(h2s, g_ffn[l], wb_f1, wb_f2, g_final, B, tf, "ffn_s")

    kv_s_shape = lambda a: a.reshape(1, B, 1, 2, G, Dh)
    conv_s = jnp.concatenate([state_conv[l][:, 1:], xm_s[:, None, :]], axis=1)
    out_s = (y_s.reshape(B, 1, Dm), kv_s_shape(kc_s), kv_s_shape(ks_s),
             win_new.reshape(1, B, -1, 2, G, Dh), c_s[None], n_s[None], m_s[None], conv_s[None])

    return (out_p[0], out_s[0]) + out_p[1:] + out_s[1:]
```

```python
import functools

import jax
import jax.numpy as jnp
import numpy as np
from jax import lax
from jax.experimental import pallas as pl
from jax.experimental.pallas import tpu as pltpu

f32 = jnp.float32
bf16 = jnp.bfloat16
i32 = jnp.int32

EPS = 1e-6
NEG = -1e30
FORCE = 1e30
ML_HEADS = 8
ML_DH = 128
ML_DIM = ML_HEADS * ML_DH
ML_CONV = 4
ML_QKV_BLOCK = 4
NSA_HEADS = 8
NSA_KV_HEADS = 2
NSA_GROUP = NSA_HEADS // NSA_KV_HEADS
NSA_DH = 128
NSA_DIM = NSA_HEADS * NSA_DH
NSA_KV_W = NSA_KV_HEADS * NSA_DH
CMP_BLOCK = 32
CMP_STRIDE = 16
SEL_BLOCK = 64
SEL_TOPN = 16
WINDOW = 512
Q_BLOCK = 128
PAGE_SIZE = 128
MEM_HEADS = 4
MEM_DH = 128
KV_ROW = 2 * NSA_KV_W
KV_CHUNKS = KV_ROW // NSA_DH

LANES = 128
SUBLANES = 8
VMEM_LIMIT = 56 * 1024 * 1024

MASK_BIAS = -1e9
REMOVED = -3.0e38
SEL_KT = 1024
LOG2E = 1.4426950408889634
SAFE_SPAN = 100.0


def _cparams(*sem):
    return pltpu.CompilerParams(dimension_semantics=sem, vmem_limit_bytes=VMEM_LIMIT)


def _mm(a, b):
    return jnp.dot(a, b, preferred_element_type=f32)


def _mm_nt(a, b):
    return lax.dot_general(a, b, (((1,), (1,)), ((), ())), preferred_element_type=f32)


def _split2(x):
    h = x.astype(bf16)
    return h, (x - h.astype(f32)).astype(bf16)


def _split3(x):
    h = x.astype(bf16)
    r = x - h.astype(f32)
    m = r.astype(bf16)
    return h, m, (r - m.astype(f32)).astype(bf16)


def _dot3(a, b, mm=_mm):
    ah, al = _split2(a)
    bh, bl = _split2(b)
    return mm(ah, bh) + mm(al, bh) + mm(ah, bl)


def _dot_sel_l(sel, x):
    h, m, l = _split3(x)
    return _mm(sel, h) + _mm(sel, m) + _mm(sel, l)


def _dot_sel_r(x, sel):
    h, m, l = _split3(x)
    return _mm(h, sel) + _mm(m, sel) + _mm(l, sel)


def _rms(x, g):
    return x * lax.rsqrt(jnp.mean(x * x, axis=-1, keepdims=True) + EPS) * g


def _div_pow2(x, d):
    assert d & (d - 1) == 0
    return lax.shift_right_logical(x, jnp.full(x.shape, d.bit_length() - 1, x.dtype))


def _sigmoid(x):
    return 1.0 / (1.0 + jnp.exp(-x))


def _log_sigmoid(x):
    return jnp.minimum(x, 0.0) - jnp.log(1.0 + jnp.exp(-jnp.abs(x)))


def _masked_softmax_rows(s, valid):
    s = jnp.where(valid, s, NEG)
    mx = jnp.max(s, axis=-1, keepdims=True)
    e = jnp.where(valid, jnp.exp(s - mx), 0.0)
    den = jnp.sum(e, axis=-1, keepdims=True)
    return e, jnp.where(den > 0.0, den, 1.0)


def _biased_exp2_rows(s2, bias):
    s2 = s2 + bias
    mx = jnp.max(s2, axis=-1, keepdims=True)
    return jnp.exp2(s2 - mx), mx


def _biased_softmax2_rows(s2, bias):
    e, mx = _biased_exp2_rows(s2, bias)
    inv = jnp.where(mx > 0.5 * NEG, 1.0 / jnp.sum(e, axis=-1, keepdims=True), 0.0)
    return e * inv


def _norm_matmul_kernel(x_ref, g_ref, w_ref, o_ref, xn_ref):
    @pl.when(pl.program_id(1) == 0)
    def _():
        xn_ref[...] = _rms(x_ref[...], g_ref[...]).astype(bf16)

    o_ref[...] = _mm(xn_ref[...], w_ref[...])


def _norm_matmul(x, g, w, n_cols, tm, tn, name):
    M, K = x.shape
    return pl.pallas_call(
        _norm_matmul_kernel,
        grid=(M // tm, n_cols // tn),
        in_specs=[pl.BlockSpec((tm, K), lambda i, j: (i, 0)),
                  pl.BlockSpec((1, K), lambda i, j: (0, 0)),
                  pl.BlockSpec((K, tn), lambda i, j: (0, j))],
        out_specs=pl.BlockSpec((tm, tn), lambda i, j: (i, j)),
        out_shape=jax.ShapeDtypeStruct((M, n_cols), f32),
        scratch_shapes=[pltpu.VMEM((tm, K), bf16)],
        compiler_params=_cparams("parallel", "arbitrary"),
        name=name,
    )(x, g.reshape(1, K), w)


def _in_proj_kernel(x_ref, g_ref, w_ref, o_ref, *rest, kv_block0):
    lin_refs, xn_ref = rest[:-1], rest[-1]
    j = pl.program_id(1)

    @pl.when(j == 0)
    def _():
        xn_ref[...] = _rms(x_ref[...], g_ref[...]).astype(bf16)

    val = _mm(xn_ref[...], w_ref[...])
    o_ref[...] = val
    rows = val.shape[0]
    for i, ref in enumerate(lin_refs):
        @pl.when(j == kv_block0 + i)
        def _():
            for c in range(KV_CHUNKS):
                ref[pl.ds(c, rows, stride=KV_CHUNKS), :] = val[:, c * LANES:(c + 1) * LANES]


def _in_proj(x, g, w, n_cols, kv_block0, n_kv, tm, name):
    M, K = x.shape
    tn = KV_ROW
    lin_spec = pl.BlockSpec((tm * KV_CHUNKS, LANES), lambda i, j: (i, 0))
    outs = pl.pallas_call(
        functools.partial(_in_proj_kernel, kv_block0=kv_block0),
        grid=(M // tm, n_cols // tn),
        in_specs=[pl.BlockSpec((tm, K), lambda i, j: (i, 0)),
                  pl.BlockSpec((1, K), lambda i, j: (0, 0)),
                  pl.BlockSpec((K, tn), lambda i, j: (0, j))],
        out_specs=[pl.BlockSpec((tm, tn), lambda i, j: (i, j))] + [lin_spec] * n_kv,
        out_shape=[jax.ShapeDtypeStruct((M, n_cols), f32)]
        + [jax.ShapeDtypeStruct((M * KV_CHUNKS, LANES), f32)] * n_kv,
        scratch_shapes=[pltpu.VMEM((tm, K), bf16)],
        compiler_params=_cparams("parallel", "arbitrary"),
        name=name,
    )(x, g.reshape(1, K), w)
    return outs[0], outs[1:]


def _out_proj_kernel(a1_ref, a2_ref, w1_ref, w2_ref, r_ref, o_ref):
    o_ref[...] = (r_ref[...] + _mm(a1_ref[...].astype(bf16), w1_ref[...])
                  + _mm(a2_ref[...].astype(bf16), w2_ref[...]))


def _out_proj(a1, a2, w, res, tm, tn, name):
    M, K1 = a1.shape
    K2 = a2.shape[1]
    assert K1 == K2 and w.shape[0] == K1 + K2
    N = w.shape[1]
    return pl.pallas_call(
        _out_proj_kernel,
        grid=(M // tm, N // tn),
        in_specs=[pl.BlockSpec((tm, K1), lambda i, j: (i, 0)),
                  pl.BlockSpec((tm, K2), lambda i, j: (i, 0)),
                  pl.BlockSpec((K1, tn), lambda i, j: (0, j)),
                  pl.BlockSpec((K2, tn), lambda i, j: (1, j)),
                  pl.BlockSpec((tm, tn), lambda i, j: (i, j))],
        out_specs=pl.BlockSpec((tm, tn), lambda i, j: (i, j)),
        out_shape=jax.ShapeDtypeStruct((M, N), f32),
        compiler_params=_cparams("parallel", "arbitrary"),
        name=name,
    )(a1, a2, w, w, res)


def _matmul_res_kernel(a_ref, w_ref, r_ref, o_ref):
    o_ref[...] = r_ref[...] + _mm(a_ref[...].astype(bf16), w_ref[...])


def _matmul_res(a, w, res, tm, tn, name):
    M, K = a.shape
    N = w.shape[1]
    return pl.pallas_call(
        _matmul_res_kernel,
        grid=(M // tm, N // tn),
        in_specs=[pl.BlockSpec((tm, K), lambda i, j: (i, 0)),
                  pl.BlockSpec((K, tn), lambda i, j: (0, j)),
                  pl.BlockSpec((tm, tn), lambda i, j: (i, j))],
        out_specs=pl.BlockSpec((tm, tn), lambda i, j: (i, j)),
        out_shape=jax.ShapeDtypeStruct((M, N), f32),
        compiler_params=_cparams("parallel", "arbitrary"),
        name=name,
    )(a, w, res)


def _ffn_kernel(h_ref, g_ref, w1_ref, w2_ref, gf_ref, y_ref, xn_ref, acc_ref):
    f = pl.program_id(1)

    @pl.when(f == 0)
    def _():
        xn_ref[...] = _rms(h_ref[...], g_ref[...]).astype(bf16)
        acc_ref[...] = jnp.zeros_like(acc_ref)

    a = _mm(xn_ref[...], w1_ref[...])
    a = jnp.square(jnp.maximum(a, 0.0))
    acc_ref[...] += _mm(a.astype(bf16), w2_ref[...])

    @pl.when(f == pl.num_programs(1) - 1)
    def _():
        y_ref[...] = _rms(h_ref[...] + acc_ref[...], gf_ref[...])


def _ffn_final(h, g, w1, w2, g_final, tm, tf, name):
    M, D = h.shape
    F = w1.shape[1]
    return pl.pallas_call(
        _ffn_kernel,
        grid=(M // tm, F // tf),
        in_specs=[pl.BlockSpec((tm, D), lambda i, j: (i, 0)),
                  pl.BlockSpec((1, D), lambda i, j: (0, 0)),
                  pl.BlockSpec((D, tf), lambda i, j: (0, j)),
                  pl.BlockSpec((tf, D), lambda i, j: (j, 0)),
                  pl.BlockSpec((1, D), lambda i, j: (0, 0))],
        out_specs=pl.BlockSpec((tm, D), lambda i, j: (i, 0)),
        out_shape=jax.ShapeDtypeStruct((M, D), f32),
        scratch_shapes=[pltpu.VMEM((tm, D), bf16), pltpu.VMEM((tm, D), f32)],
        compiler_params=_cparams("parallel", "arbitrary"),
        name=name,
    )(h, g.reshape(1, D), w1, w2, g_final.reshape(1, D))


def _mem_prompt_kernel(h_ref, g_ref, wq_ref, k_ref, v_ref, wo_ref, o_ref):
    h = h_ref[...]
    xn = _rms(h, g_ref[...]).astype(bf16)
    q = _mm(xn, wq_ref[...]) * (MEM_DH ** -0.5)
    outs = []
    for hd in range(MEM_HEADS):
        sl = slice(hd * MEM_DH, (hd + 1) * MEM_DH)
        s = _mm_nt(q[:, sl].astype(bf16), k_ref[:, sl].astype(bf16))
        e = jnp.exp(s - jnp.max(s, axis=-1, keepdims=True))
        p = e / jnp.sum(e, axis=-1, keepdims=True)
        outs.append(_mm(p.astype(bf16), v_ref[:, sl].astype(bf16)))
    o = jnp.concatenate(outs, axis=-1)
    o_ref[...] = h + _mm(o.astype(bf16), wo_ref[...])


def _mem_prompt(h, g, wq, mem_kv, wo, tm, name):
    M, D = h.shape
    HD = MEM_HEADS * MEM_DH
    ML = mem_kv.shape[0]
    return pl.pallas_call(
        _mem_prompt_kernel,
        grid=(M // tm,),
        in_specs=[pl.BlockSpec((tm, D), lambda i: (i, 0)),
                  pl.BlockSpec((1, D), lambda i: (0, 0)),
                  pl.BlockSpec((D, HD), lambda i: (0, 0)),
                  pl.BlockSpec((ML, HD), lambda i: (0, 0)),
                  pl.BlockSpec((ML, HD), lambda i: (0, 1)),
                  pl.BlockSpec((HD, D), lambda i: (0, 0))],
        out_specs=pl.BlockSpec((tm, D), lambda i: (i, 0)),
        out_shape=jax.ShapeDtypeStruct((M, D), f32),
        compiler_params=_cparams("parallel"),
        name=name,
    )(h, g.reshape(1, D), wq, mem_kv, mem_kv, wo)


def _mem_sample_kernel(q_ref, kv_ref, o_ref):
    per_tok = 2 * MEM_HEADS
    ML = kv_ref.shape[0] // per_tok
    q = q_ref[...] * (MEM_DH ** -0.5)
    outs = []
    for hd in range(MEM_HEADS):
        sl = slice(hd * MEM_DH, (hd + 1) * MEM_DH)
        k = kv_ref[pl.ds(hd, ML, stride=per_tok), :]
        v = kv_ref[pl.ds(MEM_HEADS + hd, ML, stride=per_tok), :]
        s = jnp.sum(k * q[:, sl], axis=-1, keepdims=True)
        e = jnp.exp(s - jnp.max(s, axis=0, keepdims=True))
        p = e / jnp.sum(e, axis=0, keepdims=True)
        outs.append(jnp.sum(p * v, axis=0, keepdims=True))
    o_ref[...] = jnp.concatenate(outs, axis=-1)


def _mem_sample(q, kv, name):
    B, HD = q.shape
    rows = kv.shape[1]
    out = pl.pallas_call(
        _mem_sample_kernel,
        grid=(B,),
        in_specs=[pl.BlockSpec((None, 1, HD), lambda b: (b, 0, 0)),
                  pl.BlockSpec((None, rows, MEM_DH), lambda b: (b, 0, 0))],
        out_specs=pl.BlockSpec((None, 1, HD), lambda b: (b, 0, 0)),
        out_shape=jax.ShapeDtypeStruct((B, 1, HD), f32),
        compiler_params=_cparams("parallel"),
        name=name,
    )(q.reshape(B, 1, HD), kv)
    return out.reshape(B, HD)


def _blockdiag_weights(w):
    n = w.shape[0]
    per = LANES // ML_QKV_BLOCK
    nchunk = w.shape[1] // per
    wc = w.reshape(n, nchunk, per, ML_QKV_BLOCK, ML_QKV_BLOCK)
    eye = jnp.eye(per, dtype=w.dtype)
    full = wc[:, :, :, :, None, :] * eye[None, None, :, None, :, None]
    return full.reshape(n, nchunk, LANES, LANES)


def _ml_qkv_gates(xc, xm, wbd_ref, wg_ref, bg_ref):
    xc_b, xm_b = xc.astype(bf16), xm.astype(bf16)
    qs, ks, vs = [], [], []
    for c in range(ML_DIM // LANES):
        sl = slice(c * LANES, (c + 1) * LANES)
        qs.append(_mm(xc_b[:, sl], wbd_ref[0, c]))
        ks.append(_mm(xc_b[:, sl], wbd_ref[1, c]) * (ML_DH ** -0.5))
        vs.append(_mm(xm_b[:, sl], wbd_ref[2, c]))
    q = jnp.concatenate(qs, axis=-1)
    k = jnp.concatenate(ks, axis=-1)
    v = jnp.concatenate(vs, axis=-1)
    qkv_b = jnp.concatenate([q, k, v], axis=-1).astype(bf16)
    gates = _mm(qkv_b, wg_ref[...]) + bg_ref[...]
    return q, k, v, gates, qkv_b


def _mlstm_prompt_kernel(xm_ref, og_ref, wconv_ref, bconv_ref, wbd_ref, wg_ref, wgt_ref,
                         bg_ref, bgt_ref, gh_ref, o_ref, c_ref, n_ref, m_ref, prev_sc, *, L):
    H, D = ML_HEADS, ML_DH

    @pl.when(pl.program_id(0) == 0)
    def _():
        prev_sc[...] = jnp.zeros_like(prev_sc)
        c_ref[...] = jnp.zeros_like(c_ref)
        n_ref[...] = jnp.zeros_like(n_ref)
        m_ref[...] = jnp.full(m_ref.shape, NEG, f32)

    x = xm_ref[...]
    full = jnp.concatenate([prev_sc[...], x], axis=0)
    y = bconv_ref[...]
    for j in range(ML_CONV):
        off = SUBLANES - (ML_CONV - 1) + j
        y = y + full[off:off + L] * wconv_ref[j:j + 1, :]
    prev_sc[...] = x[L - SUBLANES:L]
    xc = y * _sigmoid(y)

    q, k, v, gates, qkv_b = _ml_qkv_gates(xc, x, wbd_ref, wg_ref, bg_ref)
    gates_t = _mm_nt(wgt_ref[...], qkv_b) + bgt_ref[...]
    ig_c = gates[:, 0:H]
    lf_c = _log_sigmoid(gates[:, H:2 * H])
    ig_r = gates_t[0:H, :]
    lf_r = _log_sigmoid(gates_t[H:2 * H, :])

    t_i = lax.broadcasted_iota(i32, (L, L), 0)
    s_i = lax.broadcasted_iota(i32, (L, L), 1)
    causal = s_i <= t_i
    tri = jnp.where(causal, 1.0, 0.0).astype(bf16)
    b_c = _dot_sel_l(tri, lf_c)
    tri_u = jnp.where(t_i <= s_i, 1.0, 0.0).astype(bf16)
    b_r = _dot_sel_r(lf_r, tri_u)

    for h in range(H):
        sl = slice(h * D, (h + 1) * D)
        qh, kh, vh = q[:, sl], k[:, sl], v[:, sl]
        bc = b_c[:, h:h + 1]
        m_prev = m_ref[h:h + 1, 0:1]
        d_in = jnp.where(causal, bc - b_r[h:h + 1, :] + ig_r[h:h + 1, :], NEG)
        d_x = bc + m_prev
        m_t = jnp.maximum(d_x, jnp.max(d_in, axis=-1, keepdims=True))
        w_in = jnp.exp(d_in - m_t)
        w_x = jnp.exp(d_x - m_t)
        qb = qh.astype(bf16)
        kb = kh.astype(bf16)
        vb = vh.astype(bf16)
        s = _mm_nt(qb, kb) * w_in
        c_old = c_ref[h]
        n_old = n_ref[h:h + 1, :]
        num = _mm(s.astype(bf16), vb) + w_x * _mm(qb, c_old.astype(bf16))
        den = jnp.sum(s, axis=-1, keepdims=True) + w_x * jnp.sum(qh * n_old, axis=-1, keepdims=True)
        hh = num / jnp.maximum(jnp.abs(den), jnp.exp(-m_t))
        m_new = m_t[L - 1:L, :]
        b_last = bc[L - 1:L, :]
        g_x = jnp.exp(b_last + m_prev - m_new)
        g_s = jnp.exp(b_last - bc + ig_c[:, h:h + 1] - m_new)
        ks_ = kh * g_s
        c_ref[h] = g_x * c_old + _mm(ks_.T.astype(bf16), vb)
        n_ref[h:h + 1, :] = g_x * n_old + jnp.sum(ks_, axis=0, keepdims=True)
        m_ref[h:h + 1, :] = jnp.broadcast_to(m_new, (1, LANES))
        hn = hh * lax.rsqrt(jnp.mean(hh * hh, axis=-1, keepdims=True) + EPS) * gh_ref[:, sl]
        o_ref[:, sl] = _sigmoid(og_ref[:, sl]) * hn


def _mlstm_prompt(u, w_conv, b_conv, wbd, w_gate, w_gate_t, b_gate, g_head, L, name):
    T = u.shape[0]
    H, D = ML_HEADS, ML_DH
    nch = ML_DIM // LANES
    full2 = lambda shape: pl.BlockSpec(shape, lambda i: (0,) * len(shape))
    out, c, n, m = pl.pallas_call(
        functools.partial(_mlstm_prompt_kernel, L=L),
        grid=(T // L,),
        in_specs=[pl.BlockSpec((L, ML_DIM), lambda i: (i, 0)),
                  pl.BlockSpec((L, ML_DIM), lambda i: (i, 1)),
                  full2((ML_CONV, ML_DIM)), full2((1, ML_DIM)),
                  full2((3, nch, LANES, LANES)),
                  full2((3 * ML_DIM, 2 * H)), full2((2 * H, 3 * ML_DIM)),
                  full2((1, 2 * H)), full2((2 * H, 1)), full2((1, ML_DIM))],
        out_specs=[pl.BlockSpec((L, ML_DIM), lambda i: (i, 0)),
                   full2((H, D, D)), full2((H, D)), full2((H, LANES))],
        out_shape=[jax.ShapeDtypeStruct((T, ML_DIM), f32),
                   jax.ShapeDtypeStruct((H, D, D), f32),
                   jax.ShapeDtypeStruct((H, D), f32),
                   jax.ShapeDtypeStruct((H, LANES), f32)],
        scratch_shapes=[pltpu.VMEM((SUBLANES, ML_DIM), f32)],
        compiler_params=_cparams("arbitrary"),
        name=name,
    )(u, u, w_conv, b_conv.reshape(1, ML_DIM), wbd, w_gate, w_gate_t,
      b_gate.reshape(1, 2 * H), b_gate.reshape(2 * H, 1), g_head.reshape(1, ML_DIM))
    return out, c, n, m[:, 0]


def _mlstm_sample_pre_kernel(xm_ref, s0_ref, s1_ref, s2_ref, wconv_ref, bconv_ref, wbd_ref,
                             wg_ref, bg_ref, q_ref, k_ref, v_ref, g_ref):
    x = xm_ref[...]
    y = (bconv_ref[...] + s0_ref[...] * wconv_ref[0:1, :] + s1_ref[...] * wconv_ref[1:2, :]
         + s2_ref[...] * wconv_ref[2:3, :] + x * wconv_ref[3:4, :])
    xc = y * _sigmoid(y)
    q, k, v, gates, _ = _ml_qkv_gates(xc, x, wbd_ref, wg_ref, bg_ref)
    q_ref[...] = q
    k_ref[...] = k
    v_ref[...] = v
    g_ref[...] = gates


def _mlstm_sample_step_kernel(qc_ref, kc_ref, q_ref, k_ref, v_ref, gt_ref, og_ref, gh_ref, c_ref, n_ref, m_ref,
                              o_ref, cn_ref, nn_ref, mn_ref):
    H, D = ML_HEADS, ML_DH
    ig = gt_ref[0:H, :]
    lf = _log_sigmoid(gt_ref[H:2 * H, :])
    m_old = m_ref[...]
    m_new = jnp.maximum(lf + m_old, ig)
    w_in = jnp.exp(ig - m_new)
    w_x = jnp.exp(lf + m_old - m_new)
    mn_ref[...] = m_new
    q, k, v, n_old = q_ref[...], k_ref[...], v_ref[...], n_ref[...]
    cq = jnp.concatenate([jnp.sum(c_ref[h] * qc_ref[:, h:h + 1], axis=0, keepdims=True) for h in range(H)], axis=0)
    s = jnp.sum(q * k, axis=-1, keepdims=True) * w_in
    num = s * v + w_x * cq
    den = s + w_x * jnp.sum(n_old * q, axis=-1, keepdims=True)
    hh = num / jnp.maximum(jnp.abs(den), jnp.exp(-m_new))
    hn = hh * lax.rsqrt(jnp.mean(hh * hh, axis=-1, keepdims=True) + EPS) * gh_ref[...]
    o_ref[...] = _sigmoid(og_ref[...]) * hn
    nn_ref[...] = w_x * n_old + w_in * k
    for h in range(H):
        cn_ref[h] = w_x[h:h + 1, :] * c_ref[h] + w_in[h:h + 1, :] * (kc_ref[:, h:h + 1] * v[h:h + 1, :])


def _mlstm_sample(xm, og, conv_state, c0, n0, m0, w_conv, b_conv, wbd, w_gate, b_gate, g_head):
    B = xm.shape[0]
    H, D = ML_HEADS, ML_DH
    sds = lambda *s: jax.ShapeDtypeStruct(s, f32)
    q, k, v, gates = pl.pallas_call(
        _mlstm_sample_pre_kernel,
        out_shape=[sds(B, ML_DIM), sds(B, ML_DIM), sds(B, ML_DIM), sds(B, 2 * H)],
        compiler_params=pltpu.CompilerParams(vmem_limit_bytes=VMEM_LIMIT),
        name="mlstm_sample_pre",
    )(xm, conv_state[:, 0], conv_state[:, 1], conv_state[:, 2], w_conv, b_conv.reshape(1, ML_DIM),
      wbd, w_gate, b_gate.reshape(1, 2 * H))
    q3, k3, v3 = (a.reshape(B, H, D) for a in (q, k, v))
    per_b = lambda *s: pl.BlockSpec((None,) + s, lambda b: (b,) + (0,) * len(s))
    out, c, n, m = pl.pallas_call(
        _mlstm_sample_step_kernel,
        grid=(B,),
        in_specs=[per_b(D, H), per_b(D, H), per_b(H, D), per_b(H, D), per_b(H, D), per_b(2 * H, 1),
                  per_b(H, D), pl.BlockSpec((H, D), lambda b: (0, 0)),
                  per_b(H, D, D), per_b(H, D), per_b(H, 1)],
        out_specs=[per_b(H, D), per_b(H, D, D), per_b(H, D), per_b(H, 1)],
        out_shape=[sds(B, H, D), sds(B, H, D, D), sds(B, H, D), sds(B, H, 1)],
        compiler_params=_cparams("parallel"),
        name="mlstm_sample_step",
    )(q3.transpose(0, 2, 1), k3.transpose(0, 2, 1), q3, k3, v3, gates.reshape(B, 2 * H, 1),
      og.reshape(B, H, D), g_head.reshape(H, D), c0, n0, m0.reshape(B, H, 1))
    return out.reshape(B, ML_DIM), c, n, m.reshape(B, H)


def _compress_kernel(x_ref, pe_ref, wp_ref, wphi_ref, o_ref, f0_sc, f1_sc, mn_sc):
    step = pl.program_id(0)
    sub = x_ref.shape[0] // CMP_STRIDE
    x3 = x_ref[...].reshape(sub, CMP_STRIDE, KV_ROW)
    base = pl.multiple_of(step * sub, sub)
    for o, sc in ((0, f0_sc), (1, f1_sc)):
        y = x3 + pe_ref[o][None]
        sc[pl.ds(base, sub), :] = jnp.sum(y * _sigmoid(y) * wp_ref[o][None], axis=1)
    mn_sc[pl.ds(base, sub), :] = jnp.sum(x3, axis=1) * (1.0 / CMP_STRIDE)

    @pl.when(step == pl.num_programs(0) - 1)
    def _():
        ns = f0_sc.shape[0]
        feat = f0_sc[...] + pltpu.roll(f1_sc[...], ns - 1, axis=0)
        mn = mn_sc[...]
        pooled = (mn + pltpu.roll(mn, ns - 1, axis=0)) * (CMP_STRIDE / CMP_BLOCK)
        for c in range(2):
            for g in range(NSA_KV_HEADS):
                sl = slice((c * NSA_KV_HEADS + g) * NSA_DH, (c * NSA_KV_HEADS + g + 1) * NSA_DH)
                o_ref[:, sl] = pooled[:, sl] + _dot3(feat[:, sl], wphi_ref[c])


def _compress_tables(pe, wpos):
    def lay(a):
        r = CMP_BLOCK // CMP_STRIDE
        a = a.reshape(2, r, CMP_STRIDE, NSA_DH).transpose(1, 2, 0, 3)
        a = jnp.broadcast_to(a[:, :, :, None, :], (r, CMP_STRIDE, 2, NSA_KV_HEADS, NSA_DH))
        return a.reshape(r, CMP_STRIDE, KV_ROW)
    return lay(pe), lay(wpos)


def _compress_prompt(u, col_block, pe_t, wp_t, wphi, rows=512):
    T = u.shape[0]
    n_sub = T // CMP_STRIDE
    const = lambda shape: pl.BlockSpec(shape, lambda s: (0,) * len(shape))
    return pl.pallas_call(
        _compress_kernel,
        grid=(T // rows,),
        in_specs=[pl.BlockSpec((rows, KV_ROW), lambda s: (s, col_block)),
                  const(pe_t.shape), const(wp_t.shape), const(wphi.shape)],
        out_specs=const((n_sub, KV_ROW)),
        out_shape=jax.ShapeDtypeStruct((n_sub, KV_ROW), f32),
        scratch_shapes=[pltpu.VMEM((n_sub, KV_ROW), f32)] * 3,
        compiler_params=_cparams("arbitrary"),
        name="compress_prompt",
    )(u, pe_t, wp_t, wphi)


def _compress_paged_kernel(pt_ref, *refs, n_pages):
    pages = refs[:n_pages]
    pe_ref, wp_ref, wphi_ref, o_ref, f0_sc, f1_sc, mn_sc = refs[n_pages:]
    step = pl.program_id(1)
    sub = PAGE_SIZE // CMP_STRIDE
    tiles = CMP_STRIDE * KV_CHUNKS // SUBLANES
    packed = 2 * SUBLANES
    ptiles = CMP_STRIDE * KV_CHUNKS // packed
    out_rows = sub * SUBLANES
    for p in range(n_pages):
        x = pages[p][...]
        base = pl.multiple_of((step * n_pages + p) * out_rows, out_rows)
        xh = (x.astype(bf16) * 0.5).reshape(sub, ptiles, packed, LANES)
        for o, sc in ((0, f0_sc), (1, f1_sc)):
            y = xh + pe_ref[o][None]
            b = y * wp_ref[o][None]
            z = b + b * jnp.tanh(y)
            z = (z[:, 0] + z[:, 1]) + (z[:, 2] + z[:, 3])
            zf = z.astype(f32).reshape(sub, packed // SUBLANES, SUBLANES, LANES)
            sc[pl.ds(base, out_rows), :] = jnp.sum(zf, axis=1).reshape(out_rows, LANES)
        mean = jnp.sum(x.reshape(sub, tiles, SUBLANES, LANES), axis=1) * (1.0 / CMP_STRIDE)
        mn_sc[pl.ds(base, out_rows), :] = mean.reshape(out_rows, LANES)

    @pl.when(step == pl.num_programs(1) - 1)
    def _():
        ns = f0_sc.shape[0] // SUBLANES
        for c in range(KV_CHUNKS):
            col = lambda sc: (sc[pl.ds(c, ns, stride=SUBLANES), :]
                              + sc[pl.ds(KV_CHUNKS + c, ns, stride=SUBLANES), :])
            feat = col(f0_sc) + pltpu.roll(col(f1_sc), ns - 1, axis=0)
            mn = col(mn_sc)
            pooled = (mn + pltpu.roll(mn, ns - 1, axis=0)) * (CMP_STRIDE / CMP_BLOCK)
            o_ref[:, c * LANES:(c + 1) * LANES] = pooled + _dot3(feat, wphi_ref[c // NSA_KV_HEADS])


def _compress_paged(pool, page_table, pe_t, wp_t, wphi, pages_per_step=16):
    B, n_pages = page_table.shape
    P = pages_per_step
    n_sub = n_pages * PAGE_SIZE // CMP_STRIDE
    r = CMP_BLOCK // CMP_STRIDE
    packed = 2 * SUBLANES
    ptiles = CMP_STRIDE * KV_CHUNKS // packed
    pe4 = (0.5 * pe_t).astype(bf16).reshape(r, ptiles, packed, LANES)
    wp4 = wp_t.astype(bf16).reshape(r, ptiles, packed, LANES)
    page_rows = PAGE_SIZE * KV_CHUNKS
    specs = [pl.BlockSpec((None, page_rows, LANES),
                          functools.partial(lambda b, s, pt, p: (pt[b * n_pages + s * P + p], 0, 0), p=p))
             for p in range(P)]
    const = lambda shape: pl.BlockSpec(shape, lambda *a: (0,) * len(shape))
    gs = pltpu.PrefetchScalarGridSpec(
        num_scalar_prefetch=1, grid=(B, n_pages // P),
        in_specs=specs + [const(pe4.shape), const(wp4.shape), const(wphi.shape)],
        out_specs=pl.BlockSpec((None, n_sub, KV_ROW), lambda b, s, pt: (b, 0, 0)),
        scratch_shapes=[pltpu.VMEM((n_sub * SUBLANES, LANES), f32)] * 3)
    return pl.pallas_call(
        functools.partial(_compress_paged_kernel, n_pages=P),
        grid_spec=gs,
        out_shape=jax.ShapeDtypeStruct((B, n_sub, KV_ROW), f32),
        compiler_params=_cparams("parallel", "arbitrary"),
        name="compress_paged",
    )(page_table.reshape(-1), *([pool] * P), pe4, wp4, wphi)


def _kv_prep_kernel(ks_ref, kw_ref, ksk_ref, ksv_ref, kwk_ref, kwv_ref, kn2_ref):
    rows = ks_ref.shape[0]
    ks = ks_ref[...]
    kw = kw_ref[...]
    r = pl.program_id(0) * rows + lax.broadcasted_iota(i32, (rows, LANES), 0)
    n = lax.broadcasted_iota(i32, (rows, LANES), 1)
    onehot = jnp.where(_div_pow2(r, SEL_BLOCK) == n, 1.0, 0.0).astype(bf16)
    ones_col = jnp.where(n == 0, 1.0, 0.0).astype(bf16)

    @pl.when(pl.program_id(0) == 0)
    def _():
        kn2_ref[...] = jnp.zeros_like(kn2_ref)

    for g in range(NSA_KV_HEADS):
        ksl = slice(g * NSA_DH, (g + 1) * NSA_DH)
        vsl = slice(NSA_KV_W + g * NSA_DH, NSA_KV_W + (g + 1) * NSA_DH)
        kb = ks[:, ksl].astype(bf16)
        ksk_ref[g, :, 0:NSA_DH] = kb
        ksk_ref[g, :, NSA_DH:NSA_DH + LANES] = onehot
        ksv_ref[g, :, 0:NSA_DH] = ks[:, vsl].astype(bf16)
        ksv_ref[g, :, NSA_DH:NSA_DH + LANES] = ones_col
        kwk_ref[g] = kw[:, ksl].astype(bf16)
        kwv_ref[g, :, 0:NSA_DH] = kw[:, vsl].astype(bf16)
        kwv_ref[g, :, NSA_DH:NSA_DH + LANES] = ones_col
        kf = kb.astype(f32)
        n2 = jnp.max(jnp.sum(kf * kf, axis=-1, keepdims=True), axis=0, keepdims=True)
        kn2_ref[g] = jnp.maximum(kn2_ref[g], jnp.broadcast_to(n2, kn2_ref.shape[1:]))


def _kv_prep(u, ks_col_block, kw_col_block, rows=512):
    T = u.shape[0]
    G = NSA_KV_HEADS
    assert (T - 1) // SEL_BLOCK + 1 <= LANES
    sd = lambda w: jax.ShapeDtypeStruct((G, T, w), bf16)
    ospec = lambda w: pl.BlockSpec((G, rows, w), lambda i: (0, i, 0))
    return pl.pallas_call(
        _kv_prep_kernel,
        grid=(T // rows,),
        in_specs=[pl.BlockSpec((rows, KV_ROW), lambda i: (i, ks_col_block)),
                  pl.BlockSpec((rows, KV_ROW), lambda i: (i, kw_col_block))],
        out_specs=[ospec(NSA_DH + LANES), ospec(NSA_DH + LANES), ospec(NSA_DH), ospec(NSA_DH + LANES),
                   pl.BlockSpec((G, SUBLANES, LANES), lambda i: (0, 0, 0))],
        out_shape=[sd(NSA_DH + LANES), sd(NSA_DH + LANES), sd(NSA_DH), sd(NSA_DH + LANES),
                   jax.ShapeDtypeStruct((G, SUBLANES, LANES), f32)],
        compiler_params=_cparams("arbitrary"),
        name="kv_prep",
    )(u, u)


def _top_blocks(score, n_top, axis):
    lane = lax.broadcasted_iota(i32, score.shape, axis).astype(f32)
    width = float(score.shape[axis])
    work = score
    firsts = []
    for _ in range(n_top):
        mx = jnp.max(work, axis=axis, keepdims=True)
        first = jnp.min(jnp.where(work == mx, lane, width), axis=axis, keepdims=True)
        work = jnp.where(lane == first, REMOVED, work)
        firsts.append(first)
    return work, firsts


def _nsa_prompt_kernel(x_ref, gmix_ref, wgt_ref, q_ref, kck_ref, kcv_ref, ksk_ref, ksv_ref, kwk_ref, kwv_ref, kn2_ref,
                       o_ref, shift_sc, m_sc, acc_sc, *, n_sel):
    R, D = NSA_GROUP, NSA_DH
    QB = q_ref.shape[0]
    rows = R * QB
    qb = pl.program_id(1)
    q = q_ref[...]
    q2_b = (jnp.concatenate([q[:, r * D:(r + 1) * D] for r in range(R)], axis=0) * (D ** -0.5 * LOG2E)).astype(bf16)
    pos = qb * QB + lax.broadcasted_iota(i32, (QB, 1), 0)

    def head_bias(valid):
        return jnp.concatenate([jnp.where(valid, 0.0, NEG)] * R, axis=0)

    ns = kck_ref.shape[0]
    s = _mm_nt(q2_b, kck_ref[...].astype(bf16))
    j = lax.broadcasted_iota(i32, (1, ns), 1)
    p_c = _biased_softmax2_rows(s, head_bias(j * CMP_STRIDE + (CMP_BLOCK - 1) <= pos))
    o_c = _mm(p_c.astype(bf16), kcv_ref[...].astype(bf16))

    imp = p_c[0:QB]
    for r in range(1, R):
        imp = imp + p_c[r * QB:(r + 1) * QB]
    ratio = SEL_BLOCK // CMP_STRIDE
    off = CMP_BLOCK // CMP_STRIDE - 1
    nn = lax.broadcasted_iota(i32, (LANES, ns), 0)
    jj = lax.broadcasted_iota(i32, (LANES, ns), 1)
    overlap_t = jnp.where((jj >= ratio * nn - off) & (jj < ratio * nn + ratio), 1.0, 0.0).astype(bf16)
    imp_sel = sum(_mm_nt(overlap_t, part) for part in _split3(imp))
    n_idx = lax.broadcasted_iota(i32, (LANES, QB), 0)
    cur = _div_pow2(qb * QB + lax.broadcasted_iota(i32, (LANES, QB), 1), SEL_BLOCK)
    forced = (n_idx == 0) | (n_idx == cur) | (n_idx == cur - 1)
    score = jnp.where(forced, FORCE, jnp.where(n_idx <= cur, imp_sel, NEG))
    score = jnp.where(n_idx < n_sel, score, REMOVED)
    taken, _ = _top_blocks(score, min(SEL_TOPN, n_sel), axis=0)
    bias = jnp.where((taken < 0.5 * REMOVED) & (n_idx <= cur), 0.0, MASK_BIAS).T

    KT = SEL_KT
    last = (qb * QB + QB - 1) // KT
    q2 = q2_b.astype(f32)
    bound = jnp.sqrt(jnp.sum(q2 * q2, axis=-1, keepdims=True) * kn2_ref[0:1, 0:1]) * 1.01 + 1e-3
    k_self = ksk_ref[pl.ds(pl.multiple_of(qb * QB, QB), QB), 0:D].astype(f32)
    s_self = jnp.concatenate([jnp.sum(q2[r * QB:(r + 1) * QB] * k_self, axis=-1, keepdims=True) for r in range(R)],
                             axis=0)
    shift_sc[...] = bound
    diag_bias = head_bias(last * KT + lax.broadcasted_iota(i32, (1, KT), 1) <= pos)

    def shifted_queries(shift):
        return jnp.concatenate(
            [jnp.concatenate([q2_b[r * QB:(r + 1) * QB], (bias - shift[r * QB:(r + 1) * QB]).astype(bf16)], axis=1)
             for r in range(R)], axis=0)

    def key_tile(ref, kt):
        return ref[pl.ds(pl.multiple_of(kt * KT, KT), KT), :]

    @pl.when(jnp.max(bound - s_self) > SAFE_SPAN)
    def _():
        qp0 = shifted_queries(jnp.zeros_like(bound))

        def lane_max(kt, sk):
            mx = m_sc[...]
            for c in range(KT // LANES):
                mx = jnp.maximum(mx, sk[:, c * LANES:(c + 1) * LANES])
            m_sc[...] = mx

        def max_pass(kt, carry):
            lane_max(kt, _mm_nt(qp0, key_tile(ksk_ref, kt)))
            return carry

        m_sc[...] = jnp.full(m_sc.shape, NEG, f32)
        lax.fori_loop(0, last, max_pass, 0)
        lane_max(last, _mm_nt(qp0, key_tile(ksk_ref, last)) + diag_bias)
        shift_sc[...] = jnp.max(m_sc[...], axis=-1, keepdims=True)

    qp = shifted_queries(shift_sc[...])
    acc_sc[...] = jnp.zeros_like(acc_sc)

    def accumulate(kt, s2):
        acc_sc[...] += _mm(jnp.exp2(s2).astype(bf16), key_tile(ksv_ref, kt))

    def body(kt, carry):
        accumulate(kt, _mm_nt(qp, key_tile(ksk_ref, kt)))
        return carry

    lax.fori_loop(0, last, body, 0)
    accumulate(last, _mm_nt(qp, key_tile(ksk_ref, last)) + diag_bias)
    acc = acc_sc[...]
    o_s = acc[:, 0:D] * (1.0 / acc[:, D:D + 1])

    wlen = WINDOW + QB
    wstart = pl.multiple_of(jnp.maximum(qb * QB - WINDOW, 0), int(np.gcd(QB, WINDOW)))
    sw = _mm_nt(q2_b, kwk_ref[pl.ds(wstart, wlen), :])
    diff = pos - (wstart + lax.broadcasted_iota(i32, (1, wlen), 1))
    e_w, _ = _biased_exp2_rows(sw, head_bias((diff >= 0) & (diff <= WINDOW)))
    acc_w = _mm(e_w.astype(bf16), kwv_ref[pl.ds(wstart, wlen), :])
    o_w = acc_w[:, 0:D] * (1.0 / acc_w[:, D:D + 1])

    gate = _sigmoid(_mm(_rms(x_ref[...], gmix_ref[...]).astype(bf16), wgt_ref[...]))
    for r in range(R):
        rs = slice(r * QB, (r + 1) * QB)
        o_ref[:, r * D:(r + 1) * D] = (gate[:, r:r + 1] * o_c[rs] + gate[:, R + r:R + r + 1] * o_s[rs]
                                       + gate[:, 2 * R + r:2 * R + r + 1] * o_w[rs])


def _nsa_prompt(x, g_mix, w_gate_g, u, q_col_block, kvc, ksk, ksv, kwk, kwv, kn2, tq):
    T, Dm = x.shape
    G, R, D, QB = NSA_KV_HEADS, NSA_GROUP, NSA_DH, tq
    ns = kvc.shape[0]
    n_sel = (T - 1) // SEL_BLOCK + 1
    assert T % SEL_KT == 0 and T % QB == 0 and T >= WINDOW + QB and n_sel <= LANES
    rows = R * QB
    res = lambda w: pl.BlockSpec((None, T, w), lambda g, i: (g, 0, 0))
    return pl.pallas_call(
        functools.partial(_nsa_prompt_kernel, n_sel=n_sel),
        grid=(G, T // QB),
        in_specs=[pl.BlockSpec((QB, Dm), lambda g, i: (i, 0)),
                  pl.BlockSpec((1, Dm), lambda g, i: (0, 0)),
                  pl.BlockSpec((None, Dm, LANES), lambda g, i: (g, 0, 0)),
                  pl.BlockSpec((QB, R * D), lambda g, i: (i, q_col_block + g)),
                  pl.BlockSpec((ns, D), lambda g, i: (0, g)),
                  pl.BlockSpec((ns, D), lambda g, i: (0, G + g)),
                  res(D + LANES), res(D + LANES), res(D), res(D + LANES),
                  pl.BlockSpec((None, SUBLANES, LANES), lambda g, i: (g, 0, 0))],
        out_specs=pl.BlockSpec((QB, R * D), lambda g, i: (i, g)),
        out_shape=jax.ShapeDtypeStruct((T, NSA_DIM), f32),
        scratch_shapes=[pltpu.VMEM((rows, 1), f32), pltpu.VMEM((rows, LANES), f32),
                        pltpu.VMEM((rows, D + LANES), f32)],
        compiler_params=_cparams("parallel", "arbitrary"),
        name="nsa_prompt",
    )(x, g_mix.reshape(1, Dm), w_gate_g, u, kvc, kvc, ksk, ksv, kwk, kwv, kn2)


def _group_gate_weights(w_gate):
    Dm = w_gate.shape[0]
    G, R = NSA_KV_HEADS, NSA_GROUP
    w = w_gate.reshape(Dm, 3, G, R).transpose(2, 0, 1, 3).reshape(G, Dm, 3 * R)
    return jnp.pad(w, ((0, 0), (0, 0), (0, LANES - 3 * R)))


def _nsa_sample_cmp_kernel(q_ref, kvc_ref, oc_ref, idx_ref, *, pos, n_sel, sel_w):
    H, R, D, G = NSA_HEADS, NSA_GROUP, NSA_DH, NSA_KV_HEADS
    ns = kvc_ref.shape[0]
    qs_b = (q_ref[...] * (D ** -0.5)).astype(bf16)
    head = lax.broadcasted_iota(i32, (H, 1), 0)
    grp = _div_pow2(head, R)
    s = jnp.zeros((H, ns), f32)
    for g in range(G):
        s = jnp.where(grp == g, _mm_nt(qs_b, kvc_ref[:, g * D:(g + 1) * D].astype(bf16)), s)
    j = lax.broadcasted_iota(i32, (1, ns), 1)
    e, den = _masked_softmax_rows(s, j * CMP_STRIDE + (CMP_BLOCK - 1) <= pos)
    p_c = e / den
    p_b = p_c.astype(bf16)
    o_c = jnp.zeros((H, D), f32)
    imp = jnp.zeros((H, ns), f32)
    for g in range(G):
        o_c = jnp.where(grp == g, _mm(p_b, kvc_ref[:, (G + g) * D:(G + g + 1) * D].astype(bf16)), o_c)
        imp = jnp.where(grp == g, jnp.sum(jnp.where(grp == g, p_c, 0.0), axis=0, keepdims=True), imp)
    oc_ref[...] = o_c
    ratio = SEL_BLOCK // CMP_STRIDE
    off = CMP_BLOCK // CMP_STRIDE - 1
    nn = lax.broadcasted_iota(i32, (sel_w, ns), 0)
    jj = lax.broadcasted_iota(i32, (sel_w, ns), 1)
    overlap_t = jnp.where((jj >= ratio * nn - off) & (jj < ratio * nn + ratio), 1.0, 0.0).astype(bf16)
    imp_sel = sum(_mm_nt(overlap_t, part) for part in _split3(imp))
    n_idx = lax.broadcasted_iota(i32, (sel_w, H), 0)
    cur = pos // SEL_BLOCK
    forced = (n_idx == 0) | (n_idx == cur) | (n_idx == cur - 1)
    score = jnp.where(forced, FORCE, jnp.where(n_idx <= cur, imp_sel, NEG))
    score = jnp.where(n_idx < n_sel, score, REMOVED)
    _, firsts = _top_blocks(score, SEL_TOPN, axis=0)
    idx_ref[...] = jnp.concatenate(firsts, axis=0).astype(i32)


def _nsa_sample_cmp(q, kvc, pos):
    B, H, D = q.shape
    ns = kvc.shape[1]
    n_sel = pos // SEL_BLOCK + 1
    assert n_sel >= SEL_TOPN
    sel_w = -(-n_sel // SUBLANES) * SUBLANES
    o_c, idx = pl.pallas_call(
        functools.partial(_nsa_sample_cmp_kernel, pos=pos, n_sel=n_sel, sel_w=sel_w),
        grid=(B,),
        in_specs=[pl.BlockSpec((None, H, D), lambda b: (b, 0, 0)),
                  pl.BlockSpec((None, ns, KV_ROW), lambda b: (b, 0, 0))],
        out_specs=[pl.BlockSpec((None, H, D), lambda b: (b, 0, 0)),
                   pl.BlockSpec((None, SEL_TOPN, H), lambda b: (b, 0, 0))],
        out_shape=[jax.ShapeDtypeStruct((B, H, D), f32), jax.ShapeDtypeStruct((B, SEL_TOPN, H), i32)],
        compiler_params=_cparams("parallel"),
        name="nsa_sample_cmp",
    )(q, kvc)
    return o_c, idx[:, :, ::NSA_GROUP].transpose(0, 2, 1)


def _nsa_sample_sel_kernel(idx_ref, pt_ref, q_ref, new_ref, *refs, past):
    H, R, D, G = NSA_HEADS, NSA_GROUP, NSA_DH, NSA_KV_HEADS
    blocks, o_ref = refs[:G * SEL_TOPN], refs[G * SEL_TOPN]
    b = pl.program_id(0)
    qs_b = (q_ref[...] * (D ** -0.5)).astype(bf16)
    grp = _div_pow2(lax.broadcasted_iota(i32, (H, 1), 0), R)
    per_tile = SUBLANES // KV_CHUNKS
    half = SEL_BLOCK // per_tile
    o = jnp.zeros((H, D), f32)
    for g in range(G):
        ks, vs = [], []
        lane = lax.broadcasted_iota(i32, (1, SEL_TOPN * SEL_BLOCK), 1)
        slot = _div_pow2(lane, SEL_BLOCK)
        in_slot = lane & (SEL_BLOCK - 1)
        pk_row = (in_slot & (half - 1)) * per_tile + _div_pow2(in_slot, half)
        for i in range(SEL_TOPN):
            blk = blocks[g * SEL_TOPN + i]
            base = idx_ref[(b * G + g) * SEL_TOPN + i] * SEL_BLOCK
            pk_row = pk_row + jnp.where(slot == i, base, 0)
            for par in range(per_tile):
                pk_col = base + per_tile * lax.broadcasted_iota(i32, (half, 1), 0) + par
                old = pk_col < past
                k = blk[pl.ds(par * KV_CHUNKS + g, half, stride=SUBLANES), :]
                v = blk[pl.ds(par * KV_CHUNKS + G + g, half, stride=SUBLANES), :]
                ks.append(jnp.where(old, k, new_ref[g:g + 1, :]).astype(bf16))
                vs.append(jnp.where(old, v, new_ref[G + g:G + g + 1, :]).astype(bf16))
        s = _mm_nt(qs_b, jnp.concatenate(ks, axis=0))
        e, den = _masked_softmax_rows(s, pk_row <= past)
        o = jnp.where(grp == g, _mm((e / den).astype(bf16), jnp.concatenate(vs, axis=0)), o)
    o_ref[...] = o


def _nsa_sample_sel(q, ks_new, pool, page_table, idx, past):
    B, H, D = q.shape
    G = NSA_KV_HEADS
    n_pages = page_table.shape[1]
    halves = PAGE_SIZE // SEL_BLOCK
    blk_rows = SEL_BLOCK * KV_CHUNKS
    pool_h = pool.reshape(pool.shape[0] * halves, blk_rows, D)
    last_old = past // SEL_BLOCK - 1

    def blk_map(b, idx_r, pt_r, g, i):
        blk = jnp.minimum(idx_r[(b * G + g) * SEL_TOPN + i], last_old)
        return (pt_r[b * n_pages + blk // halves] * halves + blk % halves, 0, 0)

    specs = [pl.BlockSpec((None, blk_rows, D), functools.partial(blk_map, g=g, i=i))
             for g in range(G) for i in range(SEL_TOPN)]
    gs = pltpu.PrefetchScalarGridSpec(
        num_scalar_prefetch=2, grid=(B,),
        in_specs=[pl.BlockSpec((None, H, D), lambda b, *_: (b, 0, 0)),
                  pl.BlockSpec((None, KV_CHUNKS, D), lambda b, *_: (b, 0, 0))] + specs,
        out_specs=pl.BlockSpec((None, H, D), lambda b, *_: (b, 0, 0)))
    return pl.pallas_call(
        functools.partial(_nsa_sample_sel_kernel, past=past),
        grid_spec=gs,
        out_shape=jax.ShapeDtypeStruct((B, H, D), f32),
        compiler_params=_cparams("arbitrary"),
        name="nsa_sample_sel",
    )(idx.reshape(-1), page_table.reshape(-1), q, ks_new, *([pool_h] * (G * SEL_TOPN)))


def _nsa_sample_win_kernel(q_ref, buf_ref, new_ref, oc_ref, os_ref, gt_ref, o_ref, win_ref, *, past):
    H, R, D, G = NSA_HEADS, NSA_GROUP, NSA_DH, NSA_KV_HEADS
    rows = buf_ref.shape[0]
    Lb = rows // KV_CHUNKS
    per_tile = SUBLANES // KV_CHUNKS
    half = Lb // per_tile
    qs = q_ref[...] * (D ** -0.5)
    qs_b = qs.astype(bf16)
    grp = _div_pow2(lax.broadcasted_iota(i32, (H, 1), 0), R)
    lane = lax.broadcasted_iota(i32, (1, Lb), 1)
    key_pos = past - Lb + (lane & (half - 1)) * per_tile + _div_pow2(lane, half)
    diff = past - key_pos
    valid = (diff >= 0) & (diff <= WINDOW)
    o_w = jnp.zeros((H, D), f32)
    for g in range(G):
        chunk = lambda c: jnp.concatenate(
            [buf_ref[pl.ds(par * KV_CHUNKS + c, half, stride=SUBLANES), :] for par in range(per_tile)], axis=0)
        new_k, new_v = new_ref[g:g + 1, :], new_ref[G + g:G + g + 1, :]
        s_b = jnp.where(valid, _mm_nt(qs_b, chunk(g).astype(bf16)), NEG)
        s_n = jnp.sum(qs * new_k, axis=-1, keepdims=True)
        mx = jnp.maximum(jnp.max(s_b, axis=-1, keepdims=True), s_n)
        e_b = jnp.where(valid, jnp.exp(s_b - mx), 0.0)
        e_n = jnp.exp(s_n - mx)
        den = jnp.sum(e_b, axis=-1, keepdims=True) + e_n
        og = _mm((e_b / den).astype(bf16), chunk(G + g).astype(bf16)) + (e_n / den) * new_v
        o_w = jnp.where(grp == g, og, o_w)
    gate = _sigmoid(gt_ref[...])
    o_ref[...] = gate[:, 0:1] * oc_ref[...] + gate[:, 1:2] * os_ref[...] + gate[:, 2:3] * o_w
    win_ref[0:rows - KV_CHUNKS, :] = buf_ref[KV_CHUNKS:rows, :]
    win_ref[rows - KV_CHUNKS:rows, :] = new_ref[...]


def _nsa_sample_win(q, win_buf, kw_new, o_c, o_s, gt, past):
    B, H, D = q.shape
    rows = win_buf.shape[1]
    assert rows == WINDOW * KV_CHUNKS
    per_b = lambda *s: pl.BlockSpec((None,) + s, lambda b: (b,) + (0,) * len(s))
    return pl.pallas_call(
        functools.partial(_nsa_sample_win_kernel, past=past),
        grid=(B,),
        in_specs=[per_b(H, D), per_b(rows, D), per_b(KV_CHUNKS, D), per_b(H, D), per_b(H, D), per_b(H, 3)],
        out_specs=[per_b(H, D), per_b(rows, D)],
        out_shape=[jax.ShapeDtypeStruct((B, H, D), f32), jax.ShapeDtypeStruct((B, rows, D), f32)],
        compiler_params=_cparams("parallel"),
        name="nsa_sample_win",
    )(q, win_buf, kw_new, o_c, o_s, gt.reshape(B, 3, H).transpose(0, 2, 1))


def kernel(x_prompt, x_sample, cache_cmp_kv, cache_sel_kv, cache_win_kv, cache_mem_kv, state_mlstm_c, state_mlstm_n, state_mlstm_m, state_conv, page_table, mem_prompt, g_mix, w_in, w_conv, b_conv, w_mq, w_mk, w_mv, w_mgate, b_mgate, g_mhead, cmp_pe, cmp_wpos, cmp_wphi, w_out, g_memx, g_mems, w_mem_q, w_mem_kv, w_mem_o, g_ffn, w_ff1, w_ff2, g_final):
    depth = w_in.shape[0]
    assert depth == 1 and x_prompt.shape[0] == 1 and x_sample.shape[1] == 1
    T, Dm = x_prompt.shape[1:]
    B = x_sample.shape[0]
    G, Dh, H = NSA_KV_HEADS, NSA_DH, ML_HEADS
    past = page_table.shape[1] * PAGE_SIZE
    assert (past + 1) // CMP_STRIDE == past // CMP_STRIDE
    n_main = 2 * ML_DIM + NSA_DIM + 3 * KV_ROW
    n_gate = 3 * NSA_HEADS
    l = 0
    hp = x_prompt.reshape(T, Dm)
    hs = x_sample.reshape(B, Dm)

    wbd = _blockdiag_weights(jnp.stack([w_mq[l], w_mk[l], w_mv[l]])).astype(bf16)
    w_mg = w_mgate[l].astype(bf16)
    w_mg_t = w_mg.T
    pe_t, wp_t = _compress_tables(cmp_pe[l], cmp_wpos[l])
    wb_in, wb_out, wb_mq, wb_mkv, wb_mo, wb_f1, wb_f2 = (
        w[l].astype(bf16) for w in (w_in, w_out, w_mem_q, w_mem_kv, w_mem_o, w_ff1, w_ff2))
    w_gt = jnp.pad(wb_in[:, n_main:], ((0, 0), (0, LANES - n_gate)))
    q_cb = 2 * ML_DIM // KV_ROW
    c_cb = (2 * ML_DIM + NSA_DIM) // KV_ROW
    n_mq, n_mkv = w_mem_q.shape[-1], w_mem_kv.shape[-1]
    tm_in, tm, tn, tf, tq = 1024, 512, 512, 1024, 256

    u, (kc_lin, ks_lin, kw_lin) = _in_proj(hp, g_mix[l], wb_in, n_main, c_cb, 3, tm_in, "in_proj_p")
    ml_o, c_p, n_p, m_p = _mlstm_prompt(u, w_conv[l], b_conv[l], wbd, w_mg, w_mg_t, b_mgate[l],
                                        g_mhead[l], 128, "mlstm_prompt")
    kvc_p = _compress_prompt(u, c_cb, pe_t, wp_t, cmp_wphi[l])
    nsa_o = _nsa_prompt(hp, g_mix[l], _group_gate_weights(wb_in[:, n_main:]), u, q_cb, kvc_p,
                        *_kv_prep(u, c_cb + 1, c_cb + 2), tq)
    h1 = _out_proj(ml_o, nsa_o, wb_out, hp, tm, Dm, "out_proj_p")
    mem_kv = _norm_matmul(mem_prompt.reshape(-1, Dm), g_mems[l], wb_mkv, n_mkv, mem_prompt.shape[1], tn, "mem_kv")
    h2 = _mem_prompt(h1, g_memx[l], wb_mq, mem_kv, wb_mo, tm, "mem_attn_p")
    y_p = _ffn_final(h2, g_ffn[l], wb_f1, wb_f2, g_final, tm, tf, "ffn_p")

    kv_shape = lambda a: a.reshape(1, 1, -1, 2, G, Dh)
    out_p = (y_p.reshape(1, T, Dm),
             kv_shape(kc_lin), kv_shape(ks_lin),
             kv_shape(kw_lin[(T - min(WINDOW, T)) * KV_CHUNKS:]),
             mem_kv.reshape(1, 1, -1, 2, MEM_HEADS, MEM_DH),
             c_p[None, None], n_p[None, None], m_p[None, None],
             u[T - (ML_CONV - 1):, :ML_DIM][None, None])

    us = _norm_matmul(hs, g_mix[l], wb_in, n_main, B, tn, "in_proj_s")
    gts = _norm_matmul(hs, g_mix[l], w_gt, LANES, B, LANES, "in_gate_s")[:, :n_gate]
    xm_s, og_s = us[:, :ML_DIM], us[:, ML_DIM:2 * ML_DIM]
    q_s = us[:, 2 * ML_DIM:2 * ML_DIM + NSA_DIM].reshape(B, NSA_HEADS, Dh)
    kc_s, ks_s, kw_s = (us[:, (c_cb + i) * KV_ROW:(c_cb + i + 1) * KV_ROW] for i in range(3))
    ml_os, c_s, n_s, m_s = _mlstm_sample(xm_s, og_s, state_conv[l], state_mlstm_c[l], state_mlstm_n[l],
                                         state_mlstm_m[l], w_conv[l], b_conv[l], wbd,
                                         w_mg, b_mgate[l], g_mhead[l])
    pool_c = cache_cmp_kv.reshape(-1, PAGE_SIZE * KV_CHUNKS, Dh)
    pool_s = cache_sel_kv.reshape(-1, PAGE_SIZE * KV_CHUNKS, Dh)
    win_buf = cache_win_kv.reshape(B, -1, Dh)
    mem_buf = cache_mem_kv.reshape(B, -1, MEM_DH)
    kvc_s = _compress_paged(pool_c, page_table, pe_t, wp_t, cmp_wphi[l])
    o_cs, idx = _nsa_sample_cmp(q_s, kvc_s, past)
    o_ss = _nsa_sample_sel(q_s, ks_s.reshape(B, KV_CHUNKS, Dh), pool_s, page_table, idx, past)
    nsa_os, win_new = _nsa_sample_win(q_s, win_buf, kw_s.reshape(B, KV_CHUNKS, Dh), o_cs, o_ss, gts, past)
    h1s = _out_proj(ml_os, nsa_os.reshape(B, NSA_DIM), wb_out, hs, B, Dm, "out_proj_s")
    qm_s = _norm_matmul(h1s, g_memx[l], wb_mq, n_mq, B, n_mq, "mem_q_s")
    om_s = _mem_sample(qm_s, mem_buf, "mem_attn_s")
    h2s = _matmul_res(om_s, wb_mo, h1s, B, Dm, "mem_o_s")
    y_s = _ffn_final(h2s, g_ffn[l], wb_f1, wb_f2, g_final, B, tf, "ffn_s")

    kv_s_shape = lambda a: a.reshape(1, B, 1, 2, G, Dh)
    conv_s = jnp.concatenate([state_conv[l][:, 1:], xm_s[:, None, :]], axis=1)
    out_s = (y_s.reshape(B, 1, Dm), kv_s_shape(kc_s), kv_s_shape(ks_s),
             win_new.reshape(1, B, -1, 2, G, Dh), c_s[None], n_s[None], m_s[None], conv_s[None])

    return (out_p[0], out_s[0]) + out_p[1:] + out_s[1:]
```

```python
import functools

import jax
import jax.numpy as jnp
import numpy as np
from jax import lax
from jax.experimental import pallas as pl
from jax.experimental.pallas import tpu as pltpu

f32 = jnp.float32
bf16 = jnp.bfloat16
i32 = jnp.int32

EPS = 1e-6
NEG = -1e30
FORCE = 1e30
ML_HEADS = 8
ML_DH = 128
ML_DIM = ML_HEADS * ML_DH
ML_CONV = 4
ML_QKV_BLOCK = 4
NSA_HEADS = 8
NSA_KV_HEADS = 2
NSA_GROUP = NSA_HEADS // NSA_KV_HEADS
NSA_DH = 128
NSA_DIM = NSA_HEADS * NSA_DH
NSA_KV_W = NSA_KV_HEADS * NSA_DH
CMP_BLOCK = 32
CMP_STRIDE = 16
SEL_BLOCK = 64
SEL_TOPN = 16
WINDOW = 512
Q_BLOCK = 128
PAGE_SIZE = 128
MEM_HEADS = 4
MEM_DH = 128
KV_ROW = 2 * NSA_KV_W
KV_CHUNKS = KV_ROW // NSA_DH

LANES = 128
SUBLANES = 8
VMEM_LIMIT = 56 * 1024 * 1024

MASK_BIAS = -1e9
REMOVED = -3.0e38
SEL_KT = 1024
LOG2E = 1.4426950408889634
SAFE_SPAN = 100.0


def _cparams(*sem):
    return pltpu.CompilerParams(dimension_semantics=sem, vmem_limit_bytes=VMEM_LIMIT)


def _mm(a, b):
    return jnp.dot(a, b, preferred_element_type=f32)


def _mm_nt(a, b):
    return lax.dot_general(a, b, (((1,), (1,)), ((), ())), preferred_element_type=f32)


def _split2(x):
    h = x.astype(bf16)
    return h, (x - h.astype(f32)).astype(bf16)


def _split3(x):
    h = x.astype(bf16)
    r = x - h.astype(f32)
    m = r.astype(bf16)
    return h, m, (r - m.astype(f32)).astype(bf16)


def _dot3(a, b, mm=_mm):
    ah, al = _split2(a)
    bh, bl = _split2(b)
    return mm(ah, bh) + mm(al, bh) + mm(ah, bl)


def _dot_sel_l(sel, x):
    h, m, l = _split3(x)
    return _mm(sel, h) + _mm(sel, m) + _mm(sel, l)


def _dot_sel_r(x, sel):
    h, m, l = _split3(x)
    return _mm(h, sel) + _mm(m, sel) + _mm(l, sel)


def _rms(x, g):
    return x * lax.rsqrt(jnp.mean(x * x, axis=-1, keepdims=True) + EPS) * g


def _div_pow2(x, d):
    assert d & (d - 1) == 0
    return lax.shift_right_logical(x, jnp.full(x.shape, d.bit_length() - 1, x.dtype))


def _sigmoid(x):
    return 1.0 / (1.0 + jnp.exp(-x))


def _log_sigmoid(x):
    return jnp.minimum(x, 0.0) - jnp.log(1.0 + jnp.exp(-jnp.abs(x)))


def _masked_softmax_rows(s, valid):
    s = jnp.where(valid, s, NEG)
    mx = jnp.max(s, axis=-1, keepdims=True)
    e = jnp.where(valid, jnp.exp(s - mx), 0.0)
    den = jnp.sum(e, axis=-1, keepdims=True)
    return e, jnp.where(den > 0.0, den, 1.0)


def _biased_exp2_rows(s2, bias):
    s2 = s2 + bias
    mx = jnp.max(s2, axis=-1, keepdims=True)
    return jnp.exp2(s2 - mx), mx


def _biased_softmax2_rows(s2, bias):
    e, mx = _biased_exp2_rows(s2, bias)
    inv = jnp.where(mx > 0.5 * NEG, 1.0 / jnp.sum(e, axis=-1, keepdims=True), 0.0)
    return e * inv


def _norm_matmul_kernel(x_ref, g_ref, w_ref, o_ref, xn_ref):
    @pl.when(pl.program_id(1) == 0)
    def _():
        xn_ref[...] = _rms(x_ref[...], g_ref[...]).astype(bf16)

    o_ref[...] = _mm(xn_ref[...], w_ref[...])


def _norm_matmul(x, g, w, n_cols, tm, tn, name):
    M, K = x.shape
    return pl.pallas_call(
        _norm_matmul_kernel,
        grid=(M // tm, n_cols // tn),
        in_specs=[pl.BlockSpec((tm, K), lambda i, j: (i, 0)),
                  pl.BlockSpec((1, K), lambda i, j: (0, 0)),
                  pl.BlockSpec((K, tn), lambda i, j: (0, j))],
        out_specs=pl.BlockSpec((tm, tn), lambda i, j: (i, j)),
        out_shape=jax.ShapeDtypeStruct((M, n_cols), f32),
        scratch_shapes=[pltpu.VMEM((tm, K), bf16)],
        compiler_params=_cparams("parallel", "arbitrary"),
        name=name,
    )(x, g.reshape(1, K), w)


def _in_proj_kernel(x_ref, g_ref, w_ref, o_ref, *rest, kv_block0):
    lin_refs, xn_ref = rest[:-1], rest[-1]
    j = pl.program_id(1)

    @pl.when(j == 0)
    def _():
        xn_ref[...] = _rms(x_ref[...], g_ref[...]).astype(bf16)

    val = _mm(xn_ref[...], w_ref[...])
    o_ref[...] = val
    rows = val.shape[0]
    for i, ref in enumerate(lin_refs):
        @pl.when(j == kv_block0 + i)
        def _():
            for c in range(KV_CHUNKS):
                ref[pl.ds(c, rows, stride=KV_CHUNKS), :] = val[:, c * LANES:(c + 1) * LANES]


def _in_proj(x, g, w, n_cols, kv_block0, n_kv, tm, name):
    M, K = x.shape
    tn = KV_ROW
    lin_spec = pl.BlockSpec((tm * KV_CHUNKS, LANES), lambda i, j: (i, 0))
    outs = pl.pallas_call(
        functools.partial(_in_proj_kernel, kv_block0=kv_block0),
        grid=(M // tm, n_cols // tn),
        in_specs=[pl.BlockSpec((tm, K), lambda i, j: (i, 0)),
                  pl.BlockSpec((1, K), lambda i, j: (0, 0)),
                  pl.BlockSpec((K, tn), lambda i, j: (0, j))],
        out_specs=[pl.BlockSpec((tm, tn), lambda i, j: (i, j))] + [lin_spec] * n_kv,
        out_shape=[jax.ShapeDtypeStruct((M, n_cols), f32)]
        + [jax.ShapeDtypeStruct((M * KV_CHUNKS, LANES), f32)] * n_kv,
        scratch_shapes=[pltpu.VMEM((tm, K), bf16)],
        compiler_params=_cparams("parallel", "arbitrary"),
        name=name,
    )(x, g.reshape(1, K), w)
    return outs[0], outs[1:]


def _out_proj_kernel(a1_ref, a2_ref, w1_ref, w2_ref, r_ref, o_ref):
    o_ref[...] = (r_ref[...] + _mm(a1_ref[...].astype(bf16), w1_ref[...])
                  + _mm(a2_ref[...].astype(bf16), w2_ref[...]))


def _out_proj(a1, a2, w, res, tm, tn, name):
    M, K1 = a1.shape
    K2 = a2.shape[1]
    assert K1 == K2 and w.shape[0] == K1 + K2
    N = w.shape[1]
    return pl.pallas_call(
        _out_proj_kernel,
        grid=(M // tm, N // tn),
        in_specs=[pl.BlockSpec((tm, K1), lambda i, j: (i, 0)),
                  pl.BlockSpec((tm, K2), lambda i, j: (i, 0)),
                  pl.BlockSpec((K1, tn), lambda i, j: (0, j)),
                  pl.BlockSpec((K2, tn), lambda i, j: (1, j)),
                  pl.BlockSpec((tm, tn), lambda i, j: (i, j))],
        out_specs=pl.BlockSpec((tm, tn), lambda i, j: (i, j)),
        out_shape=jax.ShapeDtypeStruct((M, N), f32),
        compiler_params=_cparams("parallel", "arbitrary"),
        name=name,
    )(a1, a2, w, w, res)


def _matmul_res_kernel(a_ref, w_ref, r_ref, o_ref):
    o_ref[...] = r_ref[...] + _mm(a_ref[...].astype(bf16), w_ref[...])


def _matmul_res(a, w, res, tm, tn, name):
    M, K = a.shape
    N = w.shape[1]
    return pl.pallas_call(
        _matmul_res_kernel,
        grid=(M // tm, N // tn),
        in_specs=[pl.BlockSpec((tm, K), lambda i, j: (i, 0)),
                  pl.BlockSpec((K, tn), lambda i, j: (0, j)),
                  pl.BlockSpec((tm, tn), lambda i, j: (i, j))],
        out_specs=pl.BlockSpec((tm, tn), lambda i, j: (i, j)),
        out_shape=jax.ShapeDtypeStruct((M, N), f32),
        compiler_params=_cparams("parallel", "arbitrary"),
        name=name,
    )(a, w, res)


def _ffn_kernel(h_ref, g_ref, w1_ref, w2_ref, gf_ref, y_ref, xn_ref, acc_ref):
    f = pl.program_id(1)

    @pl.when(f == 0)
    def _():
        xn_ref[...] = _rms(h_ref[...], g_ref[...]).astype(bf16)
        acc_ref[...] = jnp.zeros_like(acc_ref)

    a = _mm(xn_ref[...], w1_ref[...])
    a = jnp.square(jnp.maximum(a, 0.0))
    acc_ref[...] += _mm(a.astype(bf16), w2_ref[...])

    @pl.when(f == pl.num_programs(1) - 1)
    def _():
        y_ref[...] = _rms(h_ref[...] + acc_ref[...], gf_ref[...])


def _ffn_final(h, g, w1, w2, g_final, tm, tf, name):
    M, D = h.shape
    F = w1.shape[1]
    return pl.pallas_call(
        _ffn_kernel,
        grid=(M // tm, F // tf),
        in_specs=[pl.BlockSpec((tm, D), lambda i, j: (i, 0)),
                  pl.BlockSpec((1, D), lambda i, j: (0, 0)),
                  pl.BlockSpec((D, tf), lambda i, j: (0, j)),
                  pl.BlockSpec((tf, D), lambda i, j: (j, 0)),
                  pl.BlockSpec((1, D), lambda i, j: (0, 0))],
        out_specs=pl.BlockSpec((tm, D), lambda i, j: (i, 0)),
        out_shape=jax.ShapeDtypeStruct((M, D), f32),
        scratch_shapes=[pltpu.VMEM((tm, D), bf16), pltpu.VMEM((tm, D), f32)],
        compiler_params=_cparams("parallel", "arbitrary"),
        name=name,
    )(h, g.reshape(1, D), w1, w2, g_final.reshape(1, D))


def _mem_prompt_kernel(h_ref, g_ref, wq_ref, k_ref, v_ref, wo_ref, o_ref):
    h = h_ref[...]
    xn = _rms(h, g_ref[...]).astype(bf16)
    q = _mm(xn, wq_ref[...]) * (MEM_DH ** -0.5)
    outs = []
    for hd in range(MEM_HEADS):
        sl = slice(hd * MEM_DH, (hd + 1) * MEM_DH)
        s = _mm_nt(q[:, sl].astype(bf16), k_ref[:, sl].astype(bf16))
        e = jnp.exp(s - jnp.max(s, axis=-1, keepdims=True))
        p = e / jnp.sum(e, axis=-1, keepdims=True)
        outs.append(_mm(p.astype(bf16), v_ref[:, sl].astype(bf16)))
    o = jnp.concatenate(outs, axis=-1)
    o_ref[...] = h + _mm(o.astype(bf16), wo_ref[...])


def _mem_prompt(h, g, wq, mem_kv, wo, tm, name):
    M, D = h.shape
    HD = MEM_HEADS * MEM_DH
    ML = mem_kv.shape[0]
    return pl.pallas_call(
        _mem_prompt_kernel,
        grid=(M // tm,),
        in_specs=[pl.BlockSpec((tm, D), lambda i: (i, 0)),
                  pl.BlockSpec((1, D), lambda i: (0, 0)),
                  pl.BlockSpec((D, HD), lambda i: (0, 0)),
                  pl.BlockSpec((ML, HD), lambda i: (0, 0)),
                  pl.BlockSpec((ML, HD), lambda i: (0, 1)),
                  pl.BlockSpec((HD, D), lambda i: (0, 0))],
        out_specs=pl.BlockSpec((tm, D), lambda i: (i, 0)),
        out_shape=jax.ShapeDtypeStruct((M, D), f32),
        compiler_params=_cparams("parallel"),
        name=name,
    )(h, g.reshape(1, D), wq, mem_kv, mem_kv, wo)


def _mem_sample_kernel(q_ref, kv_ref, o_ref):
    per_tok = 2 * MEM_HEADS
    ML = kv_ref.shape[0] // per_tok
    q = q_ref[...] * (MEM_DH ** -0.5)
    outs = []
    for hd in range(MEM_HEADS):
        sl = slice(hd * MEM_DH, (hd + 1) * MEM_DH)
        k = kv_ref[pl.ds(hd, ML, stride=per_tok), :]
        v = kv_ref[pl.ds(MEM_HEADS + hd, ML, stride=per_tok), :]
        s = jnp.sum(k * q[:, sl], axis=-1, keepdims=True)
        e = jnp.exp(s - jnp.max(s, axis=0, keepdims=True))
        p = e / jnp.sum(e, axis=0, keepdims=True)
        outs.append(jnp.sum(p * v, axis=0, keepdims=True))
    o_ref[...] = jnp.concatenate(outs, axis=-1)


def _mem_sample(q, kv, name):
    B, HD = q.shape
    rows = kv.shape[1]
    out = pl.pallas_call(
        _mem_sample_kernel,
        grid=(B,),
        in_specs=[pl.BlockSpec((None, 1, HD), lambda b: (b, 0, 0)),
                  pl.BlockSpec((None, rows, MEM_DH), lambda b: (b, 0, 0))],
        out_specs=pl.BlockSpec((None, 1, HD), lambda b: (b, 0, 0)),
        out_shape=jax.ShapeDtypeStruct((B, 1, HD), f32),
        compiler_params=_cparams("parallel"),
        name=name,
    )(q.reshape(B, 1, HD), kv)
    return out.reshape(B, HD)


def _blockdiag_weights(w):
    n = w.shape[0]
    per = LANES // ML_QKV_BLOCK
    nchunk = w.shape[1] // per
    wc = w.reshape(n, nchunk, per, ML_QKV_BLOCK, ML_QKV_BLOCK)
    eye = jnp.eye(per, dtype=w.dtype)
    full = wc[:, :, :, :, None, :] * eye[None, None, :, None, :, None]
    return full.reshape(n, nchunk, LANES, LANES)


def _ml_qkv_gates(xc, xm, wbd_ref, wg_ref, bg_ref):
    xc_b, xm_b = xc.astype(bf16), xm.astype(bf16)
    qs, ks, vs = [], [], []
    for c in range(ML_DIM // LANES):
        sl = slice(c * LANES, (c + 1) * LANES)
        qs.append(_mm(xc_b[:, sl], wbd_ref[0, c]))
        ks.append(_mm(xc_b[:, sl], wbd_ref[1, c]) * (ML_DH ** -0.5))
        vs.append(_mm(xm_b[:, sl], wbd_ref[2, c]))
    q = jnp.concatenate(qs, axis=-1)
    k = jnp.concatenate(ks, axis=-1)
    v = jnp.concatenate(vs, axis=-1)
    qkv_b = jnp.concatenate([q, k, v], axis=-1).astype(bf16)
    gates = _mm(qkv_b, wg_ref[...]) + bg_ref[...]
    return q, k, v, gates, qkv_b


def _mlstm_prompt_kernel(xm_ref, og_ref, wconv_ref, bconv_ref, wbd_ref, wg_ref, wgt_ref,
                         bg_ref, bgt_ref, gh_ref, o_ref, c_ref, n_ref, m_ref, prev_sc, *, L):
    H, D = ML_HEADS, ML_DH

    @pl.when(pl.program_id(0) == 0)
    def _():
        prev_sc[...] = jnp.zeros_like(prev_sc)
        c_ref[...] = jnp.zeros_like(c_ref)
        n_ref[...] = jnp.zeros_like(n_ref)
        m_ref[...] = jnp.full(m_ref.shape, NEG, f32)

    x = xm_ref[...]
    full = jnp.concatenate([prev_sc[...], x], axis=0)
    y = bconv_ref[...]
    for j in range(ML_CONV):
        off = SUBLANES - (ML_CONV - 1) + j
        y = y + full[off:off + L] * wconv_ref[j:j + 1, :]
    prev_sc[...] = x[L - SUBLANES:L]
    xc = y * _sigmoid(y)

    q, k, v, gates, qkv_b = _ml_qkv_gates(xc, x, wbd_ref, wg_ref, bg_ref)
    gates_t = _mm_nt(wgt_ref[...], qkv_b) + bgt_ref[...]
    ig_c = gates[:, 0:H]
    lf_c = _log_sigmoid(gates[:, H:2 * H])
    ig_r = gates_t[0:H, :]
    lf_r = _log_sigmoid(gates_t[H:2 * H, :])

    t_i = lax.broadcasted_iota(i32, (L, L), 0)
    s_i = lax.broadcasted_iota(i32, (L, L), 1)
    causal = s_i <= t_i
    tri = jnp.where(causal, 1.0, 0.0).astype(bf16)
    b_c = _dot_sel_l(tri, lf_c)
    tri_u = jnp.where(t_i <= s_i, 1.0, 0.0).astype(bf16)
    b_r = _dot_sel_r(lf_r, tri_u)

    for h in range(H):
        sl = slice(h * D, (h + 1) * D)
        qh, kh, vh = q[:, sl], k[:, sl], v[:, sl]
        bc = b_c[:, h:h + 1]
        m_prev = m_ref[h:h + 1, 0:1]
        d_in = jnp.where(causal, bc - b_r[h:h + 1, :] + ig_r[h:h + 1, :], NEG)
        d_x = bc + m_prev
        m_t = jnp.maximum(d_x, jnp.max(d_in, axis=-1, keepdims=True))
        w_in = jnp.exp(d_in - m_t)
        w_x = jnp.exp(d_x - m_t)
        qb = qh.astype(bf16)
        kb = kh.astype(bf16)
        vb = vh.astype(bf16)
        s = _mm_nt(qb, kb) * w_in
        c_old = c_ref[h]
        n_old = n_ref[h:h + 1, :]
        num = _mm(s.astype(bf16), vb) + w_x * _mm(qb, c_old.astype(bf16))
        den = jnp.sum(s, axis=-1, keepdims=True) + w_x * jnp.sum(qh * n_old, axis=-1, keepdims=True)
        hh = num / jnp.maximum(jnp.abs(den), jnp.exp(-m_t))
        m_new = m_t[L - 1:L, :]
        b_last = bc[L - 1:L, :]
        g_x = jnp.exp(b_last + m_prev - m_new)
        g_s = jnp.exp(b_last - bc + ig_c[:, h:h + 1] - m_new)
        ks_ = kh * g_s
        c_ref[h] = g_x * c_old + _mm(ks_.T.astype(bf16), vb)
        n_ref[h:h + 1, :] = g_x * n_old + jnp.sum(ks_, axis=0, keepdims=True)
        m_ref[h:h + 1, :] = jnp.broadcast_to(m_new, (1, LANES))
        hn = hh * lax.rsqrt(jnp.mean(hh * hh, axis=-1, keepdims=True) + EPS) * gh_ref[:, sl]
        o_ref[:, sl] = _sigmoid(og_ref[:, sl]) * hn


def _mlstm_prompt(u, w_conv, b_conv, wbd, w_gate, w_gate_t, b_gate, g_head, L, name):
    T = u.shape[0]
    H, D = ML_HEADS, ML_DH
    nch = ML_DIM // LANES
    full2 = lambda shape: pl.BlockSpec(shape, lambda i: (0,) * len(shape))
    out, c, n, m = pl.pallas_call(
        functools.partial(_mlstm_prompt_kernel, L=L),
        grid=(T // L,),
        in_specs=[pl.BlockSpec((L, ML_DIM), lambda i: (i, 0)),
                  pl.BlockSpec((L, ML_DIM), lambda i: (i, 1)),
                  full2((ML_CONV, ML_DIM)), full2((1, ML_DIM)),
                  full2((3, nch, LANES, LANES)),
                  full2((3 * ML_DIM, 2 * H)), full2((2 * H, 3 * ML_DIM)),
                  full2((1, 2 * H)), full2((2 * H, 1)), full2((1, ML_DIM))],
        out_specs=[pl.BlockSpec((L, ML_DIM), lambda i: (i, 0)),
                   full2((H, D, D)), full2((H, D)), full2((H, LANES))],
        out_shape=[jax.ShapeDtypeStruct((T, ML_DIM), f32),
                   jax.ShapeDtypeStruct((H, D, D), f32),
                   jax.ShapeDtypeStruct((H, D), f32),
                   jax.ShapeDtypeStruct((H, LANES), f32)],
        scratch_shapes=[pltpu.VMEM((SUBLANES, ML_DIM), f32)],
        compiler_params=_cparams("arbitrary"),
        name=name,
    )(u, u, w_conv, b_conv.reshape(1, ML_DIM), wbd, w_gate, w_gate_t,
      b_gate.reshape(1, 2 * H), b_gate.reshape(2 * H, 1), g_head.reshape(1, ML_DIM))
    return out, c, n, m[:, 0]


def _mlstm_sample_pre_kernel(xm_ref, s0_ref, s1_ref, s2_ref, wconv_ref, bconv_ref, wbd_ref,
                             wg_ref, bg_ref, q_ref, k_ref, v_ref, g_ref):
    x = xm_ref[...]
    y = (bconv_ref[...] + s0_ref[...] * wconv_ref[0:1, :] + s1_ref[...] * wconv_ref[1:2, :]
         + s2_ref[...] * wconv_ref[2:3, :] + x * wconv_ref[3:4, :])
    xc = y * _sigmoid(y)
    q, k, v, gates, _ = _ml_qkv_gates(xc, x, wbd_ref, wg_ref, bg_ref)
    q_ref[...] = q
    k_ref[...] = k
    v_ref[...] = v
    g_ref[...] = gates


def _mlstm_sample_step_kernel(qc_ref, kc_ref, q_ref, k_ref, v_ref, gt_ref, og_ref, gh_ref, c_ref, n_ref, m_ref,
                              o_ref, cn_ref, nn_ref, mn_ref):
    H, D = ML_HEADS, ML_DH
    ig = gt_ref[0:H, :]
    lf = _log_sigmoid(gt_ref[H:2 * H, :])
    m_old = m_ref[...]
    m_new = jnp.maximum(lf + m_old, ig)
    w_in = jnp.exp(ig - m_new)
    w_x = jnp.exp(lf + m_old - m_new)
    mn_ref[...] = m_new
    q, k, v, n_old = q_ref[...], k_ref[...], v_ref[...], n_ref[...]
    cq = jnp.concatenate([jnp.sum(c_ref[h] * qc_ref[:, h:h + 1], axis=0, keepdims=True) for h in range(H)], axis=0)
    s = jnp.sum(q * k, axis=-1, keepdims=True) * w_in
    num = s * v + w_x * cq
    den = s + w_x * jnp.sum(n_old * q, axis=-1, keepdims=True)
    hh = num / jnp.maximum(jnp.abs(den), jnp.exp(-m_new))
    hn = hh * lax.rsqrt(jnp.mean(hh * hh, axis=-1, keepdims=True) + EPS) * gh_ref[...]
    o_ref[...] = _sigmoid(og_ref[...]) * hn
    nn_ref[...] = w_x * n_old + w_in * k
    for h in range(H):
        cn_ref[h] = w_x[h:h + 1, :] * c_ref[h] + w_in[h:h + 1, :] * (kc_ref[:, h:h + 1] * v[h:h + 1, :])


def _mlstm_sample(xm, og, conv_state, c0, n0, m0, w_conv, b_conv, wbd, w_gate, b_gate, g_head):
    B = xm.shape[0]
    H, D = ML_HEADS, ML_DH
    sds = lambda *s: jax.ShapeDtypeStruct(s, f32)
    q, k, v, gates = pl.pallas_call(
        _mlstm_sample_pre_kernel,
        out_shape=[sds(B, ML_DIM), sds(B, ML_DIM), sds(B, ML_DIM), sds(B, 2 * H)],
        compiler_params=pltpu.CompilerParams(vmem_limit_bytes=VMEM_LIMIT),
        name="mlstm_sample_pre",
    )(xm, conv_state[:, 0], conv_state[:, 1], conv_state[:, 2], w_conv, b_conv.reshape(1, ML_DIM),
      wbd, w_gate, b_gate.reshape(1, 2 * H))
    q3, k3, v3 = (a.reshape(B, H, D) for a in (q, k, v))
    per_b = lambda *s: pl.BlockSpec((None,) + s, lambda b: (b,) + (0,) * len(s))
    out, c, n, m = pl.pallas_call(
        _mlstm_sample_step_kernel,
        grid=(B,),
        in_specs=[per_b(D, H), per_b(D, H), per_b(H, D), per_b(H, D), per_b(H, D), per_b(2 * H, 1),
                  per_b(H, D), pl.BlockSpec((H, D), lambda b: (0, 0)),
                  per_b(H, D, D), per_b(H, D), per_b(H, 1)],
        out_specs=[per_b(H, D), per_b(H, D, D), per_b(H, D), per_b(H, 1)],
        out_shape=[sds(B, H, D), sds(B, H, D, D), sds(B, H, D), sds(B, H, 1)],
        compiler_params=_cparams("parallel"),
        name="mlstm_sample_step",
    )(q3.transpose(0, 2, 1), k3.transpose(0, 2, 1), q3, k3, v3, gates.reshape(B, 2 * H, 1),
      og.reshape(B, H, D), g_head.reshape(H, D), c0, n0, m0.reshape(B, H, 1))
    return out.reshape(B, ML_DIM), c, n, m.reshape(B, H)


def _compress_kernel(x_ref, pe_ref, wp_ref, wphi_ref, o_ref, f0_sc, f1_sc, mn_sc):
    step = pl.program_id(0)
    sub = x_ref.shape[0] // CMP_STRIDE
    x3 = x_ref[...].reshape(sub, CMP_STRIDE, KV_ROW)
    base = pl.multiple_of(step * sub, sub)
    for o, sc in ((0, f0_sc), (1, f1_sc)):
        y = x3 + pe_ref[o][None]
        sc[pl.ds(base, sub), :] = jnp.sum(y * _sigmoid(y) * wp_ref[o][None], axis=1)
    mn_sc[pl.ds(base, sub), :] = jnp.sum(x3, axis=1) * (1.0 / CMP_STRIDE)

    @pl.when(step == pl.num_programs(0) - 1)
    def _():
        ns = f0_sc.shape[0]
        feat = f0_sc[...] + pltpu.roll(f1_sc[...], ns - 1, axis=0)
        mn = mn_sc[...]
        pooled = (mn + pltpu.roll(mn, ns - 1, axis=0)) * (CMP_STRIDE / CMP_BLOCK)
        for c in range(2):
            for g in range(NSA_KV_HEADS):
                sl = slice((c * NSA_KV_HEADS + g) * NSA_DH, (c * NSA_KV_HEADS + g + 1) * NSA_DH)
                o_ref[:, sl] = pooled[:, sl] + _dot3(feat[:, sl], wphi_ref[c])


def _compress_tables(pe, wpos):
    def lay(a):
        r = CMP_BLOCK // CMP_STRIDE
        a = a.reshape(2, r, CMP_STRIDE, NSA_DH).transpose(1, 2, 0, 3)
        a = jnp.broadcast_to(a[:, :, :, None, :], (r, CMP_STRIDE, 2, NSA_KV_HEADS, NSA_DH))
        return a.reshape(r, CMP_STRIDE, KV_ROW)
    return lay(pe), lay(wpos)


def _compress_prompt(u, col_block, pe_t, wp_t, wphi, rows=512):
    T = u.shape[0]
    n_sub = T // CMP_STRIDE
    const = lambda shape: pl.BlockSpec(shape, lambda s: (0,) * len(shape))
    return pl.pallas_call(
        _compress_kernel,
        grid=(T // rows,),
        in_specs=[pl.BlockSpec((rows, KV_ROW), lambda s: (s, col_block)),
                  const(pe_t.shape), const(wp_t.shape), const(wphi.shape)],
        out_specs=const((n_sub, KV_ROW)),
        out_shape=jax.ShapeDtypeStruct((n_sub, KV_ROW), f32),
        scratch_shapes=[pltpu.VMEM((n_sub, KV_ROW), f32)] * 3,
        compiler_params=_cparams("arbitrary"),
        name="compress_prompt",
    )(u, pe_t, wp_t, wphi)


def _compress_paged_kernel(pt_ref, *refs, n_pages):
    pages = refs[:n_pages]
    pe_ref, wp_ref, wphi_ref, o_ref, f0_sc, f1_sc, mn_sc = refs[n_pages:]
    step = pl.program_id(1)
    sub = PAGE_SIZE // CMP_STRIDE
    tiles = CMP_STRIDE * KV_CHUNKS // SUBLANES
    packed = 2 * SUBLANES
    ptiles = CMP_STRIDE * KV_CHUNKS // packed
    out_rows = sub * SUBLANES
    for p in range(n_pages):
        x = pages[p][...]
        base = pl.multiple_of((step * n_pages + p) * out_rows, out_rows)
        xh = (x.astype(bf16) * 0.5).reshape(sub, ptiles, packed, LANES)
        for o, sc in ((0, f0_sc), (1, f1_sc)):
            y = xh + pe_ref[o][None]
            b = y * wp_ref[o][None]
            z = b + b * jnp.tanh(y)
            z = (z[:, 0] + z[:, 1]) + (z[:, 2] + z[:, 3])
            zf = z.astype(f32).reshape(sub, packed // SUBLANES, SUBLANES, LANES)
            sc[pl.ds(base, out_rows), :] = jnp.sum(zf, axis=1).reshape(out_rows, LANES)
        mean = jnp.sum(x.reshape(sub, tiles, SUBLANES, LANES), axis=1) * (1.0 / CMP_STRIDE)
        mn_sc[pl.ds(base, out_rows), :] = mean.reshape(out_rows, LANES)

    @pl.when(step == pl.num_programs(1) - 1)
    def _():
        ns = f0_sc.shape[0] // SUBLANES
        for c in range(KV_CHUNKS):
            col = lambda sc: (sc[pl.ds(c, ns, stride=SUBLANES), :]
                              + sc[pl.ds(KV_CHUNKS + c, ns, stride=SUBLANES), :])
            feat = col(f0_sc) + pltpu.roll(col(f1_sc), ns - 1, axis=0)
            mn = col(mn_sc)
            pooled = (mn + pltpu.roll(mn, ns - 1, axis=0)) * (CMP_STRIDE / CMP_BLOCK)
            o_ref[:, c * LANES:(c + 1) * LANES] = pooled + _dot3(feat, wphi_ref[c // NSA_KV_HEADS])


def _compress_paged(pool, page_table, pe_t, wp_t, wphi, pages_per_step=32):
    B, n_pages = page_table.shape
    P = pages_per_step
    n_sub = n_pages * PAGE_SIZE // CMP_STRIDE
    r = CMP_BLOCK // CMP_STRIDE
    packed = 2 * SUBLANES
    ptiles = CMP_STRIDE * KV_CHUNKS // packed
    pe4 = (0.5 * pe_t).astype(bf16).reshape(r, ptiles, packed, LANES)
    wp4 = wp_t.astype(bf16).reshape(r, ptiles, packed, LANES)
    page_rows = PAGE_SIZE * KV_CHUNKS
    specs = [pl.BlockSpec((None, page_rows, LANES),
                          functools.partial(lambda b, s, pt, p: (pt[b * n_pages + s * P + p], 0, 0), p=p))
             for p in range(P)]
    const = lambda shape: pl.BlockSpec(shape, lambda *a: (0,) * len(shape))
    gs = pltpu.PrefetchScalarGridSpec(
        num_scalar_prefetch=1, grid=(B, n_pages // P),
        in_specs=specs + [const(pe4.shape), const(wp4.shape), const(wphi.shape)],
        out_specs=pl.BlockSpec((None, n_sub, KV_ROW), lambda b, s, pt: (b, 0, 0)),
        scratch_shapes=[pltpu.VMEM((n_sub * SUBLANES, LANES), f32)] * 3)
    return pl.pallas_call(
        functools.partial(_compress_paged_kernel, n_pages=P),
        grid_spec=gs,
        out_shape=jax.ShapeDtypeStruct((B, n_sub, KV_ROW), f32),
        compiler_params=_cparams("parallel", "arbitrary"),
        name="compress_paged",
    )(page_table.reshape(-1), *([pool] * P), pe4, wp4, wphi)


def _kv_prep_kernel(ks_ref, kw_ref, ksk_ref, ksv_ref, kwk_ref, kwv_ref, kn2_ref):
    rows = ks_ref.shape[0]
    ks = ks_ref[...]
    kw = kw_ref[...]
    r = pl.program_id(0) * rows + lax.broadcasted_iota(i32, (rows, LANES), 0)
    n = lax.broadcasted_iota(i32, (rows, LANES), 1)
    onehot = jnp.where(_div_pow2(r, SEL_BLOCK) == n, 1.0, 0.0).astype(bf16)
    ones_col = jnp.where(n == 0, 1.0, 0.0).astype(bf16)

    @pl.when(pl.program_id(0) == 0)
    def _():
        kn2_ref[...] = jnp.zeros_like(kn2_ref)

    for g in range(NSA_KV_HEADS):
        ksl = slice(g * NSA_DH, (g + 1) * NSA_DH)
        vsl = slice(NSA_KV_W + g * NSA_DH, NSA_KV_W + (g + 1) * NSA_DH)
        kb = ks[:, ksl].astype(bf16)
        ksk_ref[g, :, 0:NSA_DH] = kb
        ksk_ref[g, :, NSA_DH:NSA_DH + LANES] = onehot
        ksv_ref[g, :, 0:NSA_DH] = ks[:, vsl].astype(bf16)
        ksv_ref[g, :, NSA_DH:NSA_DH + LANES] = ones_col
        kwk_ref[g] = kw[:, ksl].astype(bf16)
        kwv_ref[g, :, 0:NSA_DH] = kw[:, vsl].astype(bf16)
        kwv_ref[g, :, NSA_DH:NSA_DH + LANES] = ones_col
        kf = kb.astype(f32)
        n2 = jnp.max(jnp.sum(kf * kf, axis=-1, keepdims=True), axis=0, keepdims=True)
        kn2_ref[g] = jnp.maximum(kn2_ref[g], jnp.broadcast_to(n2, kn2_ref.shape[1:]))


def _kv_prep(u, ks_col_block, kw_col_block, rows=512):
    T = u.shape[0]
    G = NSA_KV_HEADS
    assert (T - 1) // SEL_BLOCK + 1 <= LANES
    sd = lambda w: jax.ShapeDtypeStruct((G, T, w), bf16)
    ospec = lambda w: pl.BlockSpec((G, rows, w), lambda i: (0, i, 0))
    return pl.pallas_call(
        _kv_prep_kernel,
        grid=(T // rows,),
        in_specs=[pl.BlockSpec((rows, KV_ROW), lambda i: (i, ks_col_block)),
                  pl.BlockSpec((rows, KV_ROW), lambda i: (i, kw_col_block))],
        out_specs=[ospec(NSA_DH + LANES), ospec(NSA_DH + LANES), ospec(NSA_DH), ospec(NSA_DH + LANES),
                   pl.BlockSpec((G, SUBLANES, LANES), lambda i: (0, 0, 0))],
        out_shape=[sd(NSA_DH + LANES), sd(NSA_DH + LANES), sd(NSA_DH), sd(NSA_DH + LANES),
                   jax.ShapeDtypeStruct((G, SUBLANES, LANES), f32)],
        compiler_params=_cparams("arbitrary"),
        name="kv_prep",
    )(u, u)


def _top_blocks(score, n_top, axis):
    lane = lax.broadcasted_iota(i32, score.shape, axis).astype(f32)
    width = float(score.shape[axis])
    work = score
    firsts = []
    for _ in range(n_top):
        mx = jnp.max(work, axis=axis, keepdims=True)
        first = jnp.min(jnp.where(work == mx, lane, width), axis=axis, keepdims=True)
        work = jnp.where(lane == first, REMOVED, work)
        firsts.append(first)
    return work, firsts


def _nsa_prompt_kernel(x_ref, gmix_ref, wgt_ref, q_ref, kck_ref, kcv_ref, ksk_ref, ksv_ref, kwk_ref, kwv_ref, kn2_ref,
                       o_ref, shift_sc, m_sc, acc_sc, *, n_sel):
    R, D = NSA_GROUP, NSA_DH
    QB = q_ref.shape[0]
    rows = R * QB
    qb = pl.program_id(1)
    q = q_ref[...]
    q2_b = (jnp.concatenate([q[:, r * D:(r + 1) * D] for r in range(R)], axis=0) * (D ** -0.5 * LOG2E)).astype(bf16)
    pos = qb * QB + lax.broadcasted_iota(i32, (QB, 1), 0)

    def head_bias(valid):
        return jnp.concatenate([jnp.where(valid, 0.0, NEG)] * R, axis=0)

    ns = kck_ref.shape[0]
    s = _mm_nt(q2_b, kck_ref[...].astype(bf16))
    j = lax.broadcasted_iota(i32, (1, ns), 1)
    p_c = _biased_softmax2_rows(s, head_bias(j * CMP_STRIDE + (CMP_BLOCK - 1) <= pos))
    o_c = _mm(p_c.astype(bf16), kcv_ref[...].astype(bf16))

    imp = p_c[0:QB]
    for r in range(1, R):
        imp = imp + p_c[r * QB:(r + 1) * QB]
    ratio = SEL_BLOCK // CMP_STRIDE
    off = CMP_BLOCK // CMP_STRIDE - 1
    nn = lax.broadcasted_iota(i32, (LANES, ns), 0)
    jj = lax.broadcasted_iota(i32, (LANES, ns), 1)
    overlap_t = jnp.where((jj >= ratio * nn - off) & (jj < ratio * nn + ratio), 1.0, 0.0).astype(bf16)
    imp_sel = sum(_mm_nt(overlap_t, part) for part in _split3(imp))
    n_idx = lax.broadcasted_iota(i32, (LANES, QB), 0)
    cur = _div_pow2(qb * QB + lax.broadcasted_iota(i32, (LANES, QB), 1), SEL_BLOCK)
    forced = (n_idx == 0) | (n_idx == cur) | (n_idx == cur - 1)
    score = jnp.where(forced, FORCE, jnp.where(n_idx <= cur, imp_sel, NEG))
    score = jnp.where(n_idx < n_sel, score, REMOVED)
    taken, _ = _top_blocks(score, min(SEL_TOPN, n_sel), axis=0)
    bias = jnp.where((taken < 0.5 * REMOVED) & (n_idx <= cur), 0.0, MASK_BIAS).T

    KT = SEL_KT
    last = (qb * QB + QB - 1) // KT
    q2 = q2_b.astype(f32)
    bound = jnp.sqrt(jnp.sum(q2 * q2, axis=-1, keepdims=True) * kn2_ref[0:1, 0:1]) * 1.01 + 1e-3
    k_self = ksk_ref[pl.ds(pl.multiple_of(qb * QB, QB), QB), 0:D].astype(f32)
    s_self = jnp.concatenate([jnp.sum(q2[r * QB:(r + 1) * QB] * k_self, axis=-1, keepdims=True) for r in range(R)],
                             axis=0)
    shift_sc[...] = bound
    diag_bias = head_bias(last * KT + lax.broadcasted_iota(i32, (1, KT), 1) <= pos)

    def shifted_queries(shift):
        return jnp.concatenate(
            [jnp.concatenate([q2_b[r * QB:(r + 1) * QB], (bias - shift[r * QB:(r + 1) * QB]).astype(bf16)], axis=1)
             for r in range(R)], axis=0)

    def key_tile(ref, kt):
        return ref[pl.ds(pl.multiple_of(kt * KT, KT), KT), :]

    @pl.when(jnp.max(bound - s_self) > SAFE_SPAN)
    def _():
        qp0 = shifted_queries(jnp.zeros_like(bound))

        def lane_max(kt, sk):
            mx = m_sc[...]
            for c in range(KT // LANES):
                mx = jnp.maximum(mx, sk[:, c * LANES:(c + 1) * LANES])
            m_sc[...] = mx

        def max_pass(kt, carry):
            lane_max(kt, _mm_nt(qp0, key_tile(ksk_ref, kt)))
            return carry

        m_sc[...] = jnp.full(m_sc.shape, NEG, f32)
        lax.fori_loop(0, last, max_pass, 0)
        lane_max(last, _mm_nt(qp0, key_tile(ksk_ref, last)) + diag_bias)
        shift_sc[...] = jnp.max(m_sc[...], axis=-1, keepdims=True)

    qp = shifted_queries(shift_sc[...])
    acc_sc[...] = jnp.zeros_like(acc_sc)

    def accumulate(kt, s2):
        acc_sc[...] += _mm(jnp.exp2(s2).astype(bf16), key_tile(ksv_ref, kt))

    def body(kt, carry):
        accumulate(kt, _mm_nt(qp, key_tile(ksk_ref, kt)))
        return carry

    lax.fori_loop(0, last, body, 0)
    accumulate(last, _mm_nt(qp, key_tile(ksk_ref, last)) + diag_bias)
    acc = acc_sc[...]
    o_s = acc[:, 0:D] * (1.0 / acc[:, D:D + 1])

    wlen = WINDOW + QB
    wstart = pl.multiple_of(jnp.maximum(qb * QB - WINDOW, 0), int(np.gcd(QB, WINDOW)))
    sw = _mm_nt(q2_b, kwk_ref[pl.ds(wstart, wlen), :])
    diff = pos - (wstart + lax.broadcasted_iota(i32, (1, wlen), 1))
    e_w, _ = _biased_exp2_rows(sw, head_bias((diff >= 0) & (diff <= WINDOW)))
    acc_w = _mm(e_w.astype(bf16), kwv_ref[pl.ds(wstart, wlen), :])
    o_w = acc_w[:, 0:D] * (1.0 / acc_w[:, D:D + 1])

    gate = _sigmoid(_mm(_rms(x_ref[...], gmix_ref[...]).astype(bf16), wgt_ref[...]))
    for r in range(R):
        rs = slice(r * QB, (r + 1) * QB)
        o_ref[:, r * D:(r + 1) * D] = (gate[:, r:r + 1] * o_c[rs] + gate[:, R + r:R + r + 1] * o_s[rs]
                                       + gate[:, 2 * R + r:2 * R + r + 1] * o_w[rs])


def _nsa_prompt(x, g_mix, w_gate_g, u, q_col_block, kvc, ksk, ksv, kwk, kwv, kn2, tq):
    T, Dm = x.shape
    G, R, D, QB = NSA_KV_HEADS, NSA_GROUP, NSA_DH, tq
    ns = kvc.shape[0]
    n_sel = (T - 1) // SEL_BLOCK + 1
    assert T % SEL_KT == 0 and T % QB == 0 and T >= WINDOW + QB and n_sel <= LANES
    rows = R * QB
    res = lambda w: pl.BlockSpec((None, T, w), lambda g, i: (g, 0, 0))
    return pl.pallas_call(
        functools.partial(_nsa_prompt_kernel, n_sel=n_sel),
        grid=(G, T // QB),
        in_specs=[pl.BlockSpec((QB, Dm), lambda g, i: (i, 0)),
                  pl.BlockSpec((1, Dm), lambda g, i: (0, 0)),
                  pl.BlockSpec((None, Dm, LANES), lambda g, i: (g, 0, 0)),
                  pl.BlockSpec((QB, R * D), lambda g, i: (i, q_col_block + g)),
                  pl.BlockSpec((ns, D), lambda g, i: (0, g)),
                  pl.BlockSpec((ns, D), lambda g, i: (0, G + g)),
                  res(D + LANES), res(D + LANES), res(D), res(D + LANES),
                  pl.BlockSpec((None, SUBLANES, LANES), lambda g, i: (g, 0, 0))],
        out_specs=pl.BlockSpec((QB, R * D), lambda g, i: (i, g)),
        out_shape=jax.ShapeDtypeStruct((T, NSA_DIM), f32),
        scratch_shapes=[pltpu.VMEM((rows, 1), f32), pltpu.VMEM((rows, LANES), f32),
                        pltpu.VMEM((rows, D + LANES), f32)],
        compiler_params=_cparams("parallel", "arbitrary"),
        name="nsa_prompt",
    )(x, g_mix.reshape(1, Dm), w_gate_g, u, kvc, kvc, ksk, ksv, kwk, kwv, kn2)


def _group_gate_weights(w_gate):
    Dm = w_gate.shape[0]
    G, R = NSA_KV_HEADS, NSA_GROUP
    w = w_gate.reshape(Dm, 3, G, R).transpose(2, 0, 1, 3).reshape(G, Dm, 3 * R)
    return jnp.pad(w, ((0, 0), (0, 0), (0, LANES - 3 * R)))


def _nsa_sample_cmp_kernel(q_ref, kvc_ref, oc_ref, idx_ref, *, pos, n_sel, sel_w):
    H, R, D, G = NSA_HEADS, NSA_GROUP, NSA_DH, NSA_KV_HEADS
    ns = kvc_ref.shape[0]
    qs_b = (q_ref[...] * (D ** -0.5)).astype(bf16)
    head = lax.broadcasted_iota(i32, (H, 1), 0)
    grp = _div_pow2(head, R)
    s = jnp.zeros((H, ns), f32)
    for g in range(G):
        s = jnp.where(grp == g, _mm_nt(qs_b, kvc_ref[:, g * D:(g + 1) * D].astype(bf16)), s)
    j = lax.broadcasted_iota(i32, (1, ns), 1)
    e, den = _masked_softmax_rows(s, j * CMP_STRIDE + (CMP_BLOCK - 1) <= pos)
    p_c = e / den
    p_b = p_c.astype(bf16)
    o_c = jnp.zeros((H, D), f32)
    imp = jnp.zeros((H, ns), f32)
    for g in range(G):
        o_c = jnp.where(grp == g, _mm(p_b, kvc_ref[:, (G + g) * D:(G + g + 1) * D].astype(bf16)), o_c)
        imp = jnp.where(grp == g, jnp.sum(jnp.where(grp == g, p_c, 0.0), axis=0, keepdims=True), imp)
    oc_ref[...] = o_c
    ratio = SEL_BLOCK // CMP_STRIDE
    off = CMP_BLOCK // CMP_STRIDE - 1
    nn = lax.broadcasted_iota(i32, (sel_w, ns), 0)
    jj = lax.broadcasted_iota(i32, (sel_w, ns), 1)
    overlap_t = jnp.where((jj >= ratio * nn - off) & (jj < ratio * nn + ratio), 1.0, 0.0).astype(bf16)
    imp_sel = sum(_mm_nt(overlap_t, part) for part in _split3(imp))
    n_idx = lax.broadcasted_iota(i32, (sel_w, H), 0)
    cur = pos // SEL_BLOCK
    forced = (n_idx == 0) | (n_idx == cur) | (n_idx == cur - 1)
    score = jnp.where(forced, FORCE, jnp.where(n_idx <= cur, imp_sel, NEG))
    score = jnp.where(n_idx < n_sel, score, REMOVED)
    _, firsts = _top_blocks(score, SEL_TOPN, axis=0)
    idx_ref[...] = jnp.concatenate(firsts, axis=0).astype(i32)


def _nsa_sample_cmp(q, kvc, pos):
    B, H, D = q.shape
    ns = kvc.shape[1]
    n_sel = pos // SEL_BLOCK + 1
    assert n_sel >= SEL_TOPN
    sel_w = -(-n_sel // SUBLANES) * SUBLANES
    o_c, idx = pl.pallas_call(
        functools.partial(_nsa_sample_cmp_kernel, pos=pos, n_sel=n_sel, sel_w=sel_w),
        grid=(B,),
        in_specs=[pl.BlockSpec((None, H, D), lambda b: (b, 0, 0)),
                  pl.BlockSpec((None, ns, KV_ROW), lambda b: (b, 0, 0))],
        out_specs=[pl.BlockSpec((None, H, D), lambda b: (b, 0, 0)),
                   pl.BlockSpec((None, SEL_TOPN, H), lambda b: (b, 0, 0))],
        out_shape=[jax.ShapeDtypeStruct((B, H, D), f32), jax.ShapeDtypeStruct((B, SEL_TOPN, H), i32)],
        compiler_params=_cparams("parallel"),
        name="nsa_sample_cmp",
    )(q, kvc)
    return o_c, idx[:, :, ::NSA_GROUP].transpose(0, 2, 1)


def _nsa_sample_sel_kernel(idx_ref, pt_ref, q_ref, new_ref, *refs, past):
    H, R, D, G = NSA_HEADS, NSA_GROUP, NSA_DH, NSA_KV_HEADS
    blocks, o_ref = refs[:G * SEL_TOPN], refs[G * SEL_TOPN]
    b = pl.program_id(0)
    qs_b = (q_ref[...] * (D ** -0.5)).astype(bf16)
    grp = _div_pow2(lax.broadcasted_iota(i32, (H, 1), 0), R)
    per_tile = SUBLANES // KV_CHUNKS
    half = SEL_BLOCK // per_tile
    o = jnp.zeros((H, D), f32)
    for g in range(G):
        ks, vs = [], []
        lane = lax.broadcasted_iota(i32, (1, SEL_TOPN * SEL_BLOCK), 1)
        slot = _div_pow2(lane, SEL_BLOCK)
        in_slot = lane & (SEL_BLOCK - 1)
        pk_row = (in_slot & (half - 1)) * per_tile + _div_pow2(in_slot, half)
        for i in range(SEL_TOPN):
            blk = blocks[g * SEL_TOPN + i]
            base = idx_ref[(b * G + g) * SEL_TOPN + i] * SEL_BLOCK
            pk_row = pk_row + jnp.where(slot == i, base, 0)
            for par in range(per_tile):
                pk_col = base + per_tile * lax.broadcasted_iota(i32, (half, 1), 0) + par
                old = pk_col < past
                k = blk[pl.ds(par * KV_CHUNKS + g, half, stride=SUBLANES), :]
                v = blk[pl.ds(par * KV_CHUNKS + G + g, half, stride=SUBLANES), :]
                ks.append(jnp.where(old, k, new_ref[g:g + 1, :]).astype(bf16))
                vs.append(jnp.where(old, v, new_ref[G + g:G + g + 1, :]).astype(bf16))
        s = _mm_nt(qs_b, jnp.concatenate(ks, axis=0))
        e, den = _masked_softmax_rows(s, pk_row <= past)
        o = jnp.where(grp == g, _mm((e / den).astype(bf16), jnp.concatenate(vs, axis=0)), o)
    o_ref[...] = o


def _nsa_sample_sel(q, ks_new, pool, page_table, idx, past):
    B, H, D = q.shape
    G = NSA_KV_HEADS
    n_pages = page_table.shape[1]
    halves = PAGE_SIZE // SEL_BLOCK
    blk_rows = SEL_BLOCK * KV_CHUNKS
    pool_h = pool.reshape(pool.shape[0] * halves, blk_rows, D)
    last_old = past // SEL_BLOCK - 1

    def blk_map(b, idx_r, pt_r, g, i):
        blk = jnp.minimum(idx_r[(b * G + g) * SEL_TOPN + i], last_old)
        return (pt_r[b * n_pages + blk // halves] * halves + blk % halves, 0, 0)

    specs = [pl.BlockSpec((None, blk_rows, D), functools.partial(blk_map, g=g, i=i))
             for g in range(G) for i in range(SEL_TOPN)]
    gs = pltpu.PrefetchScalarGridSpec(
        num_scalar_prefetch=2, grid=(B,),
        in_specs=[pl.BlockSpec((None, H, D), lambda b, *_: (b, 0, 0)),
                  pl.BlockSpec((None, KV_CHUNKS, D), lambda b, *_: (b, 0, 0))] + specs,
        out_specs=pl.BlockSpec((None, H, D), lambda b, *_: (b, 0, 0)))
    return pl.pallas_call(
        functools.partial(_nsa_sample_sel_kernel, past=past),
        grid_spec=gs,
        out_shape=jax.ShapeDtypeStruct((B, H, D), f32),
        compiler_params=_cparams("arbitrary"),
        name="nsa_sample_sel",
    )(idx.reshape(-1), page_table.reshape(-1), q, ks_new, *([pool_h] * (G * SEL_TOPN)))


def _nsa_sample_win_kernel(q_ref, buf_ref, new_ref, oc_ref, os_ref, gt_ref, o_ref, win_ref, *, past):
    H, R, D, G = NSA_HEADS, NSA_GROUP, NSA_DH, NSA_KV_HEADS
    rows = buf_ref.shape[0]
    Lb = rows // KV_CHUNKS
    per_tile = SUBLANES // KV_CHUNKS
    half = Lb // per_tile
    qs = q_ref[...] * (D ** -0.5)
    qs_b = qs.astype(bf16)
    grp = _div_pow2(lax.broadcasted_iota(i32, (H, 1), 0), R)
    lane = lax.broadcasted_iota(i32, (1, Lb), 1)
    key_pos = past - Lb + (lane & (half - 1)) * per_tile + _div_pow2(lane, half)
    diff = past - key_pos
    valid = (diff >= 0) & (diff <= WINDOW)
    o_w = jnp.zeros((H, D), f32)
    for g in range(G):
        chunk = lambda c: jnp.concatenate(
            [buf_ref[pl.ds(par * KV_CHUNKS + c, half, stride=SUBLANES), :] for par in range(per_tile)], axis=0)
        new_k, new_v = new_ref[g:g + 1, :], new_ref[G + g:G + g + 1, :]
        s_b = jnp.where(valid, _mm_nt(qs_b, chunk(g).astype(bf16)), NEG)
        s_n = jnp.sum(qs * new_k, axis=-1, keepdims=True)
        mx = jnp.maximum(jnp.max(s_b, axis=-1, keepdims=True), s_n)
        e_b = jnp.where(valid, jnp.exp(s_b - mx), 0.0)
        e_n = jnp.exp(s_n - mx)
        den = jnp.sum(e_b, axis=-1, keepdims=True) + e_n
        og = _mm((e_b / den).astype(bf16), chunk(G + g).astype(bf16)) + (e_n / den) * new_v
        o_w = jnp.where(grp == g, og, o_w)
    gate = _sigmoid(gt_ref[...])
    o_ref[...] = gate[:, 0:1] * oc_ref[...] + gate[:, 1:2] * os_ref[...] + gate[:, 2:3] * o_w
    win_ref[0:rows - KV_CHUNKS, :] = buf_ref[KV_CHUNKS:rows, :]
    win_ref[rows - KV_CHUNKS:rows, :] = new_ref[...]


def _nsa_sample_win(q, win_buf, kw_new, o_c, o_s, gt, past):
    B, H, D = q.shape
    rows = win_buf.shape[1]
    assert rows == WINDOW * KV_CHUNKS
    per_b = lambda *s: pl.BlockSpec((None,) + s, lambda b: (b,) + (0,) * len(s))
    return pl.pallas_call(
        functools.partial(_nsa_sample_win_kernel, past=past),
        grid=(B,),
        in_specs=[per_b(H, D), per_b(rows, D), per_b(KV_CHUNKS, D), per_b(H, D), per_b(H, D), per_b(H, 3)],
        out_specs=[per_b(H, D), per_b(rows, D)],
        out_shape=[jax.ShapeDtypeStruct((B, H, D), f32), jax.ShapeDtypeStruct((B, rows, D), f32)],
        compiler_params=_cparams("parallel"),
        name="nsa_sample_win",
    )(q, win_buf, kw_new, o_c, o_s, gt.reshape(B, 3, H).transpose(0, 2, 1))


def kernel(x_prompt, x_sample, cache_cmp_kv, cache_sel_kv, cache_win_kv, cache_mem_kv, state_mlstm_c, state_mlstm_n, state_mlstm_m, state_conv, page_table, mem_prompt, g_mix, w_in, w_conv, b_conv, w_mq, w_mk, w_mv, w_mgate, b_mgate, g_mhead, cmp_pe, cmp_wpos, cmp_wphi, w_out, g_memx, g_mems, w_mem_q, w_mem_kv, w_mem_o, g_ffn, w_ff1, w_ff2, g_final):
    depth = w_in.shape[0]
    assert depth == 1 and x_prompt.shape[0] == 1 and x_sample.shape[1] == 1
    T, Dm = x_prompt.shape[1:]
    B = x_sample.shape[0]
    G, Dh, H = NSA_KV_HEADS, NSA_DH, ML_HEADS
    past = page_table.shape[1] * PAGE_SIZE
    assert (past + 1) // CMP_STRIDE == past // CMP_STRIDE
    n_main = 2 * ML_DIM + NSA_DIM + 3 * KV_ROW
    n_gate = 3 * NSA_HEADS
    l = 0
    hp = x_prompt.reshape(T, Dm)
    hs = x_sample.reshape(B, Dm)

    wbd = _blockdiag_weights(jnp.stack([w_mq[l], w_mk[l], w_mv[l]])).astype(bf16)
    w_mg = w_mgate[l].astype(bf16)
    w_mg_t = w_mg.T
    pe_t, wp_t = _compress_tables(cmp_pe[l], cmp_wpos[l])
    wb_out, wb_mq, wb_mkv, wb_mo, wb_f1, wb_f2 = (
        w[l].astype(bf16) for w in (w_out, w_mem_q, w_mem_kv, w_mem_o, w_ff1, w_ff2))
    wb_in = w_in[l, :, :n_main].astype(bf16)
    wb_gate = w_in[l, :, n_main:].astype(bf16)
    w_gt = jnp.pad(wb_gate, ((0, 0), (0, LANES - n_gate)))
    q_cb = 2 * ML_DIM // KV_ROW
    c_cb = (2 * ML_DIM + NSA_DIM) // KV_ROW
    n_mq, n_mkv = w_mem_q.shape[-1], w_mem_kv.shape[-1]
    tm_in, tm, tn, tf, tq, ml_chunk = 1024, 512, 512, 1024, 256, 256

    u, (kc_lin, ks_lin, kw_lin) = _in_proj(hp, g_mix[l], wb_in, n_main, c_cb, 3, tm_in, "in_proj_p")
    ml_o, c_p, n_p, m_p = _mlstm_prompt(u, w_conv[l], b_conv[l], wbd, w_mg, w_mg_t, b_mgate[l],
                                        g_mhead[l], ml_chunk, "mlstm_prompt")
    kvc_p = _compress_prompt(u, c_cb, pe_t, wp_t, cmp_wphi[l])
    nsa_o = _nsa_prompt(hp, g_mix[l], _group_gate_weights(wb_gate), u, q_cb, kvc_p,
                        *_kv_prep(u, c_cb + 1, c_cb + 2), tq)
    h1 = _out_proj(ml_o, nsa_o, wb_out, hp, tm, Dm, "out_proj_p")
    mem_kv = _norm_matmul(mem_prompt.reshape(-1, Dm), g_mems[l], wb_mkv, n_mkv, mem_prompt.shape[1], tn, "mem_kv")
    h2 = _mem_prompt(h1, g_memx[l], wb_mq, mem_kv, wb_mo, tm, "mem_attn_p")
    y_p = _ffn_final(h2, g_ffn[l], wb_f1, wb_f2, g_final, tm, tf, "ffn_p")

    kv_shape = lambda a: a.reshape(1, 1, -1, 2, G, Dh)
    out_p = (y_p.reshape(1, T, Dm),
             kv_shape(kc_lin), kv_shape(ks_lin),
             kv_shape(kw_lin[(T - min(WINDOW, T)) * KV_CHUNKS:]),
             mem_kv.reshape(1, 1, -1, 2, MEM_HEADS, MEM_DH),
             c_p[None, None], n_p[None, None], m_p[None, None],
             u[T - (ML_CONV - 1):, :ML_DIM][None, None])

    us = _norm_matmul(hs, g_mix[l], wb_in, n_main, B, tn, "in_proj_s")
    gts = _norm_matmul(hs, g_mix[l], w_gt, LANES, B, LANES, "in_gate_s")[:, :n_gate]
    xm_s, og_s = us[:, :ML_DIM], us[:, ML_DIM:2 * ML_DIM]
    q_s = us[:, 2 * ML_DIM:2 * ML_DIM + NSA_DIM].reshape(B, NSA_HEADS, Dh)
    kc_s, ks_s, kw_s = (us[:, (c_cb + i) * KV_ROW:(c_cb + i + 1) * KV_ROW] for i in range(3))
    ml_os, c_s, n_s, m_s = _mlstm_sample(xm_s, og_s, state_conv[l], state_mlstm_c[l], state_mlstm_n[l],
                                         state_mlstm_m[l], w_conv[l], b_conv[l], wbd,
                                         w_mg, b_mgate[l], g_mhead[l])
    pool_c = cache_cmp_kv.reshape(-1, PAGE_SIZE * KV_CHUNKS, Dh)
    pool_s = cache_sel_kv.reshape(-1, PAGE_SIZE * KV_CHUNKS, Dh)
    win_buf = cache_win_kv.reshape(B, -1, Dh)
    mem_buf = cache_mem_kv.reshape(B, -1, MEM_DH)
    kvc_s = _compress_paged(pool_c, page_table, pe_t, wp_t, cmp_wphi[l])
    o_cs, idx = _nsa_sample_cmp(q_s, kvc_s, past)
    o_ss = _nsa_sample_sel(q_s, ks_s.reshape(B, KV_CHUNKS, Dh), pool_s, page_table, idx, past)
    nsa_os, win_new = _nsa_sample_win(q_s, win_buf, kw_s.reshape(B, KV_CHUNKS, Dh), o_cs, o_ss, gts, past)
    h1s = _out_proj(ml_os, nsa_os.reshape(B, NSA_DIM), wb_out, hs, B, Dm, "out_proj_s")
    qm_s = _norm_matmul(h1s, g_memx[l], wb_mq, n_mq, B, n_mq, "mem_q_s")
    om_s = _mem_sample(qm_s, mem_buf, "mem_attn_s")
    h2s = _matmul_res(om_s, wb_mo, h1s, B, Dm, "mem_o_s")
    y_s = _ffn_final(h2s, g_ffn[l], wb_f1, wb_f2, g_final, B, tf, "ffn_s")

    kv_s_shape = lambda a: a.reshape(1, B, 1, 2, G, Dh)
    conv_s = jnp.concatenate([state_conv[l][:, 1:], xm_s[:, None, :]], axis=1)
    out_s = (y_s.reshape(B, 1, Dm), kv_s_shape(kc_s), kv_s_shape(ks_s),
             win_new.reshape(1, B, -1, 2, G, Dh), c_s[None], n_s[None], m_s[None], conv_s[None])

    return (out_p[0], out_s[0]) + out_p[1:] + out_s[1:]
```

```python
import functools

import jax
import jax.numpy as jnp
import numpy as np
from jax import lax
from jax.experimental import pallas as pl
from jax.experimental.pallas import tpu as pltpu

f32 = jnp.float32
bf16 = jnp.bfloat16
i32 = jnp.int32

EPS = 1e-6
NEG = -1e30
FORCE = 1e30
ML_HEADS = 8
ML_DH = 128
ML_DIM = ML_HEADS * ML_DH
ML_CONV = 4
ML_QKV_BLOCK = 4
NSA_HEADS = 8
NSA_KV_HEADS = 2
NSA_GROUP = NSA_HEADS // NSA_KV_HEADS
NSA_DH = 128
NSA_DIM = NSA_HEADS * NSA_DH
NSA_KV_W = NSA_KV_HEADS * NSA_DH
CMP_BLOCK = 32
CMP_STRIDE = 16
SEL_BLOCK = 64
SEL_TOPN = 16
WINDOW = 512
PAGE_SIZE = 128
MEM_HEADS = 4
MEM_DH = 128
KV_ROW = 2 * NSA_KV_W
KV_CHUNKS = KV_ROW // NSA_DH

LANES = 128
SUBLANES = 8
VMEM_LIMIT = 56 * 1024 * 1024

MASK_BIAS = -1e9
REMOVED = -3.0e38
SEL_KT = 1024
LOG2E = 1.4426950408889634
SAFE_SPAN = 100.0


def _cparams(*sem):
    return pltpu.CompilerParams(dimension_semantics=sem, vmem_limit_bytes=VMEM_LIMIT)


def _mm(a, b):
    return jnp.dot(a, b, preferred_element_type=f32)


def _mm_nt(a, b):
    return lax.dot_general(a, b, (((1,), (1,)), ((), ())), preferred_element_type=f32)


def _split2(x):
    h = x.astype(bf16)
    return h, (x - h.astype(f32)).astype(bf16)


def _split3(x):
    h = x.astype(bf16)
    r = x - h.astype(f32)
    m = r.astype(bf16)
    return h, m, (r - m.astype(f32)).astype(bf16)


def _dot3(a, b, mm=_mm):
    ah, al = _split2(a)
    bh, bl = _split2(b)
    return mm(ah, bh) + mm(al, bh) + mm(ah, bl)


def _dot_sel_l(sel, x):
    h, m, l = _split3(x)
    return _mm(sel, h) + _mm(sel, m) + _mm(sel, l)


def _dot_sel_r(x, sel):
    h, m, l = _split3(x)
    return _mm(h, sel) + _mm(m, sel) + _mm(l, sel)


def _rms(x, g):
    return x * lax.rsqrt(jnp.mean(x * x, axis=-1, keepdims=True) + EPS) * g


def _div_pow2(x, d):
    assert d & (d - 1) == 0
    return lax.shift_right_logical(x, jnp.full(x.shape, d.bit_length() - 1, x.dtype))


def _sigmoid(x):
    return 0.5 * jnp.tanh(0.5 * x) + 0.5


def _log_sigmoid(x):
    return jnp.minimum(x, 0.0) - jnp.log(1.0 + jnp.exp(-jnp.abs(x)))


def _masked_softmax_rows(s, valid):
    s = jnp.where(valid, s, NEG)
    mx = jnp.max(s, axis=-1, keepdims=True)
    e = jnp.where(valid, jnp.exp(s - mx), 0.0)
    den = jnp.sum(e, axis=-1, keepdims=True)
    return e, jnp.where(den > 0.0, den, 1.0)


def _biased_exp2_rows(s2, bias):
    s2 = s2 + bias
    mx = jnp.max(s2, axis=-1, keepdims=True)
    return jnp.exp2(s2 - mx), mx


def _biased_softmax2_rows(s2, bias):
    e, mx = _biased_exp2_rows(s2, bias)
    inv = jnp.where(mx > 0.5 * NEG, 1.0 / jnp.sum(e, axis=-1, keepdims=True), 0.0)
    return e * inv


def _norm_matmul_kernel(x_ref, g_ref, w_ref, o_ref, xn_ref):
    @pl.when(pl.program_id(1) == 0)
    def _():
        xn_ref[...] = _rms(x_ref[...], g_ref[...]).astype(bf16)

    o_ref[...] = _mm(xn_ref[...], w_ref[...])


def _norm_matmul(x, g, w, n_cols, tm, tn, name):
    M, K = x.shape
    return pl.pallas_call(
        _norm_matmul_kernel,
        grid=(M // tm, n_cols // tn),
        in_specs=[pl.BlockSpec((tm, K), lambda i, j: (i, 0)),
                  pl.BlockSpec((1, K), lambda i, j: (0, 0)),
                  pl.BlockSpec((K, tn), lambda i, j: (0, j))],
        out_specs=pl.BlockSpec((tm, tn), lambda i, j: (i, j)),
        out_shape=jax.ShapeDtypeStruct((M, n_cols), f32),
        scratch_shapes=[pltpu.VMEM((tm, K), bf16)],
        compiler_params=_cparams("parallel", "arbitrary"),
        name=name,
    )(x, g.reshape(1, K), w)


def _in_proj_kernel(x_ref, g_ref, w_ref, o_ref, *rest, kv_block0):
    lin_refs, xn_ref = rest[:-1], rest[-1]
    j = pl.program_id(1)

    @pl.when(j == 0)
    def _():
        xn_ref[...] = _rms(x_ref[...], g_ref[...]).astype(bf16)

    val = _mm(xn_ref[...], w_ref[...])
    o_ref[...] = val
    rows = val.shape[0]
    for i, ref in enumerate(lin_refs):
        @pl.when(j == kv_block0 + i)
        def _():
            for c in range(KV_CHUNKS):
                ref[pl.ds(c, rows, stride=KV_CHUNKS), :] = val[:, c * LANES:(c + 1) * LANES]


def _in_proj(x, g, w, n_cols, kv_block0, n_kv, tm, name):
    M, K = x.shape
    tn = KV_ROW
    lin_spec = pl.BlockSpec((tm * KV_CHUNKS, LANES), lambda i, j: (i, 0))
    outs = pl.pallas_call(
        functools.partial(_in_proj_kernel, kv_block0=kv_block0),
        grid=(M // tm, n_cols // tn),
        in_specs=[pl.BlockSpec((tm, K), lambda i, j: (i, 0)),
                  pl.BlockSpec((1, K), lambda i, j: (0, 0)),
                  pl.BlockSpec((K, tn), lambda i, j: (0, j))],
        out_specs=[pl.BlockSpec((tm, tn), lambda i, j: (i, j))] + [lin_spec] * n_kv,
        out_shape=[jax.ShapeDtypeStruct((M, n_cols), f32)]
        + [jax.ShapeDtypeStruct((M * KV_CHUNKS, LANES), f32)] * n_kv,
        scratch_shapes=[pltpu.VMEM((tm, K), bf16)],
        compiler_params=_cparams("parallel", "arbitrary"),
        name=name,
    )(x, g.reshape(1, K), w)
    return outs[0], outs[1:]


def _out_proj_kernel(a1_ref, a2_ref, w1_ref, w2_ref, r_ref, o_ref):
    o_ref[...] = (r_ref[...] + _mm(a1_ref[...].astype(bf16), w1_ref[...])
                  + _mm(a2_ref[...].astype(bf16), w2_ref[...]))


def _out_proj(a1, a2, w, res, tm, tn, name):
    M, K1 = a1.shape
    K2 = a2.shape[1]
    assert K1 == K2 and w.shape[0] == K1 + K2
    N = w.shape[1]
    return pl.pallas_call(
        _out_proj_kernel,
        grid=(M // tm, N // tn),
        in_specs=[pl.BlockSpec((tm, K1), lambda i, j: (i, 0)),
                  pl.BlockSpec((tm, K2), lambda i, j: (i, 0)),
                  pl.BlockSpec((K1, tn), lambda i, j: (0, j)),
                  pl.BlockSpec((K2, tn), lambda i, j: (1, j)),
                  pl.BlockSpec((tm, tn), lambda i, j: (i, j))],
        out_specs=pl.BlockSpec((tm, tn), lambda i, j: (i, j)),
        out_shape=jax.ShapeDtypeStruct((M, N), f32),
        compiler_params=_cparams("parallel", "arbitrary"),
        name=name,
    )(a1, a2, w, w, res)


def _matmul_res_kernel(a_ref, w_ref, r_ref, o_ref):
    o_ref[...] = r_ref[...] + _mm(a_ref[...].astype(bf16), w_ref[...])


def _matmul_res(a, w, res, tm, tn, name):
    M, K = a.shape
    N = w.shape[1]
    return pl.pallas_call(
        _matmul_res_kernel,
        grid=(M // tm, N // tn),
        in_specs=[pl.BlockSpec((tm, K), lambda i, j: (i, 0)),
                  pl.BlockSpec((K, tn), lambda i, j: (0, j)),
                  pl.BlockSpec((tm, tn), lambda i, j: (i, j))],
        out_specs=pl.BlockSpec((tm, tn), lambda i, j: (i, j)),
        out_shape=jax.ShapeDtypeStruct((M, N), f32),
        compiler_params=_cparams("parallel", "arbitrary"),
        name=name,
    )(a, w, res)


def _ffn_kernel(h_ref, g_ref, w1_ref, w2_ref, gf_ref, y_ref, xn_ref, acc_ref):
    f = pl.program_id(1)

    @pl.when(f == 0)
    def _():
        xn_ref[...] = _rms(h_ref[...], g_ref[...]).astype(bf16)
        acc_ref[...] = jnp.zeros_like(acc_ref)

    a = _mm(xn_ref[...], w1_ref[...])
    a = jnp.square(jnp.maximum(a, 0.0))
    acc_ref[...] += _mm(a.astype(bf16), w2_ref[...])

    @pl.when(f == pl.num_programs(1) - 1)
    def _():
        y_ref[...] = _rms(h_ref[...] + acc_ref[...], gf_ref[...])


def _ffn_final(h, g, w1, w2, g_final, tm, tf, name):
    M, D = h.shape
    F = w1.shape[1]
    return pl.pallas_call(
        _ffn_kernel,
        grid=(M // tm, F // tf),
        in_specs=[pl.BlockSpec((tm, D), lambda i, j: (i, 0)),
                  pl.BlockSpec((1, D), lambda i, j: (0, 0)),
                  pl.BlockSpec((D, tf), lambda i, j: (0, j)),
                  pl.BlockSpec((tf, D), lambda i, j: (j, 0)),
                  pl.BlockSpec((1, D), lambda i, j: (0, 0))],
        out_specs=pl.BlockSpec((tm, D), lambda i, j: (i, 0)),
        out_shape=jax.ShapeDtypeStruct((M, D), f32),
        scratch_shapes=[pltpu.VMEM((tm, D), bf16), pltpu.VMEM((tm, D), f32)],
        compiler_params=_cparams("parallel", "arbitrary"),
        name=name,
    )(h, g.reshape(1, D), w1, w2, g_final.reshape(1, D))


def _mem_prompt_kernel(h_ref, g_ref, wq_ref, k_ref, v_ref, wo_ref, o_ref):
    h = h_ref[...]
    xn = _rms(h, g_ref[...]).astype(bf16)
    q = _mm(xn, wq_ref[...]) * (MEM_DH ** -0.5)
    outs = []
    for hd in range(MEM_HEADS):
        sl = slice(hd * MEM_DH, (hd + 1) * MEM_DH)
        s = _mm_nt(q[:, sl].astype(bf16), k_ref[:, sl].astype(bf16))
        e = jnp.exp(s - jnp.max(s, axis=-1, keepdims=True))
        p = e * (1.0 / jnp.sum(e, axis=-1, keepdims=True))
        outs.append(_mm(p.astype(bf16), v_ref[:, sl].astype(bf16)))
    o = jnp.concatenate(outs, axis=-1)
    o_ref[...] = h + _mm(o.astype(bf16), wo_ref[...])


def _mem_prompt(h, g, wq, mem_kv, wo, tm, name):
    M, D = h.shape
    HD = MEM_HEADS * MEM_DH
    ML = mem_kv.shape[0]
    return pl.pallas_call(
        _mem_prompt_kernel,
        grid=(M // tm,),
        in_specs=[pl.BlockSpec((tm, D), lambda i: (i, 0)),
                  pl.BlockSpec((1, D), lambda i: (0, 0)),
                  pl.BlockSpec((D, HD), lambda i: (0, 0)),
                  pl.BlockSpec((ML, HD), lambda i: (0, 0)),
                  pl.BlockSpec((ML, HD), lambda i: (0, 1)),
                  pl.BlockSpec((HD, D), lambda i: (0, 0))],
        out_specs=pl.BlockSpec((tm, D), lambda i: (i, 0)),
        out_shape=jax.ShapeDtypeStruct((M, D), f32),
        compiler_params=_cparams("parallel"),
        name=name,
    )(h, g.reshape(1, D), wq, mem_kv, mem_kv, wo)


def _mem_sample_kernel(q_ref, kv_ref, o_ref):
    per_tok = 2 * MEM_HEADS
    ML = kv_ref.shape[0] // per_tok
    q = q_ref[...] * (MEM_DH ** -0.5)
    outs = []
    for hd in range(MEM_HEADS):
        sl = slice(hd * MEM_DH, (hd + 1) * MEM_DH)
        k = kv_ref[pl.ds(hd, ML, stride=per_tok), :]
        v = kv_ref[pl.ds(MEM_HEADS + hd, ML, stride=per_tok), :]
        s = jnp.sum(k * q[:, sl], axis=-1, keepdims=True)
        e = jnp.exp(s - jnp.max(s, axis=0, keepdims=True))
        p = e / jnp.sum(e, axis=0, keepdims=True)
        outs.append(jnp.sum(p * v, axis=0, keepdims=True))
    o_ref[...] = jnp.concatenate(outs, axis=-1)


def _mem_sample(q, kv, name):
    B, HD = q.shape
    rows = kv.shape[1]
    out = pl.pallas_call(
        _mem_sample_kernel,
        grid=(B,),
        in_specs=[pl.BlockSpec((None, 1, HD), lambda b: (b, 0, 0)),
                  pl.BlockSpec((None, rows, MEM_DH), lambda b: (b, 0, 0))],
        out_specs=pl.BlockSpec((None, 1, HD), lambda b: (b, 0, 0)),
        out_shape=jax.ShapeDtypeStruct((B, 1, HD), f32),
        compiler_params=_cparams("parallel"),
        name=name,
    )(q.reshape(B, 1, HD), kv)
    return out.reshape(B, HD)


def _blockdiag_weights(w):
    n = w.shape[0]
    per = LANES // ML_QKV_BLOCK
    nchunk = w.shape[1] // per
    wc = w.reshape(n, nchunk, per, ML_QKV_BLOCK, ML_QKV_BLOCK)
    eye = jnp.eye(per, dtype=w.dtype)
    full = wc[:, :, :, :, None, :] * eye[None, None, :, None, :, None]
    return full.reshape(n, nchunk, LANES, LANES)


def _ml_qkv_gates(xc, xm, wbd_ref, wg_ref, bg_ref):
    xc_b, xm_b = xc.astype(bf16), xm.astype(bf16)
    qs, ks, vs = [], [], []
    for c in range(ML_DIM // LANES):
        sl = slice(c * LANES, (c + 1) * LANES)
        qs.append(_mm(xc_b[:, sl], wbd_ref[0, c]))
        ks.append(_mm(xc_b[:, sl], wbd_ref[1, c]) * (ML_DH ** -0.5))
        vs.append(_mm(xm_b[:, sl], wbd_ref[2, c]))
    q = jnp.concatenate(qs, axis=-1)
    k = jnp.concatenate(ks, axis=-1)
    v = jnp.concatenate(vs, axis=-1)
    qkv_b = jnp.concatenate([q, k, v], axis=-1).astype(bf16)
    gates = _mm(qkv_b, wg_ref[...]) + bg_ref[...]
    return q, k, v, gates, qkv_b


def _mlstm_prompt_kernel(xm_ref, og_ref, wconv_ref, bconv_ref, wbd_ref, wg_ref, wgt_ref,
                         bg_ref, bgt_ref, gh_ref, o_ref, c_ref, n_ref, m_ref, prev_sc, *, L):
    H, D = ML_HEADS, ML_DH

    @pl.when(pl.program_id(0) == 0)
    def _():
        prev_sc[...] = jnp.zeros_like(prev_sc)
        c_ref[...] = jnp.zeros_like(c_ref)
        n_ref[...] = jnp.zeros_like(n_ref)
        m_ref[...] = jnp.full(m_ref.shape, NEG, f32)

    x = xm_ref[...]
    full = jnp.concatenate([prev_sc[...], x], axis=0)
    y = bconv_ref[...]
    for j in range(ML_CONV):
        off = SUBLANES - (ML_CONV - 1) + j
        y = y + full[off:off + L] * wconv_ref[j:j + 1, :]
    prev_sc[...] = x[L - SUBLANES:L]
    xc = y * _sigmoid(y)

    q, k, v, gates, qkv_b = _ml_qkv_gates(xc, x, wbd_ref, wg_ref, bg_ref)
    gates_t = _mm_nt(wgt_ref[...], qkv_b) + bgt_ref[...]
    ig_c = gates[:, 0:H]
    lf_c = _log_sigmoid(gates[:, H:2 * H])
    ig_r = gates_t[0:H, :]
    lf_r = _log_sigmoid(gates_t[H:2 * H, :])

    t_i = lax.broadcasted_iota(i32, (L, L), 0)
    s_i = lax.broadcasted_iota(i32, (L, L), 1)
    causal = s_i <= t_i
    tri = jnp.where(causal, 1.0, 0.0).astype(bf16)
    b_c = _dot_sel_l(tri, lf_c)
    tri_u = jnp.where(t_i <= s_i, 1.0, 0.0).astype(bf16)
    b_r = _dot_sel_r(lf_r, tri_u)

    for h in range(H):
        sl = slice(h * D, (h + 1) * D)
        qh, kh, vh = q[:, sl], k[:, sl], v[:, sl]
        bc = b_c[:, h:h + 1]
        m_prev = m_ref[h:h + 1, 0:1]
        d_in = jnp.where(causal, bc - b_r[h:h + 1, :] + ig_r[h:h + 1, :], NEG)
        d_x = bc + m_prev
        m_t = jnp.maximum(d_x, jnp.max(d_in, axis=-1, keepdims=True))
        w_in = jnp.exp(d_in - m_t)
        w_x = jnp.exp(d_x - m_t)
        qb = qh.astype(bf16)
        kb = kh.astype(bf16)
        vb = vh.astype(bf16)
        s = _mm_nt(qb, kb) * w_in
        c_old = c_ref[h]
        n_old = n_ref[h:h + 1, :]
        num = _mm(s.astype(bf16), vb) + w_x * _mm(qb, c_old.astype(bf16))
        den = jnp.sum(s, axis=-1, keepdims=True) + w_x * jnp.sum(qh * n_old, axis=-1, keepdims=True)
        hh = num * (1.0 / jnp.maximum(jnp.abs(den), jnp.exp(-m_t)))
        m_new = m_t[L - 1:L, :]
        b_last = bc[L - 1:L, :]
        g_x = jnp.exp(b_last + m_prev - m_new)
        g_s = jnp.exp(b_last - bc + ig_c[:, h:h + 1] - m_new)
        ks_ = kh * g_s
        c_ref[h] = g_x * c_old + _mm(ks_.T.astype(bf16), vb)
        n_ref[h:h + 1, :] = g_x * n_old + jnp.sum(ks_, axis=0, keepdims=True)
        m_ref[h:h + 1, :] = jnp.broadcast_to(m_new, (1, LANES))
        hn = hh * lax.rsqrt(jnp.mean(hh * hh, axis=-1, keepdims=True) + EPS) * gh_ref[:, sl]
        o_ref[:, sl] = _sigmoid(og_ref[:, sl]) * hn


def _mlstm_prompt(u, w_conv, b_conv, wbd, w_gate, w_gate_t, b_gate, g_head, L, name):
    T = u.shape[0]
    H, D = ML_HEADS, ML_DH
    nch = ML_DIM // LANES
    full2 = lambda shape: pl.BlockSpec(shape, lambda i: (0,) * len(shape))
    out, c, n, m = pl.pallas_call(
        functools.partial(_mlstm_prompt_kernel, L=L),
        grid=(T // L,),
        in_specs=[pl.BlockSpec((L, ML_DIM), lambda i: (i, 0)),
                  pl.BlockSpec((L, ML_DIM), lambda i: (i, 1)),
                  full2((ML_CONV, ML_DIM)), full2((1, ML_DIM)),
                  full2((3, nch, LANES, LANES)),
                  full2((3 * ML_DIM, 2 * H)), full2((2 * H, 3 * ML_DIM)),
                  full2((1, 2 * H)), full2((2 * H, 1)), full2((1, ML_DIM))],
        out_specs=[pl.BlockSpec((L, ML_DIM), lambda i: (i, 0)),
                   full2((H, D, D)), full2((H, D)), full2((H, LANES))],
        out_shape=[jax.ShapeDtypeStruct((T, ML_DIM), f32),
                   jax.ShapeDtypeStruct((H, D, D), f32),
                   jax.ShapeDtypeStruct((H, D), f32),
                   jax.ShapeDtypeStruct((H, LANES), f32)],
        scratch_shapes=[pltpu.VMEM((SUBLANES, ML_DIM), f32)],
        compiler_params=_cparams("arbitrary"),
        name=name,
    )(u, u, w_conv, b_conv.reshape(1, ML_DIM), wbd, w_gate, w_gate_t,
      b_gate.reshape(1, 2 * H), b_gate.reshape(2 * H, 1), g_head.reshape(1, ML_DIM))
    return out, c, n, m[:, 0]


def _mlstm_sample_pre_kernel(xm_ref, s0_ref, s1_ref, s2_ref, wconv_ref, bconv_ref, wbd_ref,
                             wg_ref, bg_ref, q_ref, k_ref, v_ref, g_ref):
    x = xm_ref[...]
    y = (bconv_ref[...] + s0_ref[...] * wconv_ref[0:1, :] + s1_ref[...] * wconv_ref[1:2, :]
         + s2_ref[...] * wconv_ref[2:3, :] + x * wconv_ref[3:4, :])
    xc = y * _sigmoid(y)
    q, k, v, gates, _ = _ml_qkv_gates(xc, x, wbd_ref, wg_ref, bg_ref)
    q_ref[...] = q
    k_ref[...] = k
    v_ref[...] = v
    g_ref[...] = gates


def _mlstm_sample_step_kernel(qc_ref, kc_ref, q_ref, k_ref, v_ref, gt_ref, og_ref, gh_ref, c_ref, n_ref, m_ref,
                              o_ref, cn_ref, nn_ref, mn_ref):
    H, D = ML_HEADS, ML_DH
    ig = gt_ref[0:H, :]
    lf = _log_sigmoid(gt_ref[H:2 * H, :])
    m_old = m_ref[...]
    m_new = jnp.maximum(lf + m_old, ig)
    w_in = jnp.exp(ig - m_new)
    w_x = jnp.exp(lf + m_old - m_new)
    mn_ref[...] = m_new
    q, k, v, n_old = q_ref[...], k_ref[...], v_ref[...], n_ref[...]
    cq = jnp.concatenate([jnp.sum(c_ref[h] * qc_ref[:, h:h + 1], axis=0, keepdims=True) for h in range(H)], axis=0)
    s = jnp.sum(q * k, axis=-1, keepdims=True) * w_in
    num = s * v + w_x * cq
    den = s + w_x * jnp.sum(n_old * q, axis=-1, keepdims=True)
    hh = num / jnp.maximum(jnp.abs(den), jnp.exp(-m_new))
    hn = hh * lax.rsqrt(jnp.mean(hh * hh, axis=-1, keepdims=True) + EPS) * gh_ref[...]
    o_ref[...] = _sigmoid(og_ref[...]) * hn
    nn_ref[...] = w_x * n_old + w_in * k
    for h in range(H):
        cn_ref[h] = w_x[h:h + 1, :] * c_ref[h] + w_in[h:h + 1, :] * (kc_ref[:, h:h + 1] * v[h:h + 1, :])


def _mlstm_sample(xm, og, conv_state, c0, n0, m0, w_conv, b_conv, wbd, w_gate, b_gate, g_head):
    B = xm.shape[0]
    H, D = ML_HEADS, ML_DH
    sds = lambda *s: jax.ShapeDtypeStruct(s, f32)
    q, k, v, gates = pl.pallas_call(
        _mlstm_sample_pre_kernel,
        out_shape=[sds(B, ML_DIM), sds(B, ML_DIM), sds(B, ML_DIM), sds(B, 2 * H)],
        compiler_params=pltpu.CompilerParams(vmem_limit_bytes=VMEM_LIMIT),
        name="mlstm_sample_pre",
    )(xm, conv_state[:, 0], conv_state[:, 1], conv_state[:, 2], w_conv, b_conv.reshape(1, ML_DIM),
      wbd, w_gate, b_gate.reshape(1, 2 * H))
    q3, k3, v3 = (a.reshape(B, H, D) for a in (q, k, v))
    per_b = lambda *s: pl.BlockSpec((None,) + s, lambda b: (b,) + (0,) * len(s))
    out, c, n, m = pl.pallas_call(
        _mlstm_sample_step_kernel,
        grid=(B,),
        in_specs=[per_b(D, H), per_b(D, H), per_b(H, D), per_b(H, D), per_b(H, D), per_b(2 * H, 1),
                  per_b(H, D), pl.BlockSpec((H, D), lambda b: (0, 0)),
                  per_b(H, D, D), per_b(H, D), per_b(H, 1)],
        out_specs=[per_b(H, D), per_b(H, D, D), per_b(H, D), per_b(H, 1)],
        out_shape=[sds(B, H, D), sds(B, H, D, D), sds(B, H, D), sds(B, H, 1)],
        compiler_params=_cparams("parallel"),
        name="mlstm_sample_step",
    )(q3.transpose(0, 2, 1), k3.transpose(0, 2, 1), q3, k3, v3, gates.reshape(B, 2 * H, 1),
      og.reshape(B, H, D), g_head.reshape(H, D), c0, n0, m0.reshape(B, H, 1))
    return out.reshape(B, ML_DIM), c, n, m.reshape(B, H)


def _compress_kernel(x_ref, pe_ref, wp_ref, wphi_ref, o_ref, f0_sc, f1_sc, mn_sc):
    step = pl.program_id(0)
    sub = x_ref.shape[0] // CMP_STRIDE
    x3 = x_ref[...].reshape(sub, CMP_STRIDE, KV_ROW)
    base = pl.multiple_of(step * sub, sub)
    for o, sc in ((0, f0_sc), (1, f1_sc)):
        y = x3 + pe_ref[o][None]
        sc[pl.ds(base, sub), :] = jnp.sum(y * _sigmoid(y) * wp_ref[o][None], axis=1)
    mn_sc[pl.ds(base, sub), :] = jnp.sum(x3, axis=1) * (1.0 / CMP_STRIDE)

    @pl.when(step == pl.num_programs(0) - 1)
    def _():
        ns = f0_sc.shape[0]
        feat = f0_sc[...] + pltpu.roll(f1_sc[...], ns - 1, axis=0)
        mn = mn_sc[...]
        pooled = (mn + pltpu.roll(mn, ns - 1, axis=0)) * (CMP_STRIDE / CMP_BLOCK)
        for c in range(2):
            for g in range(NSA_KV_HEADS):
                sl = slice((c * NSA_KV_HEADS + g) * NSA_DH, (c * NSA_KV_HEADS + g + 1) * NSA_DH)
                o_ref[:, sl] = pooled[:, sl] + _dot3(feat[:, sl], wphi_ref[c])


def _compress_tables(pe, wpos):
    def lay(a):
        r = CMP_BLOCK // CMP_STRIDE
        a = a.reshape(2, r, CMP_STRIDE, NSA_DH).transpose(1, 2, 0, 3)
        a = jnp.broadcast_to(a[:, :, :, None, :], (r, CMP_STRIDE, 2, NSA_KV_HEADS, NSA_DH))
        return a.reshape(r, CMP_STRIDE, KV_ROW)
    return lay(pe), lay(wpos)


def _compress_prompt(u, col_block, pe_t, wp_t, wphi, rows=512):
    T = u.shape[0]
    n_sub = T // CMP_STRIDE
    const = lambda shape: pl.BlockSpec(shape, lambda s: (0,) * len(shape))
    return pl.pallas_call(
        _compress_kernel,
        grid=(T // rows,),
        in_specs=[pl.BlockSpec((rows, KV_ROW), lambda s: (s, col_block)),
                  const(pe_t.shape), const(wp_t.shape), const(wphi.shape)],
        out_specs=const((n_sub, KV_ROW)),
        out_shape=jax.ShapeDtypeStruct((n_sub, KV_ROW), f32),
        scratch_shapes=[pltpu.VMEM((n_sub, KV_ROW), f32)] * 3,
        compiler_params=_cparams("arbitrary"),
        name="compress_prompt",
    )(u, pe_t, wp_t, wphi)


def _compress_paged_kernel(pt_ref, *refs, n_pages):
    pages = refs[:n_pages]
    pe_ref, wp_ref, wphi_ref, o_ref, f0_sc, f1_sc, mn_sc = refs[n_pages:]
    step = pl.program_id(1)
    sub = PAGE_SIZE // CMP_STRIDE
    tiles = CMP_STRIDE * KV_CHUNKS // SUBLANES
    packed = 2 * SUBLANES
    ptiles = CMP_STRIDE * KV_CHUNKS // packed
    out_rows = sub * SUBLANES
    for p in range(n_pages):
        x = pages[p][...]
        base = pl.multiple_of((step * n_pages + p) * out_rows, out_rows)
        xh = (x.astype(bf16) * 0.5).reshape(sub, ptiles, packed, LANES)
        for o, sc in ((0, f0_sc), (1, f1_sc)):
            y = xh + pe_ref[o][None]
            b = y * wp_ref[o][None]
            z = b + b * jnp.tanh(y)
            z = (z[:, 0] + z[:, 1]) + (z[:, 2] + z[:, 3])
            zf = z.astype(f32).reshape(sub, packed // SUBLANES, SUBLANES, LANES)
            sc[pl.ds(base, out_rows), :] = jnp.sum(zf, axis=1).reshape(out_rows, LANES)
        mean = jnp.sum(x.reshape(sub, tiles, SUBLANES, LANES), axis=1) * (1.0 / CMP_STRIDE)
        mn_sc[pl.ds(base, out_rows), :] = mean.reshape(out_rows, LANES)

    @pl.when(step == pl.num_programs(1) - 1)
    def _():
        ns = f0_sc.shape[0] // SUBLANES
        for c in range(KV_CHUNKS):
            col = lambda sc: (sc[pl.ds(c, ns, stride=SUBLANES), :]
                              + sc[pl.ds(KV_CHUNKS + c, ns, stride=SUBLANES), :])
            feat = col(f0_sc) + pltpu.roll(col(f1_sc), ns - 1, axis=0)
            mn = col(mn_sc)
            pooled = (mn + pltpu.roll(mn, ns - 1, axis=0)) * (CMP_STRIDE / CMP_BLOCK)
            o_ref[:, c * LANES:(c + 1) * LANES] = pooled + _dot3(feat, wphi_ref[c // NSA_KV_HEADS])


def _compress_paged(pool, page_table, pe_t, wp_t, wphi, pages_per_step=64):
    B, n_pages = page_table.shape
    P = pages_per_step
    n_sub = n_pages * PAGE_SIZE // CMP_STRIDE
    r = CMP_BLOCK // CMP_STRIDE
    packed = 2 * SUBLANES
    ptiles = CMP_STRIDE * KV_CHUNKS // packed
    pe4 = (0.5 * pe_t).astype(bf16).reshape(r, ptiles, packed, LANES)
    wp4 = wp_t.astype(bf16).reshape(r, ptiles, packed, LANES)
    page_rows = PAGE_SIZE * KV_CHUNKS
    specs = [pl.BlockSpec((None, page_rows, LANES),
                          functools.partial(lambda b, s, pt, p: (pt[b * n_pages + s * P + p], 0, 0), p=p))
             for p in range(P)]
    const = lambda shape: pl.BlockSpec(shape, lambda *a: (0,) * len(shape))
    gs = pltpu.PrefetchScalarGridSpec(
        num_scalar_prefetch=1, grid=(B, n_pages // P),
        in_specs=specs + [const(pe4.shape), const(wp4.shape), const(wphi.shape)],
        out_specs=pl.BlockSpec((None, n_sub, KV_ROW), lambda b, s, pt: (b, 0, 0)),
        scratch_shapes=[pltpu.VMEM((n_sub * SUBLANES, LANES), f32)] * 3)
    return pl.pallas_call(
        functools.partial(_compress_paged_kernel, n_pages=P),
        grid_spec=gs,
        out_shape=jax.ShapeDtypeStruct((B, n_sub, KV_ROW), f32),
        compiler_params=_cparams("parallel", "arbitrary"),
        name="compress_paged",
    )(page_table.reshape(-1), *([pool] * P), pe4, wp4, wphi)


def _kv_prep_kernel(ks_ref, kw_ref, ksk_ref, ksv_ref, kwk_ref, kwv_ref, kn2_ref):
    rows = ks_ref.shape[0]
    ks = ks_ref[...]
    kw = kw_ref[...]
    r = pl.program_id(0) * rows + lax.broadcasted_iota(i32, (rows, LANES), 0)
    n = lax.broadcasted_iota(i32, (rows, LANES), 1)
    onehot = jnp.where(_div_pow2(r, SEL_BLOCK) == n, 1.0, 0.0).astype(bf16)
    ones_col = jnp.where(n == 0, 1.0, 0.0).astype(bf16)

    @pl.when(pl.program_id(0) == 0)
    def _():
        kn2_ref[...] = jnp.zeros_like(kn2_ref)

    for g in range(NSA_KV_HEADS):
        ksl = slice(g * NSA_DH, (g + 1) * NSA_DH)
        vsl = slice(NSA_KV_W + g * NSA_DH, NSA_KV_W + (g + 1) * NSA_DH)
        kb = ks[:, ksl].astype(bf16)
        ksk_ref[g, :, 0:NSA_DH] = kb
        ksk_ref[g, :, NSA_DH:NSA_DH + LANES] = onehot
        ksv_ref[g, :, 0:NSA_DH] = ks[:, vsl].astype(bf16)
        ksv_ref[g, :, NSA_DH:NSA_DH + LANES] = ones_col
        kwk_ref[g] = kw[:, ksl].astype(bf16)
        kwv_ref[g, :, 0:NSA_DH] = kw[:, vsl].astype(bf16)
        kwv_ref[g, :, NSA_DH:NSA_DH + LANES] = ones_col
        kf = kb.astype(f32)
        n2 = jnp.max(jnp.sum(kf * kf, axis=-1, keepdims=True), axis=0, keepdims=True)
        kn2_ref[g] = jnp.maximum(kn2_ref[g], jnp.broadcast_to(n2, kn2_ref.shape[1:]))


def _kv_prep(u, ks_col_block, kw_col_block, rows=512):
    T = u.shape[0]
    G = NSA_KV_HEADS
    assert (T - 1) // SEL_BLOCK + 1 <= LANES
    sd = lambda w: jax.ShapeDtypeStruct((G, T, w), bf16)
    ospec = lambda w: pl.BlockSpec((G, rows, w), lambda i: (0, i, 0))
    return pl.pallas_call(
        _kv_prep_kernel,
        grid=(T // rows,),
        in_specs=[pl.BlockSpec((rows, KV_ROW), lambda i: (i, ks_col_block)),
                  pl.BlockSpec((rows, KV_ROW), lambda i: (i, kw_col_block))],
        out_specs=[ospec(NSA_DH + LANES), ospec(NSA_DH + LANES), ospec(NSA_DH), ospec(NSA_DH + LANES),
                   pl.BlockSpec((G, SUBLANES, LANES), lambda i: (0, 0, 0))],
        out_shape=[sd(NSA_DH + LANES), sd(NSA_DH + LANES), sd(NSA_DH), sd(NSA_DH + LANES),
                   jax.ShapeDtypeStruct((G, SUBLANES, LANES), f32)],
        compiler_params=_cparams("arbitrary"),
        name="kv_prep",
    )(u, u)


def _top_blocks(score, n_top, axis):
    lane = lax.broadcasted_iota(i32, score.shape, axis).astype(f32)
    width = float(score.shape[axis])
    work = score
    firsts = []
    for _ in range(n_top):
        mx = jnp.max(work, axis=axis, keepdims=True)
        first = jnp.min(jnp.where(work == mx, lane, width), axis=axis, keepdims=True)
        work = jnp.where(lane == first, REMOVED, work)
        firsts.append(first)
    return work, firsts


def _nsa_prompt_kernel(x_ref, gmix_ref, wgt_ref, q_ref, kck_ref, kcv_ref, ksk_ref, ksv_ref, kwk_ref, kwv_ref, kn2_ref,
                       o_ref, shift_sc, m_sc, acc_sc, *, n_sel):
    R, D = NSA_GROUP, NSA_DH
    QB = q_ref.shape[0]
    rows = R * QB
    qb = pl.program_id(1)
    q = q_ref[...]
    q2_b = (jnp.concatenate([q[:, r * D:(r + 1) * D] for r in range(R)], axis=0) * (D ** -0.5 * LOG2E)).astype(bf16)
    pos = qb * QB + lax.broadcasted_iota(i32, (QB, 1), 0)

    def head_bias(valid):
        return jnp.concatenate([jnp.where(valid, 0.0, NEG)] * R, axis=0)

    ns = kck_ref.shape[0]
    s = _mm_nt(q2_b, kck_ref[...].astype(bf16))
    j = lax.broadcasted_iota(i32, (1, ns), 1)
    p_c = _biased_softmax2_rows(s, head_bias(j * CMP_STRIDE + (CMP_BLOCK - 1) <= pos))
    o_c = _mm(p_c.astype(bf16), kcv_ref[...].astype(bf16))

    imp = p_c[0:QB]
    for r in range(1, R):
        imp = imp + p_c[r * QB:(r + 1) * QB]
    ratio = SEL_BLOCK // CMP_STRIDE
    off = CMP_BLOCK // CMP_STRIDE - 1
    nn = lax.broadcasted_iota(i32, (LANES, ns), 0)
    jj = lax.broadcasted_iota(i32, (LANES, ns), 1)
    overlap_t = jnp.where((jj >= ratio * nn - off) & (jj < ratio * nn + ratio), 1.0, 0.0).astype(bf16)
    imp_sel = sum(_mm_nt(overlap_t, part) for part in _split3(imp))
    n_idx = lax.broadcasted_iota(i32, (LANES, QB), 0)
    cur = _div_pow2(qb * QB + lax.broadcasted_iota(i32, (LANES, QB), 1), SEL_BLOCK)
    forced = (n_idx == 0) | (n_idx == cur) | (n_idx == cur - 1)
    score = jnp.where(forced, FORCE, jnp.where(n_idx <= cur, imp_sel, NEG))
    score = jnp.where(n_idx < n_sel, score, REMOVED)
    taken, _ = _top_blocks(score, min(SEL_TOPN, n_sel), axis=0)
    bias = jnp.where((taken < 0.5 * REMOVED) & (n_idx <= cur), 0.0, MASK_BIAS).T

    KT = SEL_KT
    last = (qb * QB + QB - 1) // KT
    q2 = q2_b.astype(f32)
    bound = jnp.sqrt(jnp.sum(q2 * q2, axis=-1, keepdims=True) * kn2_ref[0:1, 0:1]) * 1.01 + 1e-3
    k_self = ksk_ref[pl.ds(pl.multiple_of(qb * QB, QB), QB), 0:D].astype(f32)
    s_self = jnp.concatenate([jnp.sum(q2[r * QB:(r + 1) * QB] * k_self, axis=-1, keepdims=True) for r in range(R)],
                             axis=0)
    shift_sc[...] = bound
    diag_bias = head_bias(last * KT + lax.broadcasted_iota(i32, (1, KT), 1) <= pos)

    def shifted_queries(shift):
        return jnp.concatenate(
            [jnp.concatenate([q2_b[r * QB:(r + 1) * QB], (bias - shift[r * QB:(r + 1) * QB]).astype(bf16)], axis=1)
             for r in range(R)], axis=0)

    def key_tile(ref, kt):
        return ref[pl.ds(pl.multiple_of(kt * KT, KT), KT), :]

    @pl.when(jnp.max(bound - s_self) > SAFE_SPAN)
    def _():
        qp0 = shifted_queries(jnp.zeros_like(bound))

        def lane_max(kt, sk):
            mx = m_sc[...]
            for c in range(KT // LANES):
                mx = jnp.maximum(mx, sk[:, c * LANES:(c + 1) * LANES])
            m_sc[...] = mx

        def max_pass(kt, carry):
            lane_max(kt, _mm_nt(qp0, key_tile(ksk_ref, kt)))
            return carry

        m_sc[...] = jnp.full(m_sc.shape, NEG, f32)
        lax.fori_loop(0, last, max_pass, 0)
        lane_max(last, _mm_nt(qp0, key_tile(ksk_ref, last)) + diag_bias)
        shift_sc[...] = jnp.max(m_sc[...], axis=-1, keepdims=True)

    qp = shifted_queries(shift_sc[...])
    acc_sc[...] = jnp.zeros_like(acc_sc)

    def accumulate(kt, s2):
        acc_sc[...] += _mm(jnp.exp2(s2).astype(bf16), key_tile(ksv_ref, kt))

    def body(kt, carry):
        accumulate(kt, _mm_nt(qp, key_tile(ksk_ref, kt)))
        return carry

    lax.fori_loop(0, last, body, 0)
    accumulate(last, _mm_nt(qp, key_tile(ksk_ref, last)) + diag_bias)
    acc = acc_sc[...]
    o_s = acc[:, 0:D] * (1.0 / acc[:, D:D + 1])

    wlen = WINDOW + QB
    wstart = pl.multiple_of(jnp.maximum(qb * QB - WINDOW, 0), int(np.gcd(QB, WINDOW)))
    sw = _mm_nt(q2_b, kwk_ref[pl.ds(wstart, wlen), :])
    diff = pos - (wstart + lax.broadcasted_iota(i32, (1, wlen), 1))
    e_w, _ = _biased_exp2_rows(sw, head_bias((diff >= 0) & (diff <= WINDOW)))
    acc_w = _mm(e_w.astype(bf16), kwv_ref[pl.ds(wstart, wlen), :])
    o_w = acc_w[:, 0:D] * (1.0 / acc_w[:, D:D + 1])

    gate = _sigmoid(_mm(_rms(x_ref[...], gmix_ref[...]).astype(bf16), wgt_ref[...]))
    for r in range(R):
        rs = slice(r * QB, (r + 1) * QB)
        o_ref[:, r * D:(r + 1) * D] = (gate[:, r:r + 1] * o_c[rs] + gate[:, R + r:R + r + 1] * o_s[rs]
                                       + gate[:, 2 * R + r:2 * R + r + 1] * o_w[rs])


def _nsa_prompt(x, g_mix, w_gate_g, u, q_col_block, kvc, ksk, ksv, kwk, kwv, kn2, tq):
    T, Dm = x.shape
    G, R, D, QB = NSA_KV_HEADS, NSA_GROUP, NSA_DH, tq
    ns = kvc.shape[0]
    n_sel = (T - 1) // SEL_BLOCK + 1
    assert T % SEL_KT == 0 and T % QB == 0 and T >= WINDOW + QB and n_sel <= LANES
    rows = R * QB
    res = lambda w: pl.BlockSpec((None, T, w), lambda g, i: (g, 0, 0))
    return pl.pallas_call(
        functools.partial(_nsa_prompt_kernel, n_sel=n_sel),
        grid=(G, T // QB),
        in_specs=[pl.BlockSpec((QB, Dm), lambda g, i: (i, 0)),
                  pl.BlockSpec((1, Dm), lambda g, i: (0, 0)),
                  pl.BlockSpec((None, Dm, LANES), lambda g, i: (g, 0, 0)),
                  pl.BlockSpec((QB, R * D), lambda g, i: (i, q_col_block + g)),
                  pl.BlockSpec((ns, D), lambda g, i: (0, g)),
                  pl.BlockSpec((ns, D), lambda g, i: (0, G + g)),
                  res(D + LANES), res(D + LANES), res(D), res(D + LANES),
                  pl.BlockSpec((None, SUBLANES, LANES), lambda g, i: (g, 0, 0))],
        out_specs=pl.BlockSpec((QB, R * D), lambda g, i: (i, g)),
        out_shape=jax.ShapeDtypeStruct((T, NSA_DIM), f32),
        scratch_shapes=[pltpu.VMEM((rows, 1), f32), pltpu.VMEM((rows, LANES), f32),
                        pltpu.VMEM((rows, D + LANES), f32)],
        compiler_params=_cparams("parallel", "arbitrary"),
        name="nsa_prompt",
    )(x, g_mix.reshape(1, Dm), w_gate_g, u, kvc, kvc, ksk, ksv, kwk, kwv, kn2)


def _group_gate_weights(w_gate):
    Dm = w_gate.shape[0]
    G, R = NSA_KV_HEADS, NSA_GROUP
    w = w_gate.reshape(Dm, 3, G, R).transpose(2, 0, 1, 3).reshape(G, Dm, 3 * R)
    return jnp.pad(w, ((0, 0), (0, 0), (0, LANES - 3 * R)))


def _nsa_sample_cmp_kernel(q_ref, kvc_ref, oc_ref, idx_ref, *, pos, n_sel, sel_w):
    H, R, D, G = NSA_HEADS, NSA_GROUP, NSA_DH, NSA_KV_HEADS
    ns = kvc_ref.shape[0]
    qs_b = (q_ref[...] * (D ** -0.5)).astype(bf16)
    head = lax.broadcasted_iota(i32, (H, 1), 0)
    grp = _div_pow2(head, R)
    s = jnp.zeros((H, ns), f32)
    for g in range(G):
        s = jnp.where(grp == g, _mm_nt(qs_b, kvc_ref[:, g * D:(g + 1) * D].astype(bf16)), s)
    j = lax.broadcasted_iota(i32, (1, ns), 1)
    e, den = _masked_softmax_rows(s, j * CMP_STRIDE + (CMP_BLOCK - 1) <= pos)
    p_c = e / den
    p_b = p_c.astype(bf16)
    o_c = jnp.zeros((H, D), f32)
    imp = jnp.zeros((H, ns), f32)
    for g in range(G):
        o_c = jnp.where(grp == g, _mm(p_b, kvc_ref[:, (G + g) * D:(G + g + 1) * D].astype(bf16)), o_c)
        imp = jnp.where(grp == g, jnp.sum(jnp.where(grp == g, p_c, 0.0), axis=0, keepdims=True), imp)
    oc_ref[...] = o_c
    ratio = SEL_BLOCK // CMP_STRIDE
    off = CMP_BLOCK // CMP_STRIDE - 1
    nn = lax.broadcasted_iota(i32, (sel_w, ns), 0)
    jj = lax.broadcasted_iota(i32, (sel_w, ns), 1)
    overlap_t = jnp.where((jj >= ratio * nn - off) & (jj < ratio * nn + ratio), 1.0, 0.0).astype(bf16)
    imp_sel = sum(_mm_nt(overlap_t, part) for part in _split3(imp))
    n_idx = lax.broadcasted_iota(i32, (sel_w, H), 0)
    cur = pos // SEL_BLOCK
    forced = (n_idx == 0) | (n_idx == cur) | (n_idx == cur - 1)
    score = jnp.where(forced, FORCE, jnp.where(n_idx <= cur, imp_sel, NEG))
    score = jnp.where(n_idx < n_sel, score, REMOVED)
    _, firsts = _top_blocks(score, SEL_TOPN, axis=0)
    idx_ref[...] = jnp.concatenate(firsts, axis=0).astype(i32)


def _nsa_sample_cmp(q, kvc, pos):
    B, H, D = q.shape
    ns = kvc.shape[1]
    n_sel = pos // SEL_BLOCK + 1
    assert n_sel >= SEL_TOPN
    sel_w = -(-n_sel // SUBLANES) * SUBLANES
    o_c, idx = pl.pallas_call(
        functools.partial(_nsa_sample_cmp_kernel, pos=pos, n_sel=n_sel, sel_w=sel_w),
        grid=(B,),
        in_specs=[pl.BlockSpec((None, H, D), lambda b: (b, 0, 0)),
                  pl.BlockSpec((None, ns, KV_ROW), lambda b: (b, 0, 0))],
        out_specs=[pl.BlockSpec((None, H, D), lambda b: (b, 0, 0)),
                   pl.BlockSpec((None, SEL_TOPN, H), lambda b: (b, 0, 0))],
        out_shape=[jax.ShapeDtypeStruct((B, H, D), f32), jax.ShapeDtypeStruct((B, SEL_TOPN, H), i32)],
        compiler_params=_cparams("parallel"),
        name="nsa_sample_cmp",
    )(q, kvc)
    return o_c, idx[:, :, ::NSA_GROUP].transpose(0, 2, 1)


def _nsa_sample_sel_kernel(idx_ref, pt_ref, q_ref, new_ref, *refs, past):
    H, R, D, G = NSA_HEADS, NSA_GROUP, NSA_DH, NSA_KV_HEADS
    blocks, o_ref = refs[:G * SEL_TOPN], refs[G * SEL_TOPN]
    b = pl.program_id(0)
    qs_b = (q_ref[...] * (D ** -0.5)).astype(bf16)
    grp = _div_pow2(lax.broadcasted_iota(i32, (H, 1), 0), R)
    per_tile = SUBLANES // KV_CHUNKS
    half = SEL_BLOCK // per_tile
    o = jnp.zeros((H, D), f32)
    for g in range(G):
        ks, vs = [], []
        lane = lax.broadcasted_iota(i32, (1, SEL_TOPN * SEL_BLOCK), 1)
        slot = _div_pow2(lane, SEL_BLOCK)
        in_slot = lane & (SEL_BLOCK - 1)
        pk_row = (in_slot & (half - 1)) * per_tile + _div_pow2(in_slot, half)
        for i in range(SEL_TOPN):
            blk = blocks[g * SEL_TOPN + i]
            base = idx_ref[(b * G + g) * SEL_TOPN + i] * SEL_BLOCK
            pk_row = pk_row + jnp.where(slot == i, base, 0)
            for par in range(per_tile):
                pk_col = base + per_tile * lax.broadcasted_iota(i32, (half, 1), 0) + par
                old = pk_col < past
                k = blk[pl.ds(par * KV_CHUNKS + g, half, stride=SUBLANES), :]
                v = blk[pl.ds(par * KV_CHUNKS + G + g, half, stride=SUBLANES), :]
                ks.append(jnp.where(old, k, new_ref[g:g + 1, :]).astype(bf16))
                vs.append(jnp.where(old, v, new_ref[G + g:G + g + 1, :]).astype(bf16))
        s = _mm_nt(qs_b, jnp.concatenate(ks, axis=0))
        e, den = _masked_softmax_rows(s, pk_row <= past)
        o = jnp.where(grp == g, _mm((e / den).astype(bf16), jnp.concatenate(vs, axis=0)), o)
    o_ref[...] = o


def _nsa_sample_sel(q, ks_new, pool, page_table, idx, past):
    B, H, D = q.shape
    G = NSA_KV_HEADS
    n_pages = page_table.shape[1]
    halves = PAGE_SIZE // SEL_BLOCK
    blk_rows = SEL_BLOCK * KV_CHUNKS
    pool_h = pool.reshape(pool.shape[0] * halves, blk_rows, D)
    last_old = past // SEL_BLOCK - 1

    def blk_map(b, idx_r, pt_r, g, i):
        blk = jnp.minimum(idx_r[(b * G + g) * SEL_TOPN + i], last_old)
        return (pt_r[b * n_pages + blk // halves] * halves + blk % halves, 0, 0)

    specs = [pl.BlockSpec((None, blk_rows, D), functools.partial(blk_map, g=g, i=i))
             for g in range(G) for i in range(SEL_TOPN)]
    gs = pltpu.PrefetchScalarGridSpec(
        num_scalar_prefetch=2, grid=(B,),
        in_specs=[pl.BlockSpec((None, H, D), lambda b, *_: (b, 0, 0)),
                  pl.BlockSpec((None, KV_CHUNKS, D), lambda b, *_: (b, 0, 0))] + specs,
        out_specs=pl.BlockSpec((None, H, D), lambda b, *_: (b, 0, 0)))
    return pl.pallas_call(
        functools.partial(_nsa_sample_sel_kernel, past=past),
        grid_spec=gs,
        out_shape=jax.ShapeDtypeStruct((B, H, D), f32),
        compiler_params=_cparams("arbitrary"),
        name="nsa_sample_sel",
    )(idx.reshape(-1), page_table.reshape(-1), q, ks_new, *([pool_h] * (G * SEL_TOPN)))


def _nsa_sample_win_kernel(q_ref, buf_ref, new_ref, oc_ref, os_ref, gt_ref, o_ref, win_ref, *, past):
    H, R, D, G = NSA_HEADS, NSA_GROUP, NSA_DH, NSA_KV_HEADS
    rows = buf_ref.shape[0]
    Lb = rows // KV_CHUNKS
    per_tile = SUBLANES // KV_CHUNKS
    half = Lb // per_tile
    qs = q_ref[...] * (D ** -0.5)
    qs_b = qs.astype(bf16)
    grp = _div_pow2(lax.broadcasted_iota(i32, (H, 1), 0), R)
    lane = lax.broadcasted_iota(i32, (1, Lb), 1)
    key_pos = past - Lb + (lane & (half - 1)) * per_tile + _div_pow2(lane, half)
    diff = past - key_pos
    valid = (diff >= 0) & (diff <= WINDOW)
    o_w = jnp.zeros((H, D), f32)
    for g in range(G):
        chunk = lambda c: jnp.concatenate(
            [buf_ref[pl.ds(par * KV_CHUNKS + c, half, stride=SUBLANES), :] for par in range(per_tile)], axis=0)
        new_k, new_v = new_ref[g:g + 1, :], new_ref[G + g:G + g + 1, :]
        s_b = jnp.where(valid, _mm_nt(qs_b, chunk(g).astype(bf16)), NEG)
        s_n = jnp.sum(qs * new_k, axis=-1, keepdims=True)
        mx = jnp.maximum(jnp.max(s_b, axis=-1, keepdims=True), s_n)
        e_b = jnp.where(valid, jnp.exp(s_b - mx), 0.0)
        e_n = jnp.exp(s_n - mx)
        den = jnp.sum(e_b, axis=-1, keepdims=True) + e_n
        og = _mm((e_b / den).astype(bf16), chunk(G + g).astype(bf16)) + (e_n / den) * new_v
        o_w = jnp.where(grp == g, og, o_w)
    gate = _sigmoid(gt_ref[...])
    o_ref[...] = gate[:, 0:1] * oc_ref[...] + gate[:, 1:2] * os_ref[...] + gate[:, 2:3] * o_w
    win_ref[0:rows - KV_CHUNKS, :] = buf_ref[KV_CHUNKS:rows, :]
    win_ref[rows - KV_CHUNKS:rows, :] = new_ref[...]


def _nsa_sample_win(q, win_buf, kw_new, o_c, o_s, gt, past):
    B, H, D = q.shape
    rows = win_buf.shape[1]
    assert rows == WINDOW * KV_CHUNKS
    per_b = lambda *s: pl.BlockSpec((None,) + s, lambda b: (b,) + (0,) * len(s))
    return pl.pallas_call(
        functools.partial(_nsa_sample_win_kernel, past=past),
        grid=(B,),
        in_specs=[per_b(H, D), per_b(rows, D), per_b(KV_CHUNKS, D), per_b(H, D), per_b(H, D), per_b(H, 3)],
        out_specs=[per_b(H, D), per_b(rows, D)],
        out_shape=[jax.ShapeDtypeStruct((B, H, D), f32), jax.ShapeDtypeStruct((B, rows, D), f32)],
        compiler_params=_cparams("parallel"),
        name="nsa_sample_win",
    )(q, win_buf, kw_new, o_c, o_s, gt.reshape(B, 3, H).transpose(0, 2, 1))


def kernel(x_prompt, x_sample, cache_cmp_kv, cache_sel_kv, cache_win_kv, cache_mem_kv, state_mlstm_c, state_mlstm_n, state_mlstm_m, state_conv, page_table, mem_prompt, g_mix, w_in, w_conv, b_conv, w_mq, w_mk, w_mv, w_mgate, b_mgate, g_mhead, cmp_pe, cmp_wpos, cmp_wphi, w_out, g_memx, g_mems, w_mem_q, w_mem_kv, w_mem_o, g_ffn, w_ff1, w_ff2, g_final):
    depth = w_in.shape[0]
    assert depth == 1 and x_prompt.shape[0] == 1 and x_sample.shape[1] == 1
    T, Dm = x_prompt.shape[1:]
    B = x_sample.shape[0]
    G, Dh, H = NSA_KV_HEADS, NSA_DH, ML_HEADS
    past = page_table.shape[1] * PAGE_SIZE
    assert (past + 1) // CMP_STRIDE == past // CMP_STRIDE
    n_main = 2 * ML_DIM + NSA_DIM + 3 * KV_ROW
    n_gate = 3 * NSA_HEADS
    l = 0
    hp = x_prompt.reshape(T, Dm)
    hs = x_sample.reshape(B, Dm)

    wbd = _blockdiag_weights(jnp.stack([w_mq[l], w_mk[l], w_mv[l]])).astype(bf16)
    w_mg = w_mgate[l].astype(bf16)
    w_mg_t = w_mg.T
    pe_t, wp_t = _compress_tables(cmp_pe[l], cmp_wpos[l])
    wb_out, wb_mq, wb_mkv, wb_mo, wb_f1, wb_f2 = (
        w[l].astype(bf16) for w in (w_out, w_mem_q, w_mem_kv, w_mem_o, w_ff1, w_ff2))
    wb_in = w_in[l, :, :n_main].astype(bf16)
    wb_gate = w_in[l, :, n_main:].astype(bf16)
    w_gt = jnp.pad(wb_gate, ((0, 0), (0, LANES - n_gate)))
    q_cb = 2 * ML_DIM // KV_ROW
    c_cb = (2 * ML_DIM + NSA_DIM) // KV_ROW
    n_mq, n_mkv = w_mem_q.shape[-1], w_mem_kv.shape[-1]
    tm_in, tm, tn, tf, tq, ml_chunk = 1024, 512, 512, 1024, 256, 256

    u, (kc_lin, ks_lin, kw_lin) = _in_proj(hp, g_mix[l], wb_in, n_main, c_cb, 3, tm_in, "in_proj_p")
    ml_o, c_p, n_p, m_p = _mlstm_prompt(u, w_conv[l], b_conv[l], wbd, w_mg, w_mg_t, b_mgate[l],
                                        g_mhead[l], ml_chunk, "mlstm_prompt")
    kvc_p = _compress_prompt(u, c_cb, pe_t, wp_t, cmp_wphi[l])
    nsa_o = _nsa_prompt(hp, g_mix[l], _group_gate_weights(wb_gate), u, q_cb, kvc_p,
                        *_kv_prep(u, c_cb + 1, c_cb + 2), tq)
    h1 = _out_proj(ml_o, nsa_o, wb_out, hp, tm, Dm, "out_proj_p")
    mem_kv = _norm_matmul(mem_prompt.reshape(-1, Dm), g_mems[l], wb_mkv, n_mkv, mem_prompt.shape[1], tn, "mem_kv")
    h2 = _mem_prompt(h1, g_memx[l], wb_mq, mem_kv, wb_mo, tm, "mem_attn_p")
    y_p = _ffn_final(h2, g_ffn[l], wb_f1, wb_f2, g_final, tm, tf, "ffn_p")

    kv_shape = lambda a: a.reshape(1, 1, -1, 2, G, Dh)
    out_p = (y_p.reshape(1, T, Dm),
             kv_shape(kc_lin), kv_shape(ks_lin),
             kv_shape(kw_lin[(T - min(WINDOW, T)) * KV_CHUNKS:]),
             mem_kv.reshape(1, 1, -1, 2, MEM_HEADS, MEM_DH),
             c_p[None, None], n_p[None, None], m_p[None, None],
             u[T - (ML_CONV - 1):, :ML_DIM][None, None])

    us = _norm_matmul(hs, g_mix[l], wb_in, n_main, B, tn, "in_proj_s")
    gts = _norm_matmul(hs, g_mix[l], w_gt, LANES, B, LANES, "in_gate_s")[:, :n_gate]
    xm_s, og_s = us[:, :ML_DIM], us[:, ML_DIM:2 * ML_DIM]
    q_s = us[:, 2 * ML_DIM:2 * ML_DIM + NSA_DIM].reshape(B, NSA_HEADS, Dh)
    kc_s, ks_s, kw_s = (us[:, (c_cb + i) * KV_ROW:(c_cb + i + 1) * KV_ROW] for i in range(3))
    ml_os, c_s, n_s, m_s = _mlstm_sample(xm_s, og_s, state_conv[l], state_mlstm_c[l], state_mlstm_n[l],
                                         state_mlstm_m[l], w_conv[l], b_conv[l], wbd,
                                         w_mg, b_mgate[l], g_mhead[l])
    pool_c = cache_cmp_kv.reshape(-1, PAGE_SIZE * KV_CHUNKS, Dh)
    pool_s = cache_sel_kv.reshape(-1, PAGE_SIZE * KV_CHUNKS, Dh)
    win_buf = cache_win_kv.reshape(B, -1, Dh)
    mem_buf = cache_mem_kv.reshape(B, -1, MEM_DH)
    kvc_s = _compress_paged(pool_c, page_table, pe_t, wp_t, cmp_wphi[l])
    o_cs, idx = _nsa_sample_cmp(q_s, kvc_s, past)
    o_ss = _nsa_sample_sel(q_s, ks_s.reshape(B, KV_CHUNKS, Dh), pool_s, page_table, idx, past)
    nsa_os, win_new = _nsa_sample_win(q_s, win_buf, kw_s.reshape(B, KV_CHUNKS, Dh), o_cs, o_ss, gts, past)
    h1s = _out_proj(ml_os, nsa_os.reshape(B, NSA_DIM), wb_out, hs, B, Dm, "out_proj_s")
    qm_s = _norm_matmul(h1s, g_memx[l], wb_mq, n_mq, B, n_mq, "mem_q_s")
    om_s = _mem_sample(qm_s, mem_buf, "mem_attn_s")
    h2s = _matmul_res(om_s, wb_mo, h1s, B, Dm, "mem_o_s")
    y_s = _ffn_final(h2s, g_ffn[l], wb_f1, wb_f2, g_final, B, tf, "ffn_s")

    kv_s_shape = lambda a: a.reshape(1, B, 1, 2, G, Dh)
    conv_s = jnp.concatenate([state_conv[l][:, 1:], xm_s[:, None, :]], axis=1)
    out_s = (y_s.reshape(B, 1, Dm), kv_s_shape(kc_s), kv_s_shape(ks_s),
             win_new.reshape(1, B, -1, 2, G, Dh), c_s[None], n_s[None], m_s[None], conv_s[None])

    return (out_p[0], out_s[0]) + out_p[1:] + out_s[1:]
```

```python
import functools

import jax
import jax.numpy as jnp
import numpy as np
from jax import lax
from jax.experimental import pallas as pl
from jax.experimental.pallas import tpu as pltpu

f32 = jnp.float32
bf16 = jnp.bfloat16
i32 = jnp.int32

EPS = 1e-6
NEG = -1e30
FORCE = 1e30
ML_HEADS = 8
ML_DH = 128
ML_DIM = ML_HEADS * ML_DH
ML_CONV = 4
ML_QKV_BLOCK = 4
NSA_HEADS = 8
NSA_KV_HEADS = 2
NSA_GROUP = NSA_HEADS // NSA_KV_HEADS
NSA_DH = 128
NSA_DIM = NSA_HEADS * NSA_DH
NSA_KV_W = NSA_KV_HEADS * NSA_DH
CMP_BLOCK = 32
CMP_STRIDE = 16
SEL_BLOCK = 64
SEL_TOPN = 16
WINDOW = 512
PAGE_SIZE = 128
MEM_HEADS = 4
MEM_DH = 128
KV_ROW = 2 * NSA_KV_W
KV_CHUNKS = KV_ROW // NSA_DH

LANES = 128
SUBLANES = 8
VMEM_LIMIT = 56 * 1024 * 1024

MASK_BIAS = -1e9
REMOVED = -3.0e38
SEL_KT = 1024
LOG2E = 1.4426950408889634
SAFE_SPAN = 100.0


def _cparams(*sem):
    return pltpu.CompilerParams(dimension_semantics=sem, vmem_limit_bytes=VMEM_LIMIT)


def _mm(a, b):
    return jnp.dot(a, b, preferred_element_type=f32)


def _mm_nt(a, b):
    return lax.dot_general(a, b, (((1,), (1,)), ((), ())), preferred_element_type=f32)


def _as_bf16(w):
    return w if w.dtype == bf16 else w.astype(bf16)


def _split2(x):
    h = x.astype(bf16)
    return h, (x - h.astype(f32)).astype(bf16)


def _split3(x):
    h = x.astype(bf16)
    r = x - h.astype(f32)
    m = r.astype(bf16)
    return h, m, (r - m.astype(f32)).astype(bf16)


def _dot3(a, b, mm=_mm):
    ah, al = _split2(a)
    bh, bl = _split2(b)
    return mm(ah, bh) + mm(al, bh) + mm(ah, bl)


def _dot_sel_l(sel, x):
    h, m, l = _split3(x)
    return _mm(sel, h) + _mm(sel, m) + _mm(sel, l)


def _dot_sel_r(x, sel):
    h, m, l = _split3(x)
    return _mm(h, sel) + _mm(m, sel) + _mm(l, sel)


def _rms(x, g):
    return x * lax.rsqrt(jnp.mean(x * x, axis=-1, keepdims=True) + EPS) * g


def _div_pow2(x, d):
    assert d & (d - 1) == 0
    return lax.shift_right_logical(x, jnp.full(x.shape, d.bit_length() - 1, x.dtype))


def _sigmoid(x):
    return 0.5 * jnp.tanh(0.5 * x) + 0.5


def _log_sigmoid(x):
    return jnp.minimum(x, 0.0) - jnp.log(1.0 + jnp.exp(-jnp.abs(x)))


def _masked_softmax_rows(s, valid):
    s = jnp.where(valid, s, NEG)
    mx = jnp.max(s, axis=-1, keepdims=True)
    e = jnp.where(valid, jnp.exp(s - mx), 0.0)
    den = jnp.sum(e, axis=-1, keepdims=True)
    return e, jnp.where(den > 0.0, den, 1.0)


def _biased_exp2_rows(s2, bias):
    s2 = s2 + bias
    mx = jnp.max(s2, axis=-1, keepdims=True)
    return jnp.exp2(s2 - mx), mx


def _biased_softmax2_rows(s2, bias):
    e, mx = _biased_exp2_rows(s2, bias)
    inv = jnp.where(mx > 0.5 * NEG, 1.0 / jnp.sum(e, axis=-1, keepdims=True), 0.0)
    return e * inv


def _norm_matmul_kernel(x_ref, g_ref, w_ref, o_ref, xn_ref):
    @pl.when(pl.program_id(1) == 0)
    def _():
        xn_ref[...] = _rms(x_ref[...], g_ref[...]).astype(bf16)

    o_ref[...] = _mm(xn_ref[...], _as_bf16(w_ref[...]))


def _norm_matmul(x, g, w, n_cols, tm, tn, name):
    M, K = x.shape
    return pl.pallas_call(
        _norm_matmul_kernel,
        grid=(M // tm, n_cols // tn),
        in_specs=[pl.BlockSpec((tm, K), lambda i, j: (i, 0)),
                  pl.BlockSpec((1, K), lambda i, j: (0, 0)),
                  pl.BlockSpec((K, tn), lambda i, j: (0, j))],
        out_specs=pl.BlockSpec((tm, tn), lambda i, j: (i, j)),
        out_shape=jax.ShapeDtypeStruct((M, n_cols), f32),
        scratch_shapes=[pltpu.VMEM((tm, K), bf16)],
        compiler_params=_cparams("parallel", "arbitrary"),
        name=name,
    )(x, g.reshape(1, K), w)


def _in_proj_kernel(x_ref, g_ref, w_ref, o_ref, *rest, kv_block0):
    lin_refs, xn_ref = rest[:-1], rest[-1]
    j = pl.program_id(1)

    @pl.when(j == 0)
    def _():
        xn_ref[...] = _rms(x_ref[...], g_ref[...]).astype(bf16)

    val = _mm(xn_ref[...], _as_bf16(w_ref[...]))
    o_ref[...] = val
    rows = val.shape[0]
    for i, ref in enumerate(lin_refs):
        @pl.when(j == kv_block0 + i)
        def _():
            for c in range(KV_CHUNKS):
                ref[pl.ds(c, rows, stride=KV_CHUNKS), :] = val[:, c * LANES:(c + 1) * LANES]


def _in_proj(x, g, w, n_cols, kv_block0, n_kv, tm, name):
    M, K = x.shape
    tn = KV_ROW
    lin_spec = pl.BlockSpec((tm * KV_CHUNKS, LANES), lambda i, j: (i, 0))
    outs = pl.pallas_call(
        functools.partial(_in_proj_kernel, kv_block0=kv_block0),
        grid=(M // tm, n_cols // tn),
        in_specs=[pl.BlockSpec((tm, K), lambda i, j: (i, 0)),
                  pl.BlockSpec((1, K), lambda i, j: (0, 0)),
                  pl.BlockSpec((K, tn), lambda i, j: (0, j))],
        out_specs=[pl.BlockSpec((tm, tn), lambda i, j: (i, j))] + [lin_spec] * n_kv,
        out_shape=[jax.ShapeDtypeStruct((M, n_cols), f32)]
        + [jax.ShapeDtypeStruct((M * KV_CHUNKS, LANES), f32)] * n_kv,
        scratch_shapes=[pltpu.VMEM((tm, K), bf16)],
        compiler_params=_cparams("parallel", "arbitrary"),
        name=name,
    )(x, g.reshape(1, K), w)
    return outs[0], outs[1:]


def _out_proj_kernel(a1_ref, a2_ref, w1_ref, w2_ref, r_ref, o_ref):
    o_ref[...] = (r_ref[...] + _mm(a1_ref[...].astype(bf16), w1_ref[...])
                  + _mm(a2_ref[...].astype(bf16), w2_ref[...]))


def _out_proj(a1, a2, w, res, tm, tn, name):
    M, K1 = a1.shape
    K2 = a2.shape[1]
    assert K1 == K2 and w.shape[0] == K1 + K2
    N = w.shape[1]
    return pl.pallas_call(
        _out_proj_kernel,
        grid=(M // tm, N // tn),
        in_specs=[pl.BlockSpec((tm, K1), lambda i, j: (i, 0)),
                  pl.BlockSpec((tm, K2), lambda i, j: (i, 0)),
                  pl.BlockSpec((K1, tn), lambda i, j: (0, j)),
                  pl.BlockSpec((K2, tn), lambda i, j: (1, j)),
                  pl.BlockSpec((tm, tn), lambda i, j: (i, j))],
        out_specs=pl.BlockSpec((tm, tn), lambda i, j: (i, j)),
        out_shape=jax.ShapeDtypeStruct((M, N), f32),
        compiler_params=_cparams("parallel", "arbitrary"),
        name=name,
    )(a1, a2, w, w, res)


def _matmul_res_kernel(a_ref, w_ref, r_ref, o_ref):
    o_ref[...] = r_ref[...] + _mm(a_ref[...].astype(bf16), w_ref[...])


def _matmul_res(a, w, res, tm, tn, name):
    M, K = a.shape
    N = w.shape[1]
    return pl.pallas_call(
        _matmul_res_kernel,
        grid=(M // tm, N // tn),
        in_specs=[pl.BlockSpec((tm, K), lambda i, j: (i, 0)),
                  pl.BlockSpec((K, tn), lambda i, j: (0, j)),
                  pl.BlockSpec((tm, tn), lambda i, j: (i, j))],
        out_specs=pl.BlockSpec((tm, tn), lambda i, j: (i, j)),
        out_shape=jax.ShapeDtypeStruct((M, N), f32),
        compiler_params=_cparams("parallel", "arbitrary"),
        name=name,
    )(a, w, res)


def _ffn_kernel(h_ref, g_ref, w1_ref, w2_ref, gf_ref, y_ref, xn_ref, acc_ref):
    f = pl.program_id(1)

    @pl.when(f == 0)
    def _():
        xn_ref[...] = _rms(h_ref[...], g_ref[...]).astype(bf16)
        acc_ref[...] = jnp.zeros_like(acc_ref)

    a = _mm(xn_ref[...], w1_ref[...])
    a = jnp.square(jnp.maximum(a, 0.0))
    acc_ref[...] += _mm(a.astype(bf16), w2_ref[...])

    @pl.when(f == pl.num_programs(1) - 1)
    def _():
        y_ref[...] = _rms(h_ref[...] + acc_ref[...], gf_ref[...])


def _ffn_final(h, g, w1, w2, g_final, tm, tf, name):
    M, D = h.shape
    F = w1.shape[1]
    return pl.pallas_call(
        _ffn_kernel,
        grid=(M // tm, F // tf),
        in_specs=[pl.BlockSpec((tm, D), lambda i, j: (i, 0)),
                  pl.BlockSpec((1, D), lambda i, j: (0, 0)),
                  pl.BlockSpec((D, tf), lambda i, j: (0, j)),
                  pl.BlockSpec((tf, D), lambda i, j: (j, 0)),
                  pl.BlockSpec((1, D), lambda i, j: (0, 0))],
        out_specs=pl.BlockSpec((tm, D), lambda i, j: (i, 0)),
        out_shape=jax.ShapeDtypeStruct((M, D), f32),
        scratch_shapes=[pltpu.VMEM((tm, D), bf16), pltpu.VMEM((tm, D), f32)],
        compiler_params=_cparams("parallel", "arbitrary"),
        name=name,
    )(h, g.reshape(1, D), w1, w2, g_final.reshape(1, D))


def _mem_prompt_kernel(h_ref, g_ref, wq_ref, k_ref, v_ref, wo_ref, o_ref):
    h = h_ref[...]
    xn = _rms(h, g_ref[...]).astype(bf16)
    q = _mm(xn, wq_ref[...]) * (MEM_DH ** -0.5)
    outs = []
    for hd in range(MEM_HEADS):
        sl = slice(hd * MEM_DH, (hd + 1) * MEM_DH)
        s = _mm_nt(q[:, sl].astype(bf16), k_ref[:, sl].astype(bf16))
        e = jnp.exp(s - jnp.max(s, axis=-1, keepdims=True))
        p = e * (1.0 / jnp.sum(e, axis=-1, keepdims=True))
        outs.append(_mm(p.astype(bf16), v_ref[:, sl].astype(bf16)))
    o = jnp.concatenate(outs, axis=-1)
    o_ref[...] = h + _mm(o.astype(bf16), wo_ref[...])


def _mem_prompt(h, g, wq, mem_kv, wo, tm, name):
    M, D = h.shape
    HD = MEM_HEADS * MEM_DH
    ML = mem_kv.shape[0]
    return pl.pallas_call(
        _mem_prompt_kernel,
        grid=(M // tm,),
        in_specs=[pl.BlockSpec((tm, D), lambda i: (i, 0)),
                  pl.BlockSpec((1, D), lambda i: (0, 0)),
                  pl.BlockSpec((D, HD), lambda i: (0, 0)),
                  pl.BlockSpec((ML, HD), lambda i: (0, 0)),
                  pl.BlockSpec((ML, HD), lambda i: (0, 1)),
                  pl.BlockSpec((HD, D), lambda i: (0, 0))],
        out_specs=pl.BlockSpec((tm, D), lambda i: (i, 0)),
        out_shape=jax.ShapeDtypeStruct((M, D), f32),
        compiler_params=_cparams("parallel"),
        name=name,
    )(h, g.reshape(1, D), wq, mem_kv, mem_kv, wo)


def _mem_sample_kernel(q_ref, kv_ref, o_ref):
    per_tok = 2 * MEM_HEADS
    ML = kv_ref.shape[0] // per_tok
    q = q_ref[...] * (MEM_DH ** -0.5)
    outs = []
    for hd in range(MEM_HEADS):
        sl = slice(hd * MEM_DH, (hd + 1) * MEM_DH)
        k = kv_ref[pl.ds(hd, ML, stride=per_tok), :]
        v = kv_ref[pl.ds(MEM_HEADS + hd, ML, stride=per_tok), :]
        s = jnp.sum(k * q[:, sl], axis=-1, keepdims=True)
        e = jnp.exp(s - jnp.max(s, axis=0, keepdims=True))
        p = e / jnp.sum(e, axis=0, keepdims=True)
        outs.append(jnp.sum(p * v, axis=0, keepdims=True))
    o_ref[...] = jnp.concatenate(outs, axis=-1)


def _mem_sample(q, kv, name):
    B, HD = q.shape
    rows = kv.shape[1]
    out = pl.pallas_call(
        _mem_sample_kernel,
        grid=(B,),
        in_specs=[pl.BlockSpec((None, 1, HD), lambda b: (b, 0, 0)),
                  pl.BlockSpec((None, rows, MEM_DH), lambda b: (b, 0, 0))],
        out_specs=pl.BlockSpec((None, 1, HD), lambda b: (b, 0, 0)),
        out_shape=jax.ShapeDtypeStruct((B, 1, HD), f32),
        compiler_params=_cparams("parallel"),
        name=name,
    )(q.reshape(B, 1, HD), kv)
    return out.reshape(B, HD)


def _blockdiag_weights(w):
    n, b = w.shape[0], ML_QKV_BLOCK
    per = LANES // b
    wc = w.reshape(n, w.shape[1] // per, per, b * b)
    r, k = np.arange(LANES), np.arange(b * b)
    expand = (r[:, None] // b == np.arange(per)[None, :]).astype(np.float32)
    row_sel = (r[:, None] % b == k[None, :] // b).astype(np.float32)
    col_sel = (k[:, None] % b == r[None, :] % b).astype(np.float32)
    same_blk = (r[:, None] // b == r[None, :] // b).astype(np.float32)
    rows = jnp.einsum('rb,scbk->scrk', expand, wc) * row_sel
    return jnp.einsum('scrk,kq->scrq', rows, col_sel) * same_blk


def _ml_qkv_gates(xc, xm, wbd_ref, wg_ref, bg_ref):
    xc_b, xm_b = xc.astype(bf16), xm.astype(bf16)
    qs, ks, vs = [], [], []
    for c in range(ML_DIM // LANES):
        sl = slice(c * LANES, (c + 1) * LANES)
        qs.append(_mm(xc_b[:, sl], wbd_ref[0, c]))
        ks.append(_mm(xc_b[:, sl], wbd_ref[1, c]) * (ML_DH ** -0.5))
        vs.append(_mm(xm_b[:, sl], wbd_ref[2, c]))
    q = jnp.concatenate(qs, axis=-1)
    k = jnp.concatenate(ks, axis=-1)
    v = jnp.concatenate(vs, axis=-1)
    qkv_b = jnp.concatenate([q, k, v], axis=-1).astype(bf16)
    gates = _mm(qkv_b, wg_ref[...]) + bg_ref[...]
    return q, k, v, gates, qkv_b


def _mlstm_prompt_kernel(xm_ref, og_ref, wconv_ref, bconv_ref, wbd_ref, wg_ref, wgt_ref,
                         bg_ref, bgt_ref, gh_ref, o_ref, c_ref, n_ref, m_ref, prev_sc, *, L):
    H, D = ML_HEADS, ML_DH

    @pl.when(pl.program_id(0) == 0)
    def _():
        prev_sc[...] = jnp.zeros_like(prev_sc)
        c_ref[...] = jnp.zeros_like(c_ref)
        n_ref[...] = jnp.zeros_like(n_ref)
        m_ref[...] = jnp.full(m_ref.shape, NEG, f32)

    x = xm_ref[...]
    full = jnp.concatenate([prev_sc[...], x], axis=0)
    y = bconv_ref[...]
    for j in range(ML_CONV):
        off = SUBLANES - (ML_CONV - 1) + j
        y = y + full[off:off + L] * wconv_ref[j:j + 1, :]
    prev_sc[...] = x[L - SUBLANES:L]
    xc = y * _sigmoid(y)

    q, k, v, gates, qkv_b = _ml_qkv_gates(xc, x, wbd_ref, wg_ref, bg_ref)
    gates_t = _mm_nt(wgt_ref[...], qkv_b) + bgt_ref[...]
    ig_c = gates[:, 0:H]
    lf_c = _log_sigmoid(gates[:, H:2 * H])
    ig_r = gates_t[0:H, :]
    lf_r = _log_sigmoid(gates_t[H:2 * H, :])

    t_i = lax.broadcasted_iota(i32, (L, L), 0)
    s_i = lax.broadcasted_iota(i32, (L, L), 1)
    causal = s_i <= t_i
    tri = jnp.where(causal, 1.0, 0.0).astype(bf16)
    b_c = _dot_sel_l(tri, lf_c)
    tri_u = jnp.where(t_i <= s_i, 1.0, 0.0).astype(bf16)
    b_r = _dot_sel_r(lf_r, tri_u)

    for h in range(H):
        sl = slice(h * D, (h + 1) * D)
        qh, kh, vh = q[:, sl], k[:, sl], v[:, sl]
        bc = b_c[:, h:h + 1]
        m_prev = m_ref[h:h + 1, 0:1]
        d_in = jnp.where(causal, bc - b_r[h:h + 1, :] + ig_r[h:h + 1, :], NEG)
        d_x = bc + m_prev
        m_t = jnp.maximum(d_x, jnp.max(d_in, axis=-1, keepdims=True))
        w_in = jnp.exp(d_in - m_t)
        w_x = jnp.exp(d_x - m_t)
        qb = qh.astype(bf16)
        kb = kh.astype(bf16)
        vb = vh.astype(bf16)
        s = _mm_nt(qb, kb) * w_in
        c_old = c_ref[h]
        n_old = n_ref[h:h + 1, :]
        num = _mm(s.astype(bf16), vb) + w_x * _mm(qb, c_old.astype(bf16))
        den = jnp.sum(s, axis=-1, keepdims=True) + w_x * jnp.sum(qh * n_old, axis=-1, keepdims=True)
        hh = num * (1.0 / jnp.maximum(jnp.abs(den), jnp.exp(-m_t)))
        m_new = m_t[L - 1:L, :]
        b_last = bc[L - 1:L, :]
        g_x = jnp.exp(b_last + m_prev - m_new)
        g_s = jnp.exp(b_last - bc + ig_c[:, h:h + 1] - m_new)
        ks_ = kh * g_s
        c_ref[h] = g_x * c_old + _mm(ks_.T.astype(bf16), vb)
        n_ref[h:h + 1, :] = g_x * n_old + jnp.sum(ks_, axis=0, keepdims=True)
        m_ref[h:h + 1, :] = jnp.broadcast_to(m_new, (1, LANES))
        hn = hh * lax.rsqrt(jnp.mean(hh * hh, axis=-1, keepdims=True) + EPS) * gh_ref[:, sl]
        o_ref[:, sl] = _sigmoid(og_ref[:, sl]) * hn


def _mlstm_prompt(u, w_conv, b_conv, wbd, w_gate, w_gate_t, b_gate, g_head, L, name):
    T = u.shape[0]
    H, D = ML_HEADS, ML_DH
    nch = ML_DIM // LANES
    full2 = lambda shape: pl.BlockSpec(shape, lambda i: (0,) * len(shape))
    out, c, n, m = pl.pallas_call(
        functools.partial(_mlstm_prompt_kernel, L=L),
        grid=(T // L,),
        in_specs=[pl.BlockSpec((L, ML_DIM), lambda i: (i, 0)),
                  pl.BlockSpec((L, ML_DIM), lambda i: (i, 1)),
                  full2((ML_CONV, ML_DIM)), full2((1, ML_DIM)),
                  full2((3, nch, LANES, LANES)),
                  full2((3 * ML_DIM, 2 * H)), full2((2 * H, 3 * ML_DIM)),
                  full2((1, 2 * H)), full2((2 * H, 1)), full2((1, ML_DIM))],
        out_specs=[pl.BlockSpec((L, ML_DIM), lambda i: (i, 0)),
                   full2((H, D, D)), full2((H, D)), full2((H, LANES))],
        out_shape=[jax.ShapeDtypeStruct((T, ML_DIM), f32),
                   jax.ShapeDtypeStruct((H, D, D), f32),
                   jax.ShapeDtypeStruct((H, D), f32),
                   jax.ShapeDtypeStruct((H, LANES), f32)],
        scratch_shapes=[pltpu.VMEM((SUBLANES, ML_DIM), f32)],
        compiler_params=_cparams("arbitrary"),
        name=name,
    )(u, u, w_conv, b_conv.reshape(1, ML_DIM), wbd, w_gate, w_gate_t,
      b_gate.reshape(1, 2 * H), b_gate.reshape(2 * H, 1), g_head.reshape(1, ML_DIM))
    return out, c, n, m[:, 0]


def _mlstm_sample_pre_kernel(xm_ref, s0_ref, s1_ref, s2_ref, wconv_ref, bconv_ref, wbd_ref,
                             wg_ref, bg_ref, q_ref, k_ref, v_ref, g_ref):
    x = xm_ref[...]
    y = (bconv_ref[...] + s0_ref[...] * wconv_ref[0:1, :] + s1_ref[...] * wconv_ref[1:2, :]
         + s2_ref[...] * wconv_ref[2:3, :] + x * wconv_ref[3:4, :])
    xc = y * _sigmoid(y)
    q, k, v, gates, _ = _ml_qkv_gates(xc, x, wbd_ref, wg_ref, bg_ref)
    q_ref[...] = q
    k_ref[...] = k
    v_ref[...] = v
    g_ref[...] = gates


def _mlstm_sample_step_kernel(qc_ref, kc_ref, q_ref, k_ref, v_ref, gt_ref, og_ref, gh_ref, c_ref, n_ref, m_ref,
                              o_ref, cn_ref, nn_ref, mn_ref):
    H, D = ML_HEADS, ML_DH
    ig = gt_ref[0:H, :]
    lf = _log_sigmoid(gt_ref[H:2 * H, :])
    m_old = m_ref[...]
    m_new = jnp.maximum(lf + m_old, ig)
    w_in = jnp.exp(ig - m_new)
    w_x = jnp.exp(lf + m_old - m_new)
    mn_ref[...] = m_new
    q, k, v, n_old = q_ref[...], k_ref[...], v_ref[...], n_ref[...]
    cq = jnp.concatenate([jnp.sum(c_ref[h] * qc_ref[:, h:h + 1], axis=0, keepdims=True) for h in range(H)], axis=0)
    s = jnp.sum(q * k, axis=-1, keepdims=True) * w_in
    num = s * v + w_x * cq
    den = s + w_x * jnp.sum(n_old * q, axis=-1, keepdims=True)
    hh = num / jnp.maximum(jnp.abs(den), jnp.exp(-m_new))
    hn = hh * lax.rsqrt(jnp.mean(hh * hh, axis=-1, keepdims=True) + EPS) * gh_ref[...]
    o_ref[...] = _sigmoid(og_ref[...]) * hn
    nn_ref[...] = w_x * n_old + w_in * k
    for h in range(H):
        cn_ref[h] = w_x[h:h + 1, :] * c_ref[h] + w_in[h:h + 1, :] * (kc_ref[:, h:h + 1] * v[h:h + 1, :])


def _mlstm_sample(xm, og, conv_state, c0, n0, m0, w_conv, b_conv, wbd, w_gate, b_gate, g_head):
    B = xm.shape[0]
    H, D = ML_HEADS, ML_DH
    sds = lambda *s: jax.ShapeDtypeStruct(s, f32)
    q, k, v, gates = pl.pallas_call(
        _mlstm_sample_pre_kernel,
        out_shape=[sds(B, ML_DIM), sds(B, ML_DIM), sds(B, ML_DIM), sds(B, 2 * H)],
        compiler_params=pltpu.CompilerParams(vmem_limit_bytes=VMEM_LIMIT),
        name="mlstm_sample_pre",
    )(xm, conv_state[:, 0], conv_state[:, 1], conv_state[:, 2], w_conv, b_conv.reshape(1, ML_DIM),
      wbd, w_gate, b_gate.reshape(1, 2 * H))
    q3, k3, v3 = (a.reshape(B, H, D) for a in (q, k, v))
    per_b = lambda *s: pl.BlockSpec((None,) + s, lambda b: (b,) + (0,) * len(s))
    out, c, n, m = pl.pallas_call(
        _mlstm_sample_step_kernel,
        grid=(B,),
        in_specs=[per_b(D, H), per_b(D, H), per_b(H, D), per_b(H, D), per_b(H, D), per_b(2 * H, 1),
                  per_b(H, D), pl.BlockSpec((H, D), lambda b: (0, 0)),
                  per_b(H, D, D), per_b(H, D), per_b(H, 1)],
        out_specs=[per_b(H, D), per_b(H, D, D), per_b(H, D), per_b(H, 1)],
        out_shape=[sds(B, H, D), sds(B, H, D, D), sds(B, H, D), sds(B, H, 1)],
        compiler_params=_cparams("parallel"),
        name="mlstm_sample_step",
    )(q3.transpose(0, 2, 1), k3.transpose(0, 2, 1), q3, k3, v3, gates.reshape(B, 2 * H, 1),
      og.reshape(B, H, D), g_head.reshape(H, D), c0, n0, m0.reshape(B, H, 1))
    return out.reshape(B, ML_DIM), c, n, m.reshape(B, H)


def _compress_kernel(x_ref, pe_ref, wp_ref, wphi_ref, o_ref, f0_sc, f1_sc, mn_sc):
    step = pl.program_id(0)
    sub = x_ref.shape[0] // CMP_STRIDE
    x3 = x_ref[...].reshape(sub, CMP_STRIDE, KV_ROW)
    base = pl.multiple_of(step * sub, sub)
    for o, sc in ((0, f0_sc), (1, f1_sc)):
        y = x3 + pe_ref[o][None]
        sc[pl.ds(base, sub), :] = jnp.sum(y * _sigmoid(y) * wp_ref[o][None], axis=1)
    mn_sc[pl.ds(base, sub), :] = jnp.sum(x3, axis=1) * (1.0 / CMP_STRIDE)

    @pl.when(step == pl.num_programs(0) - 1)
    def _():
        ns = f0_sc.shape[0]
        feat = f0_sc[...] + pltpu.roll(f1_sc[...], ns - 1, axis=0)
        mn = mn_sc[...]
        pooled = (mn + pltpu.roll(mn, ns - 1, axis=0)) * (CMP_STRIDE / CMP_BLOCK)
        for c in range(2):
            for g in range(NSA_KV_HEADS):
                sl = slice((c * NSA_KV_HEADS + g) * NSA_DH, (c * NSA_KV_HEADS + g + 1) * NSA_DH)
                o_ref[:, sl] = pooled[:, sl] + _dot3(feat[:, sl], wphi_ref[c])


def _compress_tables(pe, wpos):
    def lay(a):
        r = CMP_BLOCK // CMP_STRIDE
        a = a.reshape(2, r, CMP_STRIDE, NSA_DH).transpose(1, 2, 0, 3)
        a = jnp.broadcast_to(a[:, :, :, None, :], (r, CMP_STRIDE, 2, NSA_KV_HEADS, NSA_DH))
        return a.reshape(r, CMP_STRIDE, KV_ROW)
    return lay(pe), lay(wpos)


def _compress_prompt(u, col_block, pe_t, wp_t, wphi, rows=512):
    T = u.shape[0]
    n_sub = T // CMP_STRIDE
    const = lambda shape: pl.BlockSpec(shape, lambda s: (0,) * len(shape))
    return pl.pallas_call(
        _compress_kernel,
        grid=(T // rows,),
        in_specs=[pl.BlockSpec((rows, KV_ROW), lambda s: (s, col_block)),
                  const(pe_t.shape), const(wp_t.shape), const(wphi.shape)],
        out_specs=const((n_sub, KV_ROW)),
        out_shape=jax.ShapeDtypeStruct((n_sub, KV_ROW), f32),
        scratch_shapes=[pltpu.VMEM((n_sub, KV_ROW), f32)] * 3,
        compiler_params=_cparams("arbitrary"),
        name="compress_prompt",
    )(u, pe_t, wp_t, wphi)


def _compress_paged_kernel(pt_ref, *refs, n_pages):
    pages = refs[:n_pages]
    pe_ref, wp_ref, wphi_ref, o_ref, f0_sc, f1_sc, mn_sc = refs[n_pages:]
    step = pl.program_id(1)
    sub = PAGE_SIZE // CMP_STRIDE
    tiles = CMP_STRIDE * KV_CHUNKS // SUBLANES
    packed = 2 * SUBLANES
    ptiles = CMP_STRIDE * KV_CHUNKS // packed
    out_rows = sub * SUBLANES
    for p in range(n_pages):
        x = pages[p][...]
        base = pl.multiple_of((step * n_pages + p) * out_rows, out_rows)
        xh = (x.astype(bf16) * 0.5).reshape(sub, ptiles, packed, LANES)
        for o, sc in ((0, f0_sc), (1, f1_sc)):
            y = xh + pe_ref[o][None]
            b = y * wp_ref[o][None]
            z = b + b * jnp.tanh(y)
            z = (z[:, 0] + z[:, 1]) + (z[:, 2] + z[:, 3])
            zf = z.astype(f32).reshape(sub, packed // SUBLANES, SUBLANES, LANES)
            sc[pl.ds(base, out_rows), :] = jnp.sum(zf, axis=1).reshape(out_rows, LANES)
        mean = jnp.sum(x.reshape(sub, tiles, SUBLANES, LANES), axis=1) * (1.0 / CMP_STRIDE)
        mn_sc[pl.ds(base, out_rows), :] = mean.reshape(out_rows, LANES)

    @pl.when(step == pl.num_programs(1) - 1)
    def _():
        ns = f0_sc.shape[0] // SUBLANES
        for c in range(KV_CHUNKS):
            col = lambda sc: (sc[pl.ds(c, ns, stride=SUBLANES), :]
                              + sc[pl.ds(KV_CHUNKS + c, ns, stride=SUBLANES), :])
            feat = col(f0_sc) + pltpu.roll(col(f1_sc), ns - 1, axis=0)
            mn = col(mn_sc)
            pooled = (mn + pltpu.roll(mn, ns - 1, axis=0)) * (CMP_STRIDE / CMP_BLOCK)
            o_ref[:, c * LANES:(c + 1) * LANES] = pooled + _dot3(feat, wphi_ref[c // NSA_KV_HEADS])


def _compress_paged(pool, page_table, pe_t, wp_t, wphi, pages_per_step=64):
    B, n_pages = page_table.shape
    P = pages_per_step
    n_sub = n_pages * PAGE_SIZE // CMP_STRIDE
    r = CMP_BLOCK // CMP_STRIDE
    packed = 2 * SUBLANES
    ptiles = CMP_STRIDE * KV_CHUNKS // packed
    pe4 = (0.5 * pe_t).astype(bf16).reshape(r, ptiles, packed, LANES)
    wp4 = wp_t.astype(bf16).reshape(r, ptiles, packed, LANES)
    page_rows = PAGE_SIZE * KV_CHUNKS
    specs = [pl.BlockSpec((None, page_rows, LANES),
                          functools.partial(lambda b, s, pt, p: (pt[b * n_pages + s * P + p], 0, 0), p=p))
             for p in range(P)]
    const = lambda shape: pl.BlockSpec(shape, lambda *a: (0,) * len(shape))
    gs = pltpu.PrefetchScalarGridSpec(
        num_scalar_prefetch=1, grid=(B, n_pages // P),
        in_specs=specs + [const(pe4.shape), const(wp4.shape), const(wphi.shape)],
        out_specs=pl.BlockSpec((None, n_sub, KV_ROW), lambda b, s, pt: (b, 0, 0)),
        scratch_shapes=[pltpu.VMEM((n_sub * SUBLANES, LANES), f32)] * 3)
    return pl.pallas_call(
        functools.partial(_compress_paged_kernel, n_pages=P),
        grid_spec=gs,
        out_shape=jax.ShapeDtypeStruct((B, n_sub, KV_ROW), f32),
        compiler_params=_cparams("parallel", "arbitrary"),
        name="compress_paged",
    )(page_table.reshape(-1), *([pool] * P), pe4, wp4, wphi)


def _kv_prep_kernel(ks_ref, kw_ref, ksk_ref, ksv_ref, kwk_ref, kwv_ref, kn2_ref):
    rows = ks_ref.shape[0]
    ks = ks_ref[...]
    kw = kw_ref[...]
    r = pl.program_id(0) * rows + lax.broadcasted_iota(i32, (rows, LANES), 0)
    n = lax.broadcasted_iota(i32, (rows, LANES), 1)
    onehot = jnp.where(_div_pow2(r, SEL_BLOCK) == n, 1.0, 0.0).astype(bf16)
    ones_col = jnp.where(n == 0, 1.0, 0.0).astype(bf16)

    @pl.when(pl.program_id(0) == 0)
    def _():
        kn2_ref[...] = jnp.zeros_like(kn2_ref)

    for g in range(NSA_KV_HEADS):
        ksl = slice(g * NSA_DH, (g + 1) * NSA_DH)
        vsl = slice(NSA_KV_W + g * NSA_DH, NSA_KV_W + (g + 1) * NSA_DH)
        kb = ks[:, ksl].astype(bf16)
        ksk_ref[g, :, 0:NSA_DH] = kb
        ksk_ref[g, :, NSA_DH:NSA_DH + LANES] = onehot
        ksv_ref[g, :, 0:NSA_DH] = ks[:, vsl].astype(bf16)
        ksv_ref[g, :, NSA_DH:NSA_DH + LANES] = ones_col
        kwk_ref[g] = kw[:, ksl].astype(bf16)
        kwv_ref[g, :, 0:NSA_DH] = kw[:, vsl].astype(bf16)
        kwv_ref[g, :, NSA_DH:NSA_DH + LANES] = ones_col
        kf = kb.astype(f32)
        n2 = jnp.max(jnp.sum(kf * kf, axis=-1, keepdims=True), axis=0, keepdims=True)
        kn2_ref[g] = jnp.maximum(kn2_ref[g], jnp.broadcast_to(n2, kn2_ref.shape[1:]))


def _kv_prep(u, ks_col_block, kw_col_block, rows=512):
    T = u.shape[0]
    G = NSA_KV_HEADS
    assert (T - 1) // SEL_BLOCK + 1 <= LANES
    sd = lambda w: jax.ShapeDtypeStruct((G, T, w), bf16)
    ospec = lambda w: pl.BlockSpec((G, rows, w), lambda i: (0, i, 0))
    return pl.pallas_call(
        _kv_prep_kernel,
        grid=(T // rows,),
        in_specs=[pl.BlockSpec((rows, KV_ROW), lambda i: (i, ks_col_block)),
                  pl.BlockSpec((rows, KV_ROW), lambda i: (i, kw_col_block))],
        out_specs=[ospec(NSA_DH + LANES), ospec(NSA_DH + LANES), ospec(NSA_DH), ospec(NSA_DH + LANES),
                   pl.BlockSpec((G, SUBLANES, LANES), lambda i: (0, 0, 0))],
        out_shape=[sd(NSA_DH + LANES), sd(NSA_DH + LANES), sd(NSA_DH), sd(NSA_DH + LANES),
                   jax.ShapeDtypeStruct((G, SUBLANES, LANES), f32)],
        compiler_params=_cparams("arbitrary"),
        name="kv_prep",
    )(u, u)


def _top_blocks(score, n_top, axis):
    lane = lax.broadcasted_iota(i32, score.shape, axis).astype(f32)
    width = float(score.shape[axis])
    work = score
    firsts = []
    for _ in range(n_top):
        mx = jnp.max(work, axis=axis, keepdims=True)
        first = jnp.min(jnp.where(work == mx, lane, width), axis=axis, keepdims=True)
        work = jnp.where(lane == first, REMOVED, work)
        firsts.append(first)
    return work, firsts


def _nsa_prompt_kernel(x_ref, gmix_ref, wgt_ref, q_ref, kck_ref, kcv_ref, ksk_ref, ksv_ref, kwk_ref, kwv_ref, kn2_ref,
                       o_ref, shift_sc, m_sc, acc_sc, *, n_sel):
    R, D = NSA_GROUP, NSA_DH
    QB = q_ref.shape[0]
    rows = R * QB
    qb = pl.program_id(1)
    q = q_ref[...]
    q2_b = (jnp.concatenate([q[:, r * D:(r + 1) * D] for r in range(R)], axis=0) * (D ** -0.5 * LOG2E)).astype(bf16)
    pos = qb * QB + lax.broadcasted_iota(i32, (QB, 1), 0)

    def head_bias(valid):
        return jnp.concatenate([jnp.where(valid, 0.0, NEG)] * R, axis=0)

    ns = kck_ref.shape[0]
    s = _mm_nt(q2_b, kck_ref[...].astype(bf16))
    j = lax.broadcasted_iota(i32, (1, ns), 1)
    p_c = _biased_softmax2_rows(s, head_bias(j * CMP_STRIDE + (CMP_BLOCK - 1) <= pos))
    o_c = _mm(p_c.astype(bf16), kcv_ref[...].astype(bf16))

    imp = p_c[0:QB]
    for r in range(1, R):
        imp = imp + p_c[r * QB:(r + 1) * QB]
    ratio = SEL_BLOCK // CMP_STRIDE
    off = CMP_BLOCK // CMP_STRIDE - 1
    nn = lax.broadcasted_iota(i32, (LANES, ns), 0)
    jj = lax.broadcasted_iota(i32, (LANES, ns), 1)
    overlap_t = jnp.where((jj >= ratio * nn - off) & (jj < ratio * nn + ratio), 1.0, 0.0).astype(bf16)
    imp_sel = sum(_mm_nt(overlap_t, part) for part in _split3(imp))
    n_idx = lax.broadcasted_iota(i32, (LANES, QB), 0)
    cur = _div_pow2(qb * QB + lax.broadcasted_iota(i32, (LANES, QB), 1), SEL_BLOCK)
    forced = (n_idx == 0) | (n_idx == cur) | (n_idx == cur - 1)
    score = jnp.where(forced, FORCE, jnp.where(n_idx <= cur, imp_sel, NEG))
    score = jnp.where(n_idx < n_sel, score, REMOVED)
    taken, _ = _top_blocks(score, min(SEL_TOPN, n_sel), axis=0)
    bias = jnp.where((taken < 0.5 * REMOVED) & (n_idx <= cur), 0.0, MASK_BIAS).T

    KT = SEL_KT
    last = (qb * QB + QB - 1) // KT
    q2 = q2_b.astype(f32)
    bound = jnp.sqrt(jnp.sum(q2 * q2, axis=-1, keepdims=True) * kn2_ref[0:1, 0:1]) * 1.01 + 1e-3
    k_self = ksk_ref[pl.ds(pl.multiple_of(qb * QB, QB), QB), 0:D].astype(f32)
    s_self = jnp.concatenate([jnp.sum(q2[r * QB:(r + 1) * QB] * k_self, axis=-1, keepdims=True) for r in range(R)],
                             axis=0)
    shift_sc[...] = bound
    diag_bias = head_bias(last * KT + lax.broadcasted_iota(i32, (1, KT), 1) <= pos)

    def shifted_queries(shift):
        return jnp.concatenate(
            [jnp.concatenate([q2_b[r * QB:(r + 1) * QB], (bias - shift[r * QB:(r + 1) * QB]).astype(bf16)], axis=1)
             for r in range(R)], axis=0)

    def key_tile(ref, kt):
        return ref[pl.ds(pl.multiple_of(kt * KT, KT), KT), :]

    @pl.when(jnp.max(bound - s_self) > SAFE_SPAN)
    def _():
        qp0 = shifted_queries(jnp.zeros_like(bound))

        def lane_max(kt, sk):
            mx = m_sc[...]
            for c in range(KT // LANES):
                mx = jnp.maximum(mx, sk[:, c * LANES:(c + 1) * LANES])
            m_sc[...] = mx

        def max_pass(kt, carry):
            lane_max(kt, _mm_nt(qp0, key_tile(ksk_ref, kt)))
            return carry

        m_sc[...] = jnp.full(m_sc.shape, NEG, f32)
        lax.fori_loop(0, last, max_pass, 0)
        lane_max(last, _mm_nt(qp0, key_tile(ksk_ref, last)) + diag_bias)
        shift_sc[...] = jnp.max(m_sc[...], axis=-1, keepdims=True)

    qp = shifted_queries(shift_sc[...])
    acc_sc[...] = jnp.zeros_like(acc_sc)

    def accumulate(kt, s2):
        acc_sc[...] += _mm(jnp.exp2(s2).astype(bf16), key_tile(ksv_ref, kt))

    def body(kt, carry):
        accumulate(kt, _mm_nt(qp, key_tile(ksk_ref, kt)))
        return carry

    lax.fori_loop(0, last, body, 0)
    accumulate(last, _mm_nt(qp, key_tile(ksk_ref, last)) + diag_bias)
    acc = acc_sc[...]
    o_s = acc[:, 0:D] * (1.0 / acc[:, D:D + 1])

    wlen = WINDOW + QB
    wstart = pl.multiple_of(jnp.maximum(qb * QB - WINDOW, 0), int(np.gcd(QB, WINDOW)))
    sw = _mm_nt(q2_b, kwk_ref[pl.ds(wstart, wlen), :])
    diff = pos - (wstart + lax.broadcasted_iota(i32, (1, wlen), 1))
    e_w, _ = _biased_exp2_rows(sw, head_bias((diff >= 0) & (diff <= WINDOW)))
    acc_w = _mm(e_w.astype(bf16), kwv_ref[pl.ds(wstart, wlen), :])
    o_w = acc_w[:, 0:D] * (1.0 / acc_w[:, D:D + 1])

    gate = _sigmoid(_mm(_rms(x_ref[...], gmix_ref[...]).astype(bf16), wgt_ref[...]))
    for r in range(R):
        rs = slice(r * QB, (r + 1) * QB)
        o_ref[:, r * D:(r + 1) * D] = (gate[:, r:r + 1] * o_c[rs] + gate[:, R + r:R + r + 1] * o_s[rs]
                                       + gate[:, 2 * R + r:2 * R + r + 1] * o_w[rs])


def _nsa_prompt(x, g_mix, w_gate_g, u, q_col_block, kvc, ksk, ksv, kwk, kwv, kn2, tq):
    T, Dm = x.shape
    G, R, D, QB = NSA_KV_HEADS, NSA_GROUP, NSA_DH, tq
    ns = kvc.shape[0]
    n_sel = (T - 1) // SEL_BLOCK + 1
    assert T % SEL_KT == 0 and T % QB == 0 and T >= WINDOW + QB and n_sel <= LANES
    rows = R * QB
    res = lambda w: pl.BlockSpec((None, T, w), lambda g, i: (g, 0, 0))
    return pl.pallas_call(
        functools.partial(_nsa_prompt_kernel, n_sel=n_sel),
        grid=(G, T // QB),
        in_specs=[pl.BlockSpec((QB, Dm), lambda g, i: (i, 0)),
                  pl.BlockSpec((1, Dm), lambda g, i: (0, 0)),
                  pl.BlockSpec((None, Dm, LANES), lambda g, i: (g, 0, 0)),
                  pl.BlockSpec((QB, R * D), lambda g, i: (i, q_col_block + g)),
                  pl.BlockSpec((ns, D), lambda g, i: (0, g)),
                  pl.BlockSpec((ns, D), lambda g, i: (0, G + g)),
                  res(D + LANES), res(D + LANES), res(D), res(D + LANES),
                  pl.BlockSpec((None, SUBLANES, LANES), lambda g, i: (g, 0, 0))],
        out_specs=pl.BlockSpec((QB, R * D), lambda g, i: (i, g)),
        out_shape=jax.ShapeDtypeStruct((T, NSA_DIM), f32),
        scratch_shapes=[pltpu.VMEM((rows, 1), f32), pltpu.VMEM((rows, LANES), f32),
                        pltpu.VMEM((rows, D + LANES), f32)],
        compiler_params=_cparams("parallel", "arbitrary"),
        name="nsa_prompt",
    )(x, g_mix.reshape(1, Dm), w_gate_g, u, kvc, kvc, ksk, ksv, kwk, kwv, kn2)


def _group_gate_weights(w_gate):
    Dm = w_gate.shape[0]
    G, R = NSA_KV_HEADS, NSA_GROUP
    w = w_gate.reshape(Dm, 3, G, R).transpose(2, 0, 1, 3).reshape(G, Dm, 3 * R)
    return jnp.pad(w, ((0, 0), (0, 0), (0, LANES - 3 * R)))


def _nsa_sample_cmp_kernel(q_ref, kvc_ref, oc_ref, idx_ref, *, pos, n_sel, sel_w):
    H, R, D, G = NSA_HEADS, NSA_GROUP, NSA_DH, NSA_KV_HEADS
    ns = kvc_ref.shape[0]
    qs_b = (q_ref[...] * (D ** -0.5)).astype(bf16)
    head = lax.broadcasted_iota(i32, (H, 1), 0)
    grp = _div_pow2(head, R)
    s = jnp.zeros((H, ns), f32)
    for g in range(G):
        s = jnp.where(grp == g, _mm_nt(qs_b, kvc_ref[:, g * D:(g + 1) * D].astype(bf16)), s)
    j = lax.broadcasted_iota(i32, (1, ns), 1)
    e, den = _masked_softmax_rows(s, j * CMP_STRIDE + (CMP_BLOCK - 1) <= pos)
    p_c = e / den
    p_b = p_c.astype(bf16)
    o_c = jnp.zeros((H, D), f32)
    imp = jnp.zeros((H, ns), f32)
    for g in range(G):
        o_c = jnp.where(grp == g, _mm(p_b, kvc_ref[:, (G + g) * D:(G + g + 1) * D].astype(bf16)), o_c)
        imp = jnp.where(grp == g, jnp.sum(jnp.where(grp == g, p_c, 0.0), axis=0, keepdims=True), imp)
    oc_ref[...] = o_c
    ratio = SEL_BLOCK // CMP_STRIDE
    off = CMP_BLOCK // CMP_STRIDE - 1
    nn = lax.broadcasted_iota(i32, (sel_w, ns), 0)
    jj = lax.broadcasted_iota(i32, (sel_w, ns), 1)
    overlap_t = jnp.where((jj >= ratio * nn - off) & (jj < ratio * nn + ratio), 1.0, 0.0).astype(bf16)
    imp_sel = sum(_mm_nt(overlap_t, part) for part in _split3(imp))
    n_idx = lax.broadcasted_iota(i32, (sel_w, H), 0)
    cur = pos // SEL_BLOCK
    forced = (n_idx == 0) | (n_idx == cur) | (n_idx == cur - 1)
    score = jnp.where(forced, FORCE, jnp.where(n_idx <= cur, imp_sel, NEG))
    score = jnp.where(n_idx < n_sel, score, REMOVED)
    _, firsts = _top_blocks(score, SEL_TOPN, axis=0)
    idx_ref[...] = jnp.concatenate(firsts, axis=0).astype(i32)


def _nsa_sample_cmp(q, kvc, pos):
    B, H, D = q.shape
    ns = kvc.shape[1]
    n_sel = pos // SEL_BLOCK + 1
    assert n_sel >= SEL_TOPN
    sel_w = -(-n_sel // SUBLANES) * SUBLANES
    o_c, idx = pl.pallas_call(
        functools.partial(_nsa_sample_cmp_kernel, pos=pos, n_sel=n_sel, sel_w=sel_w),
        grid=(B,),
        in_specs=[pl.BlockSpec((None, H, D), lambda b: (b, 0, 0)),
                  pl.BlockSpec((None, ns, KV_ROW), lambda b: (b, 0, 0))],
        out_specs=[pl.BlockSpec((None, H, D), lambda b: (b, 0, 0)),
                   pl.BlockSpec((None, SEL_TOPN, H), lambda b: (b, 0, 0))],
        out_shape=[jax.ShapeDtypeStruct((B, H, D), f32), jax.ShapeDtypeStruct((B, SEL_TOPN, H), i32)],
        compiler_params=_cparams("parallel"),
        name="nsa_sample_cmp",
    )(q, kvc)
    return o_c, idx[:, :, ::NSA_GROUP].transpose(0, 2, 1)


def _nsa_sample_sel_kernel(idx_ref, pt_ref, q_ref, new_ref, *refs, past):
    H, R, D, G = NSA_HEADS, NSA_GROUP, NSA_DH, NSA_KV_HEADS
    blocks, o_ref = refs[:G * SEL_TOPN], refs[G * SEL_TOPN]
    b = pl.program_id(0)
    qs_b = (q_ref[...] * (D ** -0.5)).astype(bf16)
    grp = _div_pow2(lax.broadcasted_iota(i32, (H, 1), 0), R)
    per_tile = SUBLANES // KV_CHUNKS
    half = SEL_BLOCK // per_tile
    o = jnp.zeros((H, D), f32)
    for g in range(G):
        ks, vs = [], []
        lane = lax.broadcasted_iota(i32, (1, SEL_TOPN * SEL_BLOCK), 1)
        slot = _div_pow2(lane, SEL_BLOCK)
        in_slot = lane & (SEL_BLOCK - 1)
        pk_row = (in_slot & (half - 1)) * per_tile + _div_pow2(in_slot, half)
        for i in range(SEL_TOPN):
            blk = blocks[g * SEL_TOPN + i]
            base = idx_ref[(b * G + g) * SEL_TOPN + i] * SEL_BLOCK
            pk_row = pk_row + jnp.where(slot == i, base, 0)
            for par in range(per_tile):
                pk_col = base + per_tile * lax.broadcasted_iota(i32, (half, 1), 0) + par
                old = pk_col < past
                k = blk[pl.ds(par * KV_CHUNKS + g, half, stride=SUBLANES), :]
                v = blk[pl.ds(par * KV_CHUNKS + G + g, half, stride=SUBLANES), :]
                ks.append(jnp.where(old, k, new_ref[g:g + 1, :]).astype(bf16))
                vs.append(jnp.where(old, v, new_ref[G + g:G + g + 1, :]).astype(bf16))
        s = _mm_nt(qs_b, jnp.concatenate(ks, axis=0))
        e, den = _masked_softmax_rows(s, pk_row <= past)
        o = jnp.where(grp == g, _mm((e / den).astype(bf16), jnp.concatenate(vs, axis=0)), o)
    o_ref[...] = o


def _nsa_sample_sel(q, ks_new, pool, page_table, idx, past):
    B, H, D = q.shape
    G = NSA_KV_HEADS
    n_pages = page_table.shape[1]
    halves = PAGE_SIZE // SEL_BLOCK
    blk_rows = SEL_BLOCK * KV_CHUNKS
    pool_h = pool.reshape(pool.shape[0] * halves, blk_rows, D)
    last_old = past // SEL_BLOCK - 1

    def blk_map(b, idx_r, pt_r, g, i):
        blk = jnp.minimum(idx_r[(b * G + g) * SEL_TOPN + i], last_old)
        return (pt_r[b * n_pages + blk // halves] * halves + blk % halves, 0, 0)

    specs = [pl.BlockSpec((None, blk_rows, D), functools.partial(blk_map, g=g, i=i))
             for g in range(G) for i in range(SEL_TOPN)]
    gs = pltpu.PrefetchScalarGridSpec(
        num_scalar_prefetch=2, grid=(B,),
        in_specs=[pl.BlockSpec((None, H, D), lambda b, *_: (b, 0, 0)),
                  pl.BlockSpec((None, KV_CHUNKS, D), lambda b, *_: (b, 0, 0))] + specs,
        out_specs=pl.BlockSpec((None, H, D), lambda b, *_: (b, 0, 0)))
    return pl.pallas_call(
        functools.partial(_nsa_sample_sel_kernel, past=past),
        grid_spec=gs,
        out_shape=jax.ShapeDtypeStruct((B, H, D), f32),
        compiler_params=_cparams("arbitrary"),
        name="nsa_sample_sel",
    )(idx.reshape(-1), page_table.reshape(-1), q, ks_new, *([pool_h] * (G * SEL_TOPN)))


def _nsa_sample_win_kernel(q_ref, buf_ref, new_ref, oc_ref, os_ref, gt_ref, o_ref, win_ref, *, past):
    H, R, D, G = NSA_HEADS, NSA_GROUP, NSA_DH, NSA_KV_HEADS
    rows = buf_ref.shape[0]
    Lb = rows // KV_CHUNKS
    per_tile = SUBLANES // KV_CHUNKS
    half = Lb // per_tile
    qs = q_ref[...] * (D ** -0.5)
    qs_b = qs.astype(bf16)
    grp = _div_pow2(lax.broadcasted_iota(i32, (H, 1), 0), R)
    lane = lax.broadcasted_iota(i32, (1, Lb), 1)
    key_pos = past - Lb + (lane & (half - 1)) * per_tile + _div_pow2(lane, half)
    diff = past - key_pos
    valid = (diff >= 0) & (diff <= WINDOW)
    o_w = jnp.zeros((H, D), f32)
    for g in range(G):
        chunk = lambda c: jnp.concatenate(
            [buf_ref[pl.ds(par * KV_CHUNKS + c, half, stride=SUBLANES), :] for par in range(per_tile)], axis=0)
        new_k, new_v = new_ref[g:g + 1, :], new_ref[G + g:G + g + 1, :]
        s_b = jnp.where(valid, _mm_nt(qs_b, chunk(g).astype(bf16)), NEG)
        s_n = jnp.sum(qs * new_k, axis=-1, keepdims=True)
        mx = jnp.maximum(jnp.max(s_b, axis=-1, keepdims=True), s_n)
        e_b = jnp.where(valid, jnp.exp(s_b - mx), 0.0)
        e_n = jnp.exp(s_n - mx)
        den = jnp.sum(e_b, axis=-1, keepdims=True) + e_n
        og = _mm((e_b / den).astype(bf16), chunk(G + g).astype(bf16)) + (e_n / den) * new_v
        o_w = jnp.where(grp == g, og, o_w)
    gate = _sigmoid(gt_ref[...])
    o_ref[...] = gate[:, 0:1] * oc_ref[...] + gate[:, 1:2] * os_ref[...] + gate[:, 2:3] * o_w
    win_ref[0:rows - KV_CHUNKS, :] = buf_ref[KV_CHUNKS:rows, :]
    win_ref[rows - KV_CHUNKS:rows, :] = new_ref[...]


def _nsa_sample_win(q, win_buf, kw_new, o_c, o_s, gt, past):
    B, H, D = q.shape
    rows = win_buf.shape[1]
    assert rows == WINDOW * KV_CHUNKS
    per_b = lambda *s: pl.BlockSpec((None,) + s, lambda b: (b,) + (0,) * len(s))
    return pl.pallas_call(
        functools.partial(_nsa_sample_win_kernel, past=past),
        grid=(B,),
        in_specs=[per_b(H, D), per_b(rows, D), per_b(KV_CHUNKS, D), per_b(H, D), per_b(H, D), per_b(H, 3)],
        out_specs=[per_b(H, D), per_b(rows, D)],
        out_shape=[jax.ShapeDtypeStruct((B, H, D), f32), jax.ShapeDtypeStruct((B, rows, D), f32)],
        compiler_params=_cparams("parallel"),
        name="nsa_sample_win",
    )(q, win_buf, kw_new, o_c, o_s, gt.reshape(B, 3, H).transpose(0, 2, 1))


def kernel(x_prompt, x_sample, cache_cmp_kv, cache_sel_kv, cache_win_kv, cache_mem_kv, state_mlstm_c, state_mlstm_n, state_mlstm_m, state_conv, page_table, mem_prompt, g_mix, w_in, w_conv, b_conv, w_mq, w_mk, w_mv, w_mgate, b_mgate, g_mhead, cmp_pe, cmp_wpos, cmp_wphi, w_out, g_memx, g_mems, w_mem_q, w_mem_kv, w_mem_o, g_ffn, w_ff1, w_ff2, g_final):
    depth = w_in.shape[0]
    assert depth == 1 and x_prompt.shape[0] == 1 and x_sample.shape[1] == 1
    T, Dm = x_prompt.shape[1:]
    B = x_sample.shape[0]
    G, Dh, H = NSA_KV_HEADS, NSA_DH, ML_HEADS
    past = page_table.shape[1] * PAGE_SIZE
    assert (past + 1) // CMP_STRIDE == past // CMP_STRIDE
    n_main = 2 * ML_DIM + NSA_DIM + 3 * KV_ROW
    n_gate = 3 * NSA_HEADS
    l = 0
    hp = x_prompt.reshape(T, Dm)
    hs = x_sample.reshape(B, Dm)

    wbd = _blockdiag_weights(jnp.stack([w_mq[l], w_mk[l], w_mv[l]])).astype(bf16)
    w_mg = w_mgate[l].astype(bf16)
    w_mg_t = w_mg.T
    pe_t, wp_t = _compress_tables(cmp_pe[l], cmp_wpos[l])
    wb_out, wb_mq, wb_mkv, wb_mo, wb_f1, wb_f2 = (
        w[l].astype(bf16) for w in (w_out, w_mem_q, w_mem_kv, w_mem_o, w_ff1, w_ff2))
    wb_in = w_in[l]
    wb_gate = w_in[l, :, n_main:].astype(bf16)
    w_gt = jnp.pad(wb_gate, ((0, 0), (0, LANES - n_gate)))
    q_cb = 2 * ML_DIM // KV_ROW
    c_cb = (2 * ML_DIM + NSA_DIM) // KV_ROW
    n_mq, n_mkv = w_mem_q.shape[-1], w_mem_kv.shape[-1]
    tm_in, tm, tn, tf, tq, ml_chunk = 1024, 512, 512, 1024, 256, 256

    u, (kc_lin, ks_lin, kw_lin) = _in_proj(hp, g_mix[l], wb_in, n_main, c_cb, 3, tm_in, "in_proj_p")
    ml_o, c_p, n_p, m_p = _mlstm_prompt(u, w_conv[l], b_conv[l], wbd, w_mg, w_mg_t, b_mgate[l],
                                        g_mhead[l], ml_chunk, "mlstm_prompt")
    kvc_p = _compress_prompt(u, c_cb, pe_t, wp_t, cmp_wphi[l])
    nsa_o = _nsa_prompt(hp, g_mix[l], _group_gate_weights(wb_gate), u, q_cb, kvc_p,
                        *_kv_prep(u, c_cb + 1, c_cb + 2), tq)
    h1 = _out_proj(ml_o, nsa_o, wb_out, hp, tm, Dm, "out_proj_p")
    mem_kv = _norm_matmul(mem_prompt.reshape(-1, Dm), g_mems[l], wb_mkv, n_mkv, mem_prompt.shape[1], tn, "mem_kv")
    h2 = _mem_prompt(h1, g_memx[l], wb_mq, mem_kv, wb_mo, tm, "mem_attn_p")
    y_p = _ffn_final(h2, g_ffn[l], wb_f1, wb_f2, g_final, tm, tf, "ffn_p")

    kv_shape = lambda a: a.reshape(1, 1, -1, 2, G, Dh)
    out_p = (y_p.reshape(1, T, Dm),
             kv_shape(kc_lin), kv_shape(ks_lin),
             kv_shape(kw_lin[(T - min(WINDOW, T)) * KV_CHUNKS:]),
             mem_kv.reshape(1, 1, -1, 2, MEM_HEADS, MEM_DH),
             c_p[None, None], n_p[None, None], m_p[None, None],
             u[T - (ML_CONV - 1):, :ML_DIM][None, None])

    us = _norm_matmul(hs, g_mix[l], wb_in, n_main, B, tn, "in_proj_s")
    gts = _norm_matmul(hs, g_mix[l], w_gt, LANES, B, LANES, "in_gate_s")[:, :n_gate]
    xm_s, og_s = us[:, :ML_DIM], us[:, ML_DIM:2 * ML_DIM]
    q_s = us[:, 2 * ML_DIM:2 * ML_DIM + NSA_DIM].reshape(B, NSA_HEADS, Dh)
    kc_s, ks_s, kw_s = (us[:, (c_cb + i) * KV_ROW:(c_cb + i + 1) * KV_ROW] for i in range(3))
    ml_os, c_s, n_s, m_s = _mlstm_sample(xm_s, og_s, state_conv[l], state_mlstm_c[l], state_mlstm_n[l],
                                         state_mlstm_m[l], w_conv[l], b_conv[l], wbd,
                                         w_mg, b_mgate[l], g_mhead[l])
    pool_c = cache_cmp_kv.reshape(-1, PAGE_SIZE * KV_CHUNKS, Dh)
    pool_s = cache_sel_kv.reshape(-1, PAGE_SIZE * KV_CHUNKS, Dh)
    win_buf = cache_win_kv.reshape(B, -1, Dh)
    mem_buf = cache_mem_kv.reshape(B, -1, MEM_DH)
    kvc_s = _compress_paged(pool_c, page_table, pe_t, wp_t, cmp_wphi[l])
    o_cs, idx = _nsa_sample_cmp(q_s, kvc_s, past)
    o_ss = _nsa_sample_sel(q_s, ks_s.reshape(B, KV_CHUNKS, Dh), pool_s, page_table, idx, past)
    nsa_os, win_new = _nsa_sample_win(q_s, win_buf, kw_s.reshape(B, KV_CHUNKS, Dh), o_cs, o_ss, gts, past)
    h1s = _out_proj(ml_os, nsa_os.reshape(B, NSA_DIM), wb_out, hs, B, Dm, "out_proj_s")
    qm_s = _norm_matmul(h1s, g_memx[l], wb_mq, n_mq, B, n_mq, "mem_q_s")
    om_s = _mem_sample(qm_s, mem_buf, "mem_attn_s")
    h2s = _matmul_res(om_s, wb_mo, h1s, B, Dm, "mem_o_s")
    y_s = _ffn_final(h2s, g_ffn[l], wb_f1, wb_f2, g_final, B, tf, "ffn_s")

    kv_s_shape = lambda a: a.reshape(1, B, 1, 2, G, Dh)
    conv_s = jnp.concatenate([state_conv[l][:, 1:], xm_s[:, None, :]], axis=1)
    out_s = (y_s.reshape(B, 1, Dm), kv_s_shape(kc_s), kv_s_shape(ks_s),
             win_new.reshape(1, B, -1, 2, G, Dh), c_s[None], n_s[None], m_s[None], conv_s[None])

    return (out_p[0], out_s[0]) + out_p[1:] + out_s[1:]
```

```python
import functools

import jax
import jax.numpy as jnp
import numpy as np
from jax import lax
from jax.experimental import pallas as pl
from jax.experimental.pallas import tpu as pltpu

f32 = jnp.float32
bf16 = jnp.bfloat16
i32 = jnp.int32

EPS = 1e-6
NEG = -1e30
FORCE = 1e30
ML_HEADS = 8
ML_DH = 128
ML_DIM = ML_HEADS * ML_DH
ML_CONV = 4
ML_QKV_BLOCK = 4
NSA_HEADS = 8
NSA_KV_HEADS = 2
NSA_GROUP = NSA_HEADS // NSA_KV_HEADS
NSA_DH = 128
NSA_DIM = NSA_HEADS * NSA_DH
NSA_KV_W = NSA_KV_HEADS * NSA_DH
CMP_BLOCK = 32
CMP_STRIDE = 16
SEL_BLOCK = 64
SEL_TOPN = 16
WINDOW = 512
PAGE_SIZE = 128
MEM_HEADS = 4
MEM_DH = 128
KV_ROW = 2 * NSA_KV_W
KV_CHUNKS = KV_ROW // NSA_DH

LANES = 128
SUBLANES = 8
VMEM_LIMIT = 56 * 1024 * 1024

MASK_BIAS = -1e9
REMOVED = -3.0e38
SEL_KT = 1024
LOG2E = 1.4426950408889634
SAFE_SPAN = 100.0


def _cparams(*sem):
    return pltpu.CompilerParams(dimension_semantics=sem, vmem_limit_bytes=VMEM_LIMIT)


def _mm(a, b):
    return jnp.dot(a, b, preferred_element_type=f32)


def _mm_nt(a, b):
    return lax.dot_general(a, b, (((1,), (1,)), ((), ())), preferred_element_type=f32)


def _split2(x):
    h = x.astype(bf16)
    return h, (x - h.astype(f32)).astype(bf16)


def _split3(x):
    h = x.astype(bf16)
    r = x - h.astype(f32)
    m = r.astype(bf16)
    return h, m, (r - m.astype(f32)).astype(bf16)


def _dot3(a, b, mm=_mm):
    ah, al = _split2(a)
    bh, bl = _split2(b)
    return mm(ah, bh) + mm(al, bh) + mm(ah, bl)


def _dot_sel_l(sel, x):
    h, m, l = _split3(x)
    return _mm(sel, h) + _mm(sel, m) + _mm(sel, l)


def _dot_sel_r(x, sel):
    h, m, l = _split3(x)
    return _mm(h, sel) + _mm(m, sel) + _mm(l, sel)


def _rms(x, g):
    return x * lax.rsqrt(jnp.mean(x * x, axis=-1, keepdims=True) + EPS) * g


def _div_pow2(x, d):
    assert d & (d - 1) == 0
    return lax.shift_right_logical(x, jnp.full(x.shape, d.bit_length() - 1, x.dtype))


def _sigmoid(x):
    return 0.5 * jnp.tanh(0.5 * x) + 0.5


def _log_sigmoid(x):
    return jnp.minimum(x, 0.0) - jnp.log(1.0 + jnp.exp(-jnp.abs(x)))


def _masked_softmax_rows(s, valid):
    s = jnp.where(valid, s, NEG)
    mx = jnp.max(s, axis=-1, keepdims=True)
    e = jnp.where(valid, jnp.exp(s - mx), 0.0)
    den = jnp.sum(e, axis=-1, keepdims=True)
    return e, jnp.where(den > 0.0, den, 1.0)


def _biased_exp2_rows(s2, bias):
    s2 = s2 + bias
    mx = jnp.max(s2, axis=-1, keepdims=True)
    return jnp.exp2(s2 - mx), mx


def _biased_softmax2_rows(s2, bias):
    e, mx = _biased_exp2_rows(s2, bias)
    inv = jnp.where(mx > 0.5 * NEG, 1.0 / jnp.sum(e, axis=-1, keepdims=True), 0.0)
    return e * inv


def _norm_matmul_kernel(x_ref, g_ref, w_ref, o_ref, xn_ref):
    @pl.when(pl.program_id(1) == 0)
    def _():
        xn_ref[...] = _rms(x_ref[...], g_ref[...]).astype(bf16)

    o_ref[...] = _mm(xn_ref[...], w_ref[...])


def _norm_matmul(x, g, w, n_cols, tm, tn, name):
    M, K = x.shape
    return pl.pallas_call(
        _norm_matmul_kernel,
        grid=(M // tm, n_cols // tn),
        in_specs=[pl.BlockSpec((tm, K), lambda i, j: (i, 0)),
                  pl.BlockSpec((1, K), lambda i, j: (0, 0)),
                  pl.BlockSpec((K, tn), lambda i, j: (0, j))],
        out_specs=pl.BlockSpec((tm, tn), lambda i, j: (i, j)),
        out_shape=jax.ShapeDtypeStruct((M, n_cols), f32),
        scratch_shapes=[pltpu.VMEM((tm, K), bf16)],
        compiler_params=_cparams("parallel", "arbitrary"),
        name=name,
    )(x, g.reshape(1, K), w)


def _in_proj_kernel(x_ref, g_ref, w_ref, o_ref, *rest, kv_block0):
    lin_refs, xn_ref = rest[:-1], rest[-1]
    j = pl.program_id(1)

    @pl.when(j == 0)
    def _():
        xn_ref[...] = _rms(x_ref[...], g_ref[...]).astype(bf16)

    val = _mm(xn_ref[...], w_ref[...])
    o_ref[...] = val
    rows = val.shape[0]
    for i, ref in enumerate(lin_refs):
        @pl.when(j == kv_block0 + i)
        def _():
            for c in range(KV_CHUNKS):
                ref[pl.ds(c, rows, stride=KV_CHUNKS), :] = val[:, c * LANES:(c + 1) * LANES]


def _in_proj(x, g, w, n_cols, kv_block0, n_kv, tm, name):
    M, K = x.shape
    tn = KV_ROW
    lin_spec = pl.BlockSpec((tm * KV_CHUNKS, LANES), lambda i, j: (i, 0))
    outs = pl.pallas_call(
        functools.partial(_in_proj_kernel, kv_block0=kv_block0),
        grid=(M // tm, n_cols // tn),
        in_specs=[pl.BlockSpec((tm, K), lambda i, j: (i, 0)),
                  pl.BlockSpec((1, K), lambda i, j: (0, 0)),
                  pl.BlockSpec((K, tn), lambda i, j: (0, j))],
        out_specs=[pl.BlockSpec((tm, tn), lambda i, j: (i, j))] + [lin_spec] * n_kv,
        out_shape=[jax.ShapeDtypeStruct((M, n_cols), f32)]
        + [jax.ShapeDtypeStruct((M * KV_CHUNKS, LANES), f32)] * n_kv,
        scratch_shapes=[pltpu.VMEM((tm, K), bf16)],
        compiler_params=_cparams("parallel", "arbitrary"),
        name=name,
    )(x, g.reshape(1, K), w)
    return outs[0], outs[1:]


def _out_proj_kernel(a1_ref, a2_ref, w1_ref, w2_ref, r_ref, o_ref):
    o_ref[...] = (r_ref[...] + _mm(a1_ref[...].astype(bf16), w1_ref[...])
                  + _mm(a2_ref[...].astype(bf16), w2_ref[...]))


def _out_proj(a1, a2, w, res, tm, tn, name):
    M, K1 = a1.shape
    K2 = a2.shape[1]
    assert K1 == K2 and w.shape[0] == K1 + K2
    N = w.shape[1]
    return pl.pallas_call(
        _out_proj_kernel,
        grid=(M // tm, N // tn),
        in_specs=[pl.BlockSpec((tm, K1), lambda i, j: (i, 0)),
                  pl.BlockSpec((tm, K2), lambda i, j: (i, 0)),
                  pl.BlockSpec((K1, tn), lambda i, j: (0, j)),
                  pl.BlockSpec((K2, tn), lambda i, j: (1, j)),
                  pl.BlockSpec((tm, tn), lambda i, j: (i, j))],
        out_specs=pl.BlockSpec((tm, tn), lambda i, j: (i, j)),
        out_shape=jax.ShapeDtypeStruct((M, N), f32),
        compiler_params=_cparams("parallel", "arbitrary"),
        name=name,
    )(a1, a2, w, w, res)


def _matmul_res_kernel(a_ref, w_ref, r_ref, o_ref):
    o_ref[...] = r_ref[...] + _mm(a_ref[...].astype(bf16), w_ref[...])


def _matmul_res(a, w, res, tm, tn, name):
    M, K = a.shape
    N = w.shape[1]
    return pl.pallas_call(
        _matmul_res_kernel,
        grid=(M // tm, N // tn),
        in_specs=[pl.BlockSpec((tm, K), lambda i, j: (i, 0)),
                  pl.BlockSpec((K, tn), lambda i, j: (0, j)),
                  pl.BlockSpec((tm, tn), lambda i, j: (i, j))],
        out_specs=pl.BlockSpec((tm, tn), lambda i, j: (i, j)),
        out_shape=jax.ShapeDtypeStruct((M, N), f32),
        compiler_params=_cparams("parallel", "arbitrary"),
        name=name,
    )(a, w, res)


def _ffn_kernel(h_ref, g_ref, w1_ref, w2_ref, gf_ref, y_ref, xn_ref, acc_ref):
    f = pl.program_id(1)

    @pl.when(f == 0)
    def _():
        xn_ref[...] = _rms(h_ref[...], g_ref[...]).astype(bf16)
        acc_ref[...] = jnp.zeros_like(acc_ref)

    a = _mm(xn_ref[...], w1_ref[...])
    a = jnp.square(jnp.maximum(a, 0.0))
    acc_ref[...] += _mm(a.astype(bf16), w2_ref[...])

    @pl.when(f == pl.num_programs(1) - 1)
    def _():
        y_ref[...] = _rms(h_ref[...] + acc_ref[...], gf_ref[...])


def _ffn_final(h, g, w1, w2, g_final, tm, tf, name):
    M, D = h.shape
    F = w1.shape[1]
    return pl.pallas_call(
        _ffn_kernel,
        grid=(M // tm, F // tf),
        in_specs=[pl.BlockSpec((tm, D), lambda i, j: (i, 0)),
                  pl.BlockSpec((1, D), lambda i, j: (0, 0)),
                  pl.BlockSpec((D, tf), lambda i, j: (0, j)),
                  pl.BlockSpec((tf, D), lambda i, j: (j, 0)),
                  pl.BlockSpec((1, D), lambda i, j: (0, 0))],
        out_specs=pl.BlockSpec((tm, D), lambda i, j: (i, 0)),
        out_shape=jax.ShapeDtypeStruct((M, D), f32),
        scratch_shapes=[pltpu.VMEM((tm, D), bf16), pltpu.VMEM((tm, D), f32)],
        compiler_params=_cparams("parallel", "arbitrary"),
        name=name,
    )(h, g.reshape(1, D), w1, w2, g_final.reshape(1, D))


def _mem_prompt_kernel(h_ref, g_ref, wq_ref, k_ref, v_ref, wo_ref, o_ref):
    h = h_ref[...]
    xn = _rms(h, g_ref[...]).astype(bf16)
    q = _mm(xn, wq_ref[...]) * (MEM_DH ** -0.5)
    outs = []
    for hd in range(MEM_HEADS):
        sl = slice(hd * MEM_DH, (hd + 1) * MEM_DH)
        s = _mm_nt(q[:, sl].astype(bf16), k_ref[:, sl].astype(bf16))
        e = jnp.exp(s - jnp.max(s, axis=-1, keepdims=True))
        p = e * (1.0 / jnp.sum(e, axis=-1, keepdims=True))
        outs.append(_mm(p.astype(bf16), v_ref[:, sl].astype(bf16)))
    o = jnp.concatenate(outs, axis=-1)
    o_ref[...] = h + _mm(o.astype(bf16), wo_ref[...])


def _mem_prompt(h, g, wq, mem_kv, wo, tm, name):
    M, D = h.shape
    HD = MEM_HEADS * MEM_DH
    ML = mem_kv.shape[0]
    return pl.pallas_call(
        _mem_prompt_kernel,
        grid=(M // tm,),
        in_specs=[pl.BlockSpec((tm, D), lambda i: (i, 0)),
                  pl.BlockSpec((1, D), lambda i: (0, 0)),
                  pl.BlockSpec((D, HD), lambda i: (0, 0)),
                  pl.BlockSpec((ML, HD), lambda i: (0, 0)),
                  pl.BlockSpec((ML, HD), lambda i: (0, 1)),
                  pl.BlockSpec((HD, D), lambda i: (0, 0))],
        out_specs=pl.BlockSpec((tm, D), lambda i: (i, 0)),
        out_shape=jax.ShapeDtypeStruct((M, D), f32),
        compiler_params=_cparams("parallel"),
        name=name,
    )(h, g.reshape(1, D), wq, mem_kv, mem_kv, wo)


def _mem_sample_kernel(q_ref, kv_ref, o_ref):
    per_tok = 2 * MEM_HEADS
    ML = kv_ref.shape[0] // per_tok
    q = q_ref[...] * (MEM_DH ** -0.5)
    outs = []
    for hd in range(MEM_HEADS):
        sl = slice(hd * MEM_DH, (hd + 1) * MEM_DH)
        k = kv_ref[pl.ds(hd, ML, stride=per_tok), :]
        v = kv_ref[pl.ds(MEM_HEADS + hd, ML, stride=per_tok), :]
        s = jnp.sum(k * q[:, sl], axis=-1, keepdims=True)
        e = jnp.exp(s - jnp.max(s, axis=0, keepdims=True))
        p = e / jnp.sum(e, axis=0, keepdims=True)
        outs.append(jnp.sum(p * v, axis=0, keepdims=True))
    o_ref[...] = jnp.concatenate(outs, axis=-1)


def _mem_sample(q, kv, name):
    B, HD = q.shape
    rows = kv.shape[1]
    out = pl.pallas_call(
        _mem_sample_kernel,
        grid=(B,),
        in_specs=[pl.BlockSpec((None, 1, HD), lambda b: (b, 0, 0)),
                  pl.BlockSpec((None, rows, MEM_DH), lambda b: (b, 0, 0))],
        out_specs=pl.BlockSpec((None, 1, HD), lambda b: (b, 0, 0)),
        out_shape=jax.ShapeDtypeStruct((B, 1, HD), f32),
        compiler_params=_cparams("parallel"),
        name=name,
    )(q.reshape(B, 1, HD), kv)
    return out.reshape(B, HD)


def _blockdiag_weights(w):
    n, b = w.shape[0], ML_QKV_BLOCK
    per = LANES // b
    wc = w.reshape(n, w.shape[1] // per, per, b * b)
    r, k = np.arange(LANES), np.arange(b * b)
    expand = (r[:, None] // b == np.arange(per)[None, :]).astype(np.float32)
    row_sel = (r[:, None] % b == k[None, :] // b).astype(np.float32)
    col_sel = (k[:, None] % b == r[None, :] % b).astype(np.float32)
    same_blk = (r[:, None] // b == r[None, :] // b).astype(np.float32)
    rows = jnp.einsum('rb,scbk->scrk', expand, wc) * row_sel
    return jnp.einsum('scrk,kq->scrq', rows, col_sel) * same_blk


def _ml_qkv_gates(xc, xm, wbd_ref, wg_ref, bg_ref):
    xc_b, xm_b = xc.astype(bf16), xm.astype(bf16)
    qs, ks, vs = [], [], []
    for c in range(ML_DIM // LANES):
        sl = slice(c * LANES, (c + 1) * LANES)
        qs.append(_mm(xc_b[:, sl], wbd_ref[0, c]))
        ks.append(_mm(xc_b[:, sl], wbd_ref[1, c]) * (ML_DH ** -0.5))
        vs.append(_mm(xm_b[:, sl], wbd_ref[2, c]))
    q = jnp.concatenate(qs, axis=-1)
    k = jnp.concatenate(ks, axis=-1)
    v = jnp.concatenate(vs, axis=-1)
    qkv_b = jnp.concatenate([q, k, v], axis=-1).astype(bf16)
    gates = _mm(qkv_b, wg_ref[...]) + bg_ref[...]
    return q, k, v, gates, qkv_b


def _mlstm_prompt_kernel(xm_ref, og_ref, wconv_ref, bconv_ref, wbd_ref, wg_ref, wgt_ref,
                         bg_ref, bgt_ref, gh_ref, o_ref, c_ref, n_ref, m_ref, prev_sc, *, L):
    H, D = ML_HEADS, ML_DH

    @pl.when(pl.program_id(0) == 0)
    def _():
        prev_sc[...] = jnp.zeros_like(prev_sc)
        c_ref[...] = jnp.zeros_like(c_ref)
        n_ref[...] = jnp.zeros_like(n_ref)
        m_ref[...] = jnp.full(m_ref.shape, NEG, f32)

    x = xm_ref[...]
    full = jnp.concatenate([prev_sc[...], x], axis=0)
    y = bconv_ref[...]
    for j in range(ML_CONV):
        off = SUBLANES - (ML_CONV - 1) + j
        y = y + full[off:off + L] * wconv_ref[j:j + 1, :]
    prev_sc[...] = x[L - SUBLANES:L]
    xc = y * _sigmoid(y)

    q, k, v, gates, qkv_b = _ml_qkv_gates(xc, x, wbd_ref, wg_ref, bg_ref)
    gates_t = _mm_nt(wgt_ref[...], qkv_b) + bgt_ref[...]
    ig_c = gates[:, 0:H]
    lf_c = _log_sigmoid(gates[:, H:2 * H])
    ig_r = gates_t[0:H, :]
    lf_r = _log_sigmoid(gates_t[H:2 * H, :])

    t_i = lax.broadcasted_iota(i32, (L, L), 0)
    s_i = lax.broadcasted_iota(i32, (L, L), 1)
    causal = s_i <= t_i
    tri = jnp.where(causal, 1.0, 0.0).astype(bf16)
    b_c = _dot_sel_l(tri, lf_c)
    tri_u = jnp.where(t_i <= s_i, 1.0, 0.0).astype(bf16)
    b_r = _dot_sel_r(lf_r, tri_u)

    for h in range(H):
        sl = slice(h * D, (h + 1) * D)
        qh, kh, vh = q[:, sl], k[:, sl], v[:, sl]
        bc = b_c[:, h:h + 1]
        m_prev = m_ref[h:h + 1, 0:1]
        d_in = jnp.where(causal, bc - b_r[h:h + 1, :] + ig_r[h:h + 1, :], NEG)
        d_x = bc + m_prev
        m_t = jnp.maximum(d_x, jnp.max(d_in, axis=-1, keepdims=True))
        w_in = jnp.exp(d_in - m_t)
        w_x = jnp.exp(d_x - m_t)
        qb = qh.astype(bf16)
        kb = kh.astype(bf16)
        vb = vh.astype(bf16)
        s = _mm_nt(qb, kb) * w_in
        c_old = c_ref[h]
        n_old = n_ref[h:h + 1, :]
        num = _mm(s.astype(bf16), vb) + w_x * _mm(qb, c_old.astype(bf16))
        den = jnp.sum(s, axis=-1, keepdims=True) + w_x * jnp.sum(qh * n_old, axis=-1, keepdims=True)
        hh = num * (1.0 / jnp.maximum(jnp.abs(den), jnp.exp(-m_t)))
        m_new = m_t[L - 1:L, :]
        b_last = bc[L - 1:L, :]
        g_x = jnp.exp(b_last + m_prev - m_new)
        g_s = jnp.exp(b_last - bc + ig_c[:, h:h + 1] - m_new)
        ks_ = kh * g_s
        c_ref[h] = g_x * c_old + _mm(ks_.T.astype(bf16), vb)
        n_ref[h:h + 1, :] = g_x * n_old + jnp.sum(ks_, axis=0, keepdims=True)
        m_ref[h:h + 1, :] = jnp.broadcast_to(m_new, (1, LANES))
        hn = hh * lax.rsqrt(jnp.mean(hh * hh, axis=-1, keepdims=True) + EPS) * gh_ref[:, sl]
        o_ref[:, sl] = _sigmoid(og_ref[:, sl]) * hn


def _mlstm_prompt(u, w_conv, b_conv, wbd, w_gate, w_gate_t, b_gate, g_head, L, name):
    T = u.shape[0]
    H, D = ML_HEADS, ML_DH
    nch = ML_DIM // LANES
    full2 = lambda shape: pl.BlockSpec(shape, lambda i: (0,) * len(shape))
    out, c, n, m = pl.pallas_call(
        functools.partial(_mlstm_prompt_kernel, L=L),
        grid=(T // L,),
        in_specs=[pl.BlockSpec((L, ML_DIM), lambda i: (i, 0)),
                  pl.BlockSpec((L, ML_DIM), lambda i: (i, 1)),
                  full2((ML_CONV, ML_DIM)), full2((1, ML_DIM)),
                  full2((3, nch, LANES, LANES)),
                  full2((3 * ML_DIM, 2 * H)), full2((2 * H, 3 * ML_DIM)),
                  full2((1, 2 * H)), full2((2 * H, 1)), full2((1, ML_DIM))],
        out_specs=[pl.BlockSpec((L, ML_DIM), lambda i: (i, 0)),
                   full2((H, D, D)), full2((H, D)), full2((H, LANES))],
        out_shape=[jax.ShapeDtypeStruct((T, ML_DIM), f32),
                   jax.ShapeDtypeStruct((H, D, D), f32),
                   jax.ShapeDtypeStruct((H, D), f32),
                   jax.ShapeDtypeStruct((H, LANES), f32)],
        scratch_shapes=[pltpu.VMEM((SUBLANES, ML_DIM), f32)],
        compiler_params=_cparams("arbitrary"),
        name=name,
    )(u, u, w_conv, b_conv.reshape(1, ML_DIM), wbd, w_gate, w_gate_t,
      b_gate.reshape(1, 2 * H), b_gate.reshape(2 * H, 1), g_head.reshape(1, ML_DIM))
    return out, c, n, m[:, 0]


def _mlstm_sample_pre_kernel(xm_ref, s0_ref, s1_ref, s2_ref, wconv_ref, bconv_ref, wbd_ref,
                             wg_ref, bg_ref, q_ref, k_ref, v_ref, g_ref):
    x = xm_ref[...]
    y = (bconv_ref[...] + s0_ref[...] * wconv_ref[0:1, :] + s1_ref[...] * wconv_ref[1:2, :]
         + s2_ref[...] * wconv_ref[2:3, :] + x * wconv_ref[3:4, :])
    xc = y * _sigmoid(y)
    q, k, v, gates, _ = _ml_qkv_gates(xc, x, wbd_ref, wg_ref, bg_ref)
    q_ref[...] = q
    k_ref[...] = k
    v_ref[...] = v
    g_ref[...] = gates


def _mlstm_sample_step_kernel(qc_ref, kc_ref, q_ref, k_ref, v_ref, gt_ref, og_ref, gh_ref, c_ref, n_ref, m_ref,
                              o_ref, cn_ref, nn_ref, mn_ref):
    H, D = ML_HEADS, ML_DH
    ig = gt_ref[0:H, :]
    lf = _log_sigmoid(gt_ref[H:2 * H, :])
    m_old = m_ref[...]
    m_new = jnp.maximum(lf + m_old, ig)
    w_in = jnp.exp(ig - m_new)
    w_x = jnp.exp(lf + m_old - m_new)
    mn_ref[...] = m_new
    q, k, v, n_old = q_ref[...], k_ref[...], v_ref[...], n_ref[...]
    cq = jnp.concatenate([jnp.sum(c_ref[h] * qc_ref[:, h:h + 1], axis=0, keepdims=True) for h in range(H)], axis=0)
    s = jnp.sum(q * k, axis=-1, keepdims=True) * w_in
    num = s * v + w_x * cq
    den = s + w_x * jnp.sum(n_old * q, axis=-1, keepdims=True)
    hh = num / jnp.maximum(jnp.abs(den), jnp.exp(-m_new))
    hn = hh * lax.rsqrt(jnp.mean(hh * hh, axis=-1, keepdims=True) + EPS) * gh_ref[...]
    o_ref[...] = _sigmoid(og_ref[...]) * hn
    nn_ref[...] = w_x * n_old + w_in * k
    for h in range(H):
        cn_ref[h] = w_x[h:h + 1, :] * c_ref[h] + w_in[h:h + 1, :] * (kc_ref[:, h:h + 1] * v[h:h + 1, :])


def _mlstm_sample(xm, og, conv_state, c0, n0, m0, w_conv, b_conv, wbd, w_gate, b_gate, g_head):
    B = xm.shape[0]
    H, D = ML_HEADS, ML_DH
    sds = lambda *s: jax.ShapeDtypeStruct(s, f32)
    q, k, v, gates = pl.pallas_call(
        _mlstm_sample_pre_kernel,
        out_shape=[sds(B, ML_DIM), sds(B, ML_DIM), sds(B, ML_DIM), sds(B, 2 * H)],
        compiler_params=pltpu.CompilerParams(vmem_limit_bytes=VMEM_LIMIT),
        name="mlstm_sample_pre",
    )(xm, conv_state[:, 0], conv_state[:, 1], conv_state[:, 2], w_conv, b_conv.reshape(1, ML_DIM),
      wbd, w_gate, b_gate.reshape(1, 2 * H))
    q3, k3, v3 = (a.reshape(B, H, D) for a in (q, k, v))
    per_b = lambda *s: pl.BlockSpec((None,) + s, lambda b: (b,) + (0,) * len(s))
    out, c, n, m = pl.pallas_call(
        _mlstm_sample_step_kernel,
        grid=(B,),
        in_specs=[per_b(D, H), per_b(D, H), per_b(H, D), per_b(H, D), per_b(H, D), per_b(2 * H, 1),
                  per_b(H, D), pl.BlockSpec((H, D), lambda b: (0, 0)),
                  per_b(H, D, D), per_b(H, D), per_b(H, 1)],
        out_specs=[per_b(H, D), per_b(H, D, D), per_b(H, D), per_b(H, 1)],
        out_shape=[sds(B, H, D), sds(B, H, D, D), sds(B, H, D), sds(B, H, 1)],
        compiler_params=_cparams("parallel"),
        name="mlstm_sample_step",
    )(q3.transpose(0, 2, 1), k3.transpose(0, 2, 1), q3, k3, v3, gates.reshape(B, 2 * H, 1),
      og.reshape(B, H, D), g_head.reshape(H, D), c0, n0, m0.reshape(B, H, 1))
    return out.reshape(B, ML_DIM), c, n, m.reshape(B, H)


def _compress_kernel(x_ref, pe_ref, wp_ref, wphi_ref, o_ref, f0_sc, f1_sc, mn_sc):
    step = pl.program_id(0)
    sub = x_ref.shape[0] // CMP_STRIDE
    x3 = x_ref[...].reshape(sub, CMP_STRIDE, KV_ROW)
    base = pl.multiple_of(step * sub, sub)
    for o, sc in ((0, f0_sc), (1, f1_sc)):
        y = x3 + pe_ref[o][None]
        sc[pl.ds(base, sub), :] = jnp.sum(y * _sigmoid(y) * wp_ref[o][None], axis=1)
    mn_sc[pl.ds(base, sub), :] = jnp.sum(x3, axis=1) * (1.0 / CMP_STRIDE)

    @pl.when(step == pl.num_programs(0) - 1)
    def _():
        ns = f0_sc.shape[0]
        feat = f0_sc[...] + pltpu.roll(f1_sc[...], ns - 1, axis=0)
        mn = mn_sc[...]
        pooled = (mn + pltpu.roll(mn, ns - 1, axis=0)) * (CMP_STRIDE / CMP_BLOCK)
        for c in range(2):
            for g in range(NSA_KV_HEADS):
                sl = slice((c * NSA_KV_HEADS + g) * NSA_DH, (c * NSA_KV_HEADS + g + 1) * NSA_DH)
                o_ref[:, sl] = pooled[:, sl] + _dot3(feat[:, sl], wphi_ref[c])


def _compress_tables(pe, wpos):
    def lay(a):
        r = CMP_BLOCK // CMP_STRIDE
        a = a.reshape(2, r, CMP_STRIDE, NSA_DH).transpose(1, 2, 0, 3)
        a = jnp.broadcast_to(a[:, :, :, None, :], (r, CMP_STRIDE, 2, NSA_KV_HEADS, NSA_DH))
        return a.reshape(r, CMP_STRIDE, KV_ROW)
    return lay(pe), lay(wpos)


def _compress_prompt(u, col_block, pe_t, wp_t, wphi, rows=512):
    T = u.shape[0]
    n_sub = T // CMP_STRIDE
    const = lambda shape: pl.BlockSpec(shape, lambda s: (0,) * len(shape))
    return pl.pallas_call(
        _compress_kernel,
        grid=(T // rows,),
        in_specs=[pl.BlockSpec((rows, KV_ROW), lambda s: (s, col_block)),
                  const(pe_t.shape), const(wp_t.shape), const(wphi.shape)],
        out_specs=const((n_sub, KV_ROW)),
        out_shape=jax.ShapeDtypeStruct((n_sub, KV_ROW), f32),
        scratch_shapes=[pltpu.VMEM((n_sub, KV_ROW), f32)] * 3,
        compiler_params=_cparams("arbitrary"),
        name="compress_prompt",
    )(u, pe_t, wp_t, wphi)


def _compress_paged_kernel(pt_ref, *refs, n_pages):
    pages = refs[:n_pages]
    pe_ref, wp_ref, wphi_ref, o_ref, f0_sc, f1_sc, mn_sc = refs[n_pages:]
    step = pl.program_id(1)
    sub = PAGE_SIZE // CMP_STRIDE
    tiles = CMP_STRIDE * KV_CHUNKS // SUBLANES
    packed = 2 * SUBLANES
    ptiles = CMP_STRIDE * KV_CHUNKS // packed
    out_rows = sub * SUBLANES
    for p in range(n_pages):
        x = pages[p][...]
        base = pl.multiple_of((step * n_pages + p) * out_rows, out_rows)
        xh = (x.astype(bf16) * 0.5).reshape(sub, ptiles, packed, LANES)
        for o, sc in ((0, f0_sc), (1, f1_sc)):
            y = xh + pe_ref[o][None]
            b = y * wp_ref[o][None]
            z = b + b * jnp.tanh(y)
            z = (z[:, 0] + z[:, 1]) + (z[:, 2] + z[:, 3])
            zf = z.astype(f32).reshape(sub, packed // SUBLANES, SUBLANES, LANES)
            sc[pl.ds(base, out_rows), :] = jnp.sum(zf, axis=1).reshape(out_rows, LANES)
        mean = jnp.sum(x.reshape(sub, tiles, SUBLANES, LANES), axis=1) * (1.0 / CMP_STRIDE)
        mn_sc[pl.ds(base, out_rows), :] = mean.reshape(out_rows, LANES)

    @pl.when(step == pl.num_programs(1) - 1)
    def _():
        ns = f0_sc.shape[0] // SUBLANES
        for c in range(KV_CHUNKS):
            col = lambda sc: (sc[pl.ds(c, ns, stride=SUBLANES), :]
                              + sc[pl.ds(KV_CHUNKS + c, ns, stride=SUBLANES), :])
            feat = col(f0_sc) + pltpu.roll(col(f1_sc), ns - 1, axis=0)
            mn = col(mn_sc)
            pooled = (mn + pltpu.roll(mn, ns - 1, axis=0)) * (CMP_STRIDE / CMP_BLOCK)
            o_ref[:, c * LANES:(c + 1) * LANES] = pooled + _dot3(feat, wphi_ref[c // NSA_KV_HEADS])


def _compress_paged(pool, page_table, pe_t, wp_t, wphi, pages_per_step=64):
    B, n_pages = page_table.shape
    P = pages_per_step
    n_sub = n_pages * PAGE_SIZE // CMP_STRIDE
    r = CMP_BLOCK // CMP_STRIDE
    packed = 2 * SUBLANES
    ptiles = CMP_STRIDE * KV_CHUNKS // packed
    pe4 = (0.5 * pe_t).astype(bf16).reshape(r, ptiles, packed, LANES)
    wp4 = wp_t.astype(bf16).reshape(r, ptiles, packed, LANES)
    page_rows = PAGE_SIZE * KV_CHUNKS
    specs = [pl.BlockSpec((None, page_rows, LANES),
                          functools.partial(lambda b, s, pt, p: (pt[b * n_pages + s * P + p], 0, 0), p=p))
             for p in range(P)]
    const = lambda shape: pl.BlockSpec(shape, lambda *a: (0,) * len(shape))
    gs = pltpu.PrefetchScalarGridSpec(
        num_scalar_prefetch=1, grid=(B, n_pages // P),
        in_specs=specs + [const(pe4.shape), const(wp4.shape), const(wphi.shape)],
        out_specs=pl.BlockSpec((None, n_sub, KV_ROW), lambda b, s, pt: (b, 0, 0)),
        scratch_shapes=[pltpu.VMEM((n_sub * SUBLANES, LANES), f32)] * 3)
    return pl.pallas_call(
        functools.partial(_compress_paged_kernel, n_pages=P),
        grid_spec=gs,
        out_shape=jax.ShapeDtypeStruct((B, n_sub, KV_ROW), f32),
        compiler_params=_cparams("parallel", "arbitrary"),
        name="compress_paged",
    )(page_table.reshape(-1), *([pool] * P), pe4, wp4, wphi)


def _kv_prep_kernel(ks_ref, kw_ref, ksk_ref, ksv_ref, kwk_ref, kwv_ref, kn2_ref):
    rows = ks_ref.shape[0]
    ks = ks_ref[...]
    kw = kw_ref[...]
    r = pl.program_id(0) * rows + lax.broadcasted_iota(i32, (rows, LANES), 0)
    n = lax.broadcasted_iota(i32, (rows, LANES), 1)
    onehot = jnp.where(_div_pow2(r, SEL_BLOCK) == n, 1.0, 0.0).astype(bf16)
    ones_col = jnp.where(n == 0, 1.0, 0.0).astype(bf16)

    @pl.when(pl.program_id(0) == 0)
    def _():
        kn2_ref[...] = jnp.zeros_like(kn2_ref)

    for g in range(NSA_KV_HEADS):
        ksl = slice(g * NSA_DH, (g + 1) * NSA_DH)
        vsl = slice(NSA_KV_W + g * NSA_DH, NSA_KV_W + (g + 1) * NSA_DH)
        kb = ks[:, ksl].astype(bf16)
        ksk_ref[g, :, 0:NSA_DH] = kb
        ksk_ref[g, :, NSA_DH:NSA_DH + LANES] = onehot
        ksv_ref[g, :, 0:NSA_DH] = ks[:, vsl].astype(bf16)
        ksv_ref[g, :, NSA_DH:NSA_DH + LANES] = ones_col
        kwk_ref[g] = kw[:, ksl].astype(bf16)
        kwv_ref[g, :, 0:NSA_DH] = kw[:, vsl].astype(bf16)
        kwv_ref[g, :, NSA_DH:NSA_DH + LANES] = ones_col
        kf = kb.astype(f32)
        n2 = jnp.max(jnp.sum(kf * kf, axis=-1, keepdims=True), axis=0, keepdims=True)
        kn2_ref[g] = jnp.maximum(kn2_ref[g], jnp.broadcast_to(n2, kn2_ref.shape[1:]))


def _kv_prep(u, ks_col_block, kw_col_block, rows=512):
    T = u.shape[0]
    G = NSA_KV_HEADS
    assert (T - 1) // SEL_BLOCK + 1 <= LANES
    sd = lambda w: jax.ShapeDtypeStruct((G, T, w), bf16)
    ospec = lambda w: pl.BlockSpec((G, rows, w), lambda i: (0, i, 0))
    return pl.pallas_call(
        _kv_prep_kernel,
        grid=(T // rows,),
        in_specs=[pl.BlockSpec((rows, KV_ROW), lambda i: (i, ks_col_block)),
                  pl.BlockSpec((rows, KV_ROW), lambda i: (i, kw_col_block))],
        out_specs=[ospec(NSA_DH + LANES), ospec(NSA_DH + LANES), ospec(NSA_DH), ospec(NSA_DH + LANES),
                   pl.BlockSpec((G, SUBLANES, LANES), lambda i: (0, 0, 0))],
        out_shape=[sd(NSA_DH + LANES), sd(NSA_DH + LANES), sd(NSA_DH), sd(NSA_DH + LANES),
                   jax.ShapeDtypeStruct((G, SUBLANES, LANES), f32)],
        compiler_params=_cparams("arbitrary"),
        name="kv_prep",
    )(u, u)


def _top_blocks(score, n_top, axis):
    lane = lax.broadcasted_iota(i32, score.shape, axis).astype(f32)
    width = float(score.shape[axis])
    work = score
    firsts = []
    for _ in range(n_top):
        mx = jnp.max(work, axis=axis, keepdims=True)
        first = jnp.min(jnp.where(work == mx, lane, width), axis=axis, keepdims=True)
        work = jnp.where(lane == first, REMOVED, work)
        firsts.append(first)
    return work, firsts


def _nsa_prompt_kernel(x_ref, gmix_ref, wgt_ref, q_ref, kck_ref, kcv_ref, ksk_ref, ksv_ref, kwk_ref, kwv_ref, kn2_ref,
                       o_ref, shift_sc, m_sc, acc_sc, *, n_sel):
    R, D = NSA_GROUP, NSA_DH
    QB = q_ref.shape[0]
    rows = R * QB
    qb = pl.program_id(1)
    q = q_ref[...]
    q2_b = (jnp.concatenate([q[:, r * D:(r + 1) * D] for r in range(R)], axis=0) * (D ** -0.5 * LOG2E)).astype(bf16)
    pos = qb * QB + lax.broadcasted_iota(i32, (QB, 1), 0)

    def head_bias(valid):
        return jnp.concatenate([jnp.where(valid, 0.0, NEG)] * R, axis=0)

    ns = kck_ref.shape[0]
    s = _mm_nt(q2_b, kck_ref[...].astype(bf16))
    j = lax.broadcasted_iota(i32, (1, ns), 1)
    p_c = _biased_softmax2_rows(s, head_bias(j * CMP_STRIDE + (CMP_BLOCK - 1) <= pos))
    o_c = _mm(p_c.astype(bf16), kcv_ref[...].astype(bf16))

    imp = p_c[0:QB]
    for r in range(1, R):
        imp = imp + p_c[r * QB:(r + 1) * QB]
    ratio = SEL_BLOCK // CMP_STRIDE
    off = CMP_BLOCK // CMP_STRIDE - 1
    nn = lax.broadcasted_iota(i32, (LANES, ns), 0)
    jj = lax.broadcasted_iota(i32, (LANES, ns), 1)
    overlap_t = jnp.where((jj >= ratio * nn - off) & (jj < ratio * nn + ratio), 1.0, 0.0).astype(bf16)
    imp_sel = sum(_mm_nt(overlap_t, part) for part in _split3(imp))
    n_idx = lax.broadcasted_iota(i32, (LANES, QB), 0)
    cur = _div_pow2(qb * QB + lax.broadcasted_iota(i32, (LANES, QB), 1), SEL_BLOCK)
    forced = (n_idx == 0) | (n_idx == cur) | (n_idx == cur - 1)
    score = jnp.where(forced, FORCE, jnp.where(n_idx <= cur, imp_sel, NEG))
    score = jnp.where(n_idx < n_sel, score, REMOVED)
    taken, _ = _top_blocks(score, min(SEL_TOPN, n_sel), axis=0)
    bias = jnp.where((taken < 0.5 * REMOVED) & (n_idx <= cur), 0.0, MASK_BIAS).T

    KT = SEL_KT
    last = (qb * QB + QB - 1) // KT
    q2 = q2_b.astype(f32)
    bound = jnp.sqrt(jnp.sum(q2 * q2, axis=-1, keepdims=True) * kn2_ref[0:1, 0:1]) * 1.01 + 1e-3
    k_self = ksk_ref[pl.ds(pl.multiple_of(qb * QB, QB), QB), 0:D].astype(f32)
    s_self = jnp.concatenate([jnp.sum(q2[r * QB:(r + 1) * QB] * k_self, axis=-1, keepdims=True) for r in range(R)],
                             axis=0)
    shift_sc[...] = bound
    diag_bias = head_bias(last * KT + lax.broadcasted_iota(i32, (1, KT), 1) <= pos)

    def shifted_queries(shift):
        return jnp.concatenate(
            [jnp.concatenate([q2_b[r * QB:(r + 1) * QB], (bias - shift[r * QB:(r + 1) * QB]).astype(bf16)], axis=1)
             for r in range(R)], axis=0)

    def key_tile(ref, kt):
        return ref[pl.ds(pl.multiple_of(kt * KT, KT), KT), :]

    @pl.when(jnp.max(bound - s_self) > SAFE_SPAN)
    def _():
        qp0 = shifted_queries(jnp.zeros_like(bound))

        def lane_max(kt, sk):
            mx = m_sc[...]
            for c in range(KT // LANES):
                mx = jnp.maximum(mx, sk[:, c * LANES:(c + 1) * LANES])
            m_sc[...] = mx

        def max_pass(kt, carry):
            lane_max(kt, _mm_nt(qp0, key_tile(ksk_ref, kt)))
            return carry

        m_sc[...] = jnp.full(m_sc.shape, NEG, f32)
        lax.fori_loop(0, last, max_pass, 0)
        lane_max(last, _mm_nt(qp0, key_tile(ksk_ref, last)) + diag_bias)
        shift_sc[...] = jnp.max(m_sc[...], axis=-1, keepdims=True)

    qp = shifted_queries(shift_sc[...])
    acc_sc[...] = jnp.zeros_like(acc_sc)

    def accumulate(kt, s2):
        acc_sc[...] += _mm(jnp.exp2(s2).astype(bf16), key_tile(ksv_ref, kt))

    def body(kt, carry):
        accumulate(kt, _mm_nt(qp, key_tile(ksk_ref, kt)))
        return carry

    lax.fori_loop(0, last, body, 0)
    accumulate(last, _mm_nt(qp, key_tile(ksk_ref, last)) + diag_bias)
    acc = acc_sc[...]
    o_s = acc[:, 0:D] * (1.0 / acc[:, D:D + 1])

    wlen = WINDOW + QB
    wstart = pl.multiple_of(jnp.maximum(qb * QB - WINDOW, 0), int(np.gcd(QB, WINDOW)))
    sw = _mm_nt(q2_b, kwk_ref[pl.ds(wstart, wlen), :])
    diff = pos - (wstart + lax.broadcasted_iota(i32, (1, wlen), 1))
    e_w, _ = _biased_exp2_rows(sw, head_bias((diff >= 0) & (diff <= WINDOW)))
    acc_w = _mm(e_w.astype(bf16), kwv_ref[pl.ds(wstart, wlen), :])
    o_w = acc_w[:, 0:D] * (1.0 / acc_w[:, D:D + 1])

    gate = _sigmoid(_mm(_rms(x_ref[...], gmix_ref[...]).astype(bf16), wgt_ref[...]))
    for r in range(R):
        rs = slice(r * QB, (r + 1) * QB)
        o_ref[:, r * D:(r + 1) * D] = (gate[:, r:r + 1] * o_c[rs] + gate[:, R + r:R + r + 1] * o_s[rs]
                                       + gate[:, 2 * R + r:2 * R + r + 1] * o_w[rs])


def _nsa_prompt(x, g_mix, w_gate_g, u, q_col_block, kvc, ksk, ksv, kwk, kwv, kn2, tq):
    T, Dm = x.shape
    G, R, D, QB = NSA_KV_HEADS, NSA_GROUP, NSA_DH, tq
    ns = kvc.shape[0]
    n_sel = (T - 1) // SEL_BLOCK + 1
    assert T % SEL_KT == 0 and T % QB == 0 and T >= WINDOW + QB and n_sel <= LANES
    rows = R * QB
    res = lambda w: pl.BlockSpec((None, T, w), lambda g, i: (g, 0, 0))
    return pl.pallas_call(
        functools.partial(_nsa_prompt_kernel, n_sel=n_sel),
        grid=(G, T // QB),
        in_specs=[pl.BlockSpec((QB, Dm), lambda g, i: (i, 0)),
                  pl.BlockSpec((1, Dm), lambda g, i: (0, 0)),
                  pl.BlockSpec((None, Dm, LANES), lambda g, i: (g, 0, 0)),
                  pl.BlockSpec((QB, R * D), lambda g, i: (i, q_col_block + g)),
                  pl.BlockSpec((ns, D), lambda g, i: (0, g)),
                  pl.BlockSpec((ns, D), lambda g, i: (0, G + g)),
                  res(D + LANES), res(D + LANES), res(D), res(D + LANES),
                  pl.BlockSpec((None, SUBLANES, LANES), lambda g, i: (g, 0, 0))],
        out_specs=pl.BlockSpec((QB, R * D), lambda g, i: (i, g)),
        out_shape=jax.ShapeDtypeStruct((T, NSA_DIM), f32),
        scratch_shapes=[pltpu.VMEM((rows, 1), f32), pltpu.VMEM((rows, LANES), f32),
                        pltpu.VMEM((rows, D + LANES), f32)],
        compiler_params=_cparams("parallel", "arbitrary"),
        name="nsa_prompt",
    )(x, g_mix.reshape(1, Dm), w_gate_g, u, kvc, kvc, ksk, ksv, kwk, kwv, kn2)


def _group_gate_weights(w_gate):
    Dm = w_gate.shape[0]
    G, R = NSA_KV_HEADS, NSA_GROUP
    w = w_gate.reshape(Dm, 3, G, R).transpose(2, 0, 1, 3).reshape(G, Dm, 3 * R)
    return jnp.pad(w, ((0, 0), (0, 0), (0, LANES - 3 * R)))


def _nsa_sample_cmp_kernel(q_ref, kvc_ref, oc_ref, idx_ref, *, pos, n_sel, sel_w):
    H, R, D, G = NSA_HEADS, NSA_GROUP, NSA_DH, NSA_KV_HEADS
    ns = kvc_ref.shape[0]
    qs_b = (q_ref[...] * (D ** -0.5)).astype(bf16)
    head = lax.broadcasted_iota(i32, (H, 1), 0)
    grp = _div_pow2(head, R)
    s = jnp.zeros((H, ns), f32)
    for g in range(G):
        s = jnp.where(grp == g, _mm_nt(qs_b, kvc_ref[:, g * D:(g + 1) * D].astype(bf16)), s)
    j = lax.broadcasted_iota(i32, (1, ns), 1)
    e, den = _masked_softmax_rows(s, j * CMP_STRIDE + (CMP_BLOCK - 1) <= pos)
    p_c = e / den
    p_b = p_c.astype(bf16)
    o_c = jnp.zeros((H, D), f32)
    imp = jnp.zeros((H, ns), f32)
    for g in range(G):
        o_c = jnp.where(grp == g, _mm(p_b, kvc_ref[:, (G + g) * D:(G + g + 1) * D].astype(bf16)), o_c)
        imp = jnp.where(grp == g, jnp.sum(jnp.where(grp == g, p_c, 0.0), axis=0, keepdims=True), imp)
    oc_ref[...] = o_c
    ratio = SEL_BLOCK // CMP_STRIDE
    off = CMP_BLOCK // CMP_STRIDE - 1
    nn = lax.broadcasted_iota(i32, (sel_w, ns), 0)
    jj = lax.broadcasted_iota(i32, (sel_w, ns), 1)
    overlap_t = jnp.where((jj >= ratio * nn - off) & (jj < ratio * nn + ratio), 1.0, 0.0).astype(bf16)
    imp_sel = sum(_mm_nt(overlap_t, part) for part in _split3(imp))
    n_idx = lax.broadcasted_iota(i32, (sel_w, H), 0)
    cur = pos // SEL_BLOCK
    forced = (n_idx == 0) | (n_idx == cur) | (n_idx == cur - 1)
    score = jnp.where(forced, FORCE, jnp.where(n_idx <= cur, imp_sel, NEG))
    score = jnp.where(n_idx < n_sel, score, REMOVED)
    _, firsts = _top_blocks(score, SEL_TOPN, axis=0)
    idx_ref[...] = jnp.concatenate(firsts, axis=0).astype(i32)


def _nsa_sample_cmp(q, kvc, pos):
    B, H, D = q.shape
    ns = kvc.shape[1]
    n_sel = pos // SEL_BLOCK + 1
    assert n_sel >= SEL_TOPN
    sel_w = -(-n_sel // SUBLANES) * SUBLANES
    o_c, idx = pl.pallas_call(
        functools.partial(_nsa_sample_cmp_kernel, pos=pos, n_sel=n_sel, sel_w=sel_w),
        grid=(B,),
        in_specs=[pl.BlockSpec((None, H, D), lambda b: (b, 0, 0)),
                  pl.BlockSpec((None, ns, KV_ROW), lambda b: (b, 0, 0))],
        out_specs=[pl.BlockSpec((None, H, D), lambda b: (b, 0, 0)),
                   pl.BlockSpec((None, SEL_TOPN, H), lambda b: (b, 0, 0))],
        out_shape=[jax.ShapeDtypeStruct((B, H, D), f32), jax.ShapeDtypeStruct((B, SEL_TOPN, H), i32)],
        compiler_params=_cparams("parallel"),
        name="nsa_sample_cmp",
    )(q, kvc)
    return o_c, idx[:, :, ::NSA_GROUP].transpose(0, 2, 1)


def _nsa_sample_sel_kernel(idx_ref, pt_ref, q_ref, new_ref, *refs, past):
    H, R, D, G = NSA_HEADS, NSA_GROUP, NSA_DH, NSA_KV_HEADS
    blocks, o_ref = refs[:G * SEL_TOPN], refs[G * SEL_TOPN]
    b = pl.program_id(0)
    qs_b = (q_ref[...] * (D ** -0.5)).astype(bf16)
    grp = _div_pow2(lax.broadcasted_iota(i32, (H, 1), 0), R)
    per_tile = SUBLANES // KV_CHUNKS
    half = SEL_BLOCK // per_tile
    o = jnp.zeros((H, D), f32)
    for g in range(G):
        ks, vs = [], []
        lane = lax.broadcasted_iota(i32, (1, SEL_TOPN * SEL_BLOCK), 1)
        slot = _div_pow2(lane, SEL_BLOCK)
        in_slot = lane & (SEL_BLOCK - 1)
        pk_row = (in_slot & (half - 1)) * per_tile + _div_pow2(in_slot, half)
        for i in range(SEL_TOPN):
            blk = blocks[g * SEL_TOPN + i]
            base = idx_ref[(b * G + g) * SEL_TOPN + i] * SEL_BLOCK
            pk_row = pk_row + jnp.where(slot == i, base, 0)
            for par in range(per_tile):
                pk_col = base + per_tile * lax.broadcasted_iota(i32, (half, 1), 0) + par
                old = pk_col < past
                k = blk[pl.ds(par * KV_CHUNKS + g, half, stride=SUBLANES), :]
                v = blk[pl.ds(par * KV_CHUNKS + G + g, half, stride=SUBLANES), :]
                ks.append(jnp.where(old, k, new_ref[g:g + 1, :]).astype(bf16))
                vs.append(jnp.where(old, v, new_ref[G + g:G + g + 1, :]).astype(bf16))
        s = _mm_nt(qs_b, jnp.concatenate(ks, axis=0))
        e, den = _masked_softmax_rows(s, pk_row <= past)
        o = jnp.where(grp == g, _mm((e / den).astype(bf16), jnp.concatenate(vs, axis=0)), o)
    o_ref[...] = o


def _nsa_sample_sel(q, ks_new, pool, page_table, idx, past):
    B, H, D = q.shape
    G = NSA_KV_HEADS
    n_pages = page_table.shape[1]
    halves = PAGE_SIZE // SEL_BLOCK
    blk_rows = SEL_BLOCK * KV_CHUNKS
    pool_h = pool.reshape(pool.shape[0] * halves, blk_rows, D)
    last_old = past // SEL_BLOCK - 1

    def blk_map(b, idx_r, pt_r, g, i):
        blk = jnp.minimum(idx_r[(b * G + g) * SEL_TOPN + i], last_old)
        return (pt_r[b * n_pages + blk // halves] * halves + blk % halves, 0, 0)

    specs = [pl.BlockSpec((None, blk_rows, D), functools.partial(blk_map, g=g, i=i))
             for g in range(G) for i in range(SEL_TOPN)]
    gs = pltpu.PrefetchScalarGridSpec(
        num_scalar_prefetch=2, grid=(B,),
        in_specs=[pl.BlockSpec((None, H, D), lambda b, *_: (b, 0, 0)),
                  pl.BlockSpec((None, KV_CHUNKS, D), lambda b, *_: (b, 0, 0))] + specs,
        out_specs=pl.BlockSpec((None, H, D), lambda b, *_: (b, 0, 0)))
    return pl.pallas_call(
        functools.partial(_nsa_sample_sel_kernel, past=past),
        grid_spec=gs,
        out_shape=jax.ShapeDtypeStruct((B, H, D), f32),
        compiler_params=_cparams("arbitrary"),
        name="nsa_sample_sel",
    )(idx.reshape(-1), page_table.reshape(-1), q, ks_new, *([pool_h] * (G * SEL_TOPN)))


def _nsa_sample_win_kernel(q_ref, buf_ref, new_ref, oc_ref, os_ref, gt_ref, o_ref, win_ref, *, past):
    H, R, D, G = NSA_HEADS, NSA_GROUP, NSA_DH, NSA_KV_HEADS
    rows = buf_ref.shape[0]
    Lb = rows // KV_CHUNKS
    per_tile = SUBLANES // KV_CHUNKS
    half = Lb // per_tile
    qs = q_ref[...] * (D ** -0.5)
    qs_b = qs.astype(bf16)
    grp = _div_pow2(lax.broadcasted_iota(i32, (H, 1), 0), R)
    lane = lax.broadcasted_iota(i32, (1, Lb), 1)
    key_pos = past - Lb + (lane & (half - 1)) * per_tile + _div_pow2(lane, half)
    diff = past - key_pos
    valid = (diff >= 0) & (diff <= WINDOW)
    o_w = jnp.zeros((H, D), f32)
    for g in range(G):
        chunk = lambda c: jnp.concatenate(
            [buf_ref[pl.ds(par * KV_CHUNKS + c, half, stride=SUBLANES), :] for par in range(per_tile)], axis=0)
        new_k, new_v = new_ref[g:g + 1, :], new_ref[G + g:G + g + 1, :]
        s_b = jnp.where(valid, _mm_nt(qs_b, chunk(g).astype(bf16)), NEG)
        s_n = jnp.sum(qs * new_k, axis=-1, keepdims=True)
        mx = jnp.maximum(jnp.max(s_b, axis=-1, keepdims=True), s_n)
        e_b = jnp.where(valid, jnp.exp(s_b - mx), 0.0)
        e_n = jnp.exp(s_n - mx)
        den = jnp.sum(e_b, axis=-1, keepdims=True) + e_n
        og = _mm((e_b / den).astype(bf16), chunk(G + g).astype(bf16)) + (e_n / den) * new_v
        o_w = jnp.where(grp == g, og, o_w)
    gate = _sigmoid(gt_ref[...])
    o_ref[...] = gate[:, 0:1] * oc_ref[...] + gate[:, 1:2] * os_ref[...] + gate[:, 2:3] * o_w
    win_ref[0:rows - KV_CHUNKS, :] = buf_ref[KV_CHUNKS:rows, :]
    win_ref[rows - KV_CHUNKS:rows, :] = new_ref[...]


def _nsa_sample_win(q, win_buf, kw_new, o_c, o_s, gt, past):
    B, H, D = q.shape
    rows = win_buf.shape[1]
    assert rows == WINDOW * KV_CHUNKS
    per_b = lambda *s: pl.BlockSpec((None,) + s, lambda b: (b,) + (0,) * len(s))
    return pl.pallas_call(
        functools.partial(_nsa_sample_win_kernel, past=past),
        grid=(B,),
        in_specs=[per_b(H, D), per_b(rows, D), per_b(KV_CHUNKS, D), per_b(H, D), per_b(H, D), per_b(H, 3)],
        out_specs=[per_b(H, D), per_b(rows, D)],
        out_shape=[jax.ShapeDtypeStruct((B, H, D), f32), jax.ShapeDtypeStruct((B, rows, D), f32)],
        compiler_params=_cparams("parallel"),
        name="nsa_sample_win",
    )(q, win_buf, kw_new, o_c, o_s, gt.reshape(B, 3, H).transpose(0, 2, 1))


def kernel(x_prompt, x_sample, cache_cmp_kv, cache_sel_kv, cache_win_kv, cache_mem_kv, state_mlstm_c, state_mlstm_n, state_mlstm_m, state_conv, page_table, mem_prompt, g_mix, w_in, w_conv, b_conv, w_mq, w_mk, w_mv, w_mgate, b_mgate, g_mhead, cmp_pe, cmp_wpos, cmp_wphi, w_out, g_memx, g_mems, w_mem_q, w_mem_kv, w_mem_o, g_ffn, w_ff1, w_ff2, g_final):
    depth = w_in.shape[0]
    assert depth == 1 and x_prompt.shape[0] == 1 and x_sample.shape[1] == 1
    T, Dm = x_prompt.shape[1:]
    B = x_sample.shape[0]
    G, Dh, H = NSA_KV_HEADS, NSA_DH, ML_HEADS
    past = page_table.shape[1] * PAGE_SIZE
    assert (past + 1) // CMP_STRIDE == past // CMP_STRIDE
    n_main = 2 * ML_DIM + NSA_DIM + 3 * KV_ROW
    n_gate = 3 * NSA_HEADS
    l = 0
    hp = x_prompt.reshape(T, Dm)
    hs = x_sample.reshape(B, Dm)

    wbd = _blockdiag_weights(jnp.stack([w_mq[l], w_mk[l], w_mv[l]])).astype(bf16)
    w_mg = w_mgate[l].astype(bf16)
    w_mg_t = w_mg.T
    pe_t, wp_t = _compress_tables(cmp_pe[l], cmp_wpos[l])
    wb_out, wb_mq, wb_mkv, wb_mo, wb_f1, wb_f2 = (
        w[l].astype(bf16) for w in (w_out, w_mem_q, w_mem_kv, w_mem_o, w_ff1, w_ff2))
    wb_in = w_in[l, :, :n_main].astype(bf16)
    wb_gate = w_in[l, :, n_main:].astype(bf16)
    w_gt = jnp.pad(wb_gate, ((0, 0), (0, LANES - n_gate)))
    q_cb = 2 * ML_DIM // KV_ROW
    c_cb = (2 * ML_DIM + NSA_DIM) // KV_ROW
    n_mq, n_mkv = w_mem_q.shape[-1], w_mem_kv.shape[-1]
    tm_in, tm, tn, tf, tq, ml_chunk = 1024, 512, 512, 1024, 256, 256

    u, (kc_lin, ks_lin, kw_lin) = _in_proj(hp, g_mix[l], wb_in, n_main, c_cb, 3, tm_in, "in_proj_p")
    ml_o, c_p, n_p, m_p = _mlstm_prompt(u, w_conv[l], b_conv[l], wbd, w_mg, w_mg_t, b_mgate[l],
                                        g_mhead[l], ml_chunk, "mlstm_prompt")
    kvc_p = _compress_prompt(u, c_cb, pe_t, wp_t, cmp_wphi[l])
    nsa_o = _nsa_prompt(hp, g_mix[l], _group_gate_weights(wb_gate), u, q_cb, kvc_p,
                        *_kv_prep(u, c_cb + 1, c_cb + 2), tq)
    h1 = _out_proj(ml_o, nsa_o, wb_out, hp, tm, Dm, "out_proj_p")
    mem_kv = _norm_matmul(mem_prompt.reshape(-1, Dm), g_mems[l], wb_mkv, n_mkv, mem_prompt.shape[1], tn, "mem_kv")
    h2 = _mem_prompt(h1, g_memx[l], wb_mq, mem_kv, wb_mo, tm, "mem_attn_p")
    y_p = _ffn_final(h2, g_ffn[l], wb_f1, wb_f2, g_final, tm, tf, "ffn_p")

    kv_shape = lambda a: a.reshape(1, 1, -1, 2, G, Dh)
    out_p = (y_p.reshape(1, T, Dm),
             kv_shape(kc_lin), kv_shape(ks_lin),
             kv_shape(kw_lin[(T - min(WINDOW, T)) * KV_CHUNKS:]),
             mem_kv.reshape(1, 1, -1, 2, MEM_HEADS, MEM_DH),
             c_p[None, None], n_p[None, None], m_p[None, None],
             u[T - (ML_CONV - 1):, :ML_DIM][None, None])

    us = _norm_matmul(hs, g_mix[l], wb_in, n_main, B, tn, "in_proj_s")
    gts = _norm_matmul(hs, g_mix[l], w_gt, LANES, B, LANES, "in_gate_s")[:, :n_gate]
    xm_s, og_s = us[:, :ML_DIM], us[:, ML_DIM:2 * ML_DIM]
    q_s = us[:, 2 * ML_DIM:2 * ML_DIM + NSA_DIM].reshape(B, NSA_HEADS, Dh)
    kc_s, ks_s, kw_s = (us[:, (c_cb + i) * KV_ROW:(c_cb + i + 1) * KV_ROW] for i in range(3))
    ml_os, c_s, n_s, m_s = _mlstm_sample(xm_s, og_s, state_conv[l], state_mlstm_c[l], state_mlstm_n[l],
                                         state_mlstm_m[l], w_conv[l], b_conv[l], wbd,
                                         w_mg, b_mgate[l], g_mhead[l])
    pool_c = cache_cmp_kv.reshape(-1, PAGE_SIZE * KV_CHUNKS, Dh)
    pool_s = cache_sel_kv.reshape(-1, PAGE_SIZE * KV_CHUNKS, Dh)
    win_buf = cache_win_kv.reshape(B, -1, Dh)
    mem_buf = cache_mem_kv.reshape(B, -1, MEM_DH)
    kvc_s = _compress_paged(pool_c, page_table, pe_t, wp_t, cmp_wphi[l])
    o_cs, idx = _nsa_sample_cmp(q_s, kvc_s, past)
    o_ss = _nsa_sample_sel(q_s, ks_s.reshape(B, KV_CHUNKS, Dh), pool_s, page_table, idx, past)
    nsa_os, win_new = _nsa_sample_win(q_s, win_buf, kw_s.reshape(B, KV_CHUNKS, Dh), o_cs, o_ss, gts, past)
    h1s = _out_proj(ml_os, nsa_os.reshape(B, NSA_DIM), wb_out, hs, B, Dm, "out_proj_s")
    qm_s = _norm_matmul(h1s, g_memx[l], wb_mq, n_mq, B, n_mq, "mem_q_s")
    om_s = _mem_sample(qm_s, mem_buf, "mem_attn_s")
    h2s = _matmul_res(om_s, wb_mo, h1s, B, Dm, "mem_o_s")
    y_s = _ffn_final(h2s, g_ffn[l], wb_f1, wb_f2, g_final, B, tf, "ffn_s")

    kv_s_shape = lambda a: a.reshape(1, B, 1, 2, G, Dh)
    conv_s = jnp.concatenate([state_conv[l][:, 1:], xm_s[:, None, :]], axis=1)
    out_s = (y_s.reshape(B, 1, Dm), kv_s_shape(kc_s), kv_s_shape(ks_s),
             win_new.reshape(1, B, -1, 2, G, Dh), c_s[None], n_s[None], m_s[None], conv_s[None])

    return (out_p[0], out_s[0]) + out_p[1:] + out_s[1:]
```

```python
import functools

import jax
import jax.numpy as jnp
import numpy as np
from jax import lax
from jax.experimental import pallas as pl
from jax.experimental.pallas import tpu as pltpu

f32 = jnp.float32
bf16 = jnp.bfloat16
i32 = jnp.int32

EPS = 1e-6
NEG = -1e30
FORCE = 1e30
ML_HEADS = 8
ML_DH = 128
ML_DIM = ML_HEADS * ML_DH
ML_CONV = 4
ML_QKV_BLOCK = 4
NSA_HEADS = 8
NSA_KV_HEADS = 2
NSA_GROUP = NSA_HEADS // NSA_KV_HEADS
NSA_DH = 128
NSA_DIM = NSA_HEADS * NSA_DH
NSA_KV_W = NSA_KV_HEADS * NSA_DH
CMP_BLOCK = 32
CMP_STRIDE = 16
SEL_BLOCK = 64
SEL_TOPN = 16
WINDOW = 512
PAGE_SIZE = 128
MEM_HEADS = 4
MEM_DH = 128
KV_ROW = 2 * NSA_KV_W
KV_CHUNKS = KV_ROW // NSA_DH

LANES = 128
SUBLANES = 8
VMEM_LIMIT = 56 * 1024 * 1024

MASK_BIAS = -1e9
REMOVED = -3.0e38
SEL_KT = 1024
LOG2E = 1.4426950408889634
SAFE_SPAN = 100.0


def _cparams(*sem):
    return pltpu.CompilerParams(dimension_semantics=sem, vmem_limit_bytes=VMEM_LIMIT)


def _mm(a, b):
    return jnp.dot(a, b, preferred_element_type=f32)


def _mm_nt(a, b):
    return lax.dot_general(a, b, (((1,), (1,)), ((), ())), preferred_element_type=f32)


def _split2(x):
    h = x.astype(bf16)
    return h, (x - h.astype(f32)).astype(bf16)


def _split3(x):
    h = x.astype(bf16)
    r = x - h.astype(f32)
    m = r.astype(bf16)
    return h, m, (r - m.astype(f32)).astype(bf16)


def _dot3(a, b, mm=_mm):
    ah, al = _split2(a)
    bh, bl = _split2(b)
    return mm(ah, bh) + mm(al, bh) + mm(ah, bl)


def _dot_sel_l(sel, x):
    h, m, l = _split3(x)
    return _mm(sel, h) + _mm(sel, m) + _mm(sel, l)


def _dot_sel_r(x, sel):
    h, m, l = _split3(x)
    return _mm(h, sel) + _mm(m, sel) + _mm(l, sel)


def _rms(x, g):
    return x * lax.rsqrt(jnp.mean(x * x, axis=-1, keepdims=True) + EPS) * g


def _div_pow2(x, d):
    assert d & (d - 1) == 0
    return lax.shift_right_logical(x, jnp.full(x.shape, d.bit_length() - 1, x.dtype))


def _sigmoid(x):
    return 0.5 * jnp.tanh(0.5 * x) + 0.5


def _log_sigmoid(x):
    return jnp.minimum(x, 0.0) - jnp.log(1.0 + jnp.exp(-jnp.abs(x)))


def _masked_softmax_rows(s, valid):
    s = jnp.where(valid, s, NEG)
    mx = jnp.max(s, axis=-1, keepdims=True)
    e = jnp.where(valid, jnp.exp(s - mx), 0.0)
    den = jnp.sum(e, axis=-1, keepdims=True)
    return e, jnp.where(den > 0.0, den, 1.0)


def _biased_exp2_rows(s2, bias):
    s2 = s2 + bias
    mx = jnp.max(s2, axis=-1, keepdims=True)
    return jnp.exp2(s2 - mx), mx


def _biased_softmax2_rows(s2, bias):
    e, mx = _biased_exp2_rows(s2, bias)
    inv = jnp.where(mx > 0.5 * NEG, 1.0 / jnp.sum(e, axis=-1, keepdims=True), 0.0)
    return e * inv


def _norm_matmul_kernel(x_ref, g_ref, w_ref, o_ref, xn_ref):
    @pl.when(pl.program_id(1) == 0)
    def _():
        xn_ref[...] = _rms(x_ref[...], g_ref[...]).astype(bf16)

    o_ref[...] = _mm(xn_ref[...], w_ref[...])


def _norm_matmul(x, g, w, n_cols, tm, tn, name):
    M, K = x.shape
    return pl.pallas_call(
        _norm_matmul_kernel,
        grid=(M // tm, n_cols // tn),
        in_specs=[pl.BlockSpec((tm, K), lambda i, j: (i, 0)),
                  pl.BlockSpec((1, K), lambda i, j: (0, 0)),
                  pl.BlockSpec((K, tn), lambda i, j: (0, j))],
        out_specs=pl.BlockSpec((tm, tn), lambda i, j: (i, j)),
        out_shape=jax.ShapeDtypeStruct((M, n_cols), f32),
        scratch_shapes=[pltpu.VMEM((tm, K), bf16)],
        compiler_params=_cparams("parallel", "arbitrary"),
        name=name,
    )(x, g.reshape(1, K), w)


def _in_proj_kernel(x_ref, g_ref, w_ref, o_ref, *rest, kv_block0):
    lin_refs, xn_ref = rest[:-1], rest[-1]
    j = pl.program_id(1)

    @pl.when(j == 0)
    def _():
        xn_ref[...] = _rms(x_ref[...], g_ref[...]).astype(bf16)

    val = _mm(xn_ref[...], w_ref[...])
    o_ref[...] = val
    rows = val.shape[0]
    for i, ref in enumerate(lin_refs):
        @pl.when(j == kv_block0 + i)
        def _():
            for c in range(KV_CHUNKS):
                ref[pl.ds(c, rows, stride=KV_CHUNKS), :] = val[:, c * LANES:(c + 1) * LANES]


def _in_proj(x, g, w, n_cols, kv_block0, n_kv, tm, name):
    M, K = x.shape
    tn = KV_ROW
    lin_spec = pl.BlockSpec((tm * KV_CHUNKS, LANES), lambda i, j: (i, 0))
    outs = pl.pallas_call(
        functools.partial(_in_proj_kernel, kv_block0=kv_block0),
        grid=(M // tm, n_cols // tn),
        in_specs=[pl.BlockSpec((tm, K), lambda i, j: (i, 0)),
                  pl.BlockSpec((1, K), lambda i, j: (0, 0)),
                  pl.BlockSpec((K, tn), lambda i, j: (0, j))],
        out_specs=[pl.BlockSpec((tm, tn), lambda i, j: (i, j))] + [lin_spec] * n_kv,
        out_shape=[jax.ShapeDtypeStruct((M, n_cols), f32)]
        + [jax.ShapeDtypeStruct((M * KV_CHUNKS, LANES), f32)] * n_kv,
        scratch_shapes=[pltpu.VMEM((tm, K), bf16)],
        compiler_params=_cparams("parallel", "arbitrary"),
        name=name,
    )(x, g.reshape(1, K), w)
    return outs[0], outs[1:]


def _out_proj_kernel(a1_ref, a2_ref, w1_ref, w2_ref, r_ref, o_ref):
    o_ref[...] = (r_ref[...] + _mm(a1_ref[...].astype(bf16), w1_ref[...])
                  + _mm(a2_ref[...].astype(bf16), w2_ref[...]))


def _out_proj(a1, a2, w, res, tm, tn, name):
    M, K1 = a1.shape
    K2 = a2.shape[1]
    assert K1 == K2 and w.shape[0] == K1 + K2
    N = w.shape[1]
    return pl.pallas_call(
        _out_proj_kernel,
        grid=(M // tm, N // tn),
        in_specs=[pl.BlockSpec((tm, K1), lambda i, j: (i, 0)),
                  pl.BlockSpec((tm, K2), lambda i, j: (i, 0)),
                  pl.BlockSpec((K1, tn), lambda i, j: (0, j)),
                  pl.BlockSpec((K2, tn), lambda i, j: (1, j)),
                  pl.BlockSpec((tm, tn), lambda i, j: (i, j))],
        out_specs=pl.BlockSpec((tm, tn), lambda i, j: (i, j)),
        out_shape=jax.ShapeDtypeStruct((M, N), f32),
        compiler_params=_cparams("parallel", "arbitrary"),
        name=name,
    )(a1, a2, w, w, res)


def _matmul_res_kernel(a_ref, w_ref, r_ref, o_ref):
    o_ref[...] = r_ref[...] + _mm(a_ref[...].astype(bf16), w_ref[...])


def _matmul_res(a, w, res, tm, tn, name):
    M, K = a.shape
    N = w.shape[1]
    return pl.pallas_call(
        _matmul_res_kernel,
        grid=(M // tm, N // tn),
        in_specs=[pl.BlockSpec((tm, K), lambda i, j: (i, 0)),
                  pl.BlockSpec((K, tn), lambda i, j: (0, j)),
                  pl.BlockSpec((tm, tn), lambda i, j: (i, j))],
        out_specs=pl.BlockSpec((tm, tn), lambda i, j: (i, j)),
        out_shape=jax.ShapeDtypeStruct((M, N), f32),
        compiler_params=_cparams("parallel", "arbitrary"),
        name=name,
    )(a, w, res)


def _ffn_kernel(h_ref, g_ref, w1_ref, w2_ref, gf_ref, y_ref, xn_ref, acc_ref):
    f = pl.program_id(1)

    @pl.when(f == 0)
    def _():
        xn_ref[...] = _rms(h_ref[...], g_ref[...]).astype(bf16)
        acc_ref[...] = jnp.zeros_like(acc_ref)

    a = _mm(xn_ref[...], w1_ref[...])
    a = jnp.square(jnp.maximum(a, 0.0))
    acc_ref[...] += _mm(a.astype(bf16), w2_ref[...])

    @pl.when(f == pl.num_programs(1) - 1)
    def _():
        y_ref[...] = _rms(h_ref[...] + acc_ref[...], gf_ref[...])


def _ffn_final(h, g, w1, w2, g_final, tm, tf, name):
    M, D = h.shape
    F = w1.shape[1]
    return pl.pallas_call(
        _ffn_kernel,
        grid=(M // tm, F // tf),
        in_specs=[pl.BlockSpec((tm, D), lambda i, j: (i, 0)),
                  pl.BlockSpec((1, D), lambda i, j: (0, 0)),
                  pl.BlockSpec((D, tf), lambda i, j: (0, j)),
                  pl.BlockSpec((tf, D), lambda i, j: (j, 0)),
                  pl.BlockSpec((1, D), lambda i, j: (0, 0))],
        out_specs=pl.BlockSpec((tm, D), lambda i, j: (i, 0)),
        out_shape=jax.ShapeDtypeStruct((M, D), f32),
        scratch_shapes=[pltpu.VMEM((tm, D), bf16), pltpu.VMEM((tm, D), f32)],
        compiler_params=_cparams("parallel", "arbitrary"),
        name=name,
    )(h, g.reshape(1, D), w1, w2, g_final.reshape(1, D))


def _mem_prompt_kernel(h_ref, g_ref, wq_ref, k_ref, v_ref, wo_ref, o_ref):
    h = h_ref[...]
    xn = _rms(h, g_ref[...]).astype(bf16)
    q = _mm(xn, wq_ref[...]) * (MEM_DH ** -0.5)
    outs = []
    for hd in range(MEM_HEADS):
        sl = slice(hd * MEM_DH, (hd + 1) * MEM_DH)
        s = _mm_nt(q[:, sl].astype(bf16), k_ref[:, sl].astype(bf16))
        e = jnp.exp(s - jnp.max(s, axis=-1, keepdims=True))
        p = e * (1.0 / jnp.sum(e, axis=-1, keepdims=True))
        outs.append(_mm(p.astype(bf16), v_ref[:, sl].astype(bf16)))
    o = jnp.concatenate(outs, axis=-1)
    o_ref[...] = h + _mm(o.astype(bf16), wo_ref[...])


def _mem_prompt(h, g, wq, mem_kv, wo, tm, name):
    M, D = h.shape
    HD = MEM_HEADS * MEM_DH
    ML = mem_kv.shape[0]
    return pl.pallas_call(
        _mem_prompt_kernel,
        grid=(M // tm,),
        in_specs=[pl.BlockSpec((tm, D), lambda i: (i, 0)),
                  pl.BlockSpec((1, D), lambda i: (0, 0)),
                  pl.BlockSpec((D, HD), lambda i: (0, 0)),
                  pl.BlockSpec((ML, HD), lambda i: (0, 0)),
                  pl.BlockSpec((ML, HD), lambda i: (0, 1)),
                  pl.BlockSpec((HD, D), lambda i: (0, 0))],
        out_specs=pl.BlockSpec((tm, D), lambda i: (i, 0)),
        out_shape=jax.ShapeDtypeStruct((M, D), f32),
        compiler_params=_cparams("parallel"),
        name=name,
    )(h, g.reshape(1, D), wq, mem_kv, mem_kv, wo)


def _mem_sample_kernel(q_ref, kv_ref, o_ref):
    per_tok = 2 * MEM_HEADS
    ML = kv_ref.shape[0] // per_tok
    q = q_ref[...] * (MEM_DH ** -0.5)
    outs = []
    for hd in range(MEM_HEADS):
        sl = slice(hd * MEM_DH, (hd + 1) * MEM_DH)
        k = kv_ref[pl.ds(hd, ML, stride=per_tok), :]
        v = kv_ref[pl.ds(MEM_HEADS + hd, ML, stride=per_tok), :]
        s = jnp.sum(k * q[:, sl], axis=-1, keepdims=True)
        e = jnp.exp(s - jnp.max(s, axis=0, keepdims=True))
        p = e / jnp.sum(e, axis=0, keepdims=True)
        outs.append(jnp.sum(p * v, axis=0, keepdims=True))
    o_ref[...] = jnp.concatenate(outs, axis=-1)


def _mem_sample(q, kv, name):
    B, HD = q.shape
    rows = kv.shape[1]
    out = pl.pallas_call(
        _mem_sample_kernel,
        grid=(B,),
        in_specs=[pl.BlockSpec((None, 1, HD), lambda b: (b, 0, 0)),
                  pl.BlockSpec((None, rows, MEM_DH), lambda b: (b, 0, 0))],
        out_specs=pl.BlockSpec((None, 1, HD), lambda b: (b, 0, 0)),
        out_shape=jax.ShapeDtypeStruct((B, 1, HD), f32),
        compiler_params=_cparams("parallel"),
        name=name,
    )(q.reshape(B, 1, HD), kv)
    return out.reshape(B, HD)


def _blockdiag_weights(w):
    n, b = w.shape[0], ML_QKV_BLOCK
    per = LANES // b
    wc = w.reshape(n, w.shape[1] // per, per, b * b)
    r, k = np.arange(LANES), np.arange(b * b)
    expand = (r[:, None] // b == np.arange(per)[None, :]).astype(np.float32)
    row_sel = (r[:, None] % b == k[None, :] // b).astype(np.float32)
    col_sel = (k[:, None] % b == r[None, :] % b).astype(np.float32)
    same_blk = (r[:, None] // b == r[None, :] // b).astype(np.float32)
    rows = jnp.einsum('rb,scbk->scrk', expand, wc) * row_sel
    return jnp.einsum('scrk,kq->scrq', rows, col_sel) * same_blk


def _ml_qkv_gates(xc, xm, wbd_ref, wg_ref, bg_ref):
    xc_b, xm_b = xc.astype(bf16), xm.astype(bf16)
    qs, ks, vs = [], [], []
    for c in range(ML_DIM // LANES):
        sl = slice(c * LANES, (c + 1) * LANES)
        qs.append(_mm(xc_b[:, sl], wbd_ref[0, c]))
        ks.append(_mm(xc_b[:, sl], wbd_ref[1, c]) * (ML_DH ** -0.5))
        vs.append(_mm(xm_b[:, sl], wbd_ref[2, c]))
    q = jnp.concatenate(qs, axis=-1)
    k = jnp.concatenate(ks, axis=-1)
    v = jnp.concatenate(vs, axis=-1)
    qkv_b = jnp.concatenate([q, k, v], axis=-1).astype(bf16)
    gates = _mm(qkv_b, wg_ref[...]) + bg_ref[...]
    return q, k, v, gates, qkv_b


def _mlstm_prompt_kernel(xm_ref, og_ref, wconv_ref, bconv_ref, wbd_ref, wg_ref, wgt_ref,
                         bg_ref, bgt_ref, gh_ref, o_ref, c_ref, n_ref, m_ref, prev_sc, *, L):
    H, D = ML_HEADS, ML_DH

    @pl.when(pl.program_id(0) == 0)
    def _():
        prev_sc[...] = jnp.zeros_like(prev_sc)
        c_ref[...] = jnp.zeros_like(c_ref)
        n_ref[...] = jnp.zeros_like(n_ref)
        m_ref[...] = jnp.full(m_ref.shape, NEG, f32)

    x = xm_ref[...]
    full = jnp.concatenate([prev_sc[...], x], axis=0)
    y = bconv_ref[...]
    for j in range(ML_CONV):
        off = SUBLANES - (ML_CONV - 1) + j
        y = y + full[off:off + L] * wconv_ref[j:j + 1, :]
    prev_sc[...] = x[L - SUBLANES:L]
    xc = y * _sigmoid(y)

    q, k, v, gates, qkv_b = _ml_qkv_gates(xc, x, wbd_ref, wg_ref, bg_ref)
    gates_t = _mm_nt(wgt_ref[...], qkv_b) + bgt_ref[...]
    ig_c = gates[:, 0:H]
    lf_c = _log_sigmoid(gates[:, H:2 * H])
    ig_r = gates_t[0:H, :]
    lf_r = _log_sigmoid(gates_t[H:2 * H, :])

    t_i = lax.broadcasted_iota(i32, (L, L), 0)
    s_i = lax.broadcasted_iota(i32, (L, L), 1)
    causal = s_i <= t_i
    tri = jnp.where(causal, 1.0, 0.0).astype(bf16)
    b_c = _dot_sel_l(tri, lf_c)
    tri_u = jnp.where(t_i <= s_i, 1.0, 0.0).astype(bf16)
    b_r = _dot_sel_r(lf_r, tri_u)

    for h in range(H):
        sl = slice(h * D, (h + 1) * D)
        qh, kh, vh = q[:, sl], k[:, sl], v[:, sl]
        bc = b_c[:, h:h + 1]
        m_prev = m_ref[h:h + 1, 0:1]
        d_in = jnp.where(causal, bc - b_r[h:h + 1, :] + ig_r[h:h + 1, :], NEG)
        d_x = bc + m_prev
        m_t = jnp.maximum(d_x, jnp.max(d_in, axis=-1, keepdims=True))
        w_in = jnp.exp(d_in - m_t)
        w_x = jnp.exp(d_x - m_t)
        qb = qh.astype(bf16)
        kb = kh.astype(bf16)
        vb = vh.astype(bf16)
        s = _mm_nt(qb, kb) * w_in
        c_old = c_ref[h]
        n_old = n_ref[h:h + 1, :]
        num = _mm(s.astype(bf16), vb) + w_x * _mm(qb, c_old.astype(bf16))
        den = jnp.sum(s, axis=-1, keepdims=True) + w_x * jnp.sum(qh * n_old, axis=-1, keepdims=True)
        hh = num * (1.0 / jnp.maximum(jnp.abs(den), jnp.exp(-m_t)))
        m_new = m_t[L - 1:L, :]
        b_last = bc[L - 1:L, :]
        g_x = jnp.exp(b_last + m_prev - m_new)
        g_s = jnp.exp(b_last - bc + ig_c[:, h:h + 1] - m_new)
        ks_ = kh * g_s
        c_ref[h] = g_x * c_old + _mm(ks_.T.astype(bf16), vb)
        n_ref[h:h + 1, :] = g_x * n_old + jnp.sum(ks_, axis=0, keepdims=True)
        m_ref[h:h + 1, :] = jnp.broadcast_to(m_new, (1, LANES))
        hn = hh * lax.rsqrt(jnp.mean(hh * hh, axis=-1, keepdims=True) + EPS) * gh_ref[:, sl]
        o_ref[:, sl] = _sigmoid(og_ref[:, sl]) * hn


def _mlstm_prompt(u, w_conv, b_conv, wbd, w_gate, w_gate_t, b_gate, g_head, L, name):
    T = u.shape[0]
    H, D = ML_HEADS, ML_DH
    nch = ML_DIM // LANES
    full2 = lambda shape: pl.BlockSpec(shape, lambda i: (0,) * len(shape))
    out, c, n, m = pl.pallas_call(
        functools.partial(_mlstm_prompt_kernel, L=L),
        grid=(T // L,),
        in_specs=[pl.BlockSpec((L, ML_DIM), lambda i: (i, 0)),
                  pl.BlockSpec((L, ML_DIM), lambda i: (i, 1)),
                  full2((ML_CONV, ML_DIM)), full2((1, ML_DIM)),
                  full2((3, nch, LANES, LANES)),
                  full2((3 * ML_DIM, 2 * H)), full2((2 * H, 3 * ML_DIM)),
                  full2((1, 2 * H)), full2((2 * H, 1)), full2((1, ML_DIM))],
        out_specs=[pl.BlockSpec((L, ML_DIM), lambda i: (i, 0)),
                   full2((H, D, D)), full2((H, D)), full2((H, LANES))],
        out_shape=[jax.ShapeDtypeStruct((T, ML_DIM), f32),
                   jax.ShapeDtypeStruct((H, D, D), f32),
                   jax.ShapeDtypeStruct((H, D), f32),
                   jax.ShapeDtypeStruct((H, LANES), f32)],
        scratch_shapes=[pltpu.VMEM((SUBLANES, ML_DIM), f32)],
        compiler_params=_cparams("arbitrary"),
        name=name,
    )(u, u, w_conv, b_conv.reshape(1, ML_DIM), wbd, w_gate, w_gate_t,
      b_gate.reshape(1, 2 * H), b_gate.reshape(2 * H, 1), g_head.reshape(1, ML_DIM))
    return out, c, n, m[:, 0]


def _mlstm_sample_pre_kernel(xm_ref, s0_ref, s1_ref, s2_ref, wconv_ref, bconv_ref, wbd_ref,
                             wg_ref, bg_ref, q_ref, k_ref, v_ref, g_ref):
    x = xm_ref[...]
    y = (bconv_ref[...] + s0_ref[...] * wconv_ref[0:1, :] + s1_ref[...] * wconv_ref[1:2, :]
         + s2_ref[...] * wconv_ref[2:3, :] + x * wconv_ref[3:4, :])
    xc = y * _sigmoid(y)
    q, k, v, gates, _ = _ml_qkv_gates(xc, x, wbd_ref, wg_ref, bg_ref)
    q_ref[...] = q
    k_ref[...] = k
    v_ref[...] = v
    g_ref[...] = gates


def _mlstm_sample_step_kernel(qc_ref, kc_ref, q_ref, k_ref, v_ref, gt_ref, og_ref, gh_ref, c_ref, n_ref, m_ref,
                              o_ref, cn_ref, nn_ref, mn_ref):
    H, D = ML_HEADS, ML_DH
    ig = gt_ref[0:H, :]
    lf = _log_sigmoid(gt_ref[H:2 * H, :])
    m_old = m_ref[...]
    m_new = jnp.maximum(lf + m_old, ig)
    w_in = jnp.exp(ig - m_new)
    w_x = jnp.exp(lf + m_old - m_new)
    mn_ref[...] = m_new
    q, k, v, n_old = q_ref[...], k_ref[...], v_ref[...], n_ref[...]
    cq = jnp.concatenate([jnp.sum(c_ref[h] * qc_ref[:, h:h + 1], axis=0, keepdims=True) for h in range(H)], axis=0)
    s = jnp.sum(q * k, axis=-1, keepdims=True) * w_in
    num = s * v + w_x * cq
    den = s + w_x * jnp.sum(n_old * q, axis=-1, keepdims=True)
    hh = num / jnp.maximum(jnp.abs(den), jnp.exp(-m_new))
    hn = hh * lax.rsqrt(jnp.mean(hh * hh, axis=-1, keepdims=True) + EPS) * gh_ref[...]
    o_ref[...] = _sigmoid(og_ref[...]) * hn
    nn_ref[...] = w_x * n_old + w_in * k
    for h in range(H):
        cn_ref[h] = w_x[h:h + 1, :] * c_ref[h] + w_in[h:h + 1, :] * (kc_ref[:, h:h + 1] * v[h:h + 1, :])


def _mlstm_sample(xm, og, conv_state, c0, n0, m0, w_conv, b_conv, wbd, w_gate, b_gate, g_head):
    B = xm.shape[0]
    H, D = ML_HEADS, ML_DH
    sds = lambda *s: jax.ShapeDtypeStruct(s, f32)
    q, k, v, gates = pl.pallas_call(
        _mlstm_sample_pre_kernel,
        out_shape=[sds(B, ML_DIM), sds(B, ML_DIM), sds(B, ML_DIM), sds(B, 2 * H)],
        compiler_params=pltpu.CompilerParams(vmem_limit_bytes=VMEM_LIMIT),
        name="mlstm_sample_pre",
    )(xm, conv_state[:, 0], conv_state[:, 1], conv_state[:, 2], w_conv, b_conv.reshape(1, ML_DIM),
      wbd, w_gate, b_gate.reshape(1, 2 * H))
    q3, k3, v3 = (a.reshape(B, H, D) for a in (q, k, v))
    per_b = lambda *s: pl.BlockSpec((None,) + s, lambda b: (b,) + (0,) * len(s))
    out, c, n, m = pl.pallas_call(
        _mlstm_sample_step_kernel,
        grid=(B,),
        in_specs=[per_b(D, H), per_b(D, H), per_b(H, D), per_b(H, D), per_b(H, D), per_b(2 * H, 1),
                  per_b(H, D), pl.BlockSpec((H, D), lambda b: (0, 0)),
                  per_b(H, D, D), per_b(H, D), per_b(H, 1)],
        out_specs=[per_b(H, D), per_b(H, D, D), per_b(H, D), per_b(H, 1)],
        out_shape=[sds(B, H, D), sds(B, H, D, D), sds(B, H, D), sds(B, H, 1)],
        compiler_params=_cparams("parallel"),
        name="mlstm_sample_step",
    )(q3.transpose(0, 2, 1), k3.transpose(0, 2, 1), q3, k3, v3, gates.reshape(B, 2 * H, 1),
      og.reshape(B, H, D), g_head.reshape(H, D), c0, n0, m0.reshape(B, H, 1))
    return out.reshape(B, ML_DIM), c, n, m.reshape(B, H)


def _compress_kernel(x_ref, pe_ref, wp_ref, wphi_ref, o_ref, f0_sc, f1_sc, mn_sc):
    step = pl.program_id(0)
    sub = x_ref.shape[0] // CMP_STRIDE
    x3 = x_ref[...].reshape(sub, CMP_STRIDE, KV_ROW)
    base = pl.multiple_of(step * sub, sub)
    for o, sc in ((0, f0_sc), (1, f1_sc)):
        y = x3 + pe_ref[o][None]
        sc[pl.ds(base, sub), :] = jnp.sum(y * _sigmoid(y) * wp_ref[o][None], axis=1)
    mn_sc[pl.ds(base, sub), :] = jnp.sum(x3, axis=1) * (1.0 / CMP_STRIDE)

    @pl.when(step == pl.num_programs(0) - 1)
    def _():
        ns = f0_sc.shape[0]
        feat = f0_sc[...] + pltpu.roll(f1_sc[...], ns - 1, axis=0)
        mn = mn_sc[...]
        pooled = (mn + pltpu.roll(mn, ns - 1, axis=0)) * (CMP_STRIDE / CMP_BLOCK)
        for c in range(2):
            for g in range(NSA_KV_HEADS):
                sl = slice((c * NSA_KV_HEADS + g) * NSA_DH, (c * NSA_KV_HEADS + g + 1) * NSA_DH)
                o_ref[:, sl] = pooled[:, sl] + _dot3(feat[:, sl], wphi_ref[c])


def _compress_tables(pe, wpos):
    def lay(a):
        r = CMP_BLOCK // CMP_STRIDE
        a = a.reshape(2, r, CMP_STRIDE, NSA_DH).transpose(1, 2, 0, 3)
        a = jnp.broadcast_to(a[:, :, :, None, :], (r, CMP_STRIDE, 2, NSA_KV_HEADS, NSA_DH))
        return a.reshape(r, CMP_STRIDE, KV_ROW)
    return lay(pe), lay(wpos)


def _compress_prompt(u, col_block, pe_t, wp_t, wphi, rows=512):
    T = u.shape[0]
    n_sub = T // CMP_STRIDE
    const = lambda shape: pl.BlockSpec(shape, lambda s: (0,) * len(shape))
    return pl.pallas_call(
        _compress_kernel,
        grid=(T // rows,),
        in_specs=[pl.BlockSpec((rows, KV_ROW), lambda s: (s, col_block)),
                  const(pe_t.shape), const(wp_t.shape), const(wphi.shape)],
        out_specs=const((n_sub, KV_ROW)),
        out_shape=jax.ShapeDtypeStruct((n_sub, KV_ROW), f32),
        scratch_shapes=[pltpu.VMEM((n_sub, KV_ROW), f32)] * 3,
        compiler_params=_cparams("arbitrary"),
        name="compress_prompt",
    )(u, pe_t, wp_t, wphi)


def _compress_paged_kernel(pt_ref, *refs, n_pages):
    pages = refs[:n_pages]
    pe_ref, wp_ref, wphi_ref, o_ref, f0_sc, f1_sc, mn_sc = refs[n_pages:]
    step = pl.program_id(1)
    sub = PAGE_SIZE // CMP_STRIDE
    tiles = CMP_STRIDE * KV_CHUNKS // SUBLANES
    packed = 2 * SUBLANES
    ptiles = CMP_STRIDE * KV_CHUNKS // packed
    out_rows = sub * SUBLANES
    for p in range(n_pages):
        x = pages[p][...]
        base = pl.multiple_of((step * n_pages + p) * out_rows, out_rows)
        xh = (x.astype(bf16) * 0.5).reshape(sub, ptiles, packed, LANES)
        for o, sc in ((0, f0_sc), (1, f1_sc)):
            y = xh + pe_ref[o][None]
            b = y * wp_ref[o][None]
            z = b + b * jnp.tanh(y)
            z = (z[:, 0] + z[:, 1]) + (z[:, 2] + z[:, 3])
            zf = z.astype(f32).reshape(sub, packed // SUBLANES, SUBLANES, LANES)
            sc[pl.ds(base, out_rows), :] = jnp.sum(zf, axis=1).reshape(out_rows, LANES)
        mean = jnp.sum(x.reshape(sub, tiles, SUBLANES, LANES), axis=1) * (1.0 / CMP_STRIDE)
        mn_sc[pl.ds(base, out_rows), :] = mean.reshape(out_rows, LANES)

    @pl.when(step == pl.num_programs(1) - 1)
    def _():
        ns = f0_sc.shape[0] // SUBLANES
        for c in range(KV_CHUNKS):
            col = lambda sc: (sc[pl.ds(c, ns, stride=SUBLANES), :]
                              + sc[pl.ds(KV_CHUNKS + c, ns, stride=SUBLANES), :])
            feat = col(f0_sc) + pltpu.roll(col(f1_sc), ns - 1, axis=0)
            mn = col(mn_sc)
            pooled = (mn + pltpu.roll(mn, ns - 1, axis=0)) * (CMP_STRIDE / CMP_BLOCK)
            o_ref[:, c * LANES:(c + 1) * LANES] = pooled + _dot3(feat, wphi_ref[c // NSA_KV_HEADS])


def _compress_paged(pool, page_table, pe_t, wp_t, wphi, pages_per_step=64):
    B, n_pages = page_table.shape
    P = pages_per_step
    n_sub = n_pages * PAGE_SIZE // CMP_STRIDE
    r = CMP_BLOCK // CMP_STRIDE
    packed = 2 * SUBLANES
    ptiles = CMP_STRIDE * KV_CHUNKS // packed
    pe4 = (0.5 * pe_t).astype(bf16).reshape(r, ptiles, packed, LANES)
    wp4 = wp_t.astype(bf16).reshape(r, ptiles, packed, LANES)
    page_rows = PAGE_SIZE * KV_CHUNKS
    specs = [pl.BlockSpec((None, page_rows, LANES),
                          functools.partial(lambda b, s, pt, p: (pt[b * n_pages + s * P + p], 0, 0), p=p))
             for p in range(P)]
    const = lambda shape: pl.BlockSpec(shape, lambda *a: (0,) * len(shape))
    gs = pltpu.PrefetchScalarGridSpec(
        num_scalar_prefetch=1, grid=(B, n_pages // P),
        in_specs=specs + [const(pe4.shape), const(wp4.shape), const(wphi.shape)],
        out_specs=pl.BlockSpec((None, n_sub, KV_ROW), lambda b, s, pt: (b, 0, 0)),
        scratch_shapes=[pltpu.VMEM((n_sub * SUBLANES, LANES), f32)] * 3)
    return pl.pallas_call(
        functools.partial(_compress_paged_kernel, n_pages=P),
        grid_spec=gs,
        out_shape=jax.ShapeDtypeStruct((B, n_sub, KV_ROW), f32),
        compiler_params=_cparams("parallel", "arbitrary"),
        name="compress_paged",
    )(page_table.reshape(-1), *([pool] * P), pe4, wp4, wphi)


def _kv_prep_kernel(ks_ref, kw_ref, ksk_ref, ksv_ref, kwk_ref, kwv_ref, kn2_ref):
    rows = ks_ref.shape[0]
    ks = ks_ref[...]
    kw = kw_ref[...]
    r = pl.program_id(0) * rows + lax.broadcasted_iota(i32, (rows, LANES), 0)
    n = lax.broadcasted_iota(i32, (rows, LANES), 1)
    onehot = jnp.where(_div_pow2(r, SEL_BLOCK) == n, 1.0, 0.0).astype(bf16)
    ones_col = jnp.where(n == 0, 1.0, 0.0).astype(bf16)

    @pl.when(pl.program_id(0) == 0)
    def _():
        kn2_ref[...] = jnp.zeros_like(kn2_ref)

    for g in range(NSA_KV_HEADS):
        ksl = slice(g * NSA_DH, (g + 1) * NSA_DH)
        vsl = slice(NSA_KV_W + g * NSA_DH, NSA_KV_W + (g + 1) * NSA_DH)
        kb = ks[:, ksl].astype(bf16)
        ksk_ref[g, :, 0:NSA_DH] = kb
        ksk_ref[g, :, NSA_DH:NSA_DH + LANES] = onehot
        ksv_ref[g, :, 0:NSA_DH] = ks[:, vsl].astype(bf16)
        ksv_ref[g, :, NSA_DH:NSA_DH + LANES] = ones_col
        kwk_ref[g] = kw[:, ksl].astype(bf16)
        kwv_ref[g, :, 0:NSA_DH] = kw[:, vsl].astype(bf16)
        kwv_ref[g, :, NSA_DH:NSA_DH + LANES] = ones_col
        kf = kb.astype(f32)
        n2 = jnp.max(jnp.sum(kf * kf, axis=-1, keepdims=True), axis=0, keepdims=True)
        kn2_ref[g] = jnp.maximum(kn2_ref[g], jnp.broadcast_to(n2, kn2_ref.shape[1:]))


def _kv_prep(u, ks_col_block, kw_col_block, rows=512):
    T = u.shape[0]
    G = NSA_KV_HEADS
    assert (T - 1) // SEL_BLOCK + 1 <= LANES
    sd = lambda w: jax.ShapeDtypeStruct((G, T, w), bf16)
    ospec = lambda w: pl.BlockSpec((G, rows, w), lambda i: (0, i, 0))
    return pl.pallas_call(
        _kv_prep_kernel,
        grid=(T // rows,),
        in_specs=[pl.BlockSpec((rows, KV_ROW), lambda i: (i, ks_col_block)),
                  pl.BlockSpec((rows, KV_ROW), lambda i: (i, kw_col_block))],
        out_specs=[ospec(NSA_DH + LANES), ospec(NSA_DH + LANES), ospec(NSA_DH), ospec(NSA_DH + LANES),
                   pl.BlockSpec((G, SUBLANES, LANES), lambda i: (0, 0, 0))],
        out_shape=[sd(NSA_DH + LANES), sd(NSA_DH + LANES), sd(NSA_DH), sd(NSA_DH + LANES),
                   jax.ShapeDtypeStruct((G, SUBLANES, LANES), f32)],
        compiler_params=_cparams("arbitrary"),
        name="kv_prep",
    )(u, u)


def _top_blocks(score, n_top, axis):
    lane = lax.broadcasted_iota(i32, score.shape, axis).astype(f32)
    width = float(score.shape[axis])
    work = score
    firsts = []
    for _ in range(n_top):
        mx = jnp.max(work, axis=axis, keepdims=True)
        first = jnp.min(jnp.where(work == mx, lane, width), axis=axis, keepdims=True)
        work = jnp.where(lane == first, REMOVED, work)
        firsts.append(first)
    return work, firsts


def _nsa_prompt_kernel(x_ref, gmix_ref, wgt_ref, q_ref, kck_ref, kcv_ref, ksk_ref, ksv_ref, kwk_ref, kwv_ref, kn2_ref,
                       o_ref, shift_sc, m_sc, acc_sc, *, n_sel):
    R, D = NSA_GROUP, NSA_DH
    QB = q_ref.shape[0]
    rows = R * QB
    qb = pl.program_id(1)
    q = q_ref[...]
    q2_b = (jnp.concatenate([q[:, r * D:(r + 1) * D] for r in range(R)], axis=0) * (D ** -0.5 * LOG2E)).astype(bf16)
    pos = qb * QB + lax.broadcasted_iota(i32, (QB, 1), 0)

    def head_bias(valid):
        return jnp.concatenate([jnp.where(valid, 0.0, NEG)] * R, axis=0)

    ns = kck_ref.shape[0]
    s = _mm_nt(q2_b, kck_ref[...].astype(bf16))
    j = lax.broadcasted_iota(i32, (1, ns), 1)
    p_c = _biased_softmax2_rows(s, head_bias(j * CMP_STRIDE + (CMP_BLOCK - 1) <= pos))
    o_c = _mm(p_c.astype(bf16), kcv_ref[...].astype(bf16))

    imp = p_c[0:QB]
    for r in range(1, R):
        imp = imp + p_c[r * QB:(r + 1) * QB]
    ratio = SEL_BLOCK // CMP_STRIDE
    off = CMP_BLOCK // CMP_STRIDE - 1
    nn = lax.broadcasted_iota(i32, (LANES, ns), 0)
    jj = lax.broadcasted_iota(i32, (LANES, ns), 1)
    overlap_t = jnp.where((jj >= ratio * nn - off) & (jj < ratio * nn + ratio), 1.0, 0.0).astype(bf16)
    imp_sel = sum(_mm_nt(overlap_t, part) for part in _split3(imp))
    n_idx = lax.broadcasted_iota(i32, (LANES, QB), 0)
    cur = _div_pow2(qb * QB + lax.broadcasted_iota(i32, (LANES, QB), 1), SEL_BLOCK)
    forced = (n_idx == 0) | (n_idx == cur) | (n_idx == cur - 1)
    score = jnp.where(forced, FORCE, jnp.where(n_idx <= cur, imp_sel, NEG))
    score = jnp.where(n_idx < n_sel, score, REMOVED)
    taken, _ = _top_blocks(score, min(SEL_TOPN, n_sel), axis=0)
    bias = jnp.where((taken < 0.5 * REMOVED) & (n_idx <= cur), 0.0, MASK_BIAS).T

    KT = SEL_KT
    last = (qb * QB + QB - 1) // KT
    q2 = q2_b.astype(f32)
    bound = jnp.sqrt(jnp.sum(q2 * q2, axis=-1, keepdims=True) * kn2_ref[0:1, 0:1]) * 1.01 + 1e-3
    k_self = ksk_ref[pl.ds(pl.multiple_of(qb * QB, QB), QB), 0:D].astype(f32)
    s_self = jnp.concatenate([jnp.sum(q2[r * QB:(r + 1) * QB] * k_self, axis=-1, keepdims=True) for r in range(R)],
                             axis=0)
    shift_sc[...] = bound
    diag_bias = head_bias(last * KT + lax.broadcasted_iota(i32, (1, KT), 1) <= pos)

    def shifted_queries(shift):
        return jnp.concatenate(
            [jnp.concatenate([q2_b[r * QB:(r + 1) * QB], (bias - shift[r * QB:(r + 1) * QB]).astype(bf16)], axis=1)
             for r in range(R)], axis=0)

    def key_tile(ref, kt):
        return ref[pl.ds(pl.multiple_of(kt * KT, KT), KT), :]

    @pl.when(jnp.max(bound - s_self) > SAFE_SPAN)
    def _():
        qp0 = shifted_queries(jnp.zeros_like(bound))

        def lane_max(kt, sk):
            mx = m_sc[...]
            for c in range(KT // LANES):
                mx = jnp.maximum(mx, sk[:, c * LANES:(c + 1) * LANES])
            m_sc[...] = mx

        def max_pass(kt, carry):
            lane_max(kt, _mm_nt(qp0, key_tile(ksk_ref, kt)))
            return carry

        m_sc[...] = jnp.full(m_sc.shape, NEG, f32)
        lax.fori_loop(0, last, max_pass, 0)
        lane_max(last, _mm_nt(qp0, key_tile(ksk_ref, last)) + diag_bias)
        shift_sc[...] = jnp.max(m_sc[...], axis=-1, keepdims=True)

    qp = shifted_queries(shift_sc[...])
    acc_sc[...] = jnp.zeros_like(acc_sc)

    def accumulate(kt, s2):
        acc_sc[...] += _mm(jnp.exp2(s2).astype(bf16), key_tile(ksv_ref, kt))

    def body(kt, carry):
        accumulate(kt, _mm_nt(qp, key_tile(ksk_ref, kt)))
        return carry

    lax.fori_loop(0, last, body, 0)
    accumulate(last, _mm_nt(qp, key_tile(ksk_ref, last)) + diag_bias)
    acc = acc_sc[...]
    o_s = acc[:, 0:D] * (1.0 / acc[:, D:D + 1])

    wlen = WINDOW + QB
    wstart = pl.multiple_of(jnp.maximum(qb * QB - WINDOW, 0), int(np.gcd(QB, WINDOW)))
    sw = _mm_nt(q2_b, kwk_ref[pl.ds(wstart, wlen), :])
    diff = pos - (wstart + lax.broadcasted_iota(i32, (1, wlen), 1))
    e_w, _ = _biased_exp2_rows(sw, head_bias((diff >= 0) & (diff <= WINDOW)))
    acc_w = _mm(e_w.astype(bf16), kwv_ref[pl.ds(wstart, wlen), :])
    o_w = acc_w[:, 0:D] * (1.0 / acc_w[:, D:D + 1])

    gate = _sigmoid(_mm(_rms(x_ref[...], gmix_ref[...]).astype(bf16), wgt_ref[...]))
    for r in range(R):
        rs = slice(r * QB, (r + 1) * QB)
        o_ref[:, r * D:(r + 1) * D] = (gate[:, r:r + 1] * o_c[rs] + gate[:, R + r:R + r + 1] * o_s[rs]
                                       + gate[:, 2 * R + r:2 * R + r + 1] * o_w[rs])


def _nsa_prompt(x, g_mix, w_gate_g, u, q_col_block, kvc, ksk, ksv, kwk, kwv, kn2, tq):
    T, Dm = x.shape
    G, R, D, QB = NSA_KV_HEADS, NSA_GROUP, NSA_DH, tq
    ns = kvc.shape[0]
    n_sel = (T - 1) // SEL_BLOCK + 1
    assert T % SEL_KT == 0 and T % QB == 0 and T >= WINDOW + QB and n_sel <= LANES
    rows = R * QB
    res = lambda w: pl.BlockSpec((None, T, w), lambda g, i: (g, 0, 0))
    return pl.pallas_call(
        functools.partial(_nsa_prompt_kernel, n_sel=n_sel),
        grid=(G, T // QB),
        in_specs=[pl.BlockSpec((QB, Dm), lambda g, i: (i, 0)),
                  pl.BlockSpec((1, Dm), lambda g, i: (0, 0)),
                  pl.BlockSpec((None, Dm, LANES), lambda g, i: (g, 0, 0)),
                  pl.BlockSpec((QB, R * D), lambda g, i: (i, q_col_block + g)),
                  pl.BlockSpec((ns, D), lambda g, i: (0, g)),
                  pl.BlockSpec((ns, D), lambda g, i: (0, G + g)),
                  res(D + LANES), res(D + LANES), res(D), res(D + LANES),
                  pl.BlockSpec((None, SUBLANES, LANES), lambda g, i: (g, 0, 0))],
        out_specs=pl.BlockSpec((QB, R * D), lambda g, i: (i, g)),
        out_shape=jax.ShapeDtypeStruct((T, NSA_DIM), f32),
        scratch_shapes=[pltpu.VMEM((rows, 1), f32), pltpu.VMEM((rows, LANES), f32),
                        pltpu.VMEM((rows, D + LANES), f32)],
        compiler_params=_cparams("parallel", "arbitrary"),
        name="nsa_prompt",
    )(x, g_mix.reshape(1, Dm), w_gate_g, u, kvc, kvc, ksk, ksv, kwk, kwv, kn2)


def _group_gate_weights(w_gate):
    Dm = w_gate.shape[0]
    G, R = NSA_KV_HEADS, NSA_GROUP
    w = w_gate.reshape(Dm, 3, G, R).transpose(2, 0, 1, 3).reshape(G, Dm, 3 * R)
    return jnp.pad(w, ((0, 0), (0, 0), (0, LANES - 3 * R)))


def _nsa_sample_cmp_kernel(q_ref, kvc_ref, oc_ref, idx_ref, *, pos, n_sel, sel_w):
    H, R, D, G = NSA_HEADS, NSA_GROUP, NSA_DH, NSA_KV_HEADS
    S, ns = kvc_ref.shape[0], kvc_ref.shape[1]
    head = lax.broadcasted_iota(i32, (H, 1), 0)
    grp = _div_pow2(head, R)
    j = lax.broadcasted_iota(i32, (1, ns), 1)
    imps = []
    for b in range(S):
        qs_b = (q_ref[b] * (D ** -0.5)).astype(bf16)
        s = jnp.zeros((H, ns), f32)
        for g in range(G):
            s = jnp.where(grp == g, _mm_nt(qs_b, kvc_ref[b, :, g * D:(g + 1) * D].astype(bf16)), s)
        e, den = _masked_softmax_rows(s, j * CMP_STRIDE + (CMP_BLOCK - 1) <= pos)
        p_c = e / den
        p_b = p_c.astype(bf16)
        o_c = jnp.zeros((H, D), f32)
        imp = jnp.zeros((H, ns), f32)
        for g in range(G):
            o_c = jnp.where(grp == g, _mm(p_b, kvc_ref[b, :, (G + g) * D:(G + g + 1) * D].astype(bf16)), o_c)
            imp = jnp.where(grp == g, jnp.sum(jnp.where(grp == g, p_c, 0.0), axis=0, keepdims=True), imp)
        oc_ref[b] = o_c
        imps.append(imp)
    imp = jnp.concatenate(imps, axis=0)
    ratio = SEL_BLOCK // CMP_STRIDE
    off = CMP_BLOCK // CMP_STRIDE - 1
    nn = lax.broadcasted_iota(i32, (sel_w, ns), 0)
    jj = lax.broadcasted_iota(i32, (sel_w, ns), 1)
    overlap_t = jnp.where((jj >= ratio * nn - off) & (jj < ratio * nn + ratio), 1.0, 0.0).astype(bf16)
    imp_sel = sum(_mm_nt(overlap_t, part) for part in _split3(imp))
    n_idx = lax.broadcasted_iota(i32, (sel_w, S * H), 0)
    cur = pos // SEL_BLOCK
    forced = (n_idx == 0) | (n_idx == cur) | (n_idx == cur - 1)
    score = jnp.where(forced, FORCE, jnp.where(n_idx <= cur, imp_sel, NEG))
    score = jnp.where(n_idx < n_sel, score, REMOVED)
    _, firsts = _top_blocks(score, SEL_TOPN, axis=0)
    idx_ref[...] = jnp.concatenate(firsts, axis=0).astype(i32)


def _nsa_sample_cmp(q, kvc, pos):
    B, H, D = q.shape
    ns = kvc.shape[1]
    n_sel = pos // SEL_BLOCK + 1
    assert n_sel >= SEL_TOPN
    sel_w = -(-n_sel // SUBLANES) * SUBLANES
    S = 8 if B % 8 == 0 else 1
    o_c, idx = pl.pallas_call(
        functools.partial(_nsa_sample_cmp_kernel, pos=pos, n_sel=n_sel, sel_w=sel_w),
        grid=(B // S,),
        in_specs=[pl.BlockSpec((S, H, D), lambda b: (b, 0, 0)),
                  pl.BlockSpec((S, ns, KV_ROW), lambda b: (b, 0, 0))],
        out_specs=[pl.BlockSpec((S, H, D), lambda b: (b, 0, 0)),
                   pl.BlockSpec((None, SEL_TOPN, S * H), lambda b: (b, 0, 0))],
        out_shape=[jax.ShapeDtypeStruct((B, H, D), f32), jax.ShapeDtypeStruct((B // S, SEL_TOPN, S * H), i32)],
        compiler_params=_cparams("parallel"),
        name="nsa_sample_cmp",
    )(q, kvc)
    idx = idx.reshape(B // S, SEL_TOPN, S, H)[..., ::NSA_GROUP]
    return o_c, idx.transpose(0, 2, 3, 1).reshape(B, NSA_KV_HEADS, SEL_TOPN)


def _nsa_sample_sel_kernel(idx_ref, pt_ref, q_ref, new_ref, *refs, past):
    H, R, D, G = NSA_HEADS, NSA_GROUP, NSA_DH, NSA_KV_HEADS
    blocks, o_ref = refs[:G * SEL_TOPN], refs[G * SEL_TOPN]
    b = pl.program_id(0)
    qs_b = (q_ref[...] * (D ** -0.5)).astype(bf16)
    grp = _div_pow2(lax.broadcasted_iota(i32, (H, 1), 0), R)
    per_tile = SUBLANES // KV_CHUNKS
    half = SEL_BLOCK // per_tile
    o = jnp.zeros((H, D), f32)
    for g in range(G):
        ks, vs = [], []
        lane = lax.broadcasted_iota(i32, (1, SEL_TOPN * SEL_BLOCK), 1)
        slot = _div_pow2(lane, SEL_BLOCK)
        in_slot = lane & (SEL_BLOCK - 1)
        pk_row = (in_slot & (half - 1)) * per_tile + _div_pow2(in_slot, half)
        for i in range(SEL_TOPN):
            blk = blocks[g * SEL_TOPN + i]
            base = idx_ref[(b * G + g) * SEL_TOPN + i] * SEL_BLOCK
            pk_row = pk_row + jnp.where(slot == i, base, 0)
            for par in range(per_tile):
                pk_col = base + per_tile * lax.broadcasted_iota(i32, (half, 1), 0) + par
                old = pk_col < past
                k = blk[pl.ds(par * KV_CHUNKS + g, half, stride=SUBLANES), :]
                v = blk[pl.ds(par * KV_CHUNKS + G + g, half, stride=SUBLANES), :]
                ks.append(jnp.where(old, k, new_ref[g:g + 1, :]).astype(bf16))
                vs.append(jnp.where(old, v, new_ref[G + g:G + g + 1, :]).astype(bf16))
        s = _mm_nt(qs_b, jnp.concatenate(ks, axis=0))
        e, den = _masked_softmax_rows(s, pk_row <= past)
        o = jnp.where(grp == g, _mm((e / den).astype(bf16), jnp.concatenate(vs, axis=0)), o)
    o_ref[...] = o


def _nsa_sample_sel(q, ks_new, pool, page_table, idx, past):
    B, H, D = q.shape
    G = NSA_KV_HEADS
    n_pages = page_table.shape[1]
    halves = PAGE_SIZE // SEL_BLOCK
    blk_rows = SEL_BLOCK * KV_CHUNKS
    pool_h = pool.reshape(pool.shape[0] * halves, blk_rows, D)
    last_old = past // SEL_BLOCK - 1

    def blk_map(b, idx_r, pt_r, g, i):
        blk = jnp.minimum(idx_r[(b * G + g) * SEL_TOPN + i], last_old)
        return (pt_r[b * n_pages + blk // halves] * halves + blk % halves, 0, 0)

    specs = [pl.BlockSpec((None, blk_rows, D), functools.partial(blk_map, g=g, i=i))
             for g in range(G) for i in range(SEL_TOPN)]
    gs = pltpu.PrefetchScalarGridSpec(
        num_scalar_prefetch=2, grid=(B,),
        in_specs=[pl.BlockSpec((None, H, D), lambda b, *_: (b, 0, 0)),
                  pl.BlockSpec((None, KV_CHUNKS, D), lambda b, *_: (b, 0, 0))] + specs,
        out_specs=pl.BlockSpec((None, H, D), lambda b, *_: (b, 0, 0)))
    return pl.pallas_call(
        functools.partial(_nsa_sample_sel_kernel, past=past),
        grid_spec=gs,
        out_shape=jax.ShapeDtypeStruct((B, H, D), f32),
        compiler_params=_cparams("arbitrary"),
        name="nsa_sample_sel",
    )(idx.reshape(-1), page_table.reshape(-1), q, ks_new, *([pool_h] * (G * SEL_TOPN)))


def _nsa_sample_win_kernel(q_ref, buf_ref, new_ref, oc_ref, os_ref, gt_ref, o_ref, win_ref, *, past):
    H, R, D, G = NSA_HEADS, NSA_GROUP, NSA_DH, NSA_KV_HEADS
    rows = buf_ref.shape[0]
    Lb = rows // KV_CHUNKS
    per_tile = SUBLANES // KV_CHUNKS
    half = Lb // per_tile
    qs = q_ref[...] * (D ** -0.5)
    qs_b = qs.astype(bf16)
    grp = _div_pow2(lax.broadcasted_iota(i32, (H, 1), 0), R)
    lane = lax.broadcasted_iota(i32, (1, Lb), 1)
    key_pos = past - Lb + (lane & (half - 1)) * per_tile + _div_pow2(lane, half)
    diff = past - key_pos
    valid = (diff >= 0) & (diff <= WINDOW)
    o_w = jnp.zeros((H, D), f32)
    for g in range(G):
        chunk = lambda c: jnp.concatenate(
            [buf_ref[pl.ds(par * KV_CHUNKS + c, half, stride=SUBLANES), :] for par in range(per_tile)], axis=0)
        new_k, new_v = new_ref[g:g + 1, :], new_ref[G + g:G + g + 1, :]
        s_b = jnp.where(valid, _mm_nt(qs_b, chunk(g).astype(bf16)), NEG)
        s_n = jnp.sum(qs * new_k, axis=-1, keepdims=True)
        mx = jnp.maximum(jnp.max(s_b, axis=-1, keepdims=True), s_n)
        e_b = jnp.where(valid, jnp.exp(s_b - mx), 0.0)
        e_n = jnp.exp(s_n - mx)
        den = jnp.sum(e_b, axis=-1, keepdims=True) + e_n
        og = _mm((e_b / den).astype(bf16), chunk(G + g).astype(bf16)) + (e_n / den) * new_v
        o_w = jnp.where(grp == g, og, o_w)
    gate = _sigmoid(gt_ref[...])
    o_ref[...] = gate[:, 0:1] * oc_ref[...] + gate[:, 1:2] * os_ref[...] + gate[:, 2:3] * o_w
    win_ref[0:rows - KV_CHUNKS, :] = buf_ref[KV_CHUNKS:rows, :]
    win_ref[rows - KV_CHUNKS:rows, :] = new_ref[...]


def _nsa_sample_win(q, win_buf, kw_new, o_c, o_s, gt, past):
    B, H, D = q.shape
    rows = win_buf.shape[1]
    assert rows == WINDOW * KV_CHUNKS
    per_b = lambda *s: pl.BlockSpec((None,) + s, lambda b: (b,) + (0,) * len(s))
    return pl.pallas_call(
        functools.partial(_nsa_sample_win_kernel, past=past),
        grid=(B,),
        in_specs=[per_b(H, D), per_b(rows, D), per_b(KV_CHUNKS, D), per_b(H, D), per_b(H, D), per_b(H, 3)],
        out_specs=[per_b(H, D), per_b(rows, D)],
        out_shape=[jax.ShapeDtypeStruct((B, H, D), f32), jax.ShapeDtypeStruct((B, rows, D), f32)],
        compiler_params=_cparams("parallel"),
        name="nsa_sample_win",
    )(q, win_buf, kw_new, o_c, o_s, gt.reshape(B, 3, H).transpose(0, 2, 1))


def kernel(x_prompt, x_sample, cache_cmp_kv, cache_sel_kv, cache_win_kv, cache_mem_kv, state_mlstm_c, state_mlstm_n, state_mlstm_m, state_conv, page_table, mem_prompt, g_mix, w_in, w_conv, b_conv, w_mq, w_mk, w_mv, w_mgate, b_mgate, g_mhead, cmp_pe, cmp_wpos, cmp_wphi, w_out, g_memx, g_mems, w_mem_q, w_mem_kv, w_mem_o, g_ffn, w_ff1, w_ff2, g_final):
    depth = w_in.shape[0]
    assert depth == 1 and x_prompt.shape[0] == 1 and x_sample.shape[1] == 1
    T, Dm = x_prompt.shape[1:]
    B = x_sample.shape[0]
    G, Dh, H = NSA_KV_HEADS, NSA_DH, ML_HEADS
    past = page_table.shape[1] * PAGE_SIZE
    assert (past + 1) // CMP_STRIDE == past // CMP_STRIDE
    n_main = 2 * ML_DIM + NSA_DIM + 3 * KV_ROW
    n_gate = 3 * NSA_HEADS
    l = 0
    hp = x_prompt.reshape(T, Dm)
    hs = x_sample.reshape(B, Dm)

    wbd = _blockdiag_weights(jnp.stack([w_mq[l], w_mk[l], w_mv[l]])).astype(bf16)
    w_mg = w_mgate[l].astype(bf16)
    w_mg_t = w_mg.T
    pe_t, wp_t = _compress_tables(cmp_pe[l], cmp_wpos[l])
    wb_out, wb_mq, wb_mkv, wb_mo, wb_f1, wb_f2 = (
        w[l].astype(bf16) for w in (w_out, w_mem_q, w_mem_kv, w_mem_o, w_ff1, w_ff2))
    wb_in = w_in[l, :, :n_main].astype(bf16)
    wb_gate = w_in[l, :, n_main:].astype(bf16)
    w_gt = jnp.pad(wb_gate, ((0, 0), (0, LANES - n_gate)))
    q_cb = 2 * ML_DIM // KV_ROW
    c_cb = (2 * ML_DIM + NSA_DIM) // KV_ROW
    n_mq, n_mkv = w_mem_q.shape[-1], w_mem_kv.shape[-1]
    tm_in, tm, tn, tf, tq, ml_chunk = 1024, 512, 512, 1024, 256, 256

    u, (kc_lin, ks_lin, kw_lin) = _in_proj(hp, g_mix[l], wb_in, n_main, c_cb, 3, tm_in, "in_proj_p")
    ml_o, c_p, n_p, m_p = _mlstm_prompt(u, w_conv[l], b_conv[l], wbd, w_mg, w_mg_t, b_mgate[l],
                                        g_mhead[l], ml_chunk, "mlstm_prompt")
    kvc_p = _compress_prompt(u, c_cb, pe_t, wp_t, cmp_wphi[l])
    nsa_o = _nsa_prompt(hp, g_mix[l], _group_gate_weights(wb_gate), u, q_cb, kvc_p,
                        *_kv_prep(u, c_cb + 1, c_cb + 2), tq)
    h1 = _out_proj(ml_o, nsa_o, wb_out, hp, tm, Dm, "out_proj_p")
    mem_kv = _norm_matmul(mem_prompt.reshape(-1, Dm), g_mems[l], wb_mkv, n_mkv, mem_prompt.shape[1], tn, "mem_kv")
    h2 = _mem_prompt(h1, g_memx[l], wb_mq, mem_kv, wb_mo, tm, "mem_attn_p")
    y_p = _ffn_final(h2, g_ffn[l], wb_f1, wb_f2, g_final, tm, tf, "ffn_p")

    kv_shape = lambda a: a.reshape(1, 1, -1, 2, G, Dh)
    out_p = (y_p.reshape(1, T, Dm),
             kv_shape(kc_lin), kv_shape(ks_lin),
             kv_shape(kw_lin[(T - min(WINDOW, T)) * KV_CHUNKS:]),
             mem_kv.reshape(1, 1, -1, 2, MEM_HEADS, MEM_DH),
             c_p[None, None], n_p[None, None], m_p[None, None],
             u[T - (ML_CONV - 1):, :ML_DIM][None, None])

    us = _norm_matmul(hs, g_mix[l], wb_in, n_main, B, tn, "in_proj_s")
    gts = _norm_matmul(hs, g_mix[l], w_gt, LANES, B, LANES, "in_gate_s")[:, :n_gate]
    xm_s, og_s = us[:, :ML_DIM], us[:, ML_DIM:2 * ML_DIM]
    q_s = us[:, 2 * ML_DIM:2 * ML_DIM + NSA_DIM].reshape(B, NSA_HEADS, Dh)
    kc_s, ks_s, kw_s = (us[:, (c_cb + i) * KV_ROW:(c_cb + i + 1) * KV_ROW] for i in range(3))
    ml_os, c_s, n_s, m_s = _mlstm_sample(xm_s, og_s, state_conv[l], state_mlstm_c[l], state_mlstm_n[l],
                                         state_mlstm_m[l], w_conv[l], b_conv[l], wbd,
                                         w_mg, b_mgate[l], g_mhead[l])
    pool_c = cache_cmp_kv.reshape(-1, PAGE_SIZE * KV_CHUNKS, Dh)
    pool_s = cache_sel_kv.reshape(-1, PAGE_SIZE * KV_CHUNKS, Dh)
    win_buf = cache_win_kv.reshape(B, -1, Dh)
    mem_buf = cache_mem_kv.reshape(B, -1, MEM_DH)
    kvc_s = _compress_paged(pool_c, page_table, pe_t, wp_t, cmp_wphi[l])
    o_cs, idx = _nsa_sample_cmp(q_s, kvc_s, past)
    o_ss = _nsa_sample_sel(q_s, ks_s.reshape(B, KV_CHUNKS, Dh), pool_s, page_table, idx, past)
    nsa_os, win_new = _nsa_sample_win(q_s, win_buf, kw_s.reshape(B, KV_CHUNKS, Dh), o_cs, o_ss, gts, past)
    h1s = _out_proj(ml_os, nsa_os.reshape(B, NSA_DIM), wb_out, hs, B, Dm, "out_proj_s")
    qm_s = _norm_matmul(h1s, g_memx[l], wb_mq, n_mq, B, n_mq, "mem_q_s")
    om_s = _mem_sample(qm_s, mem_buf, "mem_attn_s")
    h2s = _matmul_res(om_s, wb_mo, h1s, B, Dm, "mem_o_s")
    y_s = _ffn_final(h2s, g_ffn[l], wb_f1, wb_f2, g_final, B, tf, "ffn_s")

    kv_s_shape = lambda a: a.reshape(1, B, 1, 2, G, Dh)
    conv_s = jnp.concatenate([state_conv[l][:, 1:], xm_s[:, None, :]], axis=1)
    out_s = (y_s.reshape(B, 1, Dm), kv_s_shape(kc_s), kv_s_shape(ks_s),
             win_new.reshape(1, B, -1, 2, G, Dh), c_s[None], n_s[None], m_s[None], conv_s[None])

    return (out_p[0], out_s[0]) + out_p[1:] + out_s[1:]
```

```python
import functools

import jax
import jax.numpy as jnp
import numpy as np
from jax import lax
from jax.experimental import pallas as pl
from jax.experimental.pallas import tpu as pltpu

f32 = jnp.float32
bf16 = jnp.bfloat16
i32 = jnp.int32

EPS = 1e-6
NEG = -1e30
FORCE = 1e30
ML_HEADS = 8
ML_DH = 128
ML_DIM = ML_HEADS * ML_DH
ML_CONV = 4
ML_QKV_BLOCK = 4
NSA_HEADS = 8
NSA_KV_HEADS = 2
NSA_GROUP = NSA_HEADS // NSA_KV_HEADS
NSA_DH = 128
NSA_DIM = NSA_HEADS * NSA_DH
NSA_KV_W = NSA_KV_HEADS * NSA_DH
CMP_BLOCK = 32
CMP_STRIDE = 16
SEL_BLOCK = 64
SEL_TOPN = 16
WINDOW = 512
PAGE_SIZE = 128
MEM_HEADS = 4
MEM_DH = 128
KV_ROW = 2 * NSA_KV_W
KV_CHUNKS = KV_ROW // NSA_DH

LANES = 128
SUBLANES = 8
VMEM_LIMIT = 56 * 1024 * 1024

MASK_BIAS = -1e9
REMOVED = -3.0e38
SEL_KT = 1024
LOG2E = 1.4426950408889634
SAFE_SPAN = 100.0


def _cparams(*sem):
    return pltpu.CompilerParams(dimension_semantics=sem, vmem_limit_bytes=VMEM_LIMIT)


def _mm(a, b):
    return jnp.dot(a, b, preferred_element_type=f32)


def _mm_nt(a, b):
    return lax.dot_general(a, b, (((1,), (1,)), ((), ())), preferred_element_type=f32)


def _split2(x):
    h = x.astype(bf16)
    return h, (x - h.astype(f32)).astype(bf16)


def _split3(x):
    h = x.astype(bf16)
    r = x - h.astype(f32)
    m = r.astype(bf16)
    return h, m, (r - m.astype(f32)).astype(bf16)


def _dot3(a, b, mm=_mm):
    ah, al = _split2(a)
    bh, bl = _split2(b)
    return mm(ah, bh) + mm(al, bh) + mm(ah, bl)


def _dot_sel_l(sel, x):
    h, m, l = _split3(x)
    return _mm(sel, h) + _mm(sel, m) + _mm(sel, l)


def _dot_sel_r(x, sel):
    h, m, l = _split3(x)
    return _mm(h, sel) + _mm(m, sel) + _mm(l, sel)


def _rms(x, g):
    return x * lax.rsqrt(jnp.mean(x * x, axis=-1, keepdims=True) + EPS) * g


def _div_pow2(x, d):
    assert d & (d - 1) == 0
    return lax.shift_right_logical(x, jnp.full(x.shape, d.bit_length() - 1, x.dtype))


def _sigmoid(x):
    return 0.5 * jnp.tanh(0.5 * x) + 0.5


def _log_sigmoid(x):
    return jnp.minimum(x, 0.0) - jnp.log(1.0 + jnp.exp(-jnp.abs(x)))


def _masked_softmax_rows(s, valid):
    s = jnp.where(valid, s, NEG)
    mx = jnp.max(s, axis=-1, keepdims=True)
    e = jnp.where(valid, jnp.exp(s - mx), 0.0)
    den = jnp.sum(e, axis=-1, keepdims=True)
    return e, jnp.where(den > 0.0, den, 1.0)


def _biased_exp2_rows(s2, bias):
    s2 = s2 + bias
    mx = jnp.max(s2, axis=-1, keepdims=True)
    return jnp.exp2(s2 - mx), mx


def _biased_softmax2_rows(s2, bias):
    e, mx = _biased_exp2_rows(s2, bias)
    inv = jnp.where(mx > 0.5 * NEG, 1.0 / jnp.sum(e, axis=-1, keepdims=True), 0.0)
    return e * inv


def _norm_matmul_kernel(x_ref, g_ref, w_ref, o_ref, xn_ref):
    @pl.when(pl.program_id(1) == 0)
    def _():
        xn_ref[...] = _rms(x_ref[...], g_ref[...]).astype(bf16)

    o_ref[...] = _mm(xn_ref[...], w_ref[...])


def _norm_matmul(x, g, w, n_cols, tm, tn, name):
    M, K = x.shape
    return pl.pallas_call(
        _norm_matmul_kernel,
        grid=(M // tm, n_cols // tn),
        in_specs=[pl.BlockSpec((tm, K), lambda i, j: (i, 0)),
                  pl.BlockSpec((1, K), lambda i, j: (0, 0)),
                  pl.BlockSpec((K, tn), lambda i, j: (0, j))],
        out_specs=pl.BlockSpec((tm, tn), lambda i, j: (i, j)),
        out_shape=jax.ShapeDtypeStruct((M, n_cols), f32),
        scratch_shapes=[pltpu.VMEM((tm, K), bf16)],
        compiler_params=_cparams("parallel", "arbitrary"),
        name=name,
    )(x, g.reshape(1, K), w)


def _in_proj_kernel(x_ref, g_ref, w_ref, o_ref, *rest, kv_block0):
    lin_refs, xn_ref = rest[:-1], rest[-1]
    j = pl.program_id(1)

    @pl.when(j == 0)
    def _():
        xn_ref[...] = _rms(x_ref[...], g_ref[...]).astype(bf16)

    val = _mm(xn_ref[...], w_ref[...])
    o_ref[...] = val
    rows = val.shape[0]
    for i, ref in enumerate(lin_refs):
        @pl.when(j == kv_block0 + i)
        def _():
            for c in range(KV_CHUNKS):
                ref[pl.ds(c, rows, stride=KV_CHUNKS), :] = val[:, c * LANES:(c + 1) * LANES]


def _in_proj(x, g, w, n_cols, kv_block0, n_kv, tm, name):
    M, K = x.shape
    tn = KV_ROW
    lin_spec = pl.BlockSpec((tm * KV_CHUNKS, LANES), lambda i, j: (i, 0))
    outs = pl.pallas_call(
        functools.partial(_in_proj_kernel, kv_block0=kv_block0),
        grid=(M // tm, n_cols // tn),
        in_specs=[pl.BlockSpec((tm, K), lambda i, j: (i, 0)),
                  pl.BlockSpec((1, K), lambda i, j: (0, 0)),
                  pl.BlockSpec((K, tn), lambda i, j: (0, j))],
        out_specs=[pl.BlockSpec((tm, tn), lambda i, j: (i, j))] + [lin_spec] * n_kv,
        out_shape=[jax.ShapeDtypeStruct((M, n_cols), f32)]
        + [jax.ShapeDtypeStruct((M * KV_CHUNKS, LANES), f32)] * n_kv,
        scratch_shapes=[pltpu.VMEM((tm, K), bf16)],
        compiler_params=_cparams("parallel", "arbitrary"),
        name=name,
    )(x, g.reshape(1, K), w)
    return outs[0], outs[1:]


def _out_proj_kernel(a1_ref, a2_ref, w1_ref, w2_ref, r_ref, o_ref):
    o_ref[...] = (r_ref[...] + _mm(a1_ref[...].astype(bf16), w1_ref[...])
                  + _mm(a2_ref[...].astype(bf16), w2_ref[...]))


def _out_proj(a1, a2, w, res, tm, tn, name):
    M, K1 = a1.shape
    K2 = a2.shape[1]
    assert K1 == K2 and w.shape[0] == K1 + K2
    N = w.shape[1]
    return pl.pallas_call(
        _out_proj_kernel,
        grid=(M // tm, N // tn),
        in_specs=[pl.BlockSpec((tm, K1), lambda i, j: (i, 0)),
                  pl.BlockSpec((tm, K2), lambda i, j: (i, 0)),
                  pl.BlockSpec((K1, tn), lambda i, j: (0, j)),
                  pl.BlockSpec((K2, tn), lambda i, j: (1, j)),
                  pl.BlockSpec((tm, tn), lambda i, j: (i, j))],
        out_specs=pl.BlockSpec((tm, tn), lambda i, j: (i, j)),
        out_shape=jax.ShapeDtypeStruct((M, N), f32),
        compiler_params=_cparams("parallel", "arbitrary"),
        name=name,
    )(a1, a2, w, w, res)


def _matmul_res_kernel(a_ref, w_ref, r_ref, o_ref):
    o_ref[...] = r_ref[...] + _mm(a_ref[...].astype(bf16), w_ref[...])


def _matmul_res(a, w, res, tm, tn, name):
    M, K = a.shape
    N = w.shape[1]
    return pl.pallas_call(
        _matmul_res_kernel,
        grid=(M // tm, N // tn),
        in_specs=[pl.BlockSpec((tm, K), lambda i, j: (i, 0)),
                  pl.BlockSpec((K, tn), lambda i, j: (0, j)),
                  pl.BlockSpec((tm, tn), lambda i, j: (i, j))],
        out_specs=pl.BlockSpec((tm, tn), lambda i, j: (i, j)),
        out_shape=jax.ShapeDtypeStruct((M, N), f32),
        compiler_params=_cparams("parallel", "arbitrary"),
        name=name,
    )(a, w, res)


def _ffn_kernel(h_ref, g_ref, w1_ref, w2_ref, gf_ref, y_ref, xn_ref, acc_ref):
    f = pl.program_id(1)

    @pl.when(f == 0)
    def _():
        xn_ref[...] = _rms(h_ref[...], g_ref[...]).astype(bf16)
        acc_ref[...] = jnp.zeros_like(acc_ref)

    a = _mm(xn_ref[...], w1_ref[...])
    a = jnp.square(jnp.maximum(a, 0.0))
    acc_ref[...] += _mm(a.astype(bf16), w2_ref[...])

    @pl.when(f == pl.num_programs(1) - 1)
    def _():
        y_ref[...] = _rms(h_ref[...] + acc_ref[...], gf_ref[...])


def _ffn_final(h, g, w1, w2, g_final, tm, tf, name):
    M, D = h.shape
    F = w1.shape[1]
    return pl.pallas_call(
        _ffn_kernel,
        grid=(M // tm, F // tf),
        in_specs=[pl.BlockSpec((tm, D), lambda i, j: (i, 0)),
                  pl.BlockSpec((1, D), lambda i, j: (0, 0)),
                  pl.BlockSpec((D, tf), lambda i, j: (0, j)),
                  pl.BlockSpec((tf, D), lambda i, j: (j, 0)),
                  pl.BlockSpec((1, D), lambda i, j: (0, 0))],
        out_specs=pl.BlockSpec((tm, D), lambda i, j: (i, 0)),
        out_shape=jax.ShapeDtypeStruct((M, D), f32),
        scratch_shapes=[pltpu.VMEM((tm, D), bf16), pltpu.VMEM((tm, D), f32)],
        compiler_params=_cparams("parallel", "arbitrary"),
        name=name,
    )(h, g.reshape(1, D), w1, w2, g_final.reshape(1, D))


def _mem_prompt_kernel(h_ref, g_ref, wq_ref, k_ref, v_ref, wo_ref, o_ref):
    h = h_ref[...]
    xn = _rms(h, g_ref[...]).astype(bf16)
    q = _mm(xn, wq_ref[...]) * (MEM_DH ** -0.5)
    outs = []
    for hd in range(MEM_HEADS):
        sl = slice(hd * MEM_DH, (hd + 1) * MEM_DH)
        s = _mm_nt(q[:, sl].astype(bf16), k_ref[:, sl].astype(bf16))
        e = jnp.exp(s - jnp.max(s, axis=-1, keepdims=True))
        p = e * (1.0 / jnp.sum(e, axis=-1, keepdims=True))
        outs.append(_mm(p.astype(bf16), v_ref[:, sl].astype(bf16)))
    o = jnp.concatenate(outs, axis=-1)
    o_ref[...] = h + _mm(o.astype(bf16), wo_ref[...])


def _mem_prompt(h, g, wq, mem_kv, wo, tm, name):
    M, D = h.shape
    HD = MEM_HEADS * MEM_DH
    ML = mem_kv.shape[0]
    return pl.pallas_call(
        _mem_prompt_kernel,
        grid=(M // tm,),
        in_specs=[pl.BlockSpec((tm, D), lambda i: (i, 0)),
                  pl.BlockSpec((1, D), lambda i: (0, 0)),
                  pl.BlockSpec((D, HD), lambda i: (0, 0)),
                  pl.BlockSpec((ML, HD), lambda i: (0, 0)),
                  pl.BlockSpec((ML, HD), lambda i: (0, 1)),
                  pl.BlockSpec((HD, D), lambda i: (0, 0))],
        out_specs=pl.BlockSpec((tm, D), lambda i: (i, 0)),
        out_shape=jax.ShapeDtypeStruct((M, D), f32),
        compiler_params=_cparams("parallel"),
        name=name,
    )(h, g.reshape(1, D), wq, mem_kv, mem_kv, wo)


def _mem_sample_kernel(q_ref, kv_ref, o_ref):
    per_tok = 2 * MEM_HEADS
    ML = kv_ref.shape[1] // per_tok
    for b in range(kv_ref.shape[0]):
        q = q_ref[b] * (MEM_DH ** -0.5)
        outs = []
        for hd in range(MEM_HEADS):
            sl = slice(hd * MEM_DH, (hd + 1) * MEM_DH)
            k = kv_ref[b, pl.ds(hd, ML, stride=per_tok), :]
            v = kv_ref[b, pl.ds(MEM_HEADS + hd, ML, stride=per_tok), :]
            s = jnp.sum(k * q[:, sl], axis=-1, keepdims=True)
            e = jnp.exp(s - jnp.max(s, axis=0, keepdims=True))
            p = e / jnp.sum(e, axis=0, keepdims=True)
            outs.append(jnp.sum(p * v, axis=0, keepdims=True))
        o_ref[b] = jnp.concatenate(outs, axis=-1)


def _mem_sample(q, kv, name):
    B, HD = q.shape
    rows = kv.shape[1]
    S = 4 if B % 4 == 0 else 1
    out = pl.pallas_call(
        _mem_sample_kernel,
        grid=(B // S,),
        in_specs=[pl.BlockSpec((S, 1, HD), lambda b: (b, 0, 0)),
                  pl.BlockSpec((S, rows, MEM_DH), lambda b: (b, 0, 0))],
        out_specs=pl.BlockSpec((S, 1, HD), lambda b: (b, 0, 0)),
        out_shape=jax.ShapeDtypeStruct((B, 1, HD), f32),
        compiler_params=_cparams("parallel"),
        name=name,
    )(q.reshape(B, 1, HD), kv)
    return out.reshape(B, HD)


def _blockdiag_weights(w):
    n, b = w.shape[0], ML_QKV_BLOCK
    per = LANES // b
    wc = w.reshape(n, w.shape[1] // per, per, b * b)
    r, k = np.arange(LANES), np.arange(b * b)
    expand = (r[:, None] // b == np.arange(per)[None, :]).astype(np.float32)
    row_sel = (r[:, None] % b == k[None, :] // b).astype(np.float32)
    col_sel = (k[:, None] % b == r[None, :] % b).astype(np.float32)
    same_blk = (r[:, None] // b == r[None, :] // b).astype(np.float32)
    rows = jnp.einsum('rb,scbk->scrk', expand, wc) * row_sel
    return jnp.einsum('scrk,kq->scrq', rows, col_sel) * same_blk


def _ml_qkv_gates(xc, xm, wbd_ref, wg_ref, bg_ref):
    xc_b, xm_b = xc.astype(bf16), xm.astype(bf16)
    qs, ks, vs = [], [], []
    for c in range(ML_DIM // LANES):
        sl = slice(c * LANES, (c + 1) * LANES)
        qs.append(_mm(xc_b[:, sl], wbd_ref[0, c]))
        ks.append(_mm(xc_b[:, sl], wbd_ref[1, c]) * (ML_DH ** -0.5))
        vs.append(_mm(xm_b[:, sl], wbd_ref[2, c]))
    q = jnp.concatenate(qs, axis=-1)
    k = jnp.concatenate(ks, axis=-1)
    v = jnp.concatenate(vs, axis=-1)
    qkv_b = jnp.concatenate([q, k, v], axis=-1).astype(bf16)
    gates = _mm(qkv_b, wg_ref[...]) + bg_ref[...]
    return q, k, v, gates, qkv_b


def _mlstm_prompt_kernel(xm_ref, og_ref, wconv_ref, bconv_ref, wbd_ref, wg_ref, wgt_ref,
                         bg_ref, bgt_ref, gh_ref, o_ref, c_ref, n_ref, m_ref, prev_sc, *, L):
    H, D = ML_HEADS, ML_DH

    @pl.when(pl.program_id(0) == 0)
    def _():
        prev_sc[...] = jnp.zeros_like(prev_sc)
        c_ref[...] = jnp.zeros_like(c_ref)
        n_ref[...] = jnp.zeros_like(n_ref)
        m_ref[...] = jnp.full(m_ref.shape, NEG, f32)

    x = xm_ref[...]
    full = jnp.concatenate([prev_sc[...], x], axis=0)
    y = bconv_ref[...]
    for j in range(ML_CONV):
        off = SUBLANES - (ML_CONV - 1) + j
        y = y + full[off:off + L] * wconv_ref[j:j + 1, :]
    prev_sc[...] = x[L - SUBLANES:L]
    xc = y * _sigmoid(y)

    q, k, v, gates, qkv_b = _ml_qkv_gates(xc, x, wbd_ref, wg_ref, bg_ref)
    gates_t = _mm_nt(wgt_ref[...], qkv_b) + bgt_ref[...]
    ig_c = gates[:, 0:H]
    lf_c = _log_sigmoid(gates[:, H:2 * H])
    ig_r = gates_t[0:H, :]
    lf_r = _log_sigmoid(gates_t[H:2 * H, :])

    t_i = lax.broadcasted_iota(i32, (L, L), 0)
    s_i = lax.broadcasted_iota(i32, (L, L), 1)
    causal = s_i <= t_i
    tri = jnp.where(causal, 1.0, 0.0).astype(bf16)
    b_c = _dot_sel_l(tri, lf_c)
    tri_u = jnp.where(t_i <= s_i, 1.0, 0.0).astype(bf16)
    b_r = _dot_sel_r(lf_r, tri_u)

    for h in range(H):
        sl = slice(h * D, (h + 1) * D)
        qh, kh, vh = q[:, sl], k[:, sl], v[:, sl]
        bc = b_c[:, h:h + 1]
        m_prev = m_ref[h:h + 1, 0:1]
        d_in = jnp.where(causal, bc - b_r[h:h + 1, :] + ig_r[h:h + 1, :], NEG)
        d_x = bc + m_prev
        m_t = jnp.maximum(d_x, jnp.max(d_in, axis=-1, keepdims=True))
        w_in = jnp.exp(d_in - m_t)
        w_x = jnp.exp(d_x - m_t)
        qb = qh.astype(bf16)
        kb = kh.astype(bf16)
        vb = vh.astype(bf16)
        s = _mm_nt(qb, kb) * w_in
        c_old = c_ref[h]
        n_old = n_ref[h:h + 1, :]
        num = _mm(s.astype(bf16), vb) + w_x * _mm(qb, c_old.astype(bf16))
        den = jnp.sum(s, axis=-1, keepdims=True) + w_x * jnp.sum(qh * n_old, axis=-1, keepdims=True)
        hh = num * (1.0 / jnp.maximum(jnp.abs(den), jnp.exp(-m_t)))
        m_new = m_t[L - 1:L, :]
        b_last = bc[L - 1:L, :]
        g_x = jnp.exp(b_last + m_prev - m_new)
        g_s = jnp.exp(b_last - bc + ig_c[:, h:h + 1] - m_new)
        ks_ = kh * g_s
        c_ref[h] = g_x * c_old + _mm(ks_.T.astype(bf16), vb)
        n_ref[h:h + 1, :] = g_x * n_old + jnp.sum(ks_, axis=0, keepdims=True)
        m_ref[h:h + 1, :] = jnp.broadcast_to(m_new, (1, LANES))
        hn = hh * lax.rsqrt(jnp.mean(hh * hh, axis=-1, keepdims=True) + EPS) * gh_ref[:, sl]
        o_ref[:, sl] = _sigmoid(og_ref[:, sl]) * hn


def _mlstm_prompt(u, w_conv, b_conv, wbd, w_gate, w_gate_t, b_gate, g_head, L, name):
    T = u.shape[0]
    H, D = ML_HEADS, ML_DH
    nch = ML_DIM // LANES
    full2 = lambda shape: pl.BlockSpec(shape, lambda i: (0,) * len(shape))
    out, c, n, m = pl.pallas_call(
        functools.partial(_mlstm_prompt_kernel, L=L),
        grid=(T // L,),
        in_specs=[pl.BlockSpec((L, ML_DIM), lambda i: (i, 0)),
                  pl.BlockSpec((L, ML_DIM), lambda i: (i, 1)),
                  full2((ML_CONV, ML_DIM)), full2((1, ML_DIM)),
                  full2((3, nch, LANES, LANES)),
                  full2((3 * ML_DIM, 2 * H)), full2((2 * H, 3 * ML_DIM)),
                  full2((1, 2 * H)), full2((2 * H, 1)), full2((1, ML_DIM))],
        out_specs=[pl.BlockSpec((L, ML_DIM), lambda i: (i, 0)),
                   full2((H, D, D)), full2((H, D)), full2((H, LANES))],
        out_shape=[jax.ShapeDtypeStruct((T, ML_DIM), f32),
                   jax.ShapeDtypeStruct((H, D, D), f32),
                   jax.ShapeDtypeStruct((H, D), f32),
                   jax.ShapeDtypeStruct((H, LANES), f32)],
        scratch_shapes=[pltpu.VMEM((SUBLANES, ML_DIM), f32)],
        compiler_params=_cparams("arbitrary"),
        name=name,
    )(u, u, w_conv, b_conv.reshape(1, ML_DIM), wbd, w_gate, w_gate_t,
      b_gate.reshape(1, 2 * H), b_gate.reshape(2 * H, 1), g_head.reshape(1, ML_DIM))
    return out, c, n, m[:, 0]


def _mlstm_sample_pre_kernel(xm_ref, s0_ref, s1_ref, s2_ref, wconv_ref, bconv_ref, wbd_ref,
                             wg_ref, bg_ref, q_ref, k_ref, v_ref, g_ref):
    x = xm_ref[...]
    y = (bconv_ref[...] + s0_ref[...] * wconv_ref[0:1, :] + s1_ref[...] * wconv_ref[1:2, :]
         + s2_ref[...] * wconv_ref[2:3, :] + x * wconv_ref[3:4, :])
    xc = y * _sigmoid(y)
    q, k, v, gates, _ = _ml_qkv_gates(xc, x, wbd_ref, wg_ref, bg_ref)
    q_ref[...] = q
    k_ref[...] = k
    v_ref[...] = v
    g_ref[...] = gates


def _mlstm_sample_step_kernel(qc_ref, kc_ref, q_ref, k_ref, v_ref, gt_ref, og_ref, gh_ref, c_ref, n_ref, m_ref,
                              o_ref, cn_ref, nn_ref, mn_ref):
    H, D = ML_HEADS, ML_DH
    ig = gt_ref[0:H, :]
    lf = _log_sigmoid(gt_ref[H:2 * H, :])
    m_old = m_ref[...]
    m_new = jnp.maximum(lf + m_old, ig)
    w_in = jnp.exp(ig - m_new)
    w_x = jnp.exp(lf + m_old - m_new)
    mn_ref[...] = m_new
    q, k, v, n_old = q_ref[...], k_ref[...], v_ref[...], n_ref[...]
    cq = jnp.concatenate([jnp.sum(c_ref[h] * qc_ref[:, h:h + 1], axis=0, keepdims=True) for h in range(H)], axis=0)
    s = jnp.sum(q * k, axis=-1, keepdims=True) * w_in
    num = s * v + w_x * cq
    den = s + w_x * jnp.sum(n_old * q, axis=-1, keepdims=True)
    hh = num / jnp.maximum(jnp.abs(den), jnp.exp(-m_new))
    hn = hh * lax.rsqrt(jnp.mean(hh * hh, axis=-1, keepdims=True) + EPS) * gh_ref[...]
    o_ref[...] = _sigmoid(og_ref[...]) * hn
    nn_ref[...] = w_x * n_old + w_in * k
    for h in range(H):
        cn_ref[h] = w_x[h:h + 1, :] * c_ref[h] + w_in[h:h + 1, :] * (kc_ref[:, h:h + 1] * v[h:h + 1, :])


def _mlstm_sample(xm, og, conv_state, c0, n0, m0, w_conv, b_conv, wbd, w_gate, b_gate, g_head):
    B = xm.shape[0]
    H, D = ML_HEADS, ML_DH
    sds = lambda *s: jax.ShapeDtypeStruct(s, f32)
    q, k, v, gates = pl.pallas_call(
        _mlstm_sample_pre_kernel,
        out_shape=[sds(B, ML_DIM), sds(B, ML_DIM), sds(B, ML_DIM), sds(B, 2 * H)],
        compiler_params=pltpu.CompilerParams(vmem_limit_bytes=VMEM_LIMIT),
        name="mlstm_sample_pre",
    )(xm, conv_state[:, 0], conv_state[:, 1], conv_state[:, 2], w_conv, b_conv.reshape(1, ML_DIM),
      wbd, w_gate, b_gate.reshape(1, 2 * H))
    q3, k3, v3 = (a.reshape(B, H, D) for a in (q, k, v))
    per_b = lambda *s: pl.BlockSpec((None,) + s, lambda b: (b,) + (0,) * len(s))
    out, c, n, m = pl.pallas_call(
        _mlstm_sample_step_kernel,
        grid=(B,),
        in_specs=[per_b(D, H), per_b(D, H), per_b(H, D), per_b(H, D), per_b(H, D), per_b(2 * H, 1),
                  per_b(H, D), pl.BlockSpec((H, D), lambda b: (0, 0)),
                  per_b(H, D, D), per_b(H, D), per_b(H, 1)],
        out_specs=[per_b(H, D), per_b(H, D, D), per_b(H, D), per_b(H, 1)],
        out_shape=[sds(B, H, D), sds(B, H, D, D), sds(B, H, D), sds(B, H, 1)],
        compiler_params=_cparams("parallel"),
        name="mlstm_sample_step",
    )(q3.transpose(0, 2, 1), k3.transpose(0, 2, 1), q3, k3, v3, gates.reshape(B, 2 * H, 1),
      og.reshape(B, H, D), g_head.reshape(H, D), c0, n0, m0.reshape(B, H, 1))
    return out.reshape(B, ML_DIM), c, n, m.reshape(B, H)


def _compress_kernel(x_ref, pe_ref, wp_ref, wphi_ref, o_ref, f0_sc, f1_sc, mn_sc):
    step = pl.program_id(0)
    sub = x_ref.shape[0] // CMP_STRIDE
    x3 = x_ref[...].reshape(sub, CMP_STRIDE, KV_ROW)
    base = pl.multiple_of(step * sub, sub)
    for o, sc in ((0, f0_sc), (1, f1_sc)):
        y = x3 + pe_ref[o][None]
        sc[pl.ds(base, sub), :] = jnp.sum(y * _sigmoid(y) * wp_ref[o][None], axis=1)
    mn_sc[pl.ds(base, sub), :] = jnp.sum(x3, axis=1) * (1.0 / CMP_STRIDE)

    @pl.when(step == pl.num_programs(0) - 1)
    def _():
        ns = f0_sc.shape[0]
        feat = f0_sc[...] + pltpu.roll(f1_sc[...], ns - 1, axis=0)
        mn = mn_sc[...]
        pooled = (mn + pltpu.roll(mn, ns - 1, axis=0)) * (CMP_STRIDE / CMP_BLOCK)
        for c in range(2):
            for g in range(NSA_KV_HEADS):
                sl = slice((c * NSA_KV_HEADS + g) * NSA_DH, (c * NSA_KV_HEADS + g + 1) * NSA_DH)
                o_ref[:, sl] = pooled[:, sl] + _dot3(feat[:, sl], wphi_ref[c])


def _compress_tables(pe, wpos):
    def lay(a):
        r = CMP_BLOCK // CMP_STRIDE
        a = a.reshape(2, r, CMP_STRIDE, NSA_DH).transpose(1, 2, 0, 3)
        a = jnp.broadcast_to(a[:, :, :, None, :], (r, CMP_STRIDE, 2, NSA_KV_HEADS, NSA_DH))
        return a.reshape(r, CMP_STRIDE, KV_ROW)
    return lay(pe), lay(wpos)


def _compress_prompt(u, col_block, pe_t, wp_t, wphi, rows=512):
    T = u.shape[0]
    n_sub = T // CMP_STRIDE
    const = lambda shape: pl.BlockSpec(shape, lambda s: (0,) * len(shape))
    return pl.pallas_call(
        _compress_kernel,
        grid=(T // rows,),
        in_specs=[pl.BlockSpec((rows, KV_ROW), lambda s: (s, col_block)),
                  const(pe_t.shape), const(wp_t.shape), const(wphi.shape)],
        out_specs=const((n_sub, KV_ROW)),
        out_shape=jax.ShapeDtypeStruct((n_sub, KV_ROW), f32),
        scratch_shapes=[pltpu.VMEM((n_sub, KV_ROW), f32)] * 3,
        compiler_params=_cparams("arbitrary"),
        name="compress_prompt",
    )(u, pe_t, wp_t, wphi)


def _compress_paged_kernel(pt_ref, *refs, n_pages):
    pages = refs[:n_pages]
    pe_ref, wp_ref, wphi_ref, o_ref, f0_sc, f1_sc, mn_sc = refs[n_pages:]
    step = pl.program_id(1)
    sub = PAGE_SIZE // CMP_STRIDE
    tiles = CMP_STRIDE * KV_CHUNKS // SUBLANES
    packed = 2 * SUBLANES
    ptiles = CMP_STRIDE * KV_CHUNKS // packed
    out_rows = sub * SUBLANES
    for p in range(n_pages):
        x = pages[p][...]
        base = pl.multiple_of((step * n_pages + p) * out_rows, out_rows)
        xh = (x.astype(bf16) * 0.5).reshape(sub, ptiles, packed, LANES)
        for o, sc in ((0, f0_sc), (1, f1_sc)):
            y = xh + pe_ref[o][None]
            b = y * wp_ref[o][None]
            z = b + b * jnp.tanh(y)
            z = (z[:, 0] + z[:, 1]) + (z[:, 2] + z[:, 3])
            zf = z.astype(f32).reshape(sub, packed // SUBLANES, SUBLANES, LANES)
            sc[pl.ds(base, out_rows), :] = jnp.sum(zf, axis=1).reshape(out_rows, LANES)
        mean = jnp.sum(x.reshape(sub, tiles, SUBLANES, LANES), axis=1) * (1.0 / CMP_STRIDE)
        mn_sc[pl.ds(base, out_rows), :] = mean.reshape(out_rows, LANES)

    @pl.when(step == pl.num_programs(1) - 1)
    def _():
        ns = f0_sc.shape[0] // SUBLANES
        for c in range(KV_CHUNKS):
            col = lambda sc: (sc[pl.ds(c, ns, stride=SUBLANES), :]
                              + sc[pl.ds(KV_CHUNKS + c, ns, stride=SUBLANES), :])
            feat = col(f0_sc) + pltpu.roll(col(f1_sc), ns - 1, axis=0)
            mn = col(mn_sc)
            pooled = (mn + pltpu.roll(mn, ns - 1, axis=0)) * (CMP_STRIDE / CMP_BLOCK)
            o_ref[:, c * LANES:(c + 1) * LANES] = pooled + _dot3(feat, wphi_ref[c // NSA_KV_HEADS])


def _compress_paged(pool, page_table, pe_t, wp_t, wphi, pages_per_step=64):
    B, n_pages = page_table.shape
    P = pages_per_step
    n_sub = n_pages * PAGE_SIZE // CMP_STRIDE
    r = CMP_BLOCK // CMP_STRIDE
    packed = 2 * SUBLANES
    ptiles = CMP_STRIDE * KV_CHUNKS // packed
    pe4 = (0.5 * pe_t).astype(bf16).reshape(r, ptiles, packed, LANES)
    wp4 = wp_t.astype(bf16).reshape(r, ptiles, packed, LANES)
    page_rows = PAGE_SIZE * KV_CHUNKS
    specs = [pl.BlockSpec((None, page_rows, LANES),
                          functools.partial(lambda b, s, pt, p: (pt[b * n_pages + s * P + p], 0, 0), p=p))
             for p in range(P)]
    const = lambda shape: pl.BlockSpec(shape, lambda *a: (0,) * len(shape))
    gs = pltpu.PrefetchScalarGridSpec(
        num_scalar_prefetch=1, grid=(B, n_pages // P),
        in_specs=specs + [const(pe4.shape), const(wp4.shape), const(wphi.shape)],
        out_specs=pl.BlockSpec((None, n_sub, KV_ROW), lambda b, s, pt: (b, 0, 0)),
        scratch_shapes=[pltpu.VMEM((n_sub * SUBLANES, LANES), f32)] * 3)
    return pl.pallas_call(
        functools.partial(_compress_paged_kernel, n_pages=P),
        grid_spec=gs,
        out_shape=jax.ShapeDtypeStruct((B, n_sub, KV_ROW), f32),
        compiler_params=_cparams("parallel", "arbitrary"),
        name="compress_paged",
    )(page_table.reshape(-1), *([pool] * P), pe4, wp4, wphi)


def _kv_prep_kernel(ks_ref, kw_ref, ksk_ref, ksv_ref, kwk_ref, kwv_ref, kn2_ref):
    rows = ks_ref.shape[0]
    ks = ks_ref[...]
    kw = kw_ref[...]
    r = pl.program_id(0) * rows + lax.broadcasted_iota(i32, (rows, LANES), 0)
    n = lax.broadcasted_iota(i32, (rows, LANES), 1)
    onehot = jnp.where(_div_pow2(r, SEL_BLOCK) == n, 1.0, 0.0).astype(bf16)
    ones_col = jnp.where(n == 0, 1.0, 0.0).astype(bf16)

    @pl.when(pl.program_id(0) == 0)
    def _():
        kn2_ref[...] = jnp.zeros_like(kn2_ref)

    for g in range(NSA_KV_HEADS):
        ksl = slice(g * NSA_DH, (g + 1) * NSA_DH)
        vsl = slice(NSA_KV_W + g * NSA_DH, NSA_KV_W + (g + 1) * NSA_DH)
        kb = ks[:, ksl].astype(bf16)
        ksk_ref[g, :, 0:NSA_DH] = kb
        ksk_ref[g, :, NSA_DH:NSA_DH + LANES] = onehot
        ksv_ref[g, :, 0:NSA_DH] = ks[:, vsl].astype(bf16)
        ksv_ref[g, :, NSA_DH:NSA_DH + LANES] = ones_col
        kwk_ref[g] = kw[:, ksl].astype(bf16)
        kwv_ref[g, :, 0:NSA_DH] = kw[:, vsl].astype(bf16)
        kwv_ref[g, :, NSA_DH:NSA_DH + LANES] = ones_col
        kf = kb.astype(f32)
        n2 = jnp.max(jnp.sum(kf * kf, axis=-1, keepdims=True), axis=0, keepdims=True)
        kn2_ref[g] = jnp.maximum(kn2_ref[g], jnp.broadcast_to(n2, kn2_ref.shape[1:]))


def _kv_prep(u, ks_col_block, kw_col_block, rows=512):
    T = u.shape[0]
    G = NSA_KV_HEADS
    assert (T - 1) // SEL_BLOCK + 1 <= LANES
    sd = lambda w: jax.ShapeDtypeStruct((G, T, w), bf16)
    ospec = lambda w: pl.BlockSpec((G, rows, w), lambda i: (0, i, 0))
    return pl.pallas_call(
        _kv_prep_kernel,
        grid=(T // rows,),
        in_specs=[pl.BlockSpec((rows, KV_ROW), lambda i: (i, ks_col_block)),
                  pl.BlockSpec((rows, KV_ROW), lambda i: (i, kw_col_block))],
        out_specs=[ospec(NSA_DH + LANES), ospec(NSA_DH + LANES), ospec(NSA_DH), ospec(NSA_DH + LANES),
                   pl.BlockSpec((G, SUBLANES, LANES), lambda i: (0, 0, 0))],
        out_shape=[sd(NSA_DH + LANES), sd(NSA_DH + LANES), sd(NSA_DH), sd(NSA_DH + LANES),
                   jax.ShapeDtypeStruct((G, SUBLANES, LANES), f32)],
        compiler_params=_cparams("arbitrary"),
        name="kv_prep",
    )(u, u)


def _top_blocks(score, n_top, axis):
    lane = lax.broadcasted_iota(i32, score.shape, axis).astype(f32)
    width = float(score.shape[axis])
    work = score
    firsts = []
    for _ in range(n_top):
        mx = jnp.max(work, axis=axis, keepdims=True)
        first = jnp.min(jnp.where(work == mx, lane, width), axis=axis, keepdims=True)
        work = jnp.where(lane == first, REMOVED, work)
        firsts.append(first)
    return work, firsts


def _nsa_prompt_kernel(x_ref, gmix_ref, wgt_ref, q_ref, kck_ref, kcv_ref, ksk_ref, ksv_ref, kwk_ref, kwv_ref, kn2_ref,
                       o_ref, shift_sc, m_sc, acc_sc, *, n_sel):
    R, D = NSA_GROUP, NSA_DH
    QB = q_ref.shape[0]
    rows = R * QB
    qb = pl.program_id(1)
    q = q_ref[...]
    q2_b = (jnp.concatenate([q[:, r * D:(r + 1) * D] for r in range(R)], axis=0) * (D ** -0.5 * LOG2E)).astype(bf16)
    pos = qb * QB + lax.broadcasted_iota(i32, (QB, 1), 0)

    def head_bias(valid):
        return jnp.concatenate([jnp.where(valid, 0.0, NEG)] * R, axis=0)

    ns = kck_ref.shape[0]
    s = _mm_nt(q2_b, kck_ref[...].astype(bf16))
    j = lax.broadcasted_iota(i32, (1, ns), 1)
    p_c = _biased_softmax2_rows(s, head_bias(j * CMP_STRIDE + (CMP_BLOCK - 1) <= pos))
    o_c = _mm(p_c.astype(bf16), kcv_ref[...].astype(bf16))

    imp = p_c[0:QB]
    for r in range(1, R):
        imp = imp + p_c[r * QB:(r + 1) * QB]
    ratio = SEL_BLOCK // CMP_STRIDE
    off = CMP_BLOCK // CMP_STRIDE - 1
    nn = lax.broadcasted_iota(i32, (LANES, ns), 0)
    jj = lax.broadcasted_iota(i32, (LANES, ns), 1)
    overlap_t = jnp.where((jj >= ratio * nn - off) & (jj < ratio * nn + ratio), 1.0, 0.0).astype(bf16)
    imp_sel = sum(_mm_nt(overlap_t, part) for part in _split3(imp))
    n_idx = lax.broadcasted_iota(i32, (LANES, QB), 0)
    cur = _div_pow2(qb * QB + lax.broadcasted_iota(i32, (LANES, QB), 1), SEL_BLOCK)
    forced = (n_idx == 0) | (n_idx == cur) | (n_idx == cur - 1)
    score = jnp.where(forced, FORCE, jnp.where(n_idx <= cur, imp_sel, NEG))
    score = jnp.where(n_idx < n_sel, score, REMOVED)
    taken, _ = _top_blocks(score, min(SEL_TOPN, n_sel), axis=0)
    bias = jnp.where((taken < 0.5 * REMOVED) & (n_idx <= cur), 0.0, MASK_BIAS).T

    KT = SEL_KT
    last = (qb * QB + QB - 1) // KT
    q2 = q2_b.astype(f32)
    bound = jnp.sqrt(jnp.sum(q2 * q2, axis=-1, keepdims=True) * kn2_ref[0:1, 0:1]) * 1.01 + 1e-3
    k_self = ksk_ref[pl.ds(pl.multiple_of(qb * QB, QB), QB), 0:D].astype(f32)
    s_self = jnp.concatenate([jnp.sum(q2[r * QB:(r + 1) * QB] * k_self, axis=-1, keepdims=True) for r in range(R)],
                             axis=0)
    shift_sc[...] = bound
    diag_bias = head_bias(last * KT + lax.broadcasted_iota(i32, (1, KT), 1) <= pos)

    def shifted_queries(shift):
        return jnp.concatenate(
            [jnp.concatenate([q2_b[r * QB:(r + 1) * QB], (bias - shift[r * QB:(r + 1) * QB]).astype(bf16)], axis=1)
             for r in range(R)], axis=0)

    def key_tile(ref, kt):
        return ref[pl.ds(pl.multiple_of(kt * KT, KT), KT), :]

    @pl.when(jnp.max(bound - s_self) > SAFE_SPAN)
    def _():
        qp0 = shifted_queries(jnp.zeros_like(bound))

        def lane_max(kt, sk):
            mx = m_sc[...]
            for c in range(KT // LANES):
                mx = jnp.maximum(mx, sk[:, c * LANES:(c + 1) * LANES])
            m_sc[...] = mx

        def max_pass(kt, carry):
            lane_max(kt, _mm_nt(qp0, key_tile(ksk_ref, kt)))
            return carry

        m_sc[...] = jnp.full(m_sc.shape, NEG, f32)
        lax.fori_loop(0, last, max_pass, 0)
        lane_max(last, _mm_nt(qp0, key_tile(ksk_ref, last)) + diag_bias)
        shift_sc[...] = jnp.max(m_sc[...], axis=-1, keepdims=True)

    qp = shifted_queries(shift_sc[...])
    acc_sc[...] = jnp.zeros_like(acc_sc)

    def accumulate(kt, s2):
        acc_sc[...] += _mm(jnp.exp2(s2).astype(bf16), key_tile(ksv_ref, kt))

    def body(kt, carry):
        accumulate(kt, _mm_nt(qp, key_tile(ksk_ref, kt)))
        return carry

    lax.fori_loop(0, last, body, 0)
    accumulate(last, _mm_nt(qp, key_tile(ksk_ref, last)) + diag_bias)
    acc = acc_sc[...]
    o_s = acc[:, 0:D] * (1.0 / acc[:, D:D + 1])

    wlen = WINDOW + QB
    wstart = pl.multiple_of(jnp.maximum(qb * QB - WINDOW, 0), int(np.gcd(QB, WINDOW)))
    sw = _mm_nt(q2_b, kwk_ref[pl.ds(wstart, wlen), :])
    diff = pos - (wstart + lax.broadcasted_iota(i32, (1, wlen), 1))
    e_w, _ = _biased_exp2_rows(sw, head_bias((diff >= 0) & (diff <= WINDOW)))
    acc_w = _mm(e_w.astype(bf16), kwv_ref[pl.ds(wstart, wlen), :])
    o_w = acc_w[:, 0:D] * (1.0 / acc_w[:, D:D + 1])

    gate = _sigmoid(_mm(_rms(x_ref[...], gmix_ref[...]).astype(bf16), wgt_ref[...]))
    for r in range(R):
        rs = slice(r * QB, (r + 1) * QB)
        o_ref[:, r * D:(r + 1) * D] = (gate[:, r:r + 1] * o_c[rs] + gate[:, R + r:R + r + 1] * o_s[rs]
                                       + gate[:, 2 * R + r:2 * R + r + 1] * o_w[rs])


def _nsa_prompt(x, g_mix, w_gate_g, u, q_col_block, kvc, ksk, ksv, kwk, kwv, kn2, tq):
    T, Dm = x.shape
    G, R, D, QB = NSA_KV_HEADS, NSA_GROUP, NSA_DH, tq
    ns = kvc.shape[0]
    n_sel = (T - 1) // SEL_BLOCK + 1
    assert T % SEL_KT == 0 and T % QB == 0 and T >= WINDOW + QB and n_sel <= LANES
    rows = R * QB
    res = lambda w: pl.BlockSpec((None, T, w), lambda g, i: (g, 0, 0))
    return pl.pallas_call(
        functools.partial(_nsa_prompt_kernel, n_sel=n_sel),
        grid=(G, T // QB),
        in_specs=[pl.BlockSpec((QB, Dm), lambda g, i: (i, 0)),
                  pl.BlockSpec((1, Dm), lambda g, i: (0, 0)),
                  pl.BlockSpec((None, Dm, LANES), lambda g, i: (g, 0, 0)),
                  pl.BlockSpec((QB, R * D), lambda g, i: (i, q_col_block + g)),
                  pl.BlockSpec((ns, D), lambda g, i: (0, g)),
                  pl.BlockSpec((ns, D), lambda g, i: (0, G + g)),
                  res(D + LANES), res(D + LANES), res(D), res(D + LANES),
                  pl.BlockSpec((None, SUBLANES, LANES), lambda g, i: (g, 0, 0))],
        out_specs=pl.BlockSpec((QB, R * D), lambda g, i: (i, g)),
        out_shape=jax.ShapeDtypeStruct((T, NSA_DIM), f32),
        scratch_shapes=[pltpu.VMEM((rows, 1), f32), pltpu.VMEM((rows, LANES), f32),
                        pltpu.VMEM((rows, D + LANES), f32)],
        compiler_params=_cparams("parallel", "arbitrary"),
        name="nsa_prompt",
    )(x, g_mix.reshape(1, Dm), w_gate_g, u, kvc, kvc, ksk, ksv, kwk, kwv, kn2)


def _group_gate_weights(w_gate):
    Dm = w_gate.shape[0]
    G, R = NSA_KV_HEADS, NSA_GROUP
    w = w_gate.reshape(Dm, 3, G, R).transpose(2, 0, 1, 3).reshape(G, Dm, 3 * R)
    return jnp.pad(w, ((0, 0), (0, 0), (0, LANES - 3 * R)))


def _nsa_sample_cmp_kernel(q_ref, kvc_ref, oc_ref, idx_ref, *, pos, n_sel, sel_w):
    H, R, D, G = NSA_HEADS, NSA_GROUP, NSA_DH, NSA_KV_HEADS
    S, ns = kvc_ref.shape[0], kvc_ref.shape[1]
    head = lax.broadcasted_iota(i32, (H, 1), 0)
    grp = _div_pow2(head, R)
    j = lax.broadcasted_iota(i32, (1, ns), 1)
    imps = []
    for b in range(S):
        qs_b = (q_ref[b] * (D ** -0.5)).astype(bf16)
        s = jnp.zeros((H, ns), f32)
        for g in range(G):
            s = jnp.where(grp == g, _mm_nt(qs_b, kvc_ref[b, :, g * D:(g + 1) * D].astype(bf16)), s)
        e, den = _masked_softmax_rows(s, j * CMP_STRIDE + (CMP_BLOCK - 1) <= pos)
        p_c = e / den
        p_b = p_c.astype(bf16)
        o_c = jnp.zeros((H, D), f32)
        imp = jnp.zeros((H, ns), f32)
        for g in range(G):
            o_c = jnp.where(grp == g, _mm(p_b, kvc_ref[b, :, (G + g) * D:(G + g + 1) * D].astype(bf16)), o_c)
            imp = jnp.where(grp == g, jnp.sum(jnp.where(grp == g, p_c, 0.0), axis=0, keepdims=True), imp)
        oc_ref[b] = o_c
        imps.append(imp)
    imp = jnp.concatenate(imps, axis=0)
    ratio = SEL_BLOCK // CMP_STRIDE
    off = CMP_BLOCK // CMP_STRIDE - 1
    nn = lax.broadcasted_iota(i32, (sel_w, ns), 0)
    jj = lax.broadcasted_iota(i32, (sel_w, ns), 1)
    overlap_t = jnp.where((jj >= ratio * nn - off) & (jj < ratio * nn + ratio), 1.0, 0.0).astype(bf16)
    imp_sel = sum(_mm_nt(overlap_t, part) for part in _split3(imp))
    n_idx = lax.broadcasted_iota(i32, (sel_w, S * H), 0)
    cur = pos // SEL_BLOCK
    forced = (n_idx == 0) | (n_idx == cur) | (n_idx == cur - 1)
    score = jnp.where(forced, FORCE, jnp.where(n_idx <= cur, imp_sel, NEG))
    score = jnp.where(n_idx < n_sel, score, REMOVED)
    _, firsts = _top_blocks(score, SEL_TOPN, axis=0)
    idx_ref[...] = jnp.concatenate(firsts, axis=0).astype(i32)


def _nsa_sample_cmp(q, kvc, pos):
    B, H, D = q.shape
    ns = kvc.shape[1]
    n_sel = pos // SEL_BLOCK + 1
    assert n_sel >= SEL_TOPN
    sel_w = -(-n_sel // SUBLANES) * SUBLANES
    S = 8 if B % 8 == 0 else 1
    o_c, idx = pl.pallas_call(
        functools.partial(_nsa_sample_cmp_kernel, pos=pos, n_sel=n_sel, sel_w=sel_w),
        grid=(B // S,),
        in_specs=[pl.BlockSpec((S, H, D), lambda b: (b, 0, 0)),
                  pl.BlockSpec((S, ns, KV_ROW), lambda b: (b, 0, 0))],
        out_specs=[pl.BlockSpec((S, H, D), lambda b: (b, 0, 0)),
                   pl.BlockSpec((None, SEL_TOPN, S * H), lambda b: (b, 0, 0))],
        out_shape=[jax.ShapeDtypeStruct((B, H, D), f32), jax.ShapeDtypeStruct((B // S, SEL_TOPN, S * H), i32)],
        compiler_params=_cparams("parallel"),
        name="nsa_sample_cmp",
    )(q, kvc)
    idx = idx.reshape(B // S, SEL_TOPN, S, H)[..., ::NSA_GROUP]
    return o_c, idx.transpose(0, 2, 3, 1).reshape(B, NSA_KV_HEADS, SEL_TOPN)


def _nsa_sample_sel_kernel(idx_ref, pt_ref, q_ref, new_ref, *refs, past):
    H, R, D, G = NSA_HEADS, NSA_GROUP, NSA_DH, NSA_KV_HEADS
    blocks, o_ref = refs[:G * SEL_TOPN], refs[G * SEL_TOPN]
    b = pl.program_id(0)
    qs_b = (q_ref[...] * (D ** -0.5)).astype(bf16)
    grp = _div_pow2(lax.broadcasted_iota(i32, (H, 1), 0), R)
    per_tile = SUBLANES // KV_CHUNKS
    half = SEL_BLOCK // per_tile
    o = jnp.zeros((H, D), f32)
    for g in range(G):
        ks, vs = [], []
        lane = lax.broadcasted_iota(i32, (1, SEL_TOPN * SEL_BLOCK), 1)
        slot = _div_pow2(lane, SEL_BLOCK)
        in_slot = lane & (SEL_BLOCK - 1)
        pk_row = (in_slot & (half - 1)) * per_tile + _div_pow2(in_slot, half)
        for i in range(SEL_TOPN):
            blk = blocks[g * SEL_TOPN + i]
            base = idx_ref[(b * G + g) * SEL_TOPN + i] * SEL_BLOCK
            pk_row = pk_row + jnp.where(slot == i, base, 0)
            for par in range(per_tile):
                pk_col = base + per_tile * lax.broadcasted_iota(i32, (half, 1), 0) + par
                old = pk_col < past
                k = blk[pl.ds(par * KV_CHUNKS + g, half, stride=SUBLANES), :]
                v = blk[pl.ds(par * KV_CHUNKS + G + g, half, stride=SUBLANES), :]
                ks.append(jnp.where(old, k, new_ref[g:g + 1, :]).astype(bf16))
                vs.append(jnp.where(old, v, new_ref[G + g:G + g + 1, :]).astype(bf16))
        s = _mm_nt(qs_b, jnp.concatenate(ks, axis=0))
        e, den = _masked_softmax_rows(s, pk_row <= past)
        o = jnp.where(grp == g, _mm((e / den).astype(bf16), jnp.concatenate(vs, axis=0)), o)
    o_ref[...] = o


def _nsa_sample_sel(q, ks_new, pool, page_table, idx, past):
    B, H, D = q.shape
    G = NSA_KV_HEADS
    n_pages = page_table.shape[1]
    halves = PAGE_SIZE // SEL_BLOCK
    blk_rows = SEL_BLOCK * KV_CHUNKS
    pool_h = pool.reshape(pool.shape[0] * halves, blk_rows, D)
    last_old = past // SEL_BLOCK - 1

    def blk_map(b, idx_r, pt_r, g, i):
        blk = jnp.minimum(idx_r[(b * G + g) * SEL_TOPN + i], last_old)
        return (pt_r[b * n_pages + blk // halves] * halves + blk % halves, 0, 0)

    specs = [pl.BlockSpec((None, blk_rows, D), functools.partial(blk_map, g=g, i=i))
             for g in range(G) for i in range(SEL_TOPN)]
    gs = pltpu.PrefetchScalarGridSpec(
        num_scalar_prefetch=2, grid=(B,),
        in_specs=[pl.BlockSpec((None, H, D), lambda b, *_: (b, 0, 0)),
                  pl.BlockSpec((None, KV_CHUNKS, D), lambda b, *_: (b, 0, 0))] + specs,
        out_specs=pl.BlockSpec((None, H, D), lambda b, *_: (b, 0, 0)))
    return pl.pallas_call(
        functools.partial(_nsa_sample_sel_kernel, past=past),
        grid_spec=gs,
        out_shape=jax.ShapeDtypeStruct((B, H, D), f32),
        compiler_params=_cparams("arbitrary"),
        name="nsa_sample_sel",
    )(idx.reshape(-1), page_table.reshape(-1), q, ks_new, *([pool_h] * (G * SEL_TOPN)))


def _nsa_sample_win_kernel(q_ref, buf_ref, new_ref, oc_ref, os_ref, gt_ref, o_ref, win_ref, *, past):
    H, R, D, G = NSA_HEADS, NSA_GROUP, NSA_DH, NSA_KV_HEADS
    rows = buf_ref.shape[1]
    Lb = rows // KV_CHUNKS
    per_tile = SUBLANES // KV_CHUNKS
    half = Lb // per_tile
    grp = _div_pow2(lax.broadcasted_iota(i32, (H, 1), 0), R)
    lane = lax.broadcasted_iota(i32, (1, Lb), 1)
    key_pos = past - Lb + (lane & (half - 1)) * per_tile + _div_pow2(lane, half)
    diff = past - key_pos
    valid = (diff >= 0) & (diff <= WINDOW)
    for b in range(buf_ref.shape[0]):
        qs = q_ref[b] * (D ** -0.5)
        qs_b = qs.astype(bf16)
        o_w = jnp.zeros((H, D), f32)
        for g in range(G):
            chunk = lambda c: jnp.concatenate(
                [buf_ref[b, pl.ds(par * KV_CHUNKS + c, half, stride=SUBLANES), :] for par in range(per_tile)], axis=0)
            new_k, new_v = new_ref[b, g:g + 1, :], new_ref[b, G + g:G + g + 1, :]
            s_b = jnp.where(valid, _mm_nt(qs_b, chunk(g).astype(bf16)), NEG)
            s_n = jnp.sum(qs * new_k, axis=-1, keepdims=True)
            mx = jnp.maximum(jnp.max(s_b, axis=-1, keepdims=True), s_n)
            e_b = jnp.where(valid, jnp.exp(s_b - mx), 0.0)
            e_n = jnp.exp(s_n - mx)
            den = jnp.sum(e_b, axis=-1, keepdims=True) + e_n
            og = _mm((e_b / den).astype(bf16), chunk(G + g).astype(bf16)) + (e_n / den) * new_v
            o_w = jnp.where(grp == g, og, o_w)
        gate = _sigmoid(gt_ref[b])
        o_ref[b] = gate[:, 0:1] * oc_ref[b] + gate[:, 1:2] * os_ref[b] + gate[:, 2:3] * o_w
        win_ref[b, 0:rows - KV_CHUNKS, :] = buf_ref[b, KV_CHUNKS:rows, :]
        win_ref[b, rows - KV_CHUNKS:rows, :] = new_ref[b]


def _nsa_sample_win(q, win_buf, kw_new, o_c, o_s, gt, past):
    B, H, D = q.shape
    rows = win_buf.shape[1]
    assert rows == WINDOW * KV_CHUNKS
    S = 4 if B % 4 == 0 else 1
    per_b = lambda *s: pl.BlockSpec((S,) + s, lambda b: (b,) + (0,) * len(s))
    return pl.pallas_call(
        functools.partial(_nsa_sample_win_kernel, past=past),
        grid=(B // S,),
        in_specs=[per_b(H, D), per_b(rows, D), per_b(KV_CHUNKS, D), per_b(H, D), per_b(H, D), per_b(H, 3)],
        out_specs=[per_b(H, D), per_b(rows, D)],
        out_shape=[jax.ShapeDtypeStruct((B, H, D), f32), jax.ShapeDtypeStruct((B, rows, D), f32)],
        compiler_params=_cparams("parallel"),
        name="nsa_sample_win",
    )(q, win_buf, kw_new, o_c, o_s, gt.reshape(B, 3, H).transpose(0, 2, 1))


def kernel(x_prompt, x_sample, cache_cmp_kv, cache_sel_kv, cache_win_kv, cache_mem_kv, state_mlstm_c, state_mlstm_n, state_mlstm_m, state_conv, page_table, mem_prompt, g_mix, w_in, w_conv, b_conv, w_mq, w_mk, w_mv, w_mgate, b_mgate, g_mhead, cmp_pe, cmp_wpos, cmp_wphi, w_out, g_memx, g_mems, w_mem_q, w_mem_kv, w_mem_o, g_ffn, w_ff1, w_ff2, g_final):
    depth = w_in.shape[0]
    assert depth == 1 and x_prompt.shape[0] == 1 and x_sample.shape[1] == 1
    T, Dm = x_prompt.shape[1:]
    B = x_sample.shape[0]
    G, Dh, H = NSA_KV_HEADS, NSA_DH, ML_HEADS
    past = page_table.shape[1] * PAGE_SIZE
    assert (past + 1) // CMP_STRIDE == past // CMP_STRIDE
    n_main = 2 * ML_DIM + NSA_DIM + 3 * KV_ROW
    n_gate = 3 * NSA_HEADS
    l = 0
    hp = x_prompt.reshape(T, Dm)
    hs = x_sample.reshape(B, Dm)

    wbd = _blockdiag_weights(jnp.stack([w_mq[l], w_mk[l], w_mv[l]])).astype(bf16)
    w_mg = w_mgate[l].astype(bf16)
    w_mg_t = w_mg.T
    pe_t, wp_t = _compress_tables(cmp_pe[l], cmp_wpos[l])
    wb_out, wb_mq, wb_mkv, wb_mo, wb_f1, wb_f2 = (
        w[l].astype(bf16) for w in (w_out, w_mem_q, w_mem_kv, w_mem_o, w_ff1, w_ff2))
    wb_in = w_in[l, :, :n_main].astype(bf16)
    wb_gate = w_in[l, :, n_main:].astype(bf16)
    w_gt = jnp.pad(wb_gate, ((0, 0), (0, LANES - n_gate)))
    q_cb = 2 * ML_DIM // KV_ROW
    c_cb = (2 * ML_DIM + NSA_DIM) // KV_ROW
    n_mq, n_mkv = w_mem_q.shape[-1], w_mem_kv.shape[-1]
    tm_in, tm, tn, tf, tq, ml_chunk = 1024, 512, 512, 1024, 256, 256

    u, (kc_lin, ks_lin, kw_lin) = _in_proj(hp, g_mix[l], wb_in, n_main, c_cb, 3, tm_in, "in_proj_p")
    ml_o, c_p, n_p, m_p = _mlstm_prompt(u, w_conv[l], b_conv[l], wbd, w_mg, w_mg_t, b_mgate[l],
                                        g_mhead[l], ml_chunk, "mlstm_prompt")
    kvc_p = _compress_prompt(u, c_cb, pe_t, wp_t, cmp_wphi[l])
    nsa_o = _nsa_prompt(hp, g_mix[l], _group_gate_weights(wb_gate), u, q_cb, kvc_p,
                        *_kv_prep(u, c_cb + 1, c_cb + 2), tq)
    h1 = _out_proj(ml_o, nsa_o, wb_out, hp, tm, Dm, "out_proj_p")
    mem_kv = _norm_matmul(mem_prompt.reshape(-1, Dm), g_mems[l], wb_mkv, n_mkv, mem_prompt.shape[1], tn, "mem_kv")
    h2 = _mem_prompt(h1, g_memx[l], wb_mq, mem_kv, wb_mo, tm, "mem_attn_p")
    y_p = _ffn_final(h2, g_ffn[l], wb_f1, wb_f2, g_final, tm, tf, "ffn_p")

    kv_shape = lambda a: a.reshape(1, 1, -1, 2, G, Dh)
    out_p = (y_p.reshape(1, T, Dm),
             kv_shape(kc_lin), kv_shape(ks_lin),
             kv_shape(kw_lin[(T - min(WINDOW, T)) * KV_CHUNKS:]),
             mem_kv.reshape(1, 1, -1, 2, MEM_HEADS, MEM_DH),
             c_p[None, None], n_p[None, None], m_p[None, None],
             u[T - (ML_CONV - 1):, :ML_DIM][None, None])

    us = _norm_matmul(hs, g_mix[l], wb_in, n_main, B, tn, "in_proj_s")
    gts = _norm_matmul(hs, g_mix[l], w_gt, LANES, B, LANES, "in_gate_s")[:, :n_gate]
    xm_s, og_s = us[:, :ML_DIM], us[:, ML_DIM:2 * ML_DIM]
    q_s = us[:, 2 * ML_DIM:2 * ML_DIM + NSA_DIM].reshape(B, NSA_HEADS, Dh)
    kc_s, ks_s, kw_s = (us[:, (c_cb + i) * KV_ROW:(c_cb + i + 1) * KV_ROW] for i in range(3))
    ml_os, c_s, n_s, m_s = _mlstm_sample(xm_s, og_s, state_conv[l], state_mlstm_c[l], state_mlstm_n[l],
                                         state_mlstm_m[l], w_conv[l], b_conv[l], wbd,
                                         w_mg, b_mgate[l], g_mhead[l])
    pool_c = cache_cmp_kv.reshape(-1, PAGE_SIZE * KV_CHUNKS, Dh)
    pool_s = cache_sel_kv.reshape(-1, PAGE_SIZE * KV_CHUNKS, Dh)
    win_buf = cache_win_kv.reshape(B, -1, Dh)
    mem_buf = cache_mem_kv.reshape(B, -1, MEM_DH)
    kvc_s = _compress_paged(pool_c, page_table, pe_t, wp_t, cmp_wphi[l])
    o_cs, idx = _nsa_sample_cmp(q_s, kvc_s, past)
    o_ss = _nsa_sample_sel(q_s, ks_s.reshape(B, KV_CHUNKS, Dh), pool_s, page_table, idx, past)
    nsa_os, win_new = _nsa_sample_win(q_s, win_buf, kw_s.reshape(B, KV_CHUNKS, Dh), o_cs, o_ss, gts, past)
    h1s = _out_proj(ml_os, nsa_os.reshape(B, NSA_DIM), wb_out, hs, B, Dm, "out_proj_s")
    qm_s = _norm_matmul(h1s, g_memx[l], wb_mq, n_mq, B, n_mq, "mem_q_s")
    om_s = _mem_sample(qm_s, mem_buf, "mem_attn_s")
    h2s = _matmul_res(om_s, wb_mo, h1s, B, Dm, "mem_o_s")
    y_s = _ffn_final(h2s, g_ffn[l], wb_f1, wb_f2, g_final, B, tf, "ffn_s")

    kv_s_shape = lambda a: a.reshape(1, B, 1, 2, G, Dh)
    conv_s = jnp.concatenate([state_conv[l][:, 1:], xm_s[:, None, :]], axis=1)
    out_s = (y_s.reshape(B, 1, Dm), kv_s_shape(kc_s), kv_s_shape(ks_s),
             win_new.reshape(1, B, -1, 2, G, Dh), c_s[None], n_s[None], m_s[None], conv_s[None])

    return (out_p[0], out_s[0]) + out_p[1:] + out_s[1:]
```
